```python
import math
import jax, jax.numpy as jnp
from jax import lax
import numpy as np

D_MODEL = 1024
BATCH = 16
SEQ = 256
DEPTH = 2
DEC_BATCH = 8
DEC_SEQ = 4096
PAST_LEN = 512

GRID_W = 64
N_MIXERS = 2
N_HYENA = (DEPTH + 1) // 2
N_SSD = DEPTH // 2
RMS_EPS = 1e-6
HY_SHORT_W = 3
HY_EMB = 33
HY_BANDS = (HY_EMB - 1) // 2
HY_HIDDEN = 64
HY_SHORT_DECAY_FRAC = 0.3
HY_LONG_DECAY_FRAC = 1.5
HY_DECAY_TARGET = 1e-2
HY_MAX_DECAY = math.log(HY_DECAY_TARGET) / HY_SHORT_DECAY_FRAC
HY_MIN_DECAY = math.log(HY_DECAY_TARGET) / HY_LONG_DECAY_FRAC
SSD_D_INNER = 2 * D_MODEL
SSD_HEAD_DIM = 64
SSD_HEADS = SSD_D_INNER // SSD_HEAD_DIM
SSD_GROUPS = 4
SSD_STATE = 128
SSD_CONV_W = 5
SSD_CHUNK = 128
SSD_XBC = SSD_D_INNER + 2 * SSD_GROUPS * SSD_STATE
SSD_IN = SSD_D_INNER + SSD_XBC + 2 * SSD_HEADS
N_EXPERTS = 16
EXPERT_FF = 1024
EC_FACTOR = 2

kernel_name = 'hybrid_hyena_ssd_expert_choice_dit_step'


def _rmsnorm(x, g):
    x32 = x.astype(jnp.float32)
    y = x32 * lax.rsqrt(jnp.mean(x32 * x32, axis=-1, keepdims=True) + RMS_EPS)
    return y.astype(x.dtype) * g


def _adaln(cond, ada_w, ada_b):
    m = jax.nn.silu(cond) @ ada_w + ada_b
    return jnp.split(m[:, None, :], 6, axis=-1)


def _modulate(h, shift, scale):
    return h * (1 + scale) + shift


def _dwconv(x, w, b):
    k, ch = w.shape
    y = lax.conv_general_dilated(x, w[:, None, :].astype(x.dtype), window_strides=(1,),
                                 padding=[(k // 2, k // 2)],
                                 dimension_numbers=('NWC', 'WIO', 'NWC'),
                                 feature_group_count=ch)
    return y + b


def _sincos_2d(rows, cols, d):
    q = d // 4
    omega = 1.0 / (10000.0 ** (jnp.arange(q, dtype=jnp.float32) / q))
    t = jnp.arange(rows * cols)
    er = (t // cols).astype(jnp.float32)[:, None] * omega[None, :]
    ec = (t % cols).astype(jnp.float32)[:, None] * omega[None, :]
    return jnp.concatenate([jnp.sin(er), jnp.cos(er), jnp.sin(ec), jnp.cos(ec)], axis=-1)


def _hyena_filters(length, f_w1, f_b1, f_w2, f_b2, f_w3, f_freq):
    t = jnp.linspace(0.0, 1.0, length, dtype=jnp.float32)[:, None]
    w = 2.0 * math.pi * jnp.arange(length, dtype=jnp.float32)[:, None] / length
    f = jnp.linspace(1e-4, HY_BANDS - 1, HY_BANDS, dtype=jnp.float32)[None, :]
    z = jnp.concatenate([t, jnp.cos(f * w), -jnp.sin(f * w)], axis=-1)
    h = jnp.sin(f_freq * (z @ f_w1 + f_b1))
    h = jnp.sin(f_freq * (h @ f_w2 + f_b2))
    h = (h @ f_w3).astype(jnp.float32)
    deltas = jnp.linspace(HY_MIN_DECAY, HY_MAX_DECAY, D_MODEL, dtype=jnp.float32)
    window = jnp.exp(-t * jnp.abs(deltas)[None, :])
    return h[:, :D_MODEL] * window, h[:, D_MODEL:] * window


def _bidir_long_conv(u, h_fwd, h_bwd, bias):
    b, length, ch = u.shape
    n = 2 * length
    k = jnp.concatenate([h_fwd, jnp.zeros((1, ch), jnp.float32), h_bwd[1:][::-1]], axis=0)
    kf = jnp.fft.rfft(k, n=n, axis=0)
    u32 = u.astype(jnp.float32)
    uf = jnp.fft.rfft(u32, n=n, axis=1)
    y = jnp.fft.irfft(uf * kf[None], n=n, axis=1)[:, :length]
    return (y + u32 * bias.astype(jnp.float32)).astype(u.dtype)


def _hyena_mixer(h, in_w, in_b, short_w, short_b, f_w1, f_b1, f_w2, f_b2, f_w3, f_freq, f_bias, out_w):
    length = h.shape[1]
    u = _dwconv(h @ in_w + in_b, short_w, short_b)
    x0 = u[..., :D_MODEL]
    x1 = u[..., D_MODEL:2 * D_MODEL]
    v = u[..., 2 * D_MODEL:]
    h_fwd, h_bwd = _hyena_filters(length, f_w1, f_b1, f_w2, f_b2, f_w3, f_freq)
    v = _bidir_long_conv(v * x1, h_fwd, h_bwd, f_bias)
    return (v * x0) @ out_w


def _ssd_scan(x, dt, a, bm, cm, init):
    b, length = x.shape[:2]
    nc = length // SSD_CHUNK
    q, g, r = SSD_CHUNK, SSD_GROUPS, SSD_HEADS // SSD_GROUPS
    x = x.reshape(b, nc, q, g, r, SSD_HEAD_DIM)
    dt = dt.reshape(b, nc, q, g, r)
    bm = bm.reshape(b, nc, q, g, SSD_STATE)
    cm = cm.reshape(b, nc, q, g, SSD_STATE)
    a_cum = jnp.cumsum(dt * a.reshape(g, r), axis=2)
    xdt = x * dt[..., None]
    seg = a_cum[:, :, :, None] - a_cum[:, :, None, :]
    causal = jnp.tril(jnp.ones((q, q), dtype=bool))[None, None, :, :, None, None]
    decay = jnp.exp(jnp.where(causal, seg, -jnp.inf))
    cb = jnp.einsum('bclgn,bcsgn->bclsg', cm, bm)
    y_diag = jnp.einsum('bclsgr,bcsgrp->bclgrp', cb[..., None] * decay, xdt)
    decay_end = jnp.exp(a_cum[:, :, -1:] - a_cum)
    chunk_states = jnp.einsum('bcsgn,bcsgrp->bcgrpn', bm, xdt * decay_end[..., None])
    chunk_decay = jnp.exp(a_cum[:, :, -1])

    def step(s, inp):
        st, dec = inp
        return s * dec[..., None, None] + st, s

    init_g = init.astype(jnp.float32).reshape(b, g, r, SSD_HEAD_DIM, SSD_STATE)
    final, starts = lax.scan(step, init_g, (jnp.moveaxis(chunk_states, 1, 0), jnp.moveaxis(chunk_decay, 1, 0)))
    starts = jnp.moveaxis(starts, 0, 1)
    y_off = jnp.einsum('bclgn,bcgrpn->bclgrp', cm, starts) * jnp.exp(a_cum)[..., None]
    y = (y_diag + y_off).reshape(b, length, SSD_HEADS, SSD_HEAD_DIM)
    return y, final.reshape(b, SSD_HEADS, SSD_HEAD_DIM, SSD_STATE)


def _flip(t):
    return jnp.flip(t, axis=1)


def _ssd_mixer(h, init_f, init_b, in_w, conv_w, conv_b, dt_bias, a_log, d_skip, norm_g, out_w):
    b, length, _ = h.shape
    proj = h @ in_w
    z = proj[..., :SSD_D_INNER]
    xbc = jax.nn.silu(_dwconv(proj[..., SSD_D_INNER:SSD_D_INNER + SSD_XBC], conv_w, conv_b))
    dt_raw = proj[..., SSD_D_INNER + SSD_XBC:]
    gn = SSD_GROUPS * SSD_STATE
    xh = xbc[..., :SSD_D_INNER].reshape(b, length, SSD_HEADS, SSD_HEAD_DIM).astype(jnp.float32)
    bm = xbc[..., SSD_D_INNER:SSD_D_INNER + gn].reshape(b, length, SSD_GROUPS, SSD_STATE).astype(jnp.float32)
    cm = xbc[..., SSD_D_INNER + gn:].reshape(b, length, SSD_GROUPS, SSD_STATE).astype(jnp.float32)
    dt = jax.nn.softplus(dt_raw.astype(jnp.float32).reshape(b, length, 2, SSD_HEADS) + dt_bias.astype(jnp.float32))
    a = -jnp.exp(a_log.astype(jnp.float32))
    y_f, s_f = _ssd_scan(xh, dt[:, :, 0], a[0], bm, cm, init_f)
    y_b, s_b = _ssd_scan(_flip(xh), _flip(dt[:, :, 1]), a[1], _flip(bm), _flip(cm), init_b)
    y = y_f + _flip(y_b) + d_skip.astype(jnp.float32)[:, None] * xh
    y = y.reshape(b, length, SSD_D_INNER) * jax.nn.silu(z.astype(jnp.float32))
    y = _rmsnorm(y, norm_g).astype(h.dtype)
    return y @ out_w, s_f, s_b


def _expert_choice_ffn(h, w_router, w_gate, w_up, w_down):
    b, length, _ = h.shape
    cap = EC_FACTOR * length // N_EXPERTS
    aff = jax.nn.softmax((h @ w_router).astype(jnp.float32), axis=-1)
    gate, idx = lax.top_k(jnp.swapaxes(aff, 1, 2), cap)
    bi = jnp.arange(b)[:, None, None]
    xe = h[bi, idx]
    hid = jax.nn.silu(jnp.einsum('becd,edf->becf', xe, w_gate)) * jnp.einsum('becd,edf->becf', xe, w_up)
    ye = jnp.einsum('becf,efd->becd', hid, w_down)
    return jnp.zeros_like(h).at[bi, idx].add((gate[..., None] * ye).astype(h.dtype))


def setup_inputs(seed: int = 0) -> dict:
    key = jax.random.key(seed)
    ks = iter(jax.random.split(key, 48))

    def nrm(shape, scale):
        return jax.random.normal(next(ks), shape, jnp.float32) * scale

    dt0 = jnp.exp(jax.random.uniform(next(ks), (N_SSD, 2, SSD_HEADS), jnp.float32,
                                     math.log(1e-3), math.log(1e-1)))
    return {
        'x_prompt': nrm((BATCH, SEQ, D_MODEL), 1.0),
        'x_sample': nrm((DEC_BATCH, DEC_SEQ, D_MODEL), 1.0),
        'state_ssd': nrm((DEC_BATCH, N_SSD, 2, SSD_HEADS, SSD_HEAD_DIM, SSD_STATE), 0.1),
        'c': nrm((DEC_BATCH, D_MODEL), 1.0),
        'c_ctx': nrm((D_MODEL,), 1.0),
        'norm_g': 1.0 + nrm((DEPTH, 2, D_MODEL), 0.05),
        'ada_w': nrm((DEPTH, D_MODEL, 6 * D_MODEL), 0.5 * D_MODEL ** -0.5),
        'ada_b': nrm((DEPTH, 6 * D_MODEL), 0.01),
        'hy_in_w': nrm((N_HYENA, D_MODEL, 3 * D_MODEL), D_MODEL ** -0.5),
        'hy_in_b': nrm((N_HYENA, 3 * D_MODEL), 0.01),
        'hy_short_w': nrm((N_HYENA, HY_SHORT_W, 3 * D_MODEL), HY_SHORT_W ** -0.5),
        'hy_short_b': nrm((N_HYENA, 3 * D_MODEL), 0.01),
        'hy_f_w1': nrm((N_HYENA, HY_EMB, HY_HIDDEN), HY_EMB ** -0.5),
        'hy_f_b1': nrm((N_HYENA, HY_HIDDEN), 0.1),
        'hy_f_w2': nrm((N_HYENA, HY_HIDDEN, HY_HIDDEN), HY_HIDDEN ** -0.5),
        'hy_f_b2': nrm((N_HYENA, HY_HIDDEN), 0.1),
        'hy_f_w3': nrm((N_HYENA, HY_HIDDEN, 2 * D_MODEL), 0.1 * HY_HIDDEN ** -0.5),
        'hy_f_freq': 1.0 + nrm((N_HYENA, HY_HIDDEN), 0.05),
        'hy_f_bias': nrm((N_HYENA, D_MODEL), 1.0),
        'hy_out_w': nrm((N_HYENA, D_MODEL, D_MODEL), D_MODEL ** -0.5),
        'ssd_in_w': nrm((N_SSD, D_MODEL, SSD_IN), D_MODEL ** -0.5),
        'ssd_conv_w': nrm((N_SSD, SSD_CONV_W, SSD_XBC), SSD_CONV_W ** -0.5),
        'ssd_conv_b': nrm((N_SSD, SSD_XBC), 0.01),
        'ssd_dt_bias': dt0 + jnp.log(-jnp.expm1(-dt0)),
        'ssd_A_log': jnp.log(jax.random.uniform(next(ks), (N_SSD, 2, SSD_HEADS), jnp.float32, 1.0, 16.0)),
        'ssd_D': 1.0 + nrm((N_SSD, SSD_HEADS), 0.05),
        'ssd_norm_g': 1.0 + nrm((N_SSD, SSD_D_INNER), 0.05),
        'ssd_out_w': nrm((N_SSD, SSD_D_INNER, D_MODEL), SSD_D_INNER ** -0.5),
        'moe_router': nrm((DEPTH, D_MODEL, N_EXPERTS), D_MODEL ** -0.5),
        'moe_w_gate': nrm((DEPTH, N_EXPERTS, D_MODEL, EXPERT_FF), D_MODEL ** -0.5),
        'moe_w_up': nrm((DEPTH, N_EXPERTS, D_MODEL, EXPERT_FF), D_MODEL ** -0.5),
        'moe_w_down': nrm((DEPTH, N_EXPERTS, EXPERT_FF, D_MODEL), EXPERT_FF ** -0.5),
        'final_norm_g': 1.0 + nrm((D_MODEL,), 0.05),
    }


def reference(x_prompt, x_sample, state_ssd, c, c_ctx, norm_g, ada_w, ada_b,
              hy_in_w, hy_in_b, hy_short_w, hy_short_b, hy_f_w1, hy_f_b1, hy_f_w2, hy_f_b2,
              hy_f_w3, hy_f_freq, hy_f_bias, hy_out_w,
              ssd_in_w, ssd_conv_w, ssd_conv_b, ssd_dt_bias, ssd_A_log, ssd_D, ssd_norm_g, ssd_out_w,
              moe_router, moe_w_gate, moe_w_up, moe_w_down, final_norm_g):
    rows = x_sample.shape[1] // GRID_W
    xp = x_prompt
    xs = x_sample + _sincos_2d(rows, GRID_W, D_MODEL).astype(x_sample.dtype)[None]
    new_ssd = []
    for i in range(DEPTH):
        sh1p, sc1p, g1p, sh2p, sc2p, g2p = _adaln(c_ctx[None, :], ada_w[i], ada_b[i])
        sh1s, sc1s, g1s, sh2s, sc2s, g2s = _adaln(c, ada_w[i], ada_b[i])
        hp = _modulate(_rmsnorm(xp, norm_g[i, 0]), sh1p, sc1p)
        hs = _modulate(_rmsnorm(xs, norm_g[i, 0]), sh1s, sc1s)
        j = i // N_MIXERS
        if i % N_MIXERS == 0:
            hy = (hy_in_w[j], hy_in_b[j], hy_short_w[j], hy_short_b[j], hy_f_w1[j], hy_f_b1[j],
                  hy_f_w2[j], hy_f_b2[j], hy_f_w3[j], hy_f_freq[j], hy_f_bias[j], hy_out_w[j])
            mp = _hyena_mixer(hp, *hy)
            ms = _hyena_mixer(hs, *hy)
        else:
            sp = (ssd_in_w[j], ssd_conv_w[j], ssd_conv_b[j], ssd_dt_bias[j], ssd_A_log[j],
                  ssd_D[j], ssd_norm_g[j], ssd_out_w[j])
            zeros = jnp.zeros((xp.shape[0], SSD_HEADS, SSD_HEAD_DIM, SSD_STATE), jnp.float32)
            mp, s_f, s_b = _ssd_mixer(hp, zeros, zeros, *sp)
            new_ssd.append(jnp.stack([s_f, s_b], axis=1))
            ms, _, _ = _ssd_mixer(hs, state_ssd[:, j, 0], state_ssd[:, j, 1], *sp)
        xp = xp + g1p * mp
        xs = xs + g1s * ms
        moe = (moe_router[i], moe_w_gate[i], moe_w_up[i], moe_w_down[i])
        xp = xp + g2p * _expert_choice_ffn(_modulate(_rmsnorm(xp, norm_g[i, 1]), sh2p, sc2p), *moe)
        xs = xs + g2s * _expert_choice_ffn(_modulate(_rmsnorm(xs, norm_g[i, 1]), sh2s, sc2s), *moe)
    y_prompt = _rmsnorm(xp, final_norm_g)
    y_sample = _rmsnorm(xs, final_norm_g)
    new_state_ssd = jnp.stack(new_ssd, axis=1)
    return (y_prompt, y_sample, new_state_ssd)
```

```python
import functools
import math

import jax
import jax.numpy as jnp
from jax import lax
from jax.experimental import pallas as pl
from jax.experimental.pallas import tpu as pltpu

D_MODEL = 1024
DEPTH = 2
GRID_W = 64
N_MIXERS = 2
RMS_EPS = 1e-6
HY_EMB = 33
HY_BANDS = (HY_EMB - 1) // 2
HY_SHORT_DECAY_FRAC = 0.3
HY_LONG_DECAY_FRAC = 1.5
HY_DECAY_TARGET = 1e-2
HY_MAX_DECAY = math.log(HY_DECAY_TARGET) / HY_SHORT_DECAY_FRAC
HY_MIN_DECAY = math.log(HY_DECAY_TARGET) / HY_LONG_DECAY_FRAC
SSD_D_INNER = 2 * D_MODEL
SSD_HEAD_DIM = 64
SSD_HEADS = SSD_D_INNER // SSD_HEAD_DIM
SSD_GROUPS = 4
SSD_STATE = 128
SSD_CHUNK = 128
SSD_XBC = SSD_D_INNER + 2 * SSD_GROUPS * SSD_STATE
N_EXPERTS = 16
EC_FACTOR = 2

VMEM_LIMIT_BYTES = 48 * 1024 * 1024


def _mm_kernel(a_ref, b_ref, o_ref, acc_ref):
    @pl.when(pl.program_id(2) == 0)
    def _():
        acc_ref[...] = jnp.zeros_like(acc_ref)

    acc_ref[...] += jnp.dot(a_ref[...].astype(jnp.bfloat16), b_ref[...],
                            preferred_element_type=jnp.float32)

    @pl.when(pl.program_id(2) == pl.num_programs(2) - 1)
    def _():
        o_ref[...] = acc_ref[...]


def _pick(n, pref):
    for t in pref:
        if n % t == 0:
            return t
    return n


def _mm(a, b):
    m, k = a.shape
    n = b.shape[1]
    mp = -(-m // 8) * 8
    if mp != m:
        a = jnp.pad(a, ((0, mp - m), (0, 0)))
    tm = _pick(mp, (512, 256, 128, 64, 32, 16, 8))
    tn = _pick(n, (512, 256, 128))
    tk = _pick(k, (1024, 512, 256, 128))
    out = pl.pallas_call(
        _mm_kernel,
        grid=(mp // tm, n // tn, k // tk),
        in_specs=[pl.BlockSpec((tm, tk), lambda i, j, l: (i, l)),
                  pl.BlockSpec((tk, tn), lambda i, j, l: (l, j))],
        out_specs=pl.BlockSpec((tm, tn), lambda i, j, l: (i, j)),
        out_shape=jax.ShapeDtypeStruct((mp, n), jnp.float32),
        scratch_shapes=[pltpu.VMEM((tm, tn), jnp.float32)],
        compiler_params=pltpu.CompilerParams(
            dimension_semantics=("parallel", "parallel", "arbitrary"),
            vmem_limit_bytes=VMEM_LIMIT_BYTES),
        name="mm",
    )(a, b.astype(jnp.bfloat16))
    return out[:m]


def _mm3(a, b):
    lead = a.shape[:-1]
    return _mm(a.reshape(-1, a.shape[-1]), b).reshape(*lead, b.shape[1])


def _bmm_kernel(a_ref, b_ref, o_ref):
    o_ref[0] = jnp.dot(a_ref[0].astype(jnp.bfloat16), b_ref[0],
                       preferred_element_type=jnp.float32)


def _bmm(a, b):
    e, m, k = a.shape
    n = b.shape[2]
    tm = _pick(m, (512, 256, 128, 64, 32, 16, 8))
    tn = _pick(n, (512, 256, 128))
    return pl.pallas_call(
        _bmm_kernel,
        grid=(e, m // tm, n // tn),
        in_specs=[pl.BlockSpec((1, tm, k), lambda g, i, j: (g, i, 0)),
                  pl.BlockSpec((1, k, tn), lambda g, i, j: (g, 0, j))],
        out_specs=pl.BlockSpec((1, tm, tn), lambda g, i, j: (g, i, j)),
        out_shape=jax.ShapeDtypeStruct((e, m, n), jnp.float32),
        compiler_params=pltpu.CompilerParams(
            dimension_semantics=("parallel", "parallel", "parallel"),
            vmem_limit_bytes=VMEM_LIMIT_BYTES),
        name="bmm",
    )(a, b.astype(jnp.bfloat16))


def _rmsnorm(x, g):
    y = x * lax.rsqrt(jnp.mean(x * x, axis=-1, keepdims=True) + RMS_EPS)
    return y * g


def _adaln(cond, ada_w, ada_b):
    m = jnp.dot(jax.nn.silu(cond), ada_w, precision=lax.Precision.HIGHEST) + ada_b
    return jnp.split(m[:, None, :], 6, axis=-1)


def _modulate(h, shift, scale):
    return h * (1 + scale) + shift


def _dwconv(x, w, b):
    k, ch = w.shape
    y = lax.conv_general_dilated(x, w[:, None, :], window_strides=(1,),
                                 padding=[(k // 2, k // 2)],
                                 dimension_numbers=('NWC', 'WIO', 'NWC'),
                                 feature_group_count=ch,
                                 precision=lax.Precision.HIGHEST)
    return y + b


def _sincos_2d(rows, cols, d):
    q = d // 4
    omega = 1.0 / (10000.0 ** (jnp.arange(q, dtype=jnp.float32) / q))
    t = jnp.arange(rows * cols)
    er = (t // cols).astype(jnp.float32)[:, None] * omega[None, :]
    ec = (t % cols).astype(jnp.float32)[:, None] * omega[None, :]
    return jnp.concatenate([jnp.sin(er), jnp.cos(er), jnp.sin(ec), jnp.cos(ec)], axis=-1)


def _hyena_filters(length, f_w1, f_b1, f_w2, f_b2, f_w3, f_freq):
    hp = lax.Precision.HIGHEST
    t = jnp.linspace(0.0, 1.0, length, dtype=jnp.float32)[:, None]
    w = 2.0 * math.pi * jnp.arange(length, dtype=jnp.float32)[:, None] / length
    f = jnp.linspace(1e-4, HY_BANDS - 1, HY_BANDS, dtype=jnp.float32)[None, :]
    z = jnp.concatenate([t, jnp.cos(f * w), -jnp.sin(f * w)], axis=-1)
    h = jnp.sin(f_freq * (jnp.dot(z, f_w1, precision=hp) + f_b1))
    h = jnp.sin(f_freq * (jnp.dot(h, f_w2, precision=hp) + f_b2))
    h = jnp.dot(h, f_w3, precision=hp)
    deltas = jnp.linspace(HY_MIN_DECAY, HY_MAX_DECAY, D_MODEL, dtype=jnp.float32)
    window = jnp.exp(-t * jnp.abs(deltas)[None, :])
    return h[:, :D_MODEL] * window, h[:, D_MODEL:] * window


def _bidir_long_conv(u, h_fwd, h_bwd, bias):
    b, length, ch = u.shape
    n = 2 * length
    k = jnp.concatenate([h_fwd, jnp.zeros((1, ch), jnp.float32), h_bwd[1:][::-1]], axis=0)
    kf = jnp.fft.rfft(k, n=n, axis=0)
    uf = jnp.fft.rfft(u, n=n, axis=1)
    y = jnp.fft.irfft(uf * kf[None], n=n, axis=1)[:, :length]
    return y + u * bias


def _hyena_mixer(h, in_w, in_b, short_w, short_b, f_w1, f_b1, f_w2, f_b2, f_w3, f_freq, f_bias, out_w):
    length = h.shape[1]
    u = _dwconv(_mm3(h, in_w) + in_b, short_w, short_b)
    x0 = u[..., :D_MODEL]
    x1 = u[..., D_MODEL:2 * D_MODEL]
    v = u[..., 2 * D_MODEL:]
    h_fwd, h_bwd = _hyena_filters(length, f_w1, f_b1, f_w2, f_b2, f_w3, f_freq)
    v = _bidir_long_conv(v * x1, h_fwd, h_bwd, f_bias)
    return _mm3(v * x0, out_w)


def _ssd_scan(x, dt, a, bm, cm, init):
    hp = lax.Precision.HIGHEST
    b, length = x.shape[:2]
    nc = length // SSD_CHUNK
    q, g, r = SSD_CHUNK, SSD_GROUPS, SSD_HEADS // SSD_GROUPS
    x = x.reshape(b, nc, q, g, r, SSD_HEAD_DIM)
    dt = dt.reshape(b, nc, q, g, r)
    bm = bm.reshape(b, nc, q, g, SSD_STATE)
    cm = cm.reshape(b, nc, q, g, SSD_STATE)
    a_cum = jnp.cumsum(dt * a.reshape(g, r), axis=2)
    xdt = x * dt[..., None]
    seg = a_cum[:, :, :, None] - a_cum[:, :, None, :]
    causal = jnp.tril(jnp.ones((q, q), dtype=bool))[None, None, :, :, None, None]
    decay = jnp.exp(jnp.where(causal, seg, -jnp.inf))
    cb = jnp.einsum('bclgn,bcsgn->bclsg', cm, bm, precision=hp)
    y_diag = jnp.einsum('bclsgr,bcsgrp->bclgrp', cb[..., None] * decay, xdt, precision=hp)
    decay_end = jnp.exp(a_cum[:, :, -1:] - a_cum)
    chunk_states = jnp.einsum('bcsgn,bcsgrp->bcgrpn', bm, xdt * decay_end[..., None], precision=hp)
    chunk_decay = jnp.exp(a_cum[:, :, -1])

    def step(s, inp):
        st, dec = inp
        return s * dec[..., None, None] + st, s

    init_g = init.astype(jnp.float32).reshape(b, g, r, SSD_HEAD_DIM, SSD_STATE)
    final, starts = lax.scan(step, init_g, (jnp.moveaxis(chunk_states, 1, 0), jnp.moveaxis(chunk_decay, 1, 0)))
    starts = jnp.moveaxis(starts, 0, 1)
    y_off = jnp.einsum('bclgn,bcgrpn->bclgrp', cm, starts, precision=hp) * jnp.exp(a_cum)[..., None]
    y = (y_diag + y_off).reshape(b, length, SSD_HEADS, SSD_HEAD_DIM)
    return y, final.reshape(b, SSD_HEADS, SSD_HEAD_DIM, SSD_STATE)


def _flip(t):
    return jnp.flip(t, axis=1)


def _ssd_mixer(h, init_f, init_b, in_w, conv_w, conv_b, dt_bias, a_log, d_skip, norm_g, out_w):
    b, length, _ = h.shape
    proj = _mm3(h, in_w)
    z = proj[..., :SSD_D_INNER]
    xbc = jax.nn.silu(_dwconv(proj[..., SSD_D_INNER:SSD_D_INNER + SSD_XBC], conv_w, conv_b))
    dt_raw = proj[..., SSD_D_INNER + SSD_XBC:]
    gn = SSD_GROUPS * SSD_STATE
    xh = xbc[..., :SSD_D_INNER].reshape(b, length, SSD_HEADS, SSD_HEAD_DIM)
    bm = xbc[..., SSD_D_INNER:SSD_D_INNER + gn].reshape(b, length, SSD_GROUPS, SSD_STATE)
    cm = xbc[..., SSD_D_INNER + gn:].reshape(b, length, SSD_GROUPS, SSD_STATE)
    dt = jax.nn.softplus(dt_raw.reshape(b, length, 2, SSD_HEADS) + dt_bias)
    a = -jnp.exp(a_log)
    y_f, s_f = _ssd_scan(xh, dt[:, :, 0], a[0], bm, cm, init_f)
    y_b, s_b = _ssd_scan(_flip(xh), _flip(dt[:, :, 1]), a[1], _flip(bm), _flip(cm), init_b)
    y = y_f + _flip(y_b) + d_skip[:, None] * xh
    y = y.reshape(b, length, SSD_D_INNER) * jax.nn.silu(z)
    y = _rmsnorm(y, norm_g)
    return _mm3(y, out_w), s_f, s_b


def _expert_choice_ffn(h, w_router, w_gate, w_up, w_down):
    b, length, d = h.shape
    cap = EC_FACTOR * length // N_EXPERTS
    logits = jnp.einsum('bld,de->ble', h, w_router, precision=lax.Precision.HIGHEST)
    aff = jax.nn.softmax(logits, axis=-1)
    gate, idx = lax.top_k(jnp.swapaxes(aff, 1, 2), cap)
    bi = jnp.arange(b)[:, None, None]
    xe = h[bi, idx]
    xe = jnp.swapaxes(xe, 0, 1).reshape(N_EXPERTS, b * cap, d)
    hid = jax.nn.silu(_bmm(xe, w_gate)) * _bmm(xe, w_up)
    ye = _bmm(hid, w_down).reshape(N_EXPERTS, b, cap, d)
    ye = jnp.swapaxes(ye, 0, 1)
    return jnp.zeros_like(h).at[bi, idx].add(gate[..., None] * ye)


def kernel(x_prompt, x_sample, state_ssd, c, c_ctx, norm_g, ada_w, ada_b, hy_in_w, hy_in_b, hy_short_w, hy_short_b, hy_f_w1, hy_f_b1, hy_f_w2, hy_f_b2, hy_f_w3, hy_f_freq, hy_f_bias, hy_out_w, ssd_in_w, ssd_conv_w, ssd_conv_b, ssd_dt_bias, ssd_A_log, ssd_D, ssd_norm_g, ssd_out_w, moe_router, moe_w_gate, moe_w_up, moe_w_down, final_norm_g):
    rows = x_sample.shape[1] // GRID_W
    xp = x_prompt
    xs = x_sample + _sincos_2d(rows, GRID_W, D_MODEL)[None]
    new_ssd = []
    for i in range(DEPTH):
        sh1p, sc1p, g1p, sh2p, sc2p, g2p = _adaln(c_ctx[None, :], ada_w[i], ada_b[i])
        sh1s, sc1s, g1s, sh2s, sc2s, g2s = _adaln(c, ada_w[i], ada_b[i])
        hp = _modulate(_rmsnorm(xp, norm_g[i, 0]), sh1p, sc1p)
        hs = _modulate(_rmsnorm(xs, norm_g[i, 0]), sh1s, sc1s)
        j = i // N_MIXERS
        if i % N_MIXERS == 0:
            hy = (hy_in_w[j], hy_in_b[j], hy_short_w[j], hy_short_b[j], hy_f_w1[j], hy_f_b1[j],
                  hy_f_w2[j], hy_f_b2[j], hy_f_w3[j], hy_f_freq[j], hy_f_bias[j], hy_out_w[j])
            mp = _hyena_mixer(hp, *hy)
            ms = _hyena_mixer(hs, *hy)
        else:
            sp = (ssd_in_w[j], ssd_conv_w[j], ssd_conv_b[j], ssd_dt_bias[j], ssd_A_log[j],
                  ssd_D[j], ssd_norm_g[j], ssd_out_w[j])
            zeros = jnp.zeros((xp.shape[0], SSD_HEADS, SSD_HEAD_DIM, SSD_STATE), jnp.float32)
            mp, s_f, s_b = _ssd_mixer(hp, zeros, zeros, *sp)
            new_ssd.append(jnp.stack([s_f, s_b], axis=1))
            ms, _, _ = _ssd_mixer(hs, state_ssd[:, j, 0], state_ssd[:, j, 1], *sp)
        xp = xp + g1p * mp
        xs = xs + g1s * ms
        moe = (moe_router[i], moe_w_gate[i], moe_w_up[i], moe_w_down[i])
        xp = xp + g2p * _expert_choice_ffn(_modulate(_rmsnorm(xp, norm_g[i, 1]), sh2p, sc2p), *moe)
        xs = xs + g2s * _expert_choice_ffn(_modulate(_rmsnorm(xs, norm_g[i, 1]), sh2s, sc2s), *moe)
    y_prompt = _rmsnorm(xp, final_norm_g)
    y_sample = _rmsnorm(xs, final_norm_g)
    new_state_ssd = jnp.stack(new_ssd, axis=1)
    return (y_prompt, y_sample, new_state_ssd)
```

```python
import functools
import math

import jax
import jax.numpy as jnp
from jax import lax
from jax.experimental import pallas as pl
from jax.experimental.pallas import tpu as pltpu

D_MODEL = 1024
DEPTH = 2
GRID_W = 64
N_MIXERS = 2
RMS_EPS = 1e-6
HY_EMB = 33
HY_BANDS = (HY_EMB - 1) // 2
HY_SHORT_DECAY_FRAC = 0.3
HY_LONG_DECAY_FRAC = 1.5
HY_DECAY_TARGET = 1e-2
HY_MAX_DECAY = math.log(HY_DECAY_TARGET) / HY_SHORT_DECAY_FRAC
HY_MIN_DECAY = math.log(HY_DECAY_TARGET) / HY_LONG_DECAY_FRAC
SSD_D_INNER = 2 * D_MODEL
SSD_HEAD_DIM = 64
SSD_HEADS = SSD_D_INNER // SSD_HEAD_DIM
SSD_GROUPS = 4
SSD_STATE = 128
SSD_CHUNK = 128
SSD_XBC = SSD_D_INNER + 2 * SSD_GROUPS * SSD_STATE
N_EXPERTS = 16
EC_FACTOR = 2

VMEM_LIMIT_BYTES = 48 * 1024 * 1024


def _mm_kernel(a_ref, b_ref, o_ref, acc_ref):
    @pl.when(pl.program_id(2) == 0)
    def _():
        acc_ref[...] = jnp.zeros_like(acc_ref)

    acc_ref[...] += jnp.dot(a_ref[...].astype(jnp.bfloat16), b_ref[...],
                            preferred_element_type=jnp.float32)

    @pl.when(pl.program_id(2) == pl.num_programs(2) - 1)
    def _():
        o_ref[...] = acc_ref[...]


def _pick(n, pref):
    for t in pref:
        if n % t == 0:
            return t
    return n


def _mm(a, b):
    m, k = a.shape
    n = b.shape[1]
    mp = -(-m // 8) * 8
    if mp != m:
        a = jnp.pad(a, ((0, mp - m), (0, 0)))
    tm = _pick(mp, (512, 256, 128, 64, 32, 16, 8))
    tn = _pick(n, (512, 256, 128))
    tk = _pick(k, (1024, 512, 256, 128))
    out = pl.pallas_call(
        _mm_kernel,
        grid=(mp // tm, n // tn, k // tk),
        in_specs=[pl.BlockSpec((tm, tk), lambda i, j, l: (i, l)),
                  pl.BlockSpec((tk, tn), lambda i, j, l: (l, j))],
        out_specs=pl.BlockSpec((tm, tn), lambda i, j, l: (i, j)),
        out_shape=jax.ShapeDtypeStruct((mp, n), jnp.float32),
        scratch_shapes=[pltpu.VMEM((tm, tn), jnp.float32)],
        compiler_params=pltpu.CompilerParams(
            dimension_semantics=("parallel", "parallel", "arbitrary"),
            vmem_limit_bytes=VMEM_LIMIT_BYTES),
        name="mm",
    )(a, b.astype(jnp.bfloat16))
    return out[:m]


def _mm3(a, b):
    lead = a.shape[:-1]
    return _mm(a.reshape(-1, a.shape[-1]), b).reshape(*lead, b.shape[1])


def _bmm_kernel(a_ref, b_ref, o_ref):
    o_ref[0] = jnp.dot(a_ref[0].astype(jnp.bfloat16), b_ref[0],
                       preferred_element_type=jnp.float32)


def _bmm(a, b):
    e, m, k = a.shape
    n = b.shape[2]
    tm = _pick(m, (512, 256, 128, 64, 32, 16, 8))
    tn = _pick(n, (512, 256, 128))
    return pl.pallas_call(
        _bmm_kernel,
        grid=(e, m // tm, n // tn),
        in_specs=[pl.BlockSpec((1, tm, k), lambda g, i, j: (g, i, 0)),
                  pl.BlockSpec((1, k, tn), lambda g, i, j: (g, 0, j))],
        out_specs=pl.BlockSpec((1, tm, tn), lambda g, i, j: (g, i, j)),
        out_shape=jax.ShapeDtypeStruct((e, m, n), jnp.float32),
        compiler_params=pltpu.CompilerParams(
            dimension_semantics=("parallel", "parallel", "parallel"),
            vmem_limit_bytes=VMEM_LIMIT_BYTES),
        name="bmm",
    )(a, b.astype(jnp.bfloat16))


def _rmsnorm(x, g):
    y = x * lax.rsqrt(jnp.mean(x * x, axis=-1, keepdims=True) + RMS_EPS)
    return y * g


def _adaln(cond, ada_w, ada_b):
    m = jnp.dot(jax.nn.silu(cond), ada_w, precision=lax.Precision.HIGHEST) + ada_b
    return jnp.split(m[:, None, :], 6, axis=-1)


def _modulate(h, shift, scale):
    return h * (1 + scale) + shift


def _dwconv(x, w, b):
    k, ch = w.shape
    y = lax.conv_general_dilated(x, w[:, None, :], window_strides=(1,),
                                 padding=[(k // 2, k // 2)],
                                 dimension_numbers=('NWC', 'WIO', 'NWC'),
                                 feature_group_count=ch,
                                 precision=lax.Precision.HIGHEST)
    return y + b


def _sincos_2d(rows, cols, d):
    q = d // 4
    omega = 1.0 / (10000.0 ** (jnp.arange(q, dtype=jnp.float32) / q))
    t = jnp.arange(rows * cols)
    er = (t // cols).astype(jnp.float32)[:, None] * omega[None, :]
    ec = (t % cols).astype(jnp.float32)[:, None] * omega[None, :]
    return jnp.concatenate([jnp.sin(er), jnp.cos(er), jnp.sin(ec), jnp.cos(ec)], axis=-1)


def _hyena_filters(length, f_w1, f_b1, f_w2, f_b2, f_w3, f_freq):
    hp = lax.Precision.HIGHEST
    t = jnp.linspace(0.0, 1.0, length, dtype=jnp.float32)[:, None]
    w = 2.0 * math.pi * jnp.arange(length, dtype=jnp.float32)[:, None] / length
    f = jnp.linspace(1e-4, HY_BANDS - 1, HY_BANDS, dtype=jnp.float32)[None, :]
    z = jnp.concatenate([t, jnp.cos(f * w), -jnp.sin(f * w)], axis=-1)
    h = jnp.sin(f_freq * (jnp.dot(z, f_w1, precision=hp) + f_b1))
    h = jnp.sin(f_freq * (jnp.dot(h, f_w2, precision=hp) + f_b2))
    h = jnp.dot(h, f_w3, precision=hp)
    deltas = jnp.linspace(HY_MIN_DECAY, HY_MAX_DECAY, D_MODEL, dtype=jnp.float32)
    window = jnp.exp(-t * jnp.abs(deltas)[None, :])
    return h[:, :D_MODEL] * window, h[:, D_MODEL:] * window


def _bidir_long_conv(u, h_fwd, h_bwd, bias):
    b, length, ch = u.shape
    n = 2 * length
    k = jnp.concatenate([h_fwd, jnp.zeros((1, ch), jnp.float32), h_bwd[1:][::-1]], axis=0)
    kf = jnp.fft.rfft(k, n=n, axis=0)
    uf = jnp.fft.rfft(u, n=n, axis=1)
    y = jnp.fft.irfft(uf * kf[None], n=n, axis=1)[:, :length]
    return y + u * bias


def _hyena_mixer(h, in_w, in_b, short_w, short_b, f_w1, f_b1, f_w2, f_b2, f_w3, f_freq, f_bias, out_w):
    length = h.shape[1]
    u = _dwconv(_mm3(h, in_w) + in_b, short_w, short_b)
    x0 = u[..., :D_MODEL]
    x1 = u[..., D_MODEL:2 * D_MODEL]
    v = u[..., 2 * D_MODEL:]
    h_fwd, h_bwd = _hyena_filters(length, f_w1, f_b1, f_w2, f_b2, f_w3, f_freq)
    v = _bidir_long_conv(v * x1, h_fwd, h_bwd, f_bias)
    return _mm3(v * x0, out_w)


def _ssd_scan(x, dt, a, bm, cm, init):
    hp = lax.Precision.HIGHEST
    b, length = x.shape[:2]
    nc = length // SSD_CHUNK
    q, g, r = SSD_CHUNK, SSD_GROUPS, SSD_HEADS // SSD_GROUPS
    x = x.reshape(b, nc, q, g, r, SSD_HEAD_DIM)
    dt = dt.reshape(b, nc, q, g, r)
    bm = bm.reshape(b, nc, q, g, SSD_STATE)
    cm = cm.reshape(b, nc, q, g, SSD_STATE)
    a_cum = jnp.cumsum(dt * a.reshape(g, r), axis=2)
    xdt = x * dt[..., None]
    seg = a_cum[:, :, :, None] - a_cum[:, :, None, :]
    causal = jnp.tril(jnp.ones((q, q), dtype=bool))[None, None, :, :, None, None]
    decay = jnp.exp(jnp.where(causal, seg, -jnp.inf))
    cb = jnp.einsum('bclgn,bcsgn->bclsg', cm, bm, precision=hp)
    y_diag = jnp.einsum('bclsgr,bcsgrp->bclgrp', cb[..., None] * decay, xdt, precision=hp)
    decay_end = jnp.exp(a_cum[:, :, -1:] - a_cum)
    chunk_states = jnp.einsum('bcsgn,bcsgrp->bcgrpn', bm, xdt * decay_end[..., None], precision=hp)
    chunk_decay = jnp.exp(a_cum[:, :, -1])

    def step(s, inp):
        st, dec = inp
        return s * dec[..., None, None] + st, s

    init_g = init.astype(jnp.float32).reshape(b, g, r, SSD_HEAD_DIM, SSD_STATE)
    final, starts = lax.scan(step, init_g, (jnp.moveaxis(chunk_states, 1, 0), jnp.moveaxis(chunk_decay, 1, 0)))
    starts = jnp.moveaxis(starts, 0, 1)
    y_off = jnp.einsum('bclgn,bcgrpn->bclgrp', cm, starts, precision=hp) * jnp.exp(a_cum)[..., None]
    y = (y_diag + y_off).reshape(b, length, SSD_HEADS, SSD_HEAD_DIM)
    return y, final.reshape(b, SSD_HEADS, SSD_HEAD_DIM, SSD_STATE)


def _flip(t):
    return jnp.flip(t, axis=1)


def _ssd_mixer(h, init_f, init_b, in_w, conv_w, conv_b, dt_bias, a_log, d_skip, norm_g, out_w):
    b, length, _ = h.shape
    proj = _mm3(h, in_w)
    z = proj[..., :SSD_D_INNER]
    xbc = jax.nn.silu(_dwconv(proj[..., SSD_D_INNER:SSD_D_INNER + SSD_XBC], conv_w, conv_b))
    dt_raw = proj[..., SSD_D_INNER + SSD_XBC:]
    gn = SSD_GROUPS * SSD_STATE
    xh = xbc[..., :SSD_D_INNER].reshape(b, length, SSD_HEADS, SSD_HEAD_DIM)
    bm = xbc[..., SSD_D_INNER:SSD_D_INNER + gn].reshape(b, length, SSD_GROUPS, SSD_STATE)
    cm = xbc[..., SSD_D_INNER + gn:].reshape(b, length, SSD_GROUPS, SSD_STATE)
    dt = jax.nn.softplus(dt_raw.reshape(b, length, 2, SSD_HEADS) + dt_bias)
    a = -jnp.exp(a_log)
    y_f, s_f = _ssd_scan(xh, dt[:, :, 0], a[0], bm, cm, init_f)
    y_b, s_b = _ssd_scan(_flip(xh), _flip(dt[:, :, 1]), a[1], _flip(bm), _flip(cm), init_b)
    y = y_f + _flip(y_b) + d_skip[:, None] * xh
    y = y.reshape(b, length, SSD_D_INNER) * jax.nn.silu(z)
    y = _rmsnorm(y, norm_g)
    return _mm3(y, out_w), s_f, s_b


TOK_TILE = 256
MOE_ROWS = 512
MOE_HALVES = 2
SEG_CHUNK = 64
BF16_TILE_ROWS = 16
LANES = 128


def _split_bf16(w):
    hi = w.astype(jnp.bfloat16)
    lo = (w - hi.astype(jnp.float32)).astype(jnp.bfloat16)
    return hi, lo


def _moe_pre_kernel(x_ref, m_ref, g1_ref, ng_ref, sh_ref, sc_ref, wrh_ref, wrl_ref,
                    xo_ref, hpk_ref, lg_ref):
    x = x_ref[0] + g1_ref[0] * m_ref[0]
    xo_ref[0] = x
    ms = jnp.mean(x * x, axis=-1, keepdims=True)
    h = x * lax.rsqrt(ms + RMS_EPS) * ng_ref[...]
    h = h * (1.0 + sc_ref[0]) + sh_ref[0]
    h_hi = h.astype(jnp.bfloat16)
    h_lo = (h - h_hi.astype(jnp.float32)).astype(jnp.bfloat16)
    dn = (((1,), (1,)), ((), ()))
    lg = lax.dot_general(wrh_ref[...], h_hi, dn, preferred_element_type=jnp.float32)
    lg += lax.dot_general(wrh_ref[...], h_lo, dn, preferred_element_type=jnp.float32)
    lg += lax.dot_general(wrl_ref[...], h_hi, dn, preferred_element_type=jnp.float32)
    lg_ref[0] = lg
    half = h.shape[1] // 2
    wa = pltpu.bitcast(h_hi[:, :half].astype(jnp.float32), jnp.uint32) >> 16
    wb = pltpu.bitcast(h_hi[:, half:].astype(jnp.float32), jnp.uint32) & jnp.uint32(0xFFFF0000)
    hpk_ref[0] = wa | wb


def _moe_pre(x, m, g1, ng, sh, sc, w_router):
    b, length, d = x.shape
    tm = min(length, 512)
    wrh, wrl = _split_bf16(w_router.T)
    row = lambda i, j: (i, j, 0)
    per_b = lambda i, j: (i, 0, 0)
    full2 = lambda i, j: (0, 0)
    return pl.pallas_call(
        _moe_pre_kernel,
        grid=(b, length // tm),
        in_specs=[pl.BlockSpec((1, tm, d), row), pl.BlockSpec((1, tm, d), row),
                  pl.BlockSpec((1, 1, d), per_b), pl.BlockSpec((1, d), full2),
                  pl.BlockSpec((1, 1, d), per_b), pl.BlockSpec((1, 1, d), per_b),
                  pl.BlockSpec((N_EXPERTS, d), full2), pl.BlockSpec((N_EXPERTS, d), full2)],
        out_specs=[pl.BlockSpec((1, tm, d), row), pl.BlockSpec((1, tm, d // 2), row),
                   pl.BlockSpec((1, N_EXPERTS, tm), lambda i, j: (i, 0, j))],
        out_shape=[jax.ShapeDtypeStruct((b, length, d), jnp.float32),
                   jax.ShapeDtypeStruct((b, length, d // 2), jnp.uint32),
                   jax.ShapeDtypeStruct((b, N_EXPERTS, length), jnp.float32)],
        compiler_params=pltpu.CompilerParams(
            dimension_semantics=("parallel", "parallel"), vmem_limit_bytes=VMEM_LIMIT_BYTES),
        name="moe_pre",
    )(x, m, g1, ng.reshape(1, d), sh, sc, wrh, wrl)


def _moe_ffn_kernel(idx_ref, h_hbm, gate_ref, wg_ref, wu_ref, wd_ref, y_ref,
                    xe_ref, wgb, wub, wdb, sem):
    @pl.when(pl.program_id(1) == 0)
    def _():
        wgb[...] = wg_ref[0].astype(jnp.bfloat16)
        wub[...] = wu_ref[0].astype(jnp.bfloat16)
        wdb[...] = wd_ref[0].astype(jnp.bfloat16)

    rows = MOE_ROWS // MOE_HALVES
    for hf in range(MOE_HALVES):
        def issue(c, carry, hf=hf):
            r = idx_ref[0, 0, hf * rows + c]
            pltpu.make_async_copy(h_hbm.at[pl.ds(r, 1)], xe_ref.at[pl.ds(hf * rows + c, 1)],
                                  sem.at[hf]).start()
            return carry
        lax.fori_loop(0, rows, issue, 0, unroll=8)

    half = wgb.shape[0] // 2
    for hf in range(MOE_HALVES):
        sl = pl.ds(hf * rows, rows)
        pltpu.make_async_copy(h_hbm.at[pl.ds(0, rows)], xe_ref.at[sl], sem.at[hf]).wait()
        w = xe_ref[sl, :]
        xa = pltpu.bitcast(w << 16, jnp.float32).astype(jnp.bfloat16)
        xb = pltpu.bitcast(w & jnp.uint32(0xFFFF0000), jnp.float32).astype(jnp.bfloat16)
        hg = jnp.dot(xa, wgb[:half], preferred_element_type=jnp.float32)
        hg += jnp.dot(xb, wgb[half:], preferred_element_type=jnp.float32)
        hu = jnp.dot(xa, wub[:half], preferred_element_type=jnp.float32)
        hu += jnp.dot(xb, wub[half:], preferred_element_type=jnp.float32)
        hid = (hg * jax.nn.sigmoid(hg) * hu).astype(jnp.bfloat16)
        y = jnp.dot(hid, wdb[...], preferred_element_type=jnp.float32)
        y_ref[0, sl, :] = (y * gate_ref[0, sl, :]).astype(jnp.bfloat16)


def _moe_ffn(hpk, grow, gate, w_gate, w_up, w_down):
    e, r = grow.shape
    d, f = w_gate.shape[1], w_gate.shape[2]
    nblk = r // MOE_ROWS
    wspec = lambda shp: pl.BlockSpec((1,) + shp, lambda i, j: (i, 0, 0))
    return pl.pallas_call(
        _moe_ffn_kernel,
        grid=(e, nblk),
        in_specs=[pl.BlockSpec((1, 1, MOE_ROWS), lambda i, j: (i * nblk + j, 0, 0), memory_space=pltpu.SMEM),
                  pl.BlockSpec(memory_space=pltpu.HBM),
                  pl.BlockSpec((1, MOE_ROWS, 1), lambda i, j: (i, j, 0)),
                  wspec((d, f)), wspec((d, f)), wspec((f, d))],
        out_specs=pl.BlockSpec((1, MOE_ROWS, d), lambda i, j: (i, j, 0)),
        out_shape=jax.ShapeDtypeStruct((e, r, d), jnp.bfloat16),
        scratch_shapes=[pltpu.VMEM((MOE_ROWS, d // 2), jnp.uint32),
                        pltpu.VMEM((d, f), jnp.bfloat16), pltpu.VMEM((d, f), jnp.bfloat16),
                        pltpu.VMEM((f, d), jnp.bfloat16),
                        pltpu.SemaphoreType.DMA((MOE_HALVES,))],
        compiler_params=pltpu.CompilerParams(
            dimension_semantics=("arbitrary", "arbitrary"), vmem_limit_bytes=VMEM_LIMIT_BYTES),
        name="moe_ffn",
    )(grow.reshape(e * nblk, 1, MOE_ROWS), hpk, gate, w_gate, w_up, w_down)


def _moe_comb_kernel(cs_ref, x_ref, g2_ref, y_ref, idx_ref, o_ref, ycat, acc, *, cap, ch, ntile):
    b = pl.program_id(0)
    t = pl.program_id(1)
    base = t * TOK_TILE
    sub = lax.broadcasted_iota(jnp.int32, (TOK_TILE, LANES), 0) + base
    if ntile == 1:
        for e in range(N_EXPERTS):
            ycat[e * ch:(e + 1) * ch, :] = y_ref[e, 0:ch, :]
        v = idx_ref[0]
        tiles = [(v[:, p * LANES:(p + 1) * LANES] == sub).astype(jnp.bfloat16)
                 for p in range(N_EXPERTS * ch // LANES)]
        acc[...] = jnp.dot(jnp.concatenate(tiles, axis=1), ycat[...], preferred_element_type=jnp.float32)
    else:
        lane = lax.broadcasted_iota(jnp.int32, (1, LANES), 1)
        per = LANES // ch
        sts = []
        for e in range(N_EXPERTS):
            s0 = cs_ref[(b * N_EXPERTS + e) * (ntile + 1) + t]
            st = jnp.minimum((s0 // BF16_TILE_ROWS) * BF16_TILE_ROWS, cap - ch)
            st = pl.multiple_of(st, BF16_TILE_ROWS)
            sts.append(st)
            ycat[e * ch:(e + 1) * ch, :] = y_ref[e, pl.ds(st, ch), :]
        tiles = []
        for p in range(N_EXPERTS // per):
            v = None
            for q in range(per):
                e = p * per + q
                r = pltpu.roll(idx_ref[0, e:e + 1, :], (2 * cap - sts[e] + q * ch) % cap, 1)[:, :LANES]
                v = r if v is None else jnp.where(lane >= q * ch, r, v)
            tiles.append((v == sub).astype(jnp.bfloat16))
        acc[...] = jnp.dot(jnp.concatenate(tiles, axis=1), ycat[...], preferred_element_type=jnp.float32)
        sub_c = lax.broadcasted_iota(jnp.int32, (TOK_TILE, ch), 0) + base
        lane_c = lax.broadcasted_iota(jnp.int32, (1, ch), 1)
        for e in range(N_EXPERTS):
            s1 = cs_ref[(b * N_EXPERTS + e) * (ntile + 1) + t + 1]
            first_end = sts[e] + ch
            n_extra = jnp.maximum(s1 - first_end + ch - 1, 0) // ch

            def extra(q, carry, e=e, first_end=first_end):
                lo = first_end + q * ch
                stq = pl.multiple_of(jnp.minimum(lo, cap - ch), BF16_TILE_ROWS)
                r = pltpu.roll(idx_ref[0, e:e + 1, :], (2 * cap - stq) % cap, 1)[:, :ch]
                hit = (r == sub_c) & (lane_c + stq >= lo)
                acc[...] += jnp.dot(hit.astype(jnp.bfloat16), y_ref[e, pl.ds(stq, ch), :],
                                    preferred_element_type=jnp.float32)
                return carry
            lax.fori_loop(0, n_extra, extra, 0)
    o_ref[0] = x_ref[0] + g2_ref[0] * acc[...]


def _moe_combine(x, g2, y, idx, cs):
    b, length, d = x.shape
    cap = idx.shape[2]
    ntile = length // TOK_TILE
    ch = min(SEG_CHUNK, cap)
    if ntile == 1:
        idx_in = idx.reshape(b, 1, N_EXPERTS * cap)
        idx_spec = pl.BlockSpec((1, 1, N_EXPERTS * cap), lambda i, j, c: (i, 0, 0))
    else:
        idx_in = idx
        idx_spec = pl.BlockSpec((1, N_EXPERTS, cap), lambda i, j, c: (i, 0, 0))
    grid_spec = pltpu.PrefetchScalarGridSpec(
        num_scalar_prefetch=1,
        grid=(b, ntile),
        in_specs=[pl.BlockSpec((1, TOK_TILE, d), lambda i, j, c: (i, j, 0)),
                  pl.BlockSpec((1, 1, d), lambda i, j, c: (i, 0, 0)),
                  pl.BlockSpec((N_EXPERTS, cap, d), lambda i, j, c: (0, i, 0)),
                  idx_spec],
        out_specs=pl.BlockSpec((1, TOK_TILE, d), lambda i, j, c: (i, j, 0)),
        scratch_shapes=[pltpu.VMEM((N_EXPERTS * ch, d), jnp.bfloat16),
                        pltpu.VMEM((TOK_TILE, d), jnp.float32)])
    return pl.pallas_call(
        functools.partial(_moe_comb_kernel, cap=cap, ch=ch, ntile=ntile),
        grid_spec=grid_spec,
        out_shape=jax.ShapeDtypeStruct((b, length, d), jnp.float32),
        compiler_params=pltpu.CompilerParams(
            dimension_semantics=("arbitrary", "arbitrary"), vmem_limit_bytes=56 * 1024 * 1024),
        name="moe_combine",
    )(cs.reshape(-1).astype(jnp.int32), x, g2, y, idx_in)


def _moe_block(x, m, g1, ng, sh, sc, g2, w_router, w_gate, w_up, w_down):
    b, length, d = x.shape
    cap = EC_FACTOR * length // N_EXPERTS
    x1, hpk, lg = _moe_pre(x, m, g1, ng, sh, sc, w_router)
    aff = jax.nn.softmax(lg, axis=1)
    _, idx = lax.top_k(aff, cap)
    idx = jnp.sort(idx, axis=-1)
    gate = jnp.take_along_axis(aff, idx, axis=-1)
    ntile = length // TOK_TILE
    bounds = jnp.arange(ntile + 1, dtype=jnp.int32) * TOK_TILE
    cs = jnp.sum(idx[:, :, :, None] < bounds, axis=2, dtype=jnp.int32)
    grow = idx + (jnp.arange(b, dtype=jnp.int32) * length)[:, None, None]
    grow = jnp.swapaxes(grow, 0, 1).reshape(N_EXPERTS, b * cap)
    gate_e = jnp.swapaxes(gate, 0, 1).reshape(N_EXPERTS, b * cap, 1)
    y = _moe_ffn(hpk.reshape(b * length, d // 2), grow, gate_e, w_gate, w_up, w_down)
    return _moe_combine(x1, g2, y, idx, cs)


def kernel(x_prompt, x_sample, state_ssd, c, c_ctx, norm_g, ada_w, ada_b, hy_in_w, hy_in_b, hy_short_w, hy_short_b, hy_f_w1, hy_f_b1, hy_f_w2, hy_f_b2, hy_f_w3, hy_f_freq, hy_f_bias, hy_out_w, ssd_in_w, ssd_conv_w, ssd_conv_b, ssd_dt_bias, ssd_A_log, ssd_D, ssd_norm_g, ssd_out_w, moe_router, moe_w_gate, moe_w_up, moe_w_down, final_norm_g):
    rows = x_sample.shape[1] // GRID_W
    xp = x_prompt
    xs = x_sample + _sincos_2d(rows, GRID_W, D_MODEL)[None]
    new_ssd = []
    for i in range(DEPTH):
        sh1p, sc1p, g1p, sh2p, sc2p, g2p = _adaln(c_ctx[None, :], ada_w[i], ada_b[i])
        sh1s, sc1s, g1s, sh2s, sc2s, g2s = _adaln(c, ada_w[i], ada_b[i])
        hp = _modulate(_rmsnorm(xp, norm_g[i, 0]), sh1p, sc1p)
        hs = _modulate(_rmsnorm(xs, norm_g[i, 0]), sh1s, sc1s)
        j = i // N_MIXERS
        if i % N_MIXERS == 0:
            hy = (hy_in_w[j], hy_in_b[j], hy_short_w[j], hy_short_b[j], hy_f_w1[j], hy_f_b1[j],
                  hy_f_w2[j], hy_f_b2[j], hy_f_w3[j], hy_f_freq[j], hy_f_bias[j], hy_out_w[j])
            mp = _hyena_mixer(hp, *hy)
            ms = _hyena_mixer(hs, *hy)
        else:
            sp = (ssd_in_w[j], ssd_conv_w[j], ssd_conv_b[j], ssd_dt_bias[j], ssd_A_log[j],
                  ssd_D[j], ssd_norm_g[j], ssd_out_w[j])
            zeros = jnp.zeros((xp.shape[0], SSD_HEADS, SSD_HEAD_DIM, SSD_STATE), jnp.float32)
            mp, s_f, s_b = _ssd_mixer(hp, zeros, zeros, *sp)
            new_ssd.append(jnp.stack([s_f, s_b], axis=1))
            ms, _, _ = _ssd_mixer(hs, state_ssd[:, j, 0], state_ssd[:, j, 1], *sp)
        moe = (moe_router[i], moe_w_gate[i], moe_w_up[i], moe_w_down[i])
        bp = (xp.shape[0], 1, D_MODEL)
        xp = _moe_block(xp, mp, jnp.broadcast_to(g1p, bp), norm_g[i, 1], jnp.broadcast_to(sh2p, bp),
                        jnp.broadcast_to(sc2p, bp), jnp.broadcast_to(g2p, bp), *moe)
        xs = _moe_block(xs, ms, g1s, norm_g[i, 1], sh2s, sc2s, g2s, *moe)
    y_prompt = _rmsnorm(xp, final_norm_g)
    y_sample = _rmsnorm(xs, final_norm_g)
    new_state_ssd = jnp.stack(new_ssd, axis=1)
    return (y_prompt, y_sample, new_state_ssd)
```

```python
import functools
import math

import jax
import jax.numpy as jnp
from jax import lax
from jax.experimental import pallas as pl
from jax.experimental.pallas import tpu as pltpu

D_MODEL = 1024
DEPTH = 2
GRID_W = 64
N_MIXERS = 2
RMS_EPS = 1e-6
HY_EMB = 33
HY_BANDS = (HY_EMB - 1) // 2
HY_SHORT_DECAY_FRAC = 0.3
HY_LONG_DECAY_FRAC = 1.5
HY_DECAY_TARGET = 1e-2
HY_MAX_DECAY = math.log(HY_DECAY_TARGET) / HY_SHORT_DECAY_FRAC
HY_MIN_DECAY = math.log(HY_DECAY_TARGET) / HY_LONG_DECAY_FRAC
SSD_D_INNER = 2 * D_MODEL
SSD_HEAD_DIM = 64
SSD_HEADS = SSD_D_INNER // SSD_HEAD_DIM
SSD_GROUPS = 4
SSD_STATE = 128
SSD_CHUNK = 128
SSD_XBC = SSD_D_INNER + 2 * SSD_GROUPS * SSD_STATE
N_EXPERTS = 16
EC_FACTOR = 2

VMEM_LIMIT_BYTES = 48 * 1024 * 1024


def _mm_kernel(a_ref, b_ref, o_ref, acc_ref):
    @pl.when(pl.program_id(2) == 0)
    def _():
        acc_ref[...] = jnp.zeros_like(acc_ref)

    acc_ref[...] += jnp.dot(a_ref[...].astype(jnp.bfloat16), b_ref[...],
                            preferred_element_type=jnp.float32)

    @pl.when(pl.program_id(2) == pl.num_programs(2) - 1)
    def _():
        o_ref[...] = acc_ref[...]


def _pick(n, pref):
    for t in pref:
        if n % t == 0:
            return t
    return n


def _mm(a, b):
    m, k = a.shape
    n = b.shape[1]
    mp = -(-m // 8) * 8
    if mp != m:
        a = jnp.pad(a, ((0, mp - m), (0, 0)))
    tm = _pick(mp, (512, 256, 128, 64, 32, 16, 8))
    tn = _pick(n, (512, 256, 128))
    tk = _pick(k, (1024, 512, 256, 128))
    out = pl.pallas_call(
        _mm_kernel,
        grid=(mp // tm, n // tn, k // tk),
        in_specs=[pl.BlockSpec((tm, tk), lambda i, j, l: (i, l)),
                  pl.BlockSpec((tk, tn), lambda i, j, l: (l, j))],
        out_specs=pl.BlockSpec((tm, tn), lambda i, j, l: (i, j)),
        out_shape=jax.ShapeDtypeStruct((mp, n), jnp.float32),
        scratch_shapes=[pltpu.VMEM((tm, tn), jnp.float32)],
        compiler_params=pltpu.CompilerParams(
            dimension_semantics=("parallel", "parallel", "arbitrary"),
            vmem_limit_bytes=VMEM_LIMIT_BYTES),
        name="mm",
    )(a, b.astype(jnp.bfloat16))
    return out[:m]


def _mm3(a, b):
    lead = a.shape[:-1]
    return _mm(a.reshape(-1, a.shape[-1]), b).reshape(*lead, b.shape[1])


def _bmm_kernel(a_ref, b_ref, o_ref):
    o_ref[0] = jnp.dot(a_ref[0].astype(jnp.bfloat16), b_ref[0],
                       preferred_element_type=jnp.float32)


def _bmm(a, b):
    e, m, k = a.shape
    n = b.shape[2]
    tm = _pick(m, (512, 256, 128, 64, 32, 16, 8))
    tn = _pick(n, (512, 256, 128))
    return pl.pallas_call(
        _bmm_kernel,
        grid=(e, m // tm, n // tn),
        in_specs=[pl.BlockSpec((1, tm, k), lambda g, i, j: (g, i, 0)),
                  pl.BlockSpec((1, k, tn), lambda g, i, j: (g, 0, j))],
        out_specs=pl.BlockSpec((1, tm, tn), lambda g, i, j: (g, i, j)),
        out_shape=jax.ShapeDtypeStruct((e, m, n), jnp.float32),
        compiler_params=pltpu.CompilerParams(
            dimension_semantics=("parallel", "parallel", "parallel"),
            vmem_limit_bytes=VMEM_LIMIT_BYTES),
        name="bmm",
    )(a, b.astype(jnp.bfloat16))


def _rmsnorm(x, g):
    y = x * lax.rsqrt(jnp.mean(x * x, axis=-1, keepdims=True) + RMS_EPS)
    return y * g


def _adaln(cond, ada_w, ada_b):
    m = jnp.dot(jax.nn.silu(cond), ada_w, precision=lax.Precision.HIGHEST) + ada_b
    return jnp.split(m[:, None, :], 6, axis=-1)


def _modulate(h, shift, scale):
    return h * (1 + scale) + shift


def _dwconv(x, w, b):
    k, ch = w.shape
    y = lax.conv_general_dilated(x, w[:, None, :], window_strides=(1,),
                                 padding=[(k // 2, k // 2)],
                                 dimension_numbers=('NWC', 'WIO', 'NWC'),
                                 feature_group_count=ch,
                                 precision=lax.Precision.HIGHEST)
    return y + b


def _sincos_2d(rows, cols, d):
    q = d // 4
    omega = 1.0 / (10000.0 ** (jnp.arange(q, dtype=jnp.float32) / q))
    t = jnp.arange(rows * cols)
    er = (t // cols).astype(jnp.float32)[:, None] * omega[None, :]
    ec = (t % cols).astype(jnp.float32)[:, None] * omega[None, :]
    return jnp.concatenate([jnp.sin(er), jnp.cos(er), jnp.sin(ec), jnp.cos(ec)], axis=-1)


def _hyena_filters(length, f_w1, f_b1, f_w2, f_b2, f_w3, f_freq):
    hp = lax.Precision.HIGHEST
    t = jnp.linspace(0.0, 1.0, length, dtype=jnp.float32)[:, None]
    w = 2.0 * math.pi * jnp.arange(length, dtype=jnp.float32)[:, None] / length
    f = jnp.linspace(1e-4, HY_BANDS - 1, HY_BANDS, dtype=jnp.float32)[None, :]
    z = jnp.concatenate([t, jnp.cos(f * w), -jnp.sin(f * w)], axis=-1)
    h = jnp.sin(f_freq * (jnp.dot(z, f_w1, precision=hp) + f_b1))
    h = jnp.sin(f_freq * (jnp.dot(h, f_w2, precision=hp) + f_b2))
    h = jnp.dot(h, f_w3, precision=hp)
    deltas = jnp.linspace(HY_MIN_DECAY, HY_MAX_DECAY, D_MODEL, dtype=jnp.float32)
    window = jnp.exp(-t * jnp.abs(deltas)[None, :])
    return h[:, :D_MODEL] * window, h[:, D_MODEL:] * window


def _bidir_long_conv(u, h_fwd, h_bwd, bias):
    b, length, ch = u.shape
    n = 2 * length
    k = jnp.concatenate([h_fwd, jnp.zeros((1, ch), jnp.float32), h_bwd[1:][::-1]], axis=0)
    kf = jnp.fft.rfft(k, n=n, axis=0)
    uf = jnp.fft.rfft(u, n=n, axis=1)
    y = jnp.fft.irfft(uf * kf[None], n=n, axis=1)[:, :length]
    return y + u * bias


def _hyena_mixer(h, in_w, in_b, short_w, short_b, f_w1, f_b1, f_w2, f_b2, f_w3, f_freq, f_bias, out_w):
    length = h.shape[1]
    u = _dwconv(_mm3(h, in_w) + in_b, short_w, short_b)
    x0 = u[..., :D_MODEL]
    x1 = u[..., D_MODEL:2 * D_MODEL]
    v = u[..., 2 * D_MODEL:]
    h_fwd, h_bwd = _hyena_filters(length, f_w1, f_b1, f_w2, f_b2, f_w3, f_freq)
    v = _bidir_long_conv(v * x1, h_fwd, h_bwd, f_bias)
    return _mm3(v * x0, out_w)


def _ssd_scan(x, dt, a, bm, cm, init):
    hp = lax.Precision.HIGHEST
    b, length = x.shape[:2]
    nc = length // SSD_CHUNK
    q, g, r = SSD_CHUNK, SSD_GROUPS, SSD_HEADS // SSD_GROUPS
    x = x.reshape(b, nc, q, g, r, SSD_HEAD_DIM)
    dt = dt.reshape(b, nc, q, g, r)
    bm = bm.reshape(b, nc, q, g, SSD_STATE)
    cm = cm.reshape(b, nc, q, g, SSD_STATE)
    a_cum = jnp.cumsum(dt * a.reshape(g, r), axis=2)
    xdt = x * dt[..., None]
    seg = a_cum[:, :, :, None] - a_cum[:, :, None, :]
    causal = jnp.tril(jnp.ones((q, q), dtype=bool))[None, None, :, :, None, None]
    decay = jnp.exp(jnp.where(causal, seg, -jnp.inf))
    cb = jnp.einsum('bclgn,bcsgn->bclsg', cm, bm, precision=hp)
    y_diag = jnp.einsum('bclsgr,bcsgrp->bclgrp', cb[..., None] * decay, xdt, precision=hp)
    decay_end = jnp.exp(a_cum[:, :, -1:] - a_cum)
    chunk_states = jnp.einsum('bcsgn,bcsgrp->bcgrpn', bm, xdt * decay_end[..., None], precision=hp)
    chunk_decay = jnp.exp(a_cum[:, :, -1])

    def step(s, inp):
        st, dec = inp
        return s * dec[..., None, None] + st, s

    init_g = init.astype(jnp.float32).reshape(b, g, r, SSD_HEAD_DIM, SSD_STATE)
    final, starts = lax.scan(step, init_g, (jnp.moveaxis(chunk_states, 1, 0), jnp.moveaxis(chunk_decay, 1, 0)))
    starts = jnp.moveaxis(starts, 0, 1)
    y_off = jnp.einsum('bclgn,bcgrpn->bclgrp', cm, starts, precision=hp) * jnp.exp(a_cum)[..., None]
    y = (y_diag + y_off).reshape(b, length, SSD_HEADS, SSD_HEAD_DIM)
    return y, final.reshape(b, SSD_HEADS, SSD_HEAD_DIM, SSD_STATE)


def _flip(t):
    return jnp.flip(t, axis=1)


def _ssd_mixer(h, init_f, init_b, in_w, conv_w, conv_b, dt_bias, a_log, d_skip, norm_g, out_w):
    b, length, _ = h.shape
    proj = _mm3(h, in_w)
    z = proj[..., :SSD_D_INNER]
    xbc = jax.nn.silu(_dwconv(proj[..., SSD_D_INNER:SSD_D_INNER + SSD_XBC], conv_w, conv_b))
    dt_raw = proj[..., SSD_D_INNER + SSD_XBC:]
    gn = SSD_GROUPS * SSD_STATE
    xh = xbc[..., :SSD_D_INNER].reshape(b, length, SSD_HEADS, SSD_HEAD_DIM)
    bm = xbc[..., SSD_D_INNER:SSD_D_INNER + gn].reshape(b, length, SSD_GROUPS, SSD_STATE)
    cm = xbc[..., SSD_D_INNER + gn:].reshape(b, length, SSD_GROUPS, SSD_STATE)
    dt = jax.nn.softplus(dt_raw.reshape(b, length, 2, SSD_HEADS) + dt_bias)
    a = -jnp.exp(a_log)
    y_f, s_f = _ssd_scan(xh, dt[:, :, 0], a[0], bm, cm, init_f)
    y_b, s_b = _ssd_scan(_flip(xh), _flip(dt[:, :, 1]), a[1], _flip(bm), _flip(cm), init_b)
    y = y_f + _flip(y_b) + d_skip[:, None] * xh
    y = y.reshape(b, length, SSD_D_INNER) * jax.nn.silu(z)
    y = _rmsnorm(y, norm_g)
    return _mm3(y, out_w), s_f, s_b


SSD_GN = SSD_GROUPS * SSD_STATE
SSD_GROUP_W = SSD_D_INNER // SSD_GROUPS
SSD_HEADS_PER_GROUP = SSD_HEADS // SSD_GROUPS
ROW_TILE = 256


def _modnorm(x, ng, sh, sc):
    ms = jnp.mean(x * x, axis=-1, keepdims=True)
    return (x * lax.rsqrt(ms + RMS_EPS) * ng) * (1.0 + sc) + sh


def _ssd_in_kernel(x_ref, ng_ref, sh_ref, sc_ref, w_ref, wdt_ref, wdtt_ref,
                   z_ref, xbc_ref, dt_ref, dtt_ref):
    h = _modnorm(x_ref[0], ng_ref[...], sh_ref[0], sc_ref[0]).astype(jnp.bfloat16)
    zx = jnp.dot(h, w_ref[...], preferred_element_type=jnp.float32)
    z_ref[0] = zx[:, :SSD_D_INNER]
    xbc_ref[0] = zx[:, SSD_D_INNER:]
    dt_ref[0] = jnp.dot(h, wdt_ref[...], preferred_element_type=jnp.float32)
    dtt_ref[0] = lax.dot_general(wdtt_ref[...], h, (((1,), (1,)), ((), ())),
                                 preferred_element_type=jnp.float32)


def _ssd_in(x, ng, sh, sc, in_w):
    b, length, d = x.shape
    tm = min(length, ROW_TILE)
    nzx = SSD_D_INNER + SSD_XBC
    w = in_w[:, :nzx].astype(jnp.bfloat16)
    wdt = in_w[:, nzx:]
    wdt_p = jnp.pad(wdt, ((0, 0), (0, LANES - 2 * SSD_HEADS))).astype(jnp.bfloat16)
    wdt_t = wdt.T.astype(jnp.bfloat16)
    row = lambda i, j: (i, j, 0)
    per_b = lambda i, j: (i, 0, 0)
    full2 = lambda i, j: (0, 0)
    return pl.pallas_call(
        _ssd_in_kernel,
        grid=(b, length // tm),
        in_specs=[pl.BlockSpec((1, tm, d), row), pl.BlockSpec((1, d), full2),
                  pl.BlockSpec((1, 1, d), per_b), pl.BlockSpec((1, 1, d), per_b),
                  pl.BlockSpec((d, nzx), full2), pl.BlockSpec((d, LANES), full2),
                  pl.BlockSpec((2 * SSD_HEADS, d), full2)],
        out_specs=[pl.BlockSpec((1, tm, SSD_D_INNER), row), pl.BlockSpec((1, tm, SSD_XBC), row),
                   pl.BlockSpec((1, tm, LANES), row),
                   pl.BlockSpec((1, 2 * SSD_HEADS, tm), lambda i, j: (i, 0, j))],
        out_shape=[jax.ShapeDtypeStruct((b, length, SSD_D_INNER), jnp.float32),
                   jax.ShapeDtypeStruct((b, length, SSD_XBC), jnp.float32),
                   jax.ShapeDtypeStruct((b, length, LANES), jnp.float32),
                   jax.ShapeDtypeStruct((b, 2 * SSD_HEADS, length), jnp.float32)],
        compiler_params=pltpu.CompilerParams(
            dimension_semantics=("parallel", "parallel"), vmem_limit_bytes=VMEM_LIMIT_BYTES),
        name="ssd_in",
    )(x, ng.reshape(1, d), sh, sc, w, wdt_p, wdt_t)


CONV_PAD = 8
CONV_ROWS = 256
CONV_COLS = 256


def _dwconv_kernel(x_ref, w_ref, b_ref, o_ref, pad_ref, *, taps, silu):
    length = x_ref.shape[1]
    zeros = jnp.zeros((CONV_PAD, x_ref.shape[2]), jnp.float32)
    pad_ref[0:CONV_PAD, :] = zeros
    pad_ref[CONV_PAD + length:, :] = zeros
    pad_ref[CONV_PAD:CONV_PAD + length, :] = x_ref[0]
    rows = min(CONV_ROWS, length)
    for r in range(0, length, rows):
        acc = b_ref[...] + jnp.zeros((rows, x_ref.shape[2]), jnp.float32)
        for k in range(taps):
            off = r + CONV_PAD + k - taps // 2
            acc = acc + w_ref[k:k + 1, :] * pad_ref[off:off + rows, :]
        if silu:
            acc = acc * jax.nn.sigmoid(acc)
        o_ref[0, r:r + rows, :] = acc


def _dwconv_p(x, w, b, silu):
    bsz, length, ch = x.shape
    taps = w.shape[0]
    return pl.pallas_call(
        functools.partial(_dwconv_kernel, taps=taps, silu=silu),
        grid=(bsz, ch // CONV_COLS),
        in_specs=[pl.BlockSpec((1, length, CONV_COLS), lambda i, j: (i, 0, j)),
                  pl.BlockSpec((taps, CONV_COLS), lambda i, j: (0, j)),
                  pl.BlockSpec((1, CONV_COLS), lambda i, j: (0, j))],
        out_specs=pl.BlockSpec((1, length, CONV_COLS), lambda i, j: (i, 0, j)),
        out_shape=jax.ShapeDtypeStruct((bsz, length, ch), jnp.float32),
        scratch_shapes=[pltpu.VMEM((length + 2 * CONV_PAD, CONV_COLS), jnp.float32)],
        compiler_params=pltpu.CompilerParams(
            dimension_semantics=("parallel", "parallel"), vmem_limit_bytes=VMEM_LIMIT_BYTES),
        name="dwconv",
    )(x, w, b.reshape(1, ch))


def _split3_bf16(v):
    p1 = v.astype(jnp.bfloat16)
    r1 = v - p1.astype(jnp.float32)
    p2 = r1.astype(jnp.bfloat16)
    p3 = (r1 - p2.astype(jnp.float32)).astype(jnp.bfloat16)
    return p1, p2, p3


def _softplus(v):
    return jnp.maximum(v, 0.0) + jnp.log1p(jnp.exp(-jnp.abs(v)))


def _expand_heads(cols, g):
    q = cols.shape[0]
    lane = lax.broadcasted_iota(jnp.int32, (q, LANES), 1)
    tiles = []
    for k in range(SSD_HEADS_PER_GROUP // 2):
        ha = g * SSD_HEADS_PER_GROUP + 2 * k
        ca = jnp.broadcast_to(cols[:, ha:ha + 1], (q, LANES))
        cb = jnp.broadcast_to(cols[:, ha + 1:ha + 2], (q, LANES))
        tiles.append(jnp.where(lane < SSD_HEAD_DIM, ca, cb))
    return jnp.concatenate(tiles, axis=1)


def _ssd_scan_kernel(x_ref, b_ref, c_ref, dt_ref, dtt_ref, dtb_ref, dtbt_ref, a_ref, at_ref, init_ref,
                     y_ref, fin_ref, st_ref, *, reverse, hoff):
    ci = pl.program_id(1)

    @pl.when(ci == 0)
    def _():
        st_ref[...] = init_ref[0]

    q = SSD_CHUNK
    f32, bf16 = jnp.float32, jnp.bfloat16
    dt = _softplus(dt_ref[0][:, hoff:hoff + SSD_HEADS] + dtb_ref[...])
    dtt = _softplus(dtt_ref[0][hoff:hoff + SSD_HEADS, :] + dtbt_ref[...])
    ri = lax.broadcasted_iota(jnp.int32, (q, q), 0)
    cj = lax.broadcasted_iota(jnp.int32, (q, q), 1)
    keep = (cj >= ri) if reverse else (cj <= ri)
    tri = keep.astype(bf16)
    tri_t = ((ri >= cj) if reverse else (ri <= cj)).astype(bf16)
    acum = sum(jnp.dot(tri, p, preferred_element_type=f32) for p in _split3_bf16(dt * a_ref[...]))
    acum_t = sum(jnp.dot(p, tri_t, preferred_element_type=f32) for p in _split3_bf16(dtt * at_ref[...]))
    end = 0 if reverse else q - 1
    a_end = acum[end:end + 1, :]
    eacum = jnp.exp(acum)
    dec_end = jnp.exp(a_end - acum)
    lane = lax.broadcasted_iota(jnp.int32, (q, LANES), 1)
    for g in range(SSD_GROUPS):
        cg = c_ref[0][:, g * SSD_STATE:(g + 1) * SSD_STATE]
        bg = b_ref[0][:, g * SSD_STATE:(g + 1) * SSD_STATE]
        cg16 = cg.astype(bf16)
        cb = lax.dot_general(cg16, bg.astype(bf16), (((1,), (1,)), ((), ())), preferred_element_type=f32)
        xg = x_ref[0][:, g * SSD_GROUP_W:(g + 1) * SSD_GROUP_W]
        xdt = xg * _expand_heads(dt, g)
        xdt16 = xdt.astype(bf16)
        eac_x = _expand_heads(eacum, g)
        yd = []
        for k in range(SSD_HEADS_PER_GROUP // 2):
            xp = xdt16[:, k * LANES:(k + 1) * LANES]
            ys = []
            for hh in range(2):
                h = g * SSD_HEADS_PER_GROUP + 2 * k + hh
                seg = acum[:, h:h + 1] - acum_t[h:h + 1, :]
                lmat = jnp.exp(jnp.where(keep, seg, -jnp.inf))
                ys.append(jnp.dot((cb * lmat).astype(bf16), xp, preferred_element_type=f32))
            yd.append(jnp.where(lane < SSD_HEAD_DIM, ys[0], ys[1]))
        st = st_ref[g]
        y_off = jnp.dot(cg16, st.astype(bf16), preferred_element_type=f32) * eac_x
        y_ref[0, :, g * SSD_GROUP_W:(g + 1) * SSD_GROUP_W] = jnp.concatenate(yd, axis=1) + y_off
        xdd16 = (xdt * _expand_heads(dec_end, g)).astype(bf16)
        st_ref[g] = st * eac_x[end:end + 1, :] + jnp.dot(bg.T.astype(bf16), xdd16, preferred_element_type=f32)

    @pl.when(ci == pl.num_programs(1) - 1)
    def _():
        fin_ref[0] = st_ref[...]


def _ssd_scan_p(xbc, dt_raw, dt_raw_t, dt_bias, a, init, reverse, direction):
    b, length, _ = xbc.shape
    nc = length // SSD_CHUNK
    q = SSD_CHUNK
    cidx = (lambda j: nc - 1 - j) if reverse else (lambda j: j)
    nb = SSD_D_INNER // SSD_GN
    hoff = direction * SSD_HEADS
    full2 = lambda i, j: (0, 0)
    st_shape = (SSD_GROUPS, SSD_STATE, SSD_GROUP_W)
    return pl.pallas_call(
        functools.partial(_ssd_scan_kernel, reverse=reverse, hoff=hoff),
        grid=(b, nc),
        in_specs=[pl.BlockSpec((1, q, SSD_D_INNER), lambda i, j: (i, cidx(j), 0)),
                  pl.BlockSpec((1, q, SSD_GN), lambda i, j: (i, cidx(j), nb)),
                  pl.BlockSpec((1, q, SSD_GN), lambda i, j: (i, cidx(j), nb + 1)),
                  pl.BlockSpec((1, q, LANES), lambda i, j: (i, cidx(j), 0)),
                  pl.BlockSpec((1, 2 * SSD_HEADS, q), lambda i, j: (i, 0, cidx(j))),
                  pl.BlockSpec((1, SSD_HEADS), full2), pl.BlockSpec((SSD_HEADS, 1), full2),
                  pl.BlockSpec((1, SSD_HEADS), full2), pl.BlockSpec((SSD_HEADS, 1), full2),
                  pl.BlockSpec((1,) + st_shape, lambda i, j: (i, 0, 0, 0))],
        out_specs=[pl.BlockSpec((1, q, SSD_D_INNER), lambda i, j: (i, cidx(j), 0)),
                   pl.BlockSpec((1,) + st_shape, lambda i, j: (i, 0, 0, 0))],
        out_shape=[jax.ShapeDtypeStruct((b, length, SSD_D_INNER), jnp.float32),
                   jax.ShapeDtypeStruct((b,) + st_shape, jnp.float32)],
        scratch_shapes=[pltpu.VMEM(st_shape, jnp.float32)],
        compiler_params=pltpu.CompilerParams(
            dimension_semantics=("parallel", "arbitrary"), vmem_limit_bytes=VMEM_LIMIT_BYTES),
        name="ssd_scan",
    )(xbc, xbc, xbc, dt_raw, dt_raw_t, dt_bias.reshape(1, -1), dt_bias.reshape(-1, 1),
      a.reshape(1, -1), a.reshape(-1, 1), init)


def _ssd_out_kernel(yf_ref, yb_ref, x_ref, z_ref, dsk_ref, ng_ref, w_ref, o_ref):
    z = z_ref[0]
    y = (yf_ref[0] + yb_ref[0] + dsk_ref[...] * x_ref[0]) * (z * jax.nn.sigmoid(z))
    ms = jnp.mean(y * y, axis=-1, keepdims=True)
    y = y * lax.rsqrt(ms + RMS_EPS) * ng_ref[...]
    o_ref[0] = jnp.dot(y.astype(jnp.bfloat16), w_ref[...], preferred_element_type=jnp.float32)


def _ssd_out(yf, yb, xbc, z, d_skip, norm_g, out_w):
    b, length, di = yf.shape
    d = out_w.shape[1]
    tm = min(length, ROW_TILE)
    row = lambda i, j: (i, j, 0)
    full2 = lambda i, j: (0, 0)
    dsk = jnp.repeat(d_skip, SSD_HEAD_DIM).reshape(1, di)
    return pl.pallas_call(
        _ssd_out_kernel,
        grid=(b, length // tm),
        in_specs=[pl.BlockSpec((1, tm, di), row), pl.BlockSpec((1, tm, di), row),
                  pl.BlockSpec((1, tm, di), row), pl.BlockSpec((1, tm, di), row),
                  pl.BlockSpec((1, di), full2), pl.BlockSpec((1, di), full2),
                  pl.BlockSpec((di, d), full2)],
        out_specs=pl.BlockSpec((1, tm, d), row),
        out_shape=jax.ShapeDtypeStruct((b, length, d), jnp.float32),
        compiler_params=pltpu.CompilerParams(
            dimension_semantics=("parallel", "parallel"), vmem_limit_bytes=VMEM_LIMIT_BYTES),
        name="ssd_out",
    )(yf, yb, xbc, z, dsk, norm_g.reshape(1, di), out_w.astype(jnp.bfloat16))


def _state_to_kernel(s):
    b = s.shape[0]
    s = s.reshape(b, SSD_GROUPS, SSD_HEADS_PER_GROUP, SSD_HEAD_DIM, SSD_STATE)
    return jnp.transpose(s, (0, 1, 4, 2, 3)).reshape(b, SSD_GROUPS, SSD_STATE, SSD_GROUP_W)


def _state_from_kernel(s):
    b = s.shape[0]
    s = s.reshape(b, SSD_GROUPS, SSD_STATE, SSD_HEADS_PER_GROUP, SSD_HEAD_DIM)
    return jnp.transpose(s, (0, 1, 3, 4, 2)).reshape(b, SSD_HEADS, SSD_HEAD_DIM, SSD_STATE)


def _ssd_mixer_p(x, ng, sh, sc, init_f, init_b, in_w, conv_w, conv_b, dt_bias, a_log, d_skip, norm_g, out_w):
    z, xbc_raw, dt_raw, dt_raw_t = _ssd_in(x, ng, sh, sc, in_w)
    xbc = _dwconv_p(xbc_raw, conv_w, conv_b, silu=True)
    a = -jnp.exp(a_log)
    yf, s_f = _ssd_scan_p(xbc, dt_raw, dt_raw_t, dt_bias[0], a[0], _state_to_kernel(init_f), False, 0)
    yb, s_b = _ssd_scan_p(xbc, dt_raw, dt_raw_t, dt_bias[1], a[1], _state_to_kernel(init_b), True, 1)
    m = _ssd_out(yf, yb, xbc, z, d_skip, norm_g, out_w)
    return m, _state_from_kernel(s_f), _state_from_kernel(s_b)


TOK_TILE = 256
MOE_ROWS = 512
MOE_HALVES = 2
SEG_CHUNK = 64
BF16_TILE_ROWS = 16
LANES = 128


def _split_bf16(w):
    hi = w.astype(jnp.bfloat16)
    lo = (w - hi.astype(jnp.float32)).astype(jnp.bfloat16)
    return hi, lo


def _moe_pre_kernel(x_ref, m_ref, g1_ref, ng_ref, sh_ref, sc_ref, wrh_ref, wrl_ref,
                    xo_ref, hpk_ref, lg_ref):
    x = x_ref[0] + g1_ref[0] * m_ref[0]
    xo_ref[0] = x
    ms = jnp.mean(x * x, axis=-1, keepdims=True)
    h = x * lax.rsqrt(ms + RMS_EPS) * ng_ref[...]
    h = h * (1.0 + sc_ref[0]) + sh_ref[0]
    h_hi = h.astype(jnp.bfloat16)
    h_lo = (h - h_hi.astype(jnp.float32)).astype(jnp.bfloat16)
    dn = (((1,), (1,)), ((), ()))
    lg = lax.dot_general(wrh_ref[...], h_hi, dn, preferred_element_type=jnp.float32)
    lg += lax.dot_general(wrh_ref[...], h_lo, dn, preferred_element_type=jnp.float32)
    lg += lax.dot_general(wrl_ref[...], h_hi, dn, preferred_element_type=jnp.float32)
    lg_ref[0] = lg
    half = h.shape[1] // 2
    wa = pltpu.bitcast(h_hi[:, :half].astype(jnp.float32), jnp.uint32) >> 16
    wb = pltpu.bitcast(h_hi[:, half:].astype(jnp.float32), jnp.uint32) & jnp.uint32(0xFFFF0000)
    hpk_ref[0] = wa | wb


def _moe_pre(x, m, g1, ng, sh, sc, w_router):
    b, length, d = x.shape
    tm = min(length, 512)
    wrh, wrl = _split_bf16(w_router.T)
    row = lambda i, j: (i, j, 0)
    per_b = lambda i, j: (i, 0, 0)
    full2 = lambda i, j: (0, 0)
    return pl.pallas_call(
        _moe_pre_kernel,
        grid=(b, length // tm),
        in_specs=[pl.BlockSpec((1, tm, d), row), pl.BlockSpec((1, tm, d), row),
                  pl.BlockSpec((1, 1, d), per_b), pl.BlockSpec((1, d), full2),
                  pl.BlockSpec((1, 1, d), per_b), pl.BlockSpec((1, 1, d), per_b),
                  pl.BlockSpec((N_EXPERTS, d), full2), pl.BlockSpec((N_EXPERTS, d), full2)],
        out_specs=[pl.BlockSpec((1, tm, d), row), pl.BlockSpec((1, tm, d // 2), row),
                   pl.BlockSpec((1, N_EXPERTS, tm), lambda i, j: (i, 0, j))],
        out_shape=[jax.ShapeDtypeStruct((b, length, d), jnp.float32),
                   jax.ShapeDtypeStruct((b, length, d // 2), jnp.uint32),
                   jax.ShapeDtypeStruct((b, N_EXPERTS, length), jnp.float32)],
        compiler_params=pltpu.CompilerParams(
            dimension_semantics=("parallel", "parallel"), vmem_limit_bytes=VMEM_LIMIT_BYTES),
        name="moe_pre",
    )(x, m, g1, ng.reshape(1, d), sh, sc, wrh, wrl)


def _moe_ffn_kernel(idx_ref, h_hbm, gate_ref, wg_ref, wu_ref, wd_ref, y_ref,
                    xe_ref, wgb, wub, wdb, sem):
    @pl.when(pl.program_id(1) == 0)
    def _():
        wgb[...] = wg_ref[0].astype(jnp.bfloat16)
        wub[...] = wu_ref[0].astype(jnp.bfloat16)
        wdb[...] = wd_ref[0].astype(jnp.bfloat16)

    rows = MOE_ROWS // MOE_HALVES
    for hf in range(MOE_HALVES):
        def issue(c, carry, hf=hf):
            r = idx_ref[0, 0, hf * rows + c]
            pltpu.make_async_copy(h_hbm.at[pl.ds(r, 1)], xe_ref.at[pl.ds(hf * rows + c, 1)],
                                  sem.at[hf]).start()
            return carry
        lax.fori_loop(0, rows, issue, 0, unroll=8)

    half = wgb.shape[0] // 2
    for hf in range(MOE_HALVES):
        sl = pl.ds(hf * rows, rows)
        pltpu.make_async_copy(h_hbm.at[pl.ds(0, rows)], xe_ref.at[sl], sem.at[hf]).wait()
        w = xe_ref[sl, :]
        xa = pltpu.bitcast(w << 16, jnp.float32).astype(jnp.bfloat16)
        xb = pltpu.bitcast(w & jnp.uint32(0xFFFF0000), jnp.float32).astype(jnp.bfloat16)
        hg = jnp.dot(xa, wgb[:half], preferred_element_type=jnp.float32)
        hg += jnp.dot(xb, wgb[half:], preferred_element_type=jnp.float32)
        hu = jnp.dot(xa, wub[:half], preferred_element_type=jnp.float32)
        hu += jnp.dot(xb, wub[half:], preferred_element_type=jnp.float32)
        hid = (hg * jax.nn.sigmoid(hg) * hu).astype(jnp.bfloat16)
        y = jnp.dot(hid, wdb[...], preferred_element_type=jnp.float32)
        y_ref[0, sl, :] = (y * gate_ref[0, sl, :]).astype(jnp.bfloat16)


def _moe_ffn(hpk, grow, gate, w_gate, w_up, w_down):
    e, r = grow.shape
    d, f = w_gate.shape[1], w_gate.shape[2]
    nblk = r // MOE_ROWS
    wspec = lambda shp: pl.BlockSpec((1,) + shp, lambda i, j: (i, 0, 0))
    return pl.pallas_call(
        _moe_ffn_kernel,
        grid=(e, nblk),
        in_specs=[pl.BlockSpec((1, 1, MOE_ROWS), lambda i, j: (i * nblk + j, 0, 0), memory_space=pltpu.SMEM),
                  pl.BlockSpec(memory_space=pltpu.HBM),
                  pl.BlockSpec((1, MOE_ROWS, 1), lambda i, j: (i, j, 0)),
                  wspec((d, f)), wspec((d, f)), wspec((f, d))],
        out_specs=pl.BlockSpec((1, MOE_ROWS, d), lambda i, j: (i, j, 0)),
        out_shape=jax.ShapeDtypeStruct((e, r, d), jnp.bfloat16),
        scratch_shapes=[pltpu.VMEM((MOE_ROWS, d // 2), jnp.uint32),
                        pltpu.VMEM((d, f), jnp.bfloat16), pltpu.VMEM((d, f), jnp.bfloat16),
                        pltpu.VMEM((f, d), jnp.bfloat16),
                        pltpu.SemaphoreType.DMA((MOE_HALVES,))],
        compiler_params=pltpu.CompilerParams(
            dimension_semantics=("arbitrary", "arbitrary"), vmem_limit_bytes=VMEM_LIMIT_BYTES),
        name="moe_ffn",
    )(grow.reshape(e * nblk, 1, MOE_ROWS), hpk, gate, w_gate, w_up, w_down)


def _moe_comb_kernel(cs_ref, x_ref, g2_ref, y_ref, idx_ref, o_ref, ycat, acc, *, cap, ch, ntile):
    b = pl.program_id(0)
    t = pl.program_id(1)
    base = t * TOK_TILE
    sub = lax.broadcasted_iota(jnp.int32, (TOK_TILE, LANES), 0) + base
    if ntile == 1:
        for e in range(N_EXPERTS):
            ycat[e * ch:(e + 1) * ch, :] = y_ref[e, 0:ch, :]
        v = idx_ref[0]
        tiles = [(v[:, p * LANES:(p + 1) * LANES] == sub).astype(jnp.bfloat16)
                 for p in range(N_EXPERTS * ch // LANES)]
        acc[...] = jnp.dot(jnp.concatenate(tiles, axis=1), ycat[...], preferred_element_type=jnp.float32)
    else:
        lane = lax.broadcasted_iota(jnp.int32, (1, LANES), 1)
        per = LANES // ch
        sts = []
        for e in range(N_EXPERTS):
            s0 = cs_ref[(b * N_EXPERTS + e) * (ntile + 1) + t]
            st = jnp.minimum((s0 // BF16_TILE_ROWS) * BF16_TILE_ROWS, cap - ch)
            st = pl.multiple_of(st, BF16_TILE_ROWS)
            sts.append(st)
            ycat[e * ch:(e + 1) * ch, :] = y_ref[e, pl.ds(st, ch), :]
        tiles = []
        for p in range(N_EXPERTS // per):
            v = None
            for q in range(per):
                e = p * per + q
                r = pltpu.roll(idx_ref[0, e:e + 1, :], (2 * cap - sts[e] + q * ch) % cap, 1)[:, :LANES]
                v = r if v is None else jnp.where(lane >= q * ch, r, v)
            tiles.append((v == sub).astype(jnp.bfloat16))
        acc[...] = jnp.dot(jnp.concatenate(tiles, axis=1), ycat[...], preferred_element_type=jnp.float32)
        sub_c = lax.broadcasted_iota(jnp.int32, (TOK_TILE, ch), 0) + base
        lane_c = lax.broadcasted_iota(jnp.int32, (1, ch), 1)
        for e in range(N_EXPERTS):
            s1 = cs_ref[(b * N_EXPERTS + e) * (ntile + 1) + t + 1]
            first_end = sts[e] + ch
            n_extra = jnp.maximum(s1 - first_end + ch - 1, 0) // ch

            def extra(q, carry, e=e, first_end=first_end):
                lo = first_end + q * ch
                stq = pl.multiple_of(jnp.minimum(lo, cap - ch), BF16_TILE_ROWS)
                r = pltpu.roll(idx_ref[0, e:e + 1, :], (2 * cap - stq) % cap, 1)[:, :ch]
                hit = (r == sub_c) & (lane_c + stq >= lo)
                acc[...] += jnp.dot(hit.astype(jnp.bfloat16), y_ref[e, pl.ds(stq, ch), :],
                                    preferred_element_type=jnp.float32)
                return carry
            lax.fori_loop(0, n_extra, extra, 0)
    o_ref[0] = x_ref[0] + g2_ref[0] * acc[...]


def _moe_combine(x, g2, y, idx, cs):
    b, length, d = x.shape
    cap = idx.shape[2]
    ntile = length // TOK_TILE
    ch = min(SEG_CHUNK, cap)
    if ntile == 1:
        idx_in = idx.reshape(b, 1, N_EXPERTS * cap)
        idx_spec = pl.BlockSpec((1, 1, N_EXPERTS * cap), lambda i, j, c: (i, 0, 0))
    else:
        idx_in = idx
        idx_spec = pl.BlockSpec((1, N_EXPERTS, cap), lambda i, j, c: (i, 0, 0))
    grid_spec = pltpu.PrefetchScalarGridSpec(
        num_scalar_prefetch=1,
        grid=(b, ntile),
        in_specs=[pl.BlockSpec((1, TOK_TILE, d), lambda i, j, c: (i, j, 0)),
                  pl.BlockSpec((1, 1, d), lambda i, j, c: (i, 0, 0)),
                  pl.BlockSpec((N_EXPERTS, cap, d), lambda i, j, c: (0, i, 0)),
                  idx_spec],
        out_specs=pl.BlockSpec((1, TOK_TILE, d), lambda i, j, c: (i, j, 0)),
        scratch_shapes=[pltpu.VMEM((N_EXPERTS * ch, d), jnp.bfloat16),
                        pltpu.VMEM((TOK_TILE, d), jnp.float32)])
    return pl.pallas_call(
        functools.partial(_moe_comb_kernel, cap=cap, ch=ch, ntile=ntile),
        grid_spec=grid_spec,
        out_shape=jax.ShapeDtypeStruct((b, length, d), jnp.float32),
        compiler_params=pltpu.CompilerParams(
            dimension_semantics=("arbitrary", "arbitrary"), vmem_limit_bytes=56 * 1024 * 1024),
        name="moe_combine",
    )(cs.reshape(-1).astype(jnp.int32), x, g2, y, idx_in)


def _moe_block(x, m, g1, ng, sh, sc, g2, w_router, w_gate, w_up, w_down):
    b, length, d = x.shape
    cap = EC_FACTOR * length // N_EXPERTS
    x1, hpk, lg = _moe_pre(x, m, g1, ng, sh, sc, w_router)
    aff = jax.nn.softmax(lg, axis=1)
    _, idx = lax.top_k(aff, cap)
    idx = jnp.sort(idx, axis=-1)
    gate = jnp.take_along_axis(aff, idx, axis=-1)
    ntile = length // TOK_TILE
    bounds = jnp.arange(ntile + 1, dtype=jnp.int32) * TOK_TILE
    cs = jnp.sum(idx[:, :, :, None] < bounds, axis=2, dtype=jnp.int32)
    grow = idx + (jnp.arange(b, dtype=jnp.int32) * length)[:, None, None]
    grow = jnp.swapaxes(grow, 0, 1).reshape(N_EXPERTS, b * cap)
    gate_e = jnp.swapaxes(gate, 0, 1).reshape(N_EXPERTS, b * cap, 1)
    y = _moe_ffn(hpk.reshape(b * length, d // 2), grow, gate_e, w_gate, w_up, w_down)
    return _moe_combine(x1, g2, y, idx, cs)


def kernel(x_prompt, x_sample, state_ssd, c, c_ctx, norm_g, ada_w, ada_b, hy_in_w, hy_in_b, hy_short_w, hy_short_b, hy_f_w1, hy_f_b1, hy_f_w2, hy_f_b2, hy_f_w3, hy_f_freq, hy_f_bias, hy_out_w, ssd_in_w, ssd_conv_w, ssd_conv_b, ssd_dt_bias, ssd_A_log, ssd_D, ssd_norm_g, ssd_out_w, moe_router, moe_w_gate, moe_w_up, moe_w_down, final_norm_g):
    rows = x_sample.shape[1] // GRID_W
    xp = x_prompt
    xs = x_sample + _sincos_2d(rows, GRID_W, D_MODEL)[None]
    new_ssd = []
    for i in range(DEPTH):
        sh1p, sc1p, g1p, sh2p, sc2p, g2p = _adaln(c_ctx[None, :], ada_w[i], ada_b[i])
        sh1s, sc1s, g1s, sh2s, sc2s, g2s = _adaln(c, ada_w[i], ada_b[i])
        j = i // N_MIXERS
        bp = (xp.shape[0], 1, D_MODEL)
        if i % N_MIXERS == 0:
            hp = _modulate(_rmsnorm(xp, norm_g[i, 0]), sh1p, sc1p)
            hs = _modulate(_rmsnorm(xs, norm_g[i, 0]), sh1s, sc1s)
            hy = (hy_in_w[j], hy_in_b[j], hy_short_w[j], hy_short_b[j], hy_f_w1[j], hy_f_b1[j],
                  hy_f_w2[j], hy_f_b2[j], hy_f_w3[j], hy_f_freq[j], hy_f_bias[j], hy_out_w[j])
            mp = _hyena_mixer(hp, *hy)
            ms = _hyena_mixer(hs, *hy)
        else:
            sp = (ssd_in_w[j], ssd_conv_w[j], ssd_conv_b[j], ssd_dt_bias[j], ssd_A_log[j],
                  ssd_D[j], ssd_norm_g[j], ssd_out_w[j])
            zeros = jnp.zeros((xp.shape[0], SSD_HEADS, SSD_HEAD_DIM, SSD_STATE), jnp.float32)
            mp, s_f, s_b = _ssd_mixer_p(xp, norm_g[i, 0], jnp.broadcast_to(sh1p, bp),
                                        jnp.broadcast_to(sc1p, bp), zeros, zeros, *sp)
            new_ssd.append(jnp.stack([s_f, s_b], axis=1))
            ms, _, _ = _ssd_mixer_p(xs, norm_g[i, 0], sh1s, sc1s, state_ssd[:, j, 0], state_ssd[:, j, 1], *sp)
        moe = (moe_router[i], moe_w_gate[i], moe_w_up[i], moe_w_down[i])
        xp = _moe_block(xp, mp, jnp.broadcast_to(g1p, bp), norm_g[i, 1], jnp.broadcast_to(sh2p, bp),
                        jnp.broadcast_to(sc2p, bp), jnp.broadcast_to(g2p, bp), *moe)
        xs = _moe_block(xs, ms, g1s, norm_g[i, 1], sh2s, sc2s, g2s, *moe)
    y_prompt = _rmsnorm(xp, final_norm_g)
    y_sample = _rmsnorm(xs, final_norm_g)
    new_state_ssd = jnp.stack(new_ssd, axis=1)
    return (y_prompt, y_sample, new_state_ssd)
```

```python
import functools
import math

import jax
import jax.numpy as jnp
import numpy as np
from jax import lax
from jax.experimental import pallas as pl
from jax.experimental.pallas import tpu as pltpu

D_MODEL = 1024
DEPTH = 2
GRID_W = 64
N_MIXERS = 2
RMS_EPS = 1e-6
HY_EMB = 33
HY_BANDS = (HY_EMB - 1) // 2
HY_SHORT_DECAY_FRAC = 0.3
HY_LONG_DECAY_FRAC = 1.5
HY_DECAY_TARGET = 1e-2
HY_MAX_DECAY = math.log(HY_DECAY_TARGET) / HY_SHORT_DECAY_FRAC
HY_MIN_DECAY = math.log(HY_DECAY_TARGET) / HY_LONG_DECAY_FRAC
SSD_D_INNER = 2 * D_MODEL
SSD_HEAD_DIM = 64
SSD_HEADS = SSD_D_INNER // SSD_HEAD_DIM
SSD_GROUPS = 4
SSD_STATE = 128
SSD_CHUNK = 128
SSD_XBC = SSD_D_INNER + 2 * SSD_GROUPS * SSD_STATE
N_EXPERTS = 16
EC_FACTOR = 2

VMEM_LIMIT_BYTES = 48 * 1024 * 1024


def _mm_kernel(a_ref, b_ref, o_ref, acc_ref):
    @pl.when(pl.program_id(2) == 0)
    def _():
        acc_ref[...] = jnp.zeros_like(acc_ref)

    acc_ref[...] += jnp.dot(a_ref[...].astype(jnp.bfloat16), b_ref[...],
                            preferred_element_type=jnp.float32)

    @pl.when(pl.program_id(2) == pl.num_programs(2) - 1)
    def _():
        o_ref[...] = acc_ref[...]


def _pick(n, pref):
    for t in pref:
        if n % t == 0:
            return t
    return n


def _mm(a, b):
    m, k = a.shape
    n = b.shape[1]
    mp = -(-m // 8) * 8
    if mp != m:
        a = jnp.pad(a, ((0, mp - m), (0, 0)))
    tm = _pick(mp, (512, 256, 128, 64, 32, 16, 8))
    tn = _pick(n, (512, 256, 128))
    tk = _pick(k, (1024, 512, 256, 128))
    out = pl.pallas_call(
        _mm_kernel,
        grid=(mp // tm, n // tn, k // tk),
        in_specs=[pl.BlockSpec((tm, tk), lambda i, j, l: (i, l)),
                  pl.BlockSpec((tk, tn), lambda i, j, l: (l, j))],
        out_specs=pl.BlockSpec((tm, tn), lambda i, j, l: (i, j)),
        out_shape=jax.ShapeDtypeStruct((mp, n), jnp.float32),
        scratch_shapes=[pltpu.VMEM((tm, tn), jnp.float32)],
        compiler_params=pltpu.CompilerParams(
            dimension_semantics=("parallel", "parallel", "arbitrary"),
            vmem_limit_bytes=VMEM_LIMIT_BYTES),
        name="mm",
    )(a, b.astype(jnp.bfloat16))
    return out[:m]


def _mm3(a, b):
    lead = a.shape[:-1]
    return _mm(a.reshape(-1, a.shape[-1]), b).reshape(*lead, b.shape[1])


def _bmm_kernel(a_ref, b_ref, o_ref):
    o_ref[0] = jnp.dot(a_ref[0].astype(jnp.bfloat16), b_ref[0],
                       preferred_element_type=jnp.float32)


def _bmm(a, b):
    e, m, k = a.shape
    n = b.shape[2]
    tm = _pick(m, (512, 256, 128, 64, 32, 16, 8))
    tn = _pick(n, (512, 256, 128))
    return pl.pallas_call(
        _bmm_kernel,
        grid=(e, m // tm, n // tn),
        in_specs=[pl.BlockSpec((1, tm, k), lambda g, i, j: (g, i, 0)),
                  pl.BlockSpec((1, k, tn), lambda g, i, j: (g, 0, j))],
        out_specs=pl.BlockSpec((1, tm, tn), lambda g, i, j: (g, i, j)),
        out_shape=jax.ShapeDtypeStruct((e, m, n), jnp.float32),
        compiler_params=pltpu.CompilerParams(
            dimension_semantics=("parallel", "parallel", "parallel"),
            vmem_limit_bytes=VMEM_LIMIT_BYTES),
        name="bmm",
    )(a, b.astype(jnp.bfloat16))


def _rmsnorm(x, g):
    y = x * lax.rsqrt(jnp.mean(x * x, axis=-1, keepdims=True) + RMS_EPS)
    return y * g


def _adaln(cond, ada_w, ada_b):
    m = jnp.dot(jax.nn.silu(cond), ada_w, precision=lax.Precision.HIGHEST) + ada_b
    return jnp.split(m[:, None, :], 6, axis=-1)


def _modulate(h, shift, scale):
    return h * (1 + scale) + shift


def _dwconv(x, w, b):
    k, ch = w.shape
    y = lax.conv_general_dilated(x, w[:, None, :], window_strides=(1,),
                                 padding=[(k // 2, k // 2)],
                                 dimension_numbers=('NWC', 'WIO', 'NWC'),
                                 feature_group_count=ch,
                                 precision=lax.Precision.HIGHEST)
    return y + b


def _sincos_2d(rows, cols, d):
    q = d // 4
    omega = 1.0 / (10000.0 ** (jnp.arange(q, dtype=jnp.float32) / q))
    t = jnp.arange(rows * cols)
    er = (t // cols).astype(jnp.float32)[:, None] * omega[None, :]
    ec = (t % cols).astype(jnp.float32)[:, None] * omega[None, :]
    return jnp.concatenate([jnp.sin(er), jnp.cos(er), jnp.sin(ec), jnp.cos(ec)], axis=-1)


def _hyena_filters(length, f_w1, f_b1, f_w2, f_b2, f_w3, f_freq):
    hp = lax.Precision.HIGHEST
    t = jnp.linspace(0.0, 1.0, length, dtype=jnp.float32)[:, None]
    w = 2.0 * math.pi * jnp.arange(length, dtype=jnp.float32)[:, None] / length
    f = jnp.linspace(1e-4, HY_BANDS - 1, HY_BANDS, dtype=jnp.float32)[None, :]
    z = jnp.concatenate([t, jnp.cos(f * w), -jnp.sin(f * w)], axis=-1)
    h = jnp.sin(f_freq * (jnp.dot(z, f_w1, precision=hp) + f_b1))
    h = jnp.sin(f_freq * (jnp.dot(h, f_w2, precision=hp) + f_b2))
    h = jnp.dot(h, f_w3, precision=hp)
    deltas = jnp.linspace(HY_MIN_DECAY, HY_MAX_DECAY, D_MODEL, dtype=jnp.float32)
    window = jnp.exp(-t * jnp.abs(deltas)[None, :])
    return h[:, :D_MODEL] * window, h[:, D_MODEL:] * window


def _bidir_long_conv(u, h_fwd, h_bwd, bias):
    b, length, ch = u.shape
    n = 2 * length
    k = jnp.concatenate([h_fwd, jnp.zeros((1, ch), jnp.float32), h_bwd[1:][::-1]], axis=0)
    kf = jnp.fft.rfft(k, n=n, axis=0)
    uf = jnp.fft.rfft(u, n=n, axis=1)
    y = jnp.fft.irfft(uf * kf[None], n=n, axis=1)[:, :length]
    return y + u * bias


def _hyena_mixer(h, in_w, in_b, short_w, short_b, f_w1, f_b1, f_w2, f_b2, f_w3, f_freq, f_bias, out_w):
    length = h.shape[1]
    u = _dwconv(_mm3(h, in_w) + in_b, short_w, short_b)
    x0 = u[..., :D_MODEL]
    x1 = u[..., D_MODEL:2 * D_MODEL]
    v = u[..., 2 * D_MODEL:]
    h_fwd, h_bwd = _hyena_filters(length, f_w1, f_b1, f_w2, f_b2, f_w3, f_freq)
    v = _bidir_long_conv(v * x1, h_fwd, h_bwd, f_bias)
    return _mm3(v * x0, out_w)


def _ssd_scan(x, dt, a, bm, cm, init):
    hp = lax.Precision.HIGHEST
    b, length = x.shape[:2]
    nc = length // SSD_CHUNK
    q, g, r = SSD_CHUNK, SSD_GROUPS, SSD_HEADS // SSD_GROUPS
    x = x.reshape(b, nc, q, g, r, SSD_HEAD_DIM)
    dt = dt.reshape(b, nc, q, g, r)
    bm = bm.reshape(b, nc, q, g, SSD_STATE)
    cm = cm.reshape(b, nc, q, g, SSD_STATE)
    a_cum = jnp.cumsum(dt * a.reshape(g, r), axis=2)
    xdt = x * dt[..., None]
    seg = a_cum[:, :, :, None] - a_cum[:, :, None, :]
    causal = jnp.tril(jnp.ones((q, q), dtype=bool))[None, None, :, :, None, None]
    decay = jnp.exp(jnp.where(causal, seg, -jnp.inf))
    cb = jnp.einsum('bclgn,bcsgn->bclsg', cm, bm, precision=hp)
    y_diag = jnp.einsum('bclsgr,bcsgrp->bclgrp', cb[..., None] * decay, xdt, precision=hp)
    decay_end = jnp.exp(a_cum[:, :, -1:] - a_cum)
    chunk_states = jnp.einsum('bcsgn,bcsgrp->bcgrpn', bm, xdt * decay_end[..., None], precision=hp)
    chunk_decay = jnp.exp(a_cum[:, :, -1])

    def step(s, inp):
        st, dec = inp
        return s * dec[..., None, None] + st, s

    init_g = init.astype(jnp.float32).reshape(b, g, r, SSD_HEAD_DIM, SSD_STATE)
    final, starts = lax.scan(step, init_g, (jnp.moveaxis(chunk_states, 1, 0), jnp.moveaxis(chunk_decay, 1, 0)))
    starts = jnp.moveaxis(starts, 0, 1)
    y_off = jnp.einsum('bclgn,bcgrpn->bclgrp', cm, starts, precision=hp) * jnp.exp(a_cum)[..., None]
    y = (y_diag + y_off).reshape(b, length, SSD_HEADS, SSD_HEAD_DIM)
    return y, final.reshape(b, SSD_HEADS, SSD_HEAD_DIM, SSD_STATE)


def _flip(t):
    return jnp.flip(t, axis=1)


def _ssd_mixer(h, init_f, init_b, in_w, conv_w, conv_b, dt_bias, a_log, d_skip, norm_g, out_w):
    b, length, _ = h.shape
    proj = _mm3(h, in_w)
    z = proj[..., :SSD_D_INNER]
    xbc = jax.nn.silu(_dwconv(proj[..., SSD_D_INNER:SSD_D_INNER + SSD_XBC], conv_w, conv_b))
    dt_raw = proj[..., SSD_D_INNER + SSD_XBC:]
    gn = SSD_GROUPS * SSD_STATE
    xh = xbc[..., :SSD_D_INNER].reshape(b, length, SSD_HEADS, SSD_HEAD_DIM)
    bm = xbc[..., SSD_D_INNER:SSD_D_INNER + gn].reshape(b, length, SSD_GROUPS, SSD_STATE)
    cm = xbc[..., SSD_D_INNER + gn:].reshape(b, length, SSD_GROUPS, SSD_STATE)
    dt = jax.nn.softplus(dt_raw.reshape(b, length, 2, SSD_HEADS) + dt_bias)
    a = -jnp.exp(a_log)
    y_f, s_f = _ssd_scan(xh, dt[:, :, 0], a[0], bm, cm, init_f)
    y_b, s_b = _ssd_scan(_flip(xh), _flip(dt[:, :, 1]), a[1], _flip(bm), _flip(cm), init_b)
    y = y_f + _flip(y_b) + d_skip[:, None] * xh
    y = y.reshape(b, length, SSD_D_INNER) * jax.nn.silu(z)
    y = _rmsnorm(y, norm_g)
    return _mm3(y, out_w), s_f, s_b


SSD_GN = SSD_GROUPS * SSD_STATE
SSD_GROUP_W = SSD_D_INNER // SSD_GROUPS
SSD_HEADS_PER_GROUP = SSD_HEADS // SSD_GROUPS
ROW_TILE = 256


def _modnorm(x, ng, sh, sc):
    ms = jnp.mean(x * x, axis=-1, keepdims=True)
    return (x * lax.rsqrt(ms + RMS_EPS) * ng) * (1.0 + sc) + sh


def _ssd_in_kernel(x_ref, ng_ref, sh_ref, sc_ref, w_ref, wdt_ref, wdtt_ref,
                   z_ref, xbc_ref, dt_ref, dtt_ref):
    h = _modnorm(x_ref[0], ng_ref[...], sh_ref[0], sc_ref[0]).astype(jnp.bfloat16)
    zx = jnp.dot(h, w_ref[...], preferred_element_type=jnp.float32)
    z_ref[0] = zx[:, :SSD_D_INNER]
    xbc_ref[0] = zx[:, SSD_D_INNER:]
    dt_ref[0] = jnp.dot(h, wdt_ref[...], preferred_element_type=jnp.float32)
    dtt_ref[0] = lax.dot_general(wdtt_ref[...], h, (((1,), (1,)), ((), ())),
                                 preferred_element_type=jnp.float32)


def _ssd_in(x, ng, sh, sc, in_w):
    b, length, d = x.shape
    tm = min(length, ROW_TILE)
    nzx = SSD_D_INNER + SSD_XBC
    w = in_w[:, :nzx].astype(jnp.bfloat16)
    wdt = in_w[:, nzx:]
    wdt_p = jnp.pad(wdt, ((0, 0), (0, LANES - 2 * SSD_HEADS))).astype(jnp.bfloat16)
    wdt_t = wdt.T.astype(jnp.bfloat16)
    row = lambda i, j: (i, j, 0)
    per_b = lambda i, j: (i, 0, 0)
    full2 = lambda i, j: (0, 0)
    return pl.pallas_call(
        _ssd_in_kernel,
        grid=(b, length // tm),
        in_specs=[pl.BlockSpec((1, tm, d), row), pl.BlockSpec((1, d), full2),
                  pl.BlockSpec((1, 1, d), per_b), pl.BlockSpec((1, 1, d), per_b),
                  pl.BlockSpec((d, nzx), full2), pl.BlockSpec((d, LANES), full2),
                  pl.BlockSpec((2 * SSD_HEADS, d), full2)],
        out_specs=[pl.BlockSpec((1, tm, SSD_D_INNER), row), pl.BlockSpec((1, tm, SSD_XBC), row),
                   pl.BlockSpec((1, tm, LANES), row),
                   pl.BlockSpec((1, 2 * SSD_HEADS, tm), lambda i, j: (i, 0, j))],
        out_shape=[jax.ShapeDtypeStruct((b, length, SSD_D_INNER), jnp.float32),
                   jax.ShapeDtypeStruct((b, length, SSD_XBC), jnp.float32),
                   jax.ShapeDtypeStruct((b, length, LANES), jnp.float32),
                   jax.ShapeDtypeStruct((b, 2 * SSD_HEADS, length), jnp.float32)],
        compiler_params=pltpu.CompilerParams(
            dimension_semantics=("parallel", "parallel"), vmem_limit_bytes=VMEM_LIMIT_BYTES),
        name="ssd_in",
    )(x, ng.reshape(1, d), sh, sc, w, wdt_p, wdt_t)


CONV_PAD = 8
CONV_ROWS = 256
CONV_COLS = 256


def _dwconv_kernel(x_ref, w_ref, b_ref, o_ref, pad_ref, *, taps, silu):
    length = x_ref.shape[1]
    zeros = jnp.zeros((CONV_PAD, x_ref.shape[2]), jnp.float32)
    pad_ref[0:CONV_PAD, :] = zeros
    pad_ref[CONV_PAD + length:, :] = zeros
    pad_ref[CONV_PAD:CONV_PAD + length, :] = x_ref[0]
    rows = min(CONV_ROWS, length)
    for r in range(0, length, rows):
        acc = b_ref[...] + jnp.zeros((rows, x_ref.shape[2]), jnp.float32)
        for k in range(taps):
            off = r + CONV_PAD + k - taps // 2
            acc = acc + w_ref[k:k + 1, :] * pad_ref[off:off + rows, :]
        if silu:
            acc = acc * jax.nn.sigmoid(acc)
        o_ref[0, r:r + rows, :] = acc


def _dwconv_p(x, w, b, silu):
    bsz, length, ch = x.shape
    taps = w.shape[0]
    return pl.pallas_call(
        functools.partial(_dwconv_kernel, taps=taps, silu=silu),
        grid=(bsz, ch // CONV_COLS),
        in_specs=[pl.BlockSpec((1, length, CONV_COLS), lambda i, j: (i, 0, j)),
                  pl.BlockSpec((taps, CONV_COLS), lambda i, j: (0, j)),
                  pl.BlockSpec((1, CONV_COLS), lambda i, j: (0, j))],
        out_specs=pl.BlockSpec((1, length, CONV_COLS), lambda i, j: (i, 0, j)),
        out_shape=jax.ShapeDtypeStruct((bsz, length, ch), jnp.float32),
        scratch_shapes=[pltpu.VMEM((length + 2 * CONV_PAD, CONV_COLS), jnp.float32)],
        compiler_params=pltpu.CompilerParams(
            dimension_semantics=("parallel", "parallel"), vmem_limit_bytes=VMEM_LIMIT_BYTES),
        name="dwconv",
    )(x, w, b.reshape(1, ch))


def _split3_bf16(v):
    p1 = v.astype(jnp.bfloat16)
    r1 = v - p1.astype(jnp.float32)
    p2 = r1.astype(jnp.bfloat16)
    p3 = (r1 - p2.astype(jnp.float32)).astype(jnp.bfloat16)
    return p1, p2, p3


def _softplus(v):
    return jnp.maximum(v, 0.0) + jnp.log1p(jnp.exp(-jnp.abs(v)))


def _expand_heads(cols, g):
    q = cols.shape[0]
    lane = lax.broadcasted_iota(jnp.int32, (q, LANES), 1)
    tiles = []
    for k in range(SSD_HEADS_PER_GROUP // 2):
        ha = g * SSD_HEADS_PER_GROUP + 2 * k
        ca = jnp.broadcast_to(cols[:, ha:ha + 1], (q, LANES))
        cb = jnp.broadcast_to(cols[:, ha + 1:ha + 2], (q, LANES))
        tiles.append(jnp.where(lane < SSD_HEAD_DIM, ca, cb))
    return jnp.concatenate(tiles, axis=1)


def _ssd_scan_kernel(x_ref, b_ref, c_ref, dt_ref, dtt_ref, dtb_ref, dtbt_ref, a_ref, at_ref, init_ref,
                     y_ref, fin_ref, st_ref, *, reverse, hoff):
    ci = pl.program_id(1)

    @pl.when(ci == 0)
    def _():
        st_ref[...] = init_ref[0]

    q = SSD_CHUNK
    f32, bf16 = jnp.float32, jnp.bfloat16
    dt = _softplus(dt_ref[0][:, hoff:hoff + SSD_HEADS] + dtb_ref[...])
    dtt = _softplus(dtt_ref[0][hoff:hoff + SSD_HEADS, :] + dtbt_ref[...])
    ri = lax.broadcasted_iota(jnp.int32, (q, q), 0)
    cj = lax.broadcasted_iota(jnp.int32, (q, q), 1)
    keep = (cj >= ri) if reverse else (cj <= ri)
    tri = keep.astype(bf16)
    tri_t = ((ri >= cj) if reverse else (ri <= cj)).astype(bf16)
    acum = sum(jnp.dot(tri, p, preferred_element_type=f32) for p in _split3_bf16(dt * a_ref[...]))
    acum_t = sum(jnp.dot(p, tri_t, preferred_element_type=f32) for p in _split3_bf16(dtt * at_ref[...]))
    end = 0 if reverse else q - 1
    a_end = acum[end:end + 1, :]
    eacum = jnp.exp(acum)
    dec_end = jnp.exp(a_end - acum)
    lane = lax.broadcasted_iota(jnp.int32, (q, LANES), 1)
    for g in range(SSD_GROUPS):
        cg = c_ref[0][:, g * SSD_STATE:(g + 1) * SSD_STATE]
        bg = b_ref[0][:, g * SSD_STATE:(g + 1) * SSD_STATE]
        cg16 = cg.astype(bf16)
        cb = lax.dot_general(cg16, bg.astype(bf16), (((1,), (1,)), ((), ())), preferred_element_type=f32)
        xg = x_ref[0][:, g * SSD_GROUP_W:(g + 1) * SSD_GROUP_W]
        xdt = xg * _expand_heads(dt, g)
        xdt16 = xdt.astype(bf16)
        eac_x = _expand_heads(eacum, g)
        yd = []
        for k in range(SSD_HEADS_PER_GROUP // 2):
            xp = xdt16[:, k * LANES:(k + 1) * LANES]
            ys = []
            for hh in range(2):
                h = g * SSD_HEADS_PER_GROUP + 2 * k + hh
                seg = acum[:, h:h + 1] - acum_t[h:h + 1, :]
                lmat = jnp.exp(jnp.where(keep, seg, -jnp.inf))
                ys.append(jnp.dot((cb * lmat).astype(bf16), xp, preferred_element_type=f32))
            yd.append(jnp.where(lane < SSD_HEAD_DIM, ys[0], ys[1]))
        st = st_ref[g]
        y_off = jnp.dot(cg16, st.astype(bf16), preferred_element_type=f32) * eac_x
        y_ref[0, :, g * SSD_GROUP_W:(g + 1) * SSD_GROUP_W] = jnp.concatenate(yd, axis=1) + y_off
        xdd16 = (xdt * _expand_heads(dec_end, g)).astype(bf16)
        st_ref[g] = st * eac_x[end:end + 1, :] + jnp.dot(bg.T.astype(bf16), xdd16, preferred_element_type=f32)

    @pl.when(ci == pl.num_programs(1) - 1)
    def _():
        fin_ref[0] = st_ref[...]


def _ssd_scan_p(xbc, dt_raw, dt_raw_t, dt_bias, a, init, reverse, direction):
    b, length, _ = xbc.shape
    nc = length // SSD_CHUNK
    q = SSD_CHUNK
    cidx = (lambda j: nc - 1 - j) if reverse else (lambda j: j)
    nb = SSD_D_INNER // SSD_GN
    hoff = direction * SSD_HEADS
    full2 = lambda i, j: (0, 0)
    st_shape = (SSD_GROUPS, SSD_STATE, SSD_GROUP_W)
    return pl.pallas_call(
        functools.partial(_ssd_scan_kernel, reverse=reverse, hoff=hoff),
        grid=(b, nc),
        in_specs=[pl.BlockSpec((1, q, SSD_D_INNER), lambda i, j: (i, cidx(j), 0)),
                  pl.BlockSpec((1, q, SSD_GN), lambda i, j: (i, cidx(j), nb)),
                  pl.BlockSpec((1, q, SSD_GN), lambda i, j: (i, cidx(j), nb + 1)),
                  pl.BlockSpec((1, q, LANES), lambda i, j: (i, cidx(j), 0)),
                  pl.BlockSpec((1, 2 * SSD_HEADS, q), lambda i, j: (i, 0, cidx(j))),
                  pl.BlockSpec((1, SSD_HEADS), full2), pl.BlockSpec((SSD_HEADS, 1), full2),
                  pl.BlockSpec((1, SSD_HEADS), full2), pl.BlockSpec((SSD_HEADS, 1), full2),
                  pl.BlockSpec((1,) + st_shape, lambda i, j: (i, 0, 0, 0))],
        out_specs=[pl.BlockSpec((1, q, SSD_D_INNER), lambda i, j: (i, cidx(j), 0)),
                   pl.BlockSpec((1,) + st_shape, lambda i, j: (i, 0, 0, 0))],
        out_shape=[jax.ShapeDtypeStruct((b, length, SSD_D_INNER), jnp.float32),
                   jax.ShapeDtypeStruct((b,) + st_shape, jnp.float32)],
        scratch_shapes=[pltpu.VMEM(st_shape, jnp.float32)],
        compiler_params=pltpu.CompilerParams(
            dimension_semantics=("parallel", "arbitrary"), vmem_limit_bytes=VMEM_LIMIT_BYTES),
        name="ssd_scan",
    )(xbc, xbc, xbc, dt_raw, dt_raw_t, dt_bias.reshape(1, -1), dt_bias.reshape(-1, 1),
      a.reshape(1, -1), a.reshape(-1, 1), init)


def _ssd_out_kernel(yf_ref, yb_ref, x_ref, z_ref, dsk_ref, ng_ref, w_ref, o_ref):
    z = z_ref[0]
    y = (yf_ref[0] + yb_ref[0] + dsk_ref[...] * x_ref[0]) * (z * jax.nn.sigmoid(z))
    ms = jnp.mean(y * y, axis=-1, keepdims=True)
    y = y * lax.rsqrt(ms + RMS_EPS) * ng_ref[...]
    o_ref[0] = jnp.dot(y.astype(jnp.bfloat16), w_ref[...], preferred_element_type=jnp.float32)


def _ssd_out(yf, yb, xbc, z, d_skip, norm_g, out_w):
    b, length, di = yf.shape
    d = out_w.shape[1]
    tm = min(length, ROW_TILE)
    row = lambda i, j: (i, j, 0)
    full2 = lambda i, j: (0, 0)
    dsk = jnp.repeat(d_skip, SSD_HEAD_DIM).reshape(1, di)
    return pl.pallas_call(
        _ssd_out_kernel,
        grid=(b, length // tm),
        in_specs=[pl.BlockSpec((1, tm, di), row), pl.BlockSpec((1, tm, di), row),
                  pl.BlockSpec((1, tm, di), row), pl.BlockSpec((1, tm, di), row),
                  pl.BlockSpec((1, di), full2), pl.BlockSpec((1, di), full2),
                  pl.BlockSpec((di, d), full2)],
        out_specs=pl.BlockSpec((1, tm, d), row),
        out_shape=jax.ShapeDtypeStruct((b, length, d), jnp.float32),
        compiler_params=pltpu.CompilerParams(
            dimension_semantics=("parallel", "parallel"), vmem_limit_bytes=VMEM_LIMIT_BYTES),
        name="ssd_out",
    )(yf, yb, xbc, z, dsk, norm_g.reshape(1, di), out_w.astype(jnp.bfloat16))


def _state_to_kernel(s):
    b = s.shape[0]
    s = s.reshape(b, SSD_GROUPS, SSD_HEADS_PER_GROUP, SSD_HEAD_DIM, SSD_STATE)
    return jnp.transpose(s, (0, 1, 4, 2, 3)).reshape(b, SSD_GROUPS, SSD_STATE, SSD_GROUP_W)


def _state_from_kernel(s):
    b = s.shape[0]
    s = s.reshape(b, SSD_GROUPS, SSD_STATE, SSD_HEADS_PER_GROUP, SSD_HEAD_DIM)
    return jnp.transpose(s, (0, 1, 3, 4, 2)).reshape(b, SSD_HEADS, SSD_HEAD_DIM, SSD_STATE)


def _ssd_mixer_p(x, ng, sh, sc, init_f, init_b, in_w, conv_w, conv_b, dt_bias, a_log, d_skip, norm_g, out_w):
    z, xbc_raw, dt_raw, dt_raw_t = _ssd_in(x, ng, sh, sc, in_w)
    xbc = _dwconv_p(xbc_raw, conv_w, conv_b, silu=True)
    a = -jnp.exp(a_log)
    yf, s_f = _ssd_scan_p(xbc, dt_raw, dt_raw_t, dt_bias[0], a[0], _state_to_kernel(init_f), False, 0)
    yb, s_b = _ssd_scan_p(xbc, dt_raw, dt_raw_t, dt_bias[1], a[1], _state_to_kernel(init_b), True, 1)
    m = _ssd_out(yf, yb, xbc, z, d_skip, norm_g, out_w)
    return m, _state_from_kernel(s_f), _state_from_kernel(s_b)


HY_BLOCK = 256
HY_NFFT = 2 * HY_BLOCK
HY_CC = 128
HY_MAC_ROWS = 32
HY_HIDDEN = 64
HY_FEAT_ROWS = 64


def _odd_dft_tables():
    n = HY_NFFT
    m = np.arange(n, dtype=np.int64)[:, None]
    f = np.arange(n // 2, dtype=np.int64)[None, :]
    ang = 2.0 * np.pi * (((2 * f + 1) * m) % (2 * n)).astype(np.float64) / (2 * n)
    return np.cos(ang), np.sin(ang)


def _hy_in_kernel(x_ref, ng_ref, sh_ref, sc_ref, w_ref, b_ref, o_ref):
    h = _modnorm(x_ref[0], ng_ref[...], sh_ref[0], sc_ref[0]).astype(jnp.bfloat16)
    o_ref[0] = jnp.dot(h, w_ref[...], preferred_element_type=jnp.float32) + b_ref[...]


def _hy_in(x, ng, sh, sc, in_w, in_b):
    b, length, d = x.shape
    n = in_w.shape[1]
    tm = min(length, ROW_TILE)
    row = lambda i, j: (i, j, 0)
    per_b = lambda i, j: (i, 0, 0)
    full2 = lambda i, j: (0, 0)
    return pl.pallas_call(
        _hy_in_kernel,
        grid=(b, length // tm),
        in_specs=[pl.BlockSpec((1, tm, d), row), pl.BlockSpec((1, d), full2),
                  pl.BlockSpec((1, 1, d), per_b), pl.BlockSpec((1, 1, d), per_b),
                  pl.BlockSpec((d, n), full2), pl.BlockSpec((1, n), full2)],
        out_specs=pl.BlockSpec((1, tm, n), row),
        out_shape=jax.ShapeDtypeStruct((b, length, n), jnp.float32),
        compiler_params=pltpu.CompilerParams(
            dimension_semantics=("parallel", "parallel"), vmem_limit_bytes=VMEM_LIMIT_BYTES),
        name="hy_in",
    )(x, ng.reshape(1, d), sh, sc, in_w.astype(jnp.bfloat16), in_b.reshape(1, n))


def _dot3(a, b):
    a_hi, a_lo = _split_bf16(a)
    b_hi, b_lo = _split_bf16(b)
    f32 = jnp.float32
    return (jnp.dot(a_hi, b_hi, preferred_element_type=f32) + jnp.dot(a_lo, b_hi, preferred_element_type=f32)
            + jnp.dot(a_hi, b_lo, preferred_element_type=f32))


def _hy_filter_kernel(w1t_ref, b1_ref, w2t_ref, b2_ref, w3t_ref, fr_ref, dl_ref, o_ref, *, length):
    k = pl.program_id(0)
    q = (lax.broadcasted_iota(jnp.int32, (1, HY_BLOCK), 1) + k * HY_BLOCK)
    pos = jnp.abs(q - length).astype(jnp.float32)
    t = pos / float(length - 1)
    w = (2.0 * math.pi / length) * pos
    band = lax.broadcasted_iota(jnp.int32, (HY_BANDS, 1), 0).astype(jnp.float32)
    fb = 1e-4 + band * ((HY_BANDS - 1 - 1e-4) / (HY_BANDS - 1))
    z = jnp.concatenate([jnp.broadcast_to(t, (8, HY_BLOCK)), jnp.cos(fb * w), -jnp.sin(fb * w),
                         jnp.zeros((HY_FEAT_ROWS - 8 - 2 * HY_BANDS, HY_BLOCK), jnp.float32)], axis=0)
    h = jnp.sin(fr_ref[...] * (_dot3(w1t_ref[...], z) + b1_ref[...]))
    h = jnp.sin(fr_ref[...] * (_dot3(w2t_ref[...], h) + b2_ref[...]))
    kt = _dot3(w3t_ref[0], h)
    o_ref[0] = kt * jnp.exp(-t * dl_ref[...])


def _hy_filter(length, f_w1, f_b1, f_w2, f_b2, f_w3, f_freq):
    d = f_w3.shape[1] // 2
    nk = 2 * length // HY_BLOCK
    w1t = jnp.concatenate([f_w1[0:1].T, jnp.zeros((HY_HIDDEN, 7), jnp.float32), f_w1[1:].T,
                           jnp.zeros((HY_HIDDEN, HY_FEAT_ROWS - 8 - 2 * HY_BANDS), jnp.float32)], axis=1)
    w3t = jnp.stack([f_w3[:, d:].T, f_w3[:, :d].T])
    deltas = jnp.abs(jnp.linspace(HY_MIN_DECAY, HY_MAX_DECAY, d, dtype=jnp.float32)).reshape(d, 1)
    col = lambda v: v.reshape(HY_HIDDEN, 1)
    full2 = lambda k: (0, 0)
    half = length // HY_BLOCK
    return pl.pallas_call(
        functools.partial(_hy_filter_kernel, length=length),
        grid=(nk,),
        in_specs=[pl.BlockSpec((HY_HIDDEN, HY_FEAT_ROWS), full2), pl.BlockSpec((HY_HIDDEN, 1), full2),
                  pl.BlockSpec((HY_HIDDEN, HY_HIDDEN), full2), pl.BlockSpec((HY_HIDDEN, 1), full2),
                  pl.BlockSpec((1, d, HY_HIDDEN), lambda k: (k // half, 0, 0)),
                  pl.BlockSpec((HY_HIDDEN, 1), full2), pl.BlockSpec((d, 1), full2)],
        out_specs=pl.BlockSpec((1, d, HY_BLOCK), lambda k: (k, 0, 0)),
        out_shape=jax.ShapeDtypeStruct((nk, d, HY_BLOCK), jnp.float32),
        compiler_params=pltpu.CompilerParams(
            dimension_semantics=("parallel",), vmem_limit_bytes=VMEM_LIMIT_BYTES),
        name="hy_filter",
    )(w1t, col(f_b1), f_w2.T, col(f_b2), w3t, col(f_freq), deltas)


def _hy_gspec_kernel(hi_ref, lo_ref, ft_ref, fb_ref, o_ref):
    o_ref[0] = _dot3(hi_ref[0], ft_ref[...]) + _dot3(lo_ref[0], fb_ref[...])


def _hy_gspec(kt):
    nk, d, _ = kt.shape
    cos, sin = _odd_dft_tables()
    top = np.concatenate([cos[:HY_BLOCK], -sin[:HY_BLOCK]], axis=1)
    bot = -np.concatenate([cos[HY_BLOCK:], -sin[HY_BLOCK:]], axis=1)
    bot[0] = 0.0
    tm = 512
    full2 = lambda e, i: (0, 0)
    return pl.pallas_call(
        _hy_gspec_kernel,
        grid=(nk - 1, d // tm),
        in_specs=[pl.BlockSpec((1, tm, HY_BLOCK), lambda e, i: (e + 1, i, 0)),
                  pl.BlockSpec((1, tm, HY_BLOCK), lambda e, i: (e, i, 0)),
                  pl.BlockSpec((HY_BLOCK, HY_NFFT), full2), pl.BlockSpec((HY_BLOCK, HY_NFFT), full2)],
        out_specs=pl.BlockSpec((1, tm, HY_NFFT), lambda e, i: (e, i, 0)),
        out_shape=jax.ShapeDtypeStruct((nk - 1, d, HY_NFFT), jnp.float32),
        compiler_params=pltpu.CompilerParams(
            dimension_semantics=("parallel", "parallel"), vmem_limit_bytes=VMEM_LIMIT_BYTES),
        name="hy_gspec",
    )(kt, kt, jnp.asarray(top, jnp.float32), jnp.asarray(bot, jnp.float32))


def _hy_conv_kernel(x0_ref, x1_ref, v_ref, g_ref, fb_ref, ff_ref, fi_ref, o_ref, lhs_ref, u_ref, y_ref, *, nb):
    cc, bsz = HY_CC, HY_BLOCK
    for j in range(nb):
        sl = slice(j * bsz, (j + 1) * bsz)
        wj = v_ref[0, sl, :] * x1_ref[0, sl, :]
        lhs_ref[j * cc:(j + 1) * cc, :] = wj.T.astype(jnp.bfloat16)
    u_ref[...] = jnp.dot(lhs_ref[...], ff_ref[...], preferred_element_type=jnp.float32)

    def per_out_block(i, carry):
        def per_rows(rc, carry2):
            rows = pl.ds(pl.multiple_of(rc * HY_MAC_ROWS, HY_MAC_ROWS), HY_MAC_ROWS)
            acc_r = jnp.zeros((HY_MAC_ROWS, bsz), jnp.float32)
            acc_i = jnp.zeros((HY_MAC_ROWS, bsz), jnp.float32)
            for j in range(nb):
                e = i - j + (nb - 1)
                gr = g_ref[e, rows, 0:bsz]
                gi = g_ref[e, rows, bsz:2 * bsz]
                urows = pl.ds(pl.multiple_of(j * cc + rc * HY_MAC_ROWS, HY_MAC_ROWS), HY_MAC_ROWS)
                ur = u_ref[urows, 0:bsz]
                ui = u_ref[urows, bsz:2 * bsz]
                acc_r = acc_r + gr * ur - gi * ui
                acc_i = acc_i + gr * ui + gi * ur
            yrows = pl.ds(pl.multiple_of(i * cc + rc * HY_MAC_ROWS, HY_MAC_ROWS), HY_MAC_ROWS)
            y_ref[yrows, 0:bsz] = acc_r.astype(jnp.bfloat16)
            y_ref[yrows, bsz:2 * bsz] = acc_i.astype(jnp.bfloat16)
            return carry2
        return lax.fori_loop(0, cc // HY_MAC_ROWS, per_rows, carry)
    lax.fori_loop(0, nb, per_out_block, 0)

    yt = jnp.dot(y_ref[...], fi_ref[...], preferred_element_type=jnp.float32)
    for i in range(nb):
        sl = slice(i * bsz, (i + 1) * bsz)
        w = v_ref[0, sl, :] * x1_ref[0, sl, :]
        o_ref[0, sl, :] = (yt[i * cc:(i + 1) * cc, :].T + fb_ref[...] * w) * x0_ref[0, sl, :]


def _hy_conv(u, g, f_bias):
    b, length, d3 = u.shape
    d = d3 // 3
    nb = length // HY_BLOCK
    ncb = d // HY_CC
    cos, sin = _odd_dft_tables()
    fwd = np.concatenate([cos[:HY_BLOCK], -sin[:HY_BLOCK]], axis=1)
    inv = (2.0 / HY_NFFT) * np.concatenate([cos[:HY_BLOCK].T, -sin[:HY_BLOCK].T], axis=0)
    col = lambda off: pl.BlockSpec((1, length, HY_CC), lambda c, i, off=off: (i, 0, off + c))
    full2 = lambda c, i: (0, 0)
    return pl.pallas_call(
        functools.partial(_hy_conv_kernel, nb=nb),
        grid=(ncb, b),
        in_specs=[col(0), col(ncb), col(2 * ncb),
                  pl.BlockSpec((2 * nb - 1, HY_CC, HY_NFFT), lambda c, i: (0, c, 0)),
                  pl.BlockSpec((1, HY_CC), lambda c, i: (0, c)),
                  pl.BlockSpec((HY_BLOCK, HY_NFFT), full2), pl.BlockSpec((HY_NFFT, HY_BLOCK), full2)],
        out_specs=pl.BlockSpec((1, length, HY_CC), lambda c, i: (i, 0, c)),
        out_shape=jax.ShapeDtypeStruct((b, length, d), jnp.float32),
        scratch_shapes=[pltpu.VMEM((nb * HY_CC, HY_BLOCK), jnp.bfloat16),
                        pltpu.VMEM((nb * HY_CC, HY_NFFT), jnp.float32),
                        pltpu.VMEM((nb * HY_CC, HY_NFFT), jnp.bfloat16)],
        compiler_params=pltpu.CompilerParams(
            dimension_semantics=("parallel", "arbitrary"), vmem_limit_bytes=56 * 1024 * 1024),
        name="hy_conv",
    )(u, u, u, g, f_bias.reshape(1, d), jnp.asarray(fwd, jnp.bfloat16), jnp.asarray(inv, jnp.bfloat16))


def _hyena_mixer_p(x, ng, sh, sc, in_w, in_b, short_w, short_b, f_w1, f_b1, f_w2, f_b2, f_w3, f_freq, f_bias, out_w):
    length = x.shape[1]
    u = _dwconv_p(_hy_in(x, ng, sh, sc, in_w, in_b), short_w, short_b, silu=False)
    g = _hy_gspec(_hy_filter(length, f_w1, f_b1, f_w2, f_b2, f_w3, f_freq))
    return _mm3(_hy_conv(u, g, f_bias), out_w)


TOK_TILE = 256
MOE_ROWS = 512
MOE_HALVES = 2
SEG_CHUNK = 64
BF16_TILE_ROWS = 16
LANES = 128


def _split_bf16(w):
    hi = w.astype(jnp.bfloat16)
    lo = (w - hi.astype(jnp.float32)).astype(jnp.bfloat16)
    return hi, lo


def _moe_pre_kernel(x_ref, m_ref, g1_ref, ng_ref, sh_ref, sc_ref, wrh_ref, wrl_ref,
                    xo_ref, hpk_ref, lg_ref):
    x = x_ref[0] + g1_ref[0] * m_ref[0]
    xo_ref[0] = x
    ms = jnp.mean(x * x, axis=-1, keepdims=True)
    h = x * lax.rsqrt(ms + RMS_EPS) * ng_ref[...]
    h = h * (1.0 + sc_ref[0]) + sh_ref[0]
    h_hi = h.astype(jnp.bfloat16)
    h_lo = (h - h_hi.astype(jnp.float32)).astype(jnp.bfloat16)
    dn = (((1,), (1,)), ((), ()))
    lg = lax.dot_general(wrh_ref[...], h_hi, dn, preferred_element_type=jnp.float32)
    lg += lax.dot_general(wrh_ref[...], h_lo, dn, preferred_element_type=jnp.float32)
    lg += lax.dot_general(wrl_ref[...], h_hi, dn, preferred_element_type=jnp.float32)
    lg_ref[0] = lg
    half = h.shape[1] // 2
    wa = pltpu.bitcast(h_hi[:, :half].astype(jnp.float32), jnp.uint32) >> 16
    wb = pltpu.bitcast(h_hi[:, half:].astype(jnp.float32), jnp.uint32) & jnp.uint32(0xFFFF0000)
    hpk_ref[0] = wa | wb


def _moe_pre(x, m, g1, ng, sh, sc, w_router):
    b, length, d = x.shape
    tm = min(length, 512)
    wrh, wrl = _split_bf16(w_router.T)
    row = lambda i, j: (i, j, 0)
    per_b = lambda i, j: (i, 0, 0)
    full2 = lambda i, j: (0, 0)
    return pl.pallas_call(
        _moe_pre_kernel,
        grid=(b, length // tm),
        in_specs=[pl.BlockSpec((1, tm, d), row), pl.BlockSpec((1, tm, d), row),
                  pl.BlockSpec((1, 1, d), per_b), pl.BlockSpec((1, d), full2),
                  pl.BlockSpec((1, 1, d), per_b), pl.BlockSpec((1, 1, d), per_b),
                  pl.BlockSpec((N_EXPERTS, d), full2), pl.BlockSpec((N_EXPERTS, d), full2)],
        out_specs=[pl.BlockSpec((1, tm, d), row), pl.BlockSpec((1, tm, d // 2), row),
                   pl.BlockSpec((1, N_EXPERTS, tm), lambda i, j: (i, 0, j))],
        out_shape=[jax.ShapeDtypeStruct((b, length, d), jnp.float32),
                   jax.ShapeDtypeStruct((b, length, d // 2), jnp.uint32),
                   jax.ShapeDtypeStruct((b, N_EXPERTS, length), jnp.float32)],
        compiler_params=pltpu.CompilerParams(
            dimension_semantics=("parallel", "parallel"), vmem_limit_bytes=VMEM_LIMIT_BYTES),
        name="moe_pre",
    )(x, m, g1, ng.reshape(1, d), sh, sc, wrh, wrl)


def _moe_ffn_kernel(idx_ref, h_hbm, gate_ref, wg_ref, wu_ref, wd_ref, y_ref,
                    xe_ref, wgb, wub, wdb, sem):
    @pl.when(pl.program_id(1) == 0)
    def _():
        wgb[...] = wg_ref[0].astype(jnp.bfloat16)
        wub[...] = wu_ref[0].astype(jnp.bfloat16)
        wdb[...] = wd_ref[0].astype(jnp.bfloat16)

    rows = MOE_ROWS // MOE_HALVES
    for hf in range(MOE_HALVES):
        def issue(c, carry, hf=hf):
            r = idx_ref[0, 0, hf * rows + c]
            pltpu.make_async_copy(h_hbm.at[pl.ds(r, 1)], xe_ref.at[pl.ds(hf * rows + c, 1)],
                                  sem.at[hf]).start()
            return carry
        lax.fori_loop(0, rows, issue, 0, unroll=8)

    half = wgb.shape[0] // 2
    for hf in range(MOE_HALVES):
        sl = pl.ds(hf * rows, rows)
        pltpu.make_async_copy(h_hbm.at[pl.ds(0, rows)], xe_ref.at[sl], sem.at[hf]).wait()
        w = xe_ref[sl, :]
        xa = pltpu.bitcast(w << 16, jnp.float32).astype(jnp.bfloat16)
        xb = pltpu.bitcast(w & jnp.uint32(0xFFFF0000), jnp.float32).astype(jnp.bfloat16)
        hg = jnp.dot(xa, wgb[:half], preferred_element_type=jnp.float32)
        hg += jnp.dot(xb, wgb[half:], preferred_element_type=jnp.float32)
        hu = jnp.dot(xa, wub[:half], preferred_element_type=jnp.float32)
        hu += jnp.dot(xb, wub[half:], preferred_element_type=jnp.float32)
        hid = (hg * jax.nn.sigmoid(hg) * hu).astype(jnp.bfloat16)
        y = jnp.dot(hid, wdb[...], preferred_element_type=jnp.float32)
        y_ref[0, sl, :] = (y * gate_ref[0, sl, :]).astype(jnp.bfloat16)


def _moe_ffn(hpk, grow, gate, w_gate, w_up, w_down):
    e, r = grow.shape
    d, f = w_gate.shape[1], w_gate.shape[2]
    nblk = r // MOE_ROWS
    wspec = lambda shp: pl.BlockSpec((1,) + shp, lambda i, j: (i, 0, 0))
    return pl.pallas_call(
        _moe_ffn_kernel,
        grid=(e, nblk),
        in_specs=[pl.BlockSpec((1, 1, MOE_ROWS), lambda i, j: (i * nblk + j, 0, 0), memory_space=pltpu.SMEM),
                  pl.BlockSpec(memory_space=pltpu.HBM),
                  pl.BlockSpec((1, MOE_ROWS, 1), lambda i, j: (i, j, 0)),
                  wspec((d, f)), wspec((d, f)), wspec((f, d))],
        out_specs=pl.BlockSpec((1, MOE_ROWS, d), lambda i, j: (i, j, 0)),
        out_shape=jax.ShapeDtypeStruct((e, r, d), jnp.bfloat16),
        scratch_shapes=[pltpu.VMEM((MOE_ROWS, d // 2), jnp.uint32),
                        pltpu.VMEM((d, f), jnp.bfloat16), pltpu.VMEM((d, f), jnp.bfloat16),
                        pltpu.VMEM((f, d), jnp.bfloat16),
                        pltpu.SemaphoreType.DMA((MOE_HALVES,))],
        compiler_params=pltpu.CompilerParams(
            dimension_semantics=("arbitrary", "arbitrary"), vmem_limit_bytes=VMEM_LIMIT_BYTES),
        name="moe_ffn",
    )(grow.reshape(e * nblk, 1, MOE_ROWS), hpk, gate, w_gate, w_up, w_down)


def _moe_comb_kernel(cs_ref, x_ref, g2_ref, y_ref, idx_ref, o_ref, ycat, acc, *, cap, ch, ntile):
    b = pl.program_id(0)
    t = pl.program_id(1)
    base = t * TOK_TILE
    sub = lax.broadcasted_iota(jnp.int32, (TOK_TILE, LANES), 0) + base
    if ntile == 1:
        for e in range(N_EXPERTS):
            ycat[e * ch:(e + 1) * ch, :] = y_ref[e, 0:ch, :]
        v = idx_ref[0]
        tiles = [(v[:, p * LANES:(p + 1) * LANES] == sub).astype(jnp.bfloat16)
                 for p in range(N_EXPERTS * ch // LANES)]
        acc[...] = jnp.dot(jnp.concatenate(tiles, axis=1), ycat[...], preferred_element_type=jnp.float32)
    else:
        lane = lax.broadcasted_iota(jnp.int32, (1, LANES), 1)
        per = LANES // ch
        sts = []
        for e in range(N_EXPERTS):
            s0 = cs_ref[(b * N_EXPERTS + e) * (ntile + 1) + t]
            st = jnp.minimum((s0 // BF16_TILE_ROWS) * BF16_TILE_ROWS, cap - ch)
            st = pl.multiple_of(st, BF16_TILE_ROWS)
            sts.append(st)
            ycat[e * ch:(e + 1) * ch, :] = y_ref[e, pl.ds(st, ch), :]
        tiles = []
        for p in range(N_EXPERTS // per):
            v = None
            for q in range(per):
                e = p * per + q
                r = pltpu.roll(idx_ref[0, e:e + 1, :], (2 * cap - sts[e] + q * ch) % cap, 1)[:, :LANES]
                v = r if v is None else jnp.where(lane >= q * ch, r, v)
            tiles.append((v == sub).astype(jnp.bfloat16))
        acc[...] = jnp.dot(jnp.concatenate(tiles, axis=1), ycat[...], preferred_element_type=jnp.float32)
        sub_c = lax.broadcasted_iota(jnp.int32, (TOK_TILE, ch), 0) + base
        lane_c = lax.broadcasted_iota(jnp.int32, (1, ch), 1)
        for e in range(N_EXPERTS):
            s1 = cs_ref[(b * N_EXPERTS + e) * (ntile + 1) + t + 1]
            first_end = sts[e] + ch
            n_extra = jnp.maximum(s1 - first_end + ch - 1, 0) // ch

            def extra(q, carry, e=e, first_end=first_end):
                lo = first_end + q * ch
                stq = pl.multiple_of(jnp.minimum(lo, cap - ch), BF16_TILE_ROWS)
                r = pltpu.roll(idx_ref[0, e:e + 1, :], (2 * cap - stq) % cap, 1)[:, :ch]
                hit = (r == sub_c) & (lane_c + stq >= lo)
                acc[...] += jnp.dot(hit.astype(jnp.bfloat16), y_ref[e, pl.ds(stq, ch), :],
                                    preferred_element_type=jnp.float32)
                return carry
            lax.fori_loop(0, n_extra, extra, 0)
    o_ref[0] = x_ref[0] + g2_ref[0] * acc[...]


def _moe_combine(x, g2, y, idx, cs):
    b, length, d = x.shape
    cap = idx.shape[2]
    ntile = length // TOK_TILE
    ch = min(SEG_CHUNK, cap)
    if ntile == 1:
        idx_in = idx.reshape(b, 1, N_EXPERTS * cap)
        idx_spec = pl.BlockSpec((1, 1, N_EXPERTS * cap), lambda i, j, c: (i, 0, 0))
    else:
        idx_in = idx
        idx_spec = pl.BlockSpec((1, N_EXPERTS, cap), lambda i, j, c: (i, 0, 0))
    grid_spec = pltpu.PrefetchScalarGridSpec(
        num_scalar_prefetch=1,
        grid=(b, ntile),
        in_specs=[pl.BlockSpec((1, TOK_TILE, d), lambda i, j, c: (i, j, 0)),
                  pl.BlockSpec((1, 1, d), lambda i, j, c: (i, 0, 0)),
                  pl.BlockSpec((N_EXPERTS, cap, d), lambda i, j, c: (0, i, 0)),
                  idx_spec],
        out_specs=pl.BlockSpec((1, TOK_TILE, d), lambda i, j, c: (i, j, 0)),
        scratch_shapes=[pltpu.VMEM((N_EXPERTS * ch, d), jnp.bfloat16),
                        pltpu.VMEM((TOK_TILE, d), jnp.float32)])
    return pl.pallas_call(
        functools.partial(_moe_comb_kernel, cap=cap, ch=ch, ntile=ntile),
        grid_spec=grid_spec,
        out_shape=jax.ShapeDtypeStruct((b, length, d), jnp.float32),
        compiler_params=pltpu.CompilerParams(
            dimension_semantics=("arbitrary", "arbitrary"), vmem_limit_bytes=56 * 1024 * 1024),
        name="moe_combine",
    )(cs.reshape(-1).astype(jnp.int32), x, g2, y, idx_in)


def _moe_block(x, m, g1, ng, sh, sc, g2, w_router, w_gate, w_up, w_down):
    b, length, d = x.shape
    cap = EC_FACTOR * length // N_EXPERTS
    x1, hpk, lg = _moe_pre(x, m, g1, ng, sh, sc, w_router)
    aff = jax.nn.softmax(lg, axis=1)
    _, idx = lax.top_k(aff, cap)
    idx = jnp.sort(idx, axis=-1)
    gate = jnp.take_along_axis(aff, idx, axis=-1)
    ntile = length // TOK_TILE
    bounds = jnp.arange(ntile + 1, dtype=jnp.int32) * TOK_TILE
    cs = jnp.sum(idx[:, :, :, None] < bounds, axis=2, dtype=jnp.int32)
    grow = idx + (jnp.arange(b, dtype=jnp.int32) * length)[:, None, None]
    grow = jnp.swapaxes(grow, 0, 1).reshape(N_EXPERTS, b * cap)
    gate_e = jnp.swapaxes(gate, 0, 1).reshape(N_EXPERTS, b * cap, 1)
    y = _moe_ffn(hpk.reshape(b * length, d // 2), grow, gate_e, w_gate, w_up, w_down)
    return _moe_combine(x1, g2, y, idx, cs)


def kernel(x_prompt, x_sample, state_ssd, c, c_ctx, norm_g, ada_w, ada_b, hy_in_w, hy_in_b, hy_short_w, hy_short_b, hy_f_w1, hy_f_b1, hy_f_w2, hy_f_b2, hy_f_w3, hy_f_freq, hy_f_bias, hy_out_w, ssd_in_w, ssd_conv_w, ssd_conv_b, ssd_dt_bias, ssd_A_log, ssd_D, ssd_norm_g, ssd_out_w, moe_router, moe_w_gate, moe_w_up, moe_w_down, final_norm_g):
    rows = x_sample.shape[1] // GRID_W
    xp = x_prompt
    xs = x_sample + _sincos_2d(rows, GRID_W, D_MODEL)[None]
    new_ssd = []
    for i in range(DEPTH):
        sh1p, sc1p, g1p, sh2p, sc2p, g2p = _adaln(c_ctx[None, :], ada_w[i], ada_b[i])
        sh1s, sc1s, g1s, sh2s, sc2s, g2s = _adaln(c, ada_w[i], ada_b[i])
        j = i // N_MIXERS
        bp = (xp.shape[0], 1, D_MODEL)
        if i % N_MIXERS == 0:
            hy = (hy_in_w[j], hy_in_b[j], hy_short_w[j], hy_short_b[j], hy_f_w1[j], hy_f_b1[j],
                  hy_f_w2[j], hy_f_b2[j], hy_f_w3[j], hy_f_freq[j], hy_f_bias[j], hy_out_w[j])
            mp = _hyena_mixer_p(xp, norm_g[i, 0], jnp.broadcast_to(sh1p, bp), jnp.broadcast_to(sc1p, bp), *hy)
            ms = _hyena_mixer_p(xs, norm_g[i, 0], sh1s, sc1s, *hy)
        else:
            sp = (ssd_in_w[j], ssd_conv_w[j], ssd_conv_b[j], ssd_dt_bias[j], ssd_A_log[j],
                  ssd_D[j], ssd_norm_g[j], ssd_out_w[j])
            zeros = jnp.zeros((xp.shape[0], SSD_HEADS, SSD_HEAD_DIM, SSD_STATE), jnp.float32)
            mp, s_f, s_b = _ssd_mixer_p(xp, norm_g[i, 0], jnp.broadcast_to(sh1p, bp),
                                        jnp.broadcast_to(sc1p, bp), zeros, zeros, *sp)
            new_ssd.append(jnp.stack([s_f, s_b], axis=1))
            ms, _, _ = _ssd_mixer_p(xs, norm_g[i, 0], sh1s, sc1s, state_ssd[:, j, 0], state_ssd[:, j, 1], *sp)
        moe = (moe_router[i], moe_w_gate[i], moe_w_up[i], moe_w_down[i])
        xp = _moe_block(xp, mp, jnp.broadcast_to(g1p, bp), norm_g[i, 1], jnp.broadcast_to(sh2p, bp),
                        jnp.broadcast_to(sc2p, bp), jnp.broadcast_to(g2p, bp), *moe)
        xs = _moe_block(xs, ms, g1s, norm_g[i, 1], sh2s, sc2s, g2s, *moe)
    y_prompt = _rmsnorm(xp, final_norm_g)
    y_sample = _rmsnorm(xs, final_norm_g)
    new_state_ssd = jnp.stack(new_ssd, axis=1)
    return (y_prompt, y_sample, new_state_ssd)
```

```python
import functools
import math

import jax
import jax.numpy as jnp
import numpy as np
from jax import lax
from jax.experimental import pallas as pl
from jax.experimental.pallas import tpu as pltpu

D_MODEL = 1024
DEPTH = 2
GRID_W = 64
N_MIXERS = 2
RMS_EPS = 1e-6
HY_EMB = 33
HY_BANDS = (HY_EMB - 1) // 2
HY_SHORT_DECAY_FRAC = 0.3
HY_LONG_DECAY_FRAC = 1.5
HY_DECAY_TARGET = 1e-2
HY_MAX_DECAY = math.log(HY_DECAY_TARGET) / HY_SHORT_DECAY_FRAC
HY_MIN_DECAY = math.log(HY_DECAY_TARGET) / HY_LONG_DECAY_FRAC
SSD_D_INNER = 2 * D_MODEL
SSD_HEAD_DIM = 64
SSD_HEADS = SSD_D_INNER // SSD_HEAD_DIM
SSD_GROUPS = 4
SSD_STATE = 128
SSD_CHUNK = 128
SSD_XBC = SSD_D_INNER + 2 * SSD_GROUPS * SSD_STATE
N_EXPERTS = 16
EC_FACTOR = 2

VMEM_LIMIT_BYTES = 48 * 1024 * 1024


def _mm_kernel(a_ref, b_ref, o_ref, acc_ref):
    @pl.when(pl.program_id(2) == 0)
    def _():
        acc_ref[...] = jnp.zeros_like(acc_ref)

    acc_ref[...] += jnp.dot(a_ref[...].astype(jnp.bfloat16), b_ref[...],
                            preferred_element_type=jnp.float32)

    @pl.when(pl.program_id(2) == pl.num_programs(2) - 1)
    def _():
        o_ref[...] = acc_ref[...]


def _pick(n, pref):
    for t in pref:
        if n % t == 0:
            return t
    return n


def _mm(a, b):
    m, k = a.shape
    n = b.shape[1]
    mp = -(-m // 8) * 8
    if mp != m:
        a = jnp.pad(a, ((0, mp - m), (0, 0)))
    tm = _pick(mp, (512, 256, 128, 64, 32, 16, 8))
    tn = _pick(n, (512, 256, 128))
    tk = _pick(k, (1024, 512, 256, 128))
    out = pl.pallas_call(
        _mm_kernel,
        grid=(mp // tm, n // tn, k // tk),
        in_specs=[pl.BlockSpec((tm, tk), lambda i, j, l: (i, l)),
                  pl.BlockSpec((tk, tn), lambda i, j, l: (l, j))],
        out_specs=pl.BlockSpec((tm, tn), lambda i, j, l: (i, j)),
        out_shape=jax.ShapeDtypeStruct((mp, n), jnp.float32),
        scratch_shapes=[pltpu.VMEM((tm, tn), jnp.float32)],
        compiler_params=pltpu.CompilerParams(
            dimension_semantics=("parallel", "parallel", "arbitrary"),
            vmem_limit_bytes=VMEM_LIMIT_BYTES),
        name="mm",
    )(a, b.astype(jnp.bfloat16))
    return out[:m]


def _mm3(a, b):
    lead = a.shape[:-1]
    return _mm(a.reshape(-1, a.shape[-1]), b).reshape(*lead, b.shape[1])


def _bmm_kernel(a_ref, b_ref, o_ref):
    o_ref[0] = jnp.dot(a_ref[0].astype(jnp.bfloat16), b_ref[0],
                       preferred_element_type=jnp.float32)


def _bmm(a, b):
    e, m, k = a.shape
    n = b.shape[2]
    tm = _pick(m, (512, 256, 128, 64, 32, 16, 8))
    tn = _pick(n, (512, 256, 128))
    return pl.pallas_call(
        _bmm_kernel,
        grid=(e, m // tm, n // tn),
        in_specs=[pl.BlockSpec((1, tm, k), lambda g, i, j: (g, i, 0)),
                  pl.BlockSpec((1, k, tn), lambda g, i, j: (g, 0, j))],
        out_specs=pl.BlockSpec((1, tm, tn), lambda g, i, j: (g, i, j)),
        out_shape=jax.ShapeDtypeStruct((e, m, n), jnp.float32),
        compiler_params=pltpu.CompilerParams(
            dimension_semantics=("parallel", "parallel", "parallel"),
            vmem_limit_bytes=VMEM_LIMIT_BYTES),
        name="bmm",
    )(a, b.astype(jnp.bfloat16))


def _rmsnorm(x, g):
    y = x * lax.rsqrt(jnp.mean(x * x, axis=-1, keepdims=True) + RMS_EPS)
    return y * g


def _adaln(cond, ada_w, ada_b):
    m = jnp.dot(jax.nn.silu(cond), ada_w, precision=lax.Precision.HIGHEST) + ada_b
    return jnp.split(m[:, None, :], 6, axis=-1)


def _modulate(h, shift, scale):
    return h * (1 + scale) + shift


def _dwconv(x, w, b):
    k, ch = w.shape
    y = lax.conv_general_dilated(x, w[:, None, :], window_strides=(1,),
                                 padding=[(k // 2, k // 2)],
                                 dimension_numbers=('NWC', 'WIO', 'NWC'),
                                 feature_group_count=ch,
                                 precision=lax.Precision.HIGHEST)
    return y + b


def _sincos_2d(rows, cols, d):
    q = d // 4
    omega = 1.0 / (10000.0 ** (jnp.arange(q, dtype=jnp.float32) / q))
    t = jnp.arange(rows * cols)
    er = (t // cols).astype(jnp.float32)[:, None] * omega[None, :]
    ec = (t % cols).astype(jnp.float32)[:, None] * omega[None, :]
    return jnp.concatenate([jnp.sin(er), jnp.cos(er), jnp.sin(ec), jnp.cos(ec)], axis=-1)


def _hyena_filters(length, f_w1, f_b1, f_w2, f_b2, f_w3, f_freq):
    hp = lax.Precision.HIGHEST
    t = jnp.linspace(0.0, 1.0, length, dtype=jnp.float32)[:, None]
    w = 2.0 * math.pi * jnp.arange(length, dtype=jnp.float32)[:, None] / length
    f = jnp.linspace(1e-4, HY_BANDS - 1, HY_BANDS, dtype=jnp.float32)[None, :]
    z = jnp.concatenate([t, jnp.cos(f * w), -jnp.sin(f * w)], axis=-1)
    h = jnp.sin(f_freq * (jnp.dot(z, f_w1, precision=hp) + f_b1))
    h = jnp.sin(f_freq * (jnp.dot(h, f_w2, precision=hp) + f_b2))
    h = jnp.dot(h, f_w3, precision=hp)
    deltas = jnp.linspace(HY_MIN_DECAY, HY_MAX_DECAY, D_MODEL, dtype=jnp.float32)
    window = jnp.exp(-t * jnp.abs(deltas)[None, :])
    return h[:, :D_MODEL] * window, h[:, D_MODEL:] * window


def _bidir_long_conv(u, h_fwd, h_bwd, bias):
    b, length, ch = u.shape
    n = 2 * length
    k = jnp.concatenate([h_fwd, jnp.zeros((1, ch), jnp.float32), h_bwd[1:][::-1]], axis=0)
    kf = jnp.fft.rfft(k, n=n, axis=0)
    uf = jnp.fft.rfft(u, n=n, axis=1)
    y = jnp.fft.irfft(uf * kf[None], n=n, axis=1)[:, :length]
    return y + u * bias


def _hyena_mixer(h, in_w, in_b, short_w, short_b, f_w1, f_b1, f_w2, f_b2, f_w3, f_freq, f_bias, out_w):
    length = h.shape[1]
    u = _dwconv(_mm3(h, in_w) + in_b, short_w, short_b)
    x0 = u[..., :D_MODEL]
    x1 = u[..., D_MODEL:2 * D_MODEL]
    v = u[..., 2 * D_MODEL:]
    h_fwd, h_bwd = _hyena_filters(length, f_w1, f_b1, f_w2, f_b2, f_w3, f_freq)
    v = _bidir_long_conv(v * x1, h_fwd, h_bwd, f_bias)
    return _mm3(v * x0, out_w)


def _ssd_scan(x, dt, a, bm, cm, init):
    hp = lax.Precision.HIGHEST
    b, length = x.shape[:2]
    nc = length // SSD_CHUNK
    q, g, r = SSD_CHUNK, SSD_GROUPS, SSD_HEADS // SSD_GROUPS
    x = x.reshape(b, nc, q, g, r, SSD_HEAD_DIM)
    dt = dt.reshape(b, nc, q, g, r)
    bm = bm.reshape(b, nc, q, g, SSD_STATE)
    cm = cm.reshape(b, nc, q, g, SSD_STATE)
    a_cum = jnp.cumsum(dt * a.reshape(g, r), axis=2)
    xdt = x * dt[..., None]
    seg = a_cum[:, :, :, None] - a_cum[:, :, None, :]
    causal = jnp.tril(jnp.ones((q, q), dtype=bool))[None, None, :, :, None, None]
    decay = jnp.exp(jnp.where(causal, seg, -jnp.inf))
    cb = jnp.einsum('bclgn,bcsgn->bclsg', cm, bm, precision=hp)
    y_diag = jnp.einsum('bclsgr,bcsgrp->bclgrp', cb[..., None] * decay, xdt, precision=hp)
    decay_end = jnp.exp(a_cum[:, :, -1:] - a_cum)
    chunk_states = jnp.einsum('bcsgn,bcsgrp->bcgrpn', bm, xdt * decay_end[..., None], precision=hp)
    chunk_decay = jnp.exp(a_cum[:, :, -1])

    def step(s, inp):
        st, dec = inp
        return s * dec[..., None, None] + st, s

    init_g = init.astype(jnp.float32).reshape(b, g, r, SSD_HEAD_DIM, SSD_STATE)
    final, starts = lax.scan(step, init_g, (jnp.moveaxis(chunk_states, 1, 0), jnp.moveaxis(chunk_decay, 1, 0)))
    starts = jnp.moveaxis(starts, 0, 1)
    y_off = jnp.einsum('bclgn,bcgrpn->bclgrp', cm, starts, precision=hp) * jnp.exp(a_cum)[..., None]
    y = (y_diag + y_off).reshape(b, length, SSD_HEADS, SSD_HEAD_DIM)
    return y, final.reshape(b, SSD_HEADS, SSD_HEAD_DIM, SSD_STATE)


def _flip(t):
    return jnp.flip(t, axis=1)


def _ssd_mixer(h, init_f, init_b, in_w, conv_w, conv_b, dt_bias, a_log, d_skip, norm_g, out_w):
    b, length, _ = h.shape
    proj = _mm3(h, in_w)
    z = proj[..., :SSD_D_INNER]
    xbc = jax.nn.silu(_dwconv(proj[..., SSD_D_INNER:SSD_D_INNER + SSD_XBC], conv_w, conv_b))
    dt_raw = proj[..., SSD_D_INNER + SSD_XBC:]
    gn = SSD_GROUPS * SSD_STATE
    xh = xbc[..., :SSD_D_INNER].reshape(b, length, SSD_HEADS, SSD_HEAD_DIM)
    bm = xbc[..., SSD_D_INNER:SSD_D_INNER + gn].reshape(b, length, SSD_GROUPS, SSD_STATE)
    cm = xbc[..., SSD_D_INNER + gn:].reshape(b, length, SSD_GROUPS, SSD_STATE)
    dt = jax.nn.softplus(dt_raw.reshape(b, length, 2, SSD_HEADS) + dt_bias)
    a = -jnp.exp(a_log)
    y_f, s_f = _ssd_scan(xh, dt[:, :, 0], a[0], bm, cm, init_f)
    y_b, s_b = _ssd_scan(_flip(xh), _flip(dt[:, :, 1]), a[1], _flip(bm), _flip(cm), init_b)
    y = y_f + _flip(y_b) + d_skip[:, None] * xh
    y = y.reshape(b, length, SSD_D_INNER) * jax.nn.silu(z)
    y = _rmsnorm(y, norm_g)
    return _mm3(y, out_w), s_f, s_b


SSD_GN = SSD_GROUPS * SSD_STATE
SSD_GROUP_W = SSD_D_INNER // SSD_GROUPS
SSD_HEADS_PER_GROUP = SSD_HEADS // SSD_GROUPS
ROW_TILE = 256


def _modnorm(x, ng, sh, sc):
    ms = jnp.mean(x * x, axis=-1, keepdims=True)
    return (x * lax.rsqrt(ms + RMS_EPS) * ng) * (1.0 + sc) + sh


def _ssd_in_kernel(x_ref, ng_ref, sh_ref, sc_ref, w_ref, wdt_ref, wdtt_ref,
                   z_ref, xbc_ref, dt_ref, dtt_ref):
    h = _modnorm(x_ref[0], ng_ref[...], sh_ref[0], sc_ref[0]).astype(jnp.bfloat16)
    zx = jnp.dot(h, w_ref[...], preferred_element_type=jnp.float32)
    z_ref[0] = zx[:, :SSD_D_INNER].astype(jnp.bfloat16)
    xbc_ref[0] = zx[:, SSD_D_INNER:]
    dt_ref[0] = jnp.dot(h, wdt_ref[...], preferred_element_type=jnp.float32)
    dtt_ref[0] = lax.dot_general(wdtt_ref[...], h, (((1,), (1,)), ((), ())),
                                 preferred_element_type=jnp.float32)


def _ssd_in(x, ng, sh, sc, in_w):
    b, length, d = x.shape
    tm = min(length, ROW_TILE)
    nzx = SSD_D_INNER + SSD_XBC
    w = in_w[:, :nzx].astype(jnp.bfloat16)
    wdt = in_w[:, nzx:]
    wdt_p = jnp.pad(wdt, ((0, 0), (0, LANES - 2 * SSD_HEADS))).astype(jnp.bfloat16)
    wdt_t = wdt.T.astype(jnp.bfloat16)
    row = lambda i, j: (i, j, 0)
    per_b = lambda i, j: (i, 0, 0)
    full2 = lambda i, j: (0, 0)
    return pl.pallas_call(
        _ssd_in_kernel,
        grid=(b, length // tm),
        in_specs=[pl.BlockSpec((1, tm, d), row), pl.BlockSpec((1, d), full2),
                  pl.BlockSpec((1, 1, d), per_b), pl.BlockSpec((1, 1, d), per_b),
                  pl.BlockSpec((d, nzx), full2), pl.BlockSpec((d, LANES), full2),
                  pl.BlockSpec((2 * SSD_HEADS, d), full2)],
        out_specs=[pl.BlockSpec((1, tm, SSD_D_INNER), row), pl.BlockSpec((1, tm, SSD_XBC), row),
                   pl.BlockSpec((1, tm, LANES), row),
                   pl.BlockSpec((1, 2 * SSD_HEADS, tm), lambda i, j: (i, 0, j))],
        out_shape=[jax.ShapeDtypeStruct((b, length, SSD_D_INNER), jnp.bfloat16),
                   jax.ShapeDtypeStruct((b, length, SSD_XBC), jnp.float32),
                   jax.ShapeDtypeStruct((b, length, LANES), jnp.float32),
                   jax.ShapeDtypeStruct((b, 2 * SSD_HEADS, length), jnp.float32)],
        compiler_params=pltpu.CompilerParams(
            dimension_semantics=("parallel", "parallel"), vmem_limit_bytes=VMEM_LIMIT_BYTES),
        name="ssd_in",
    )(x, ng.reshape(1, d), sh, sc, w, wdt_p, wdt_t)


CONV_PAD = 8
CONV_ROWS = 256
CONV_COLS = 256


def _dwconv_kernel(x_ref, w_ref, b_ref, o_ref, pad_ref, *, taps, silu):
    length = x_ref.shape[1]
    zeros = jnp.zeros((CONV_PAD, x_ref.shape[2]), jnp.float32)
    pad_ref[0:CONV_PAD, :] = zeros
    pad_ref[CONV_PAD + length:, :] = zeros
    pad_ref[CONV_PAD:CONV_PAD + length, :] = x_ref[0]
    rows = min(CONV_ROWS, length)
    for r in range(0, length, rows):
        acc = b_ref[...] + jnp.zeros((rows, x_ref.shape[2]), jnp.float32)
        for k in range(taps):
            off = r + CONV_PAD + k - taps // 2
            acc = acc + w_ref[k:k + 1, :] * pad_ref[off:off + rows, :]
        if silu:
            acc = acc * jax.nn.sigmoid(acc)
        o_ref[0, r:r + rows, :] = acc


def _dwconv_p(x, w, b, silu):
    bsz, length, ch = x.shape
    taps = w.shape[0]
    return pl.pallas_call(
        functools.partial(_dwconv_kernel, taps=taps, silu=silu),
        grid=(bsz, ch // CONV_COLS),
        in_specs=[pl.BlockSpec((1, length, CONV_COLS), lambda i, j: (i, 0, j)),
                  pl.BlockSpec((taps, CONV_COLS), lambda i, j: (0, j)),
                  pl.BlockSpec((1, CONV_COLS), lambda i, j: (0, j))],
        out_specs=pl.BlockSpec((1, length, CONV_COLS), lambda i, j: (i, 0, j)),
        out_shape=jax.ShapeDtypeStruct((bsz, length, ch), jnp.float32),
        scratch_shapes=[pltpu.VMEM((length + 2 * CONV_PAD, CONV_COLS), jnp.float32)],
        compiler_params=pltpu.CompilerParams(
            dimension_semantics=("parallel", "parallel"), vmem_limit_bytes=VMEM_LIMIT_BYTES),
        name="dwconv",
    )(x, w, b.reshape(1, ch))


def _split3_bf16(v):
    p1 = v.astype(jnp.bfloat16)
    r1 = v - p1.astype(jnp.float32)
    p2 = r1.astype(jnp.bfloat16)
    p3 = (r1 - p2.astype(jnp.float32)).astype(jnp.bfloat16)
    return p1, p2, p3


def _softplus(v):
    return jnp.maximum(v, 0.0) + jnp.log1p(jnp.exp(-jnp.abs(v)))


def _expand_heads(cols, g):
    q = cols.shape[0]
    lane = lax.broadcasted_iota(jnp.int32, (q, LANES), 1)
    tiles = []
    for k in range(SSD_HEADS_PER_GROUP // 2):
        ha = g * SSD_HEADS_PER_GROUP + 2 * k
        ca = jnp.broadcast_to(cols[:, ha:ha + 1], (q, LANES))
        cb = jnp.broadcast_to(cols[:, ha + 1:ha + 2], (q, LANES))
        tiles.append(jnp.where(lane < SSD_HEAD_DIM, ca, cb))
    return jnp.concatenate(tiles, axis=1)


def _ssd_scan_kernel(x_ref, b_ref, c_ref, dt_ref, dtt_ref, dtb_ref, dtbt_ref, a_ref, at_ref, init_ref, extra_ref,
                     y_ref, fin_ref, st_ref, *, reverse, hoff, add_prev):
    ci = pl.program_id(1)

    @pl.when(ci == 0)
    def _():
        st_ref[...] = init_ref[0]

    q = SSD_CHUNK
    f32, bf16 = jnp.float32, jnp.bfloat16
    dt = _softplus(dt_ref[0][:, hoff:hoff + SSD_HEADS] + dtb_ref[...])
    dtt = _softplus(dtt_ref[0][hoff:hoff + SSD_HEADS, :] + dtbt_ref[...])
    ri = lax.broadcasted_iota(jnp.int32, (q, q), 0)
    cj = lax.broadcasted_iota(jnp.int32, (q, q), 1)
    keep = (cj >= ri) if reverse else (cj <= ri)
    tri = keep.astype(bf16)
    tri_t = ((ri >= cj) if reverse else (ri <= cj)).astype(bf16)
    acum = sum(jnp.dot(tri, p, preferred_element_type=f32) for p in _split3_bf16(dt * a_ref[...]))
    acum_t = sum(jnp.dot(p, tri_t, preferred_element_type=f32) for p in _split3_bf16(dtt * at_ref[...]))
    end = 0 if reverse else q - 1
    a_end = acum[end:end + 1, :]
    eacum = jnp.exp(acum)
    dec_end = jnp.exp(a_end - acum)
    lane = lax.broadcasted_iota(jnp.int32, (q, LANES), 1)
    for g in range(SSD_GROUPS):
        cg = c_ref[0][:, g * SSD_STATE:(g + 1) * SSD_STATE]
        bg = b_ref[0][:, g * SSD_STATE:(g + 1) * SSD_STATE]
        cg16 = cg.astype(bf16)
        cb = lax.dot_general(cg16, bg.astype(bf16), (((1,), (1,)), ((), ())), preferred_element_type=f32)
        xg = x_ref[0][:, g * SSD_GROUP_W:(g + 1) * SSD_GROUP_W]
        xdt = xg * _expand_heads(dt, g)
        xdt16 = xdt.astype(bf16)
        eac_x = _expand_heads(eacum, g)
        yd = []
        for k in range(SSD_HEADS_PER_GROUP // 2):
            xp = xdt16[:, k * LANES:(k + 1) * LANES]
            ys = []
            for hh in range(2):
                h = g * SSD_HEADS_PER_GROUP + 2 * k + hh
                seg = acum[:, h:h + 1] - acum_t[h:h + 1, :]
                lmat = jnp.exp(jnp.where(keep, seg, -jnp.inf))
                ys.append(jnp.dot((cb * lmat).astype(bf16), xp, preferred_element_type=f32))
            yd.append(jnp.where(lane < SSD_HEAD_DIM, ys[0], ys[1]))
        st = st_ref[g]
        y_off = jnp.dot(cg16, st.astype(bf16), preferred_element_type=f32) * eac_x
        cols = slice(g * SSD_GROUP_W, (g + 1) * SSD_GROUP_W)
        if add_prev:
            other = extra_ref[0, :, cols].astype(f32)
        else:
            other = extra_ref[:, cols] * xg
        y_ref[0, :, cols] = (jnp.concatenate(yd, axis=1) + y_off + other).astype(bf16)
        xdd16 = (xdt * _expand_heads(dec_end, g)).astype(bf16)
        st_ref[g] = st * eac_x[end:end + 1, :] + jnp.dot(bg.T.astype(bf16), xdd16, preferred_element_type=f32)

    @pl.when(ci == pl.num_programs(1) - 1)
    def _():
        fin_ref[0] = st_ref[...]


def _ssd_scan_p(xbc, dt_raw, dt_raw_t, dt_bias, a, init, reverse, direction, y_prev=None, d_skip=None):
    b, length, _ = xbc.shape
    nc = length // SSD_CHUNK
    q = SSD_CHUNK
    cidx = (lambda j: nc - 1 - j) if reverse else (lambda j: j)
    nb = SSD_D_INNER // SSD_GN
    hoff = direction * SSD_HEADS
    full2 = lambda i, j: (0, 0)
    st_shape = (SSD_GROUPS, SSD_STATE, SSD_GROUP_W)
    add_prev = y_prev is not None
    if add_prev:
        extra = y_prev
        extra_spec = pl.BlockSpec((1, q, SSD_D_INNER), lambda i, j: (i, cidx(j), 0))
    else:
        extra = jnp.repeat(d_skip, SSD_HEAD_DIM).reshape(1, SSD_D_INNER)
        extra_spec = pl.BlockSpec((1, SSD_D_INNER), full2)
    return pl.pallas_call(
        functools.partial(_ssd_scan_kernel, reverse=reverse, hoff=hoff, add_prev=add_prev),
        grid=(b, nc),
        in_specs=[pl.BlockSpec((1, q, SSD_D_INNER), lambda i, j: (i, cidx(j), 0)),
                  pl.BlockSpec((1, q, SSD_GN), lambda i, j: (i, cidx(j), nb)),
                  pl.BlockSpec((1, q, SSD_GN), lambda i, j: (i, cidx(j), nb + 1)),
                  pl.BlockSpec((1, q, LANES), lambda i, j: (i, cidx(j), 0)),
                  pl.BlockSpec((1, 2 * SSD_HEADS, q), lambda i, j: (i, 0, cidx(j))),
                  pl.BlockSpec((1, SSD_HEADS), full2), pl.BlockSpec((SSD_HEADS, 1), full2),
                  pl.BlockSpec((1, SSD_HEADS), full2), pl.BlockSpec((SSD_HEADS, 1), full2),
                  pl.BlockSpec((1,) + st_shape, lambda i, j: (i, 0, 0, 0)),
                  extra_spec],
        out_specs=[pl.BlockSpec((1, q, SSD_D_INNER), lambda i, j: (i, cidx(j), 0)),
                   pl.BlockSpec((1,) + st_shape, lambda i, j: (i, 0, 0, 0))],
        out_shape=[jax.ShapeDtypeStruct((b, length, SSD_D_INNER), jnp.bfloat16),
                   jax.ShapeDtypeStruct((b,) + st_shape, jnp.float32)],
        scratch_shapes=[pltpu.VMEM(st_shape, jnp.float32)],
        compiler_params=pltpu.CompilerParams(
            dimension_semantics=("parallel", "arbitrary"), vmem_limit_bytes=VMEM_LIMIT_BYTES),
        name="ssd_scan",
    )(xbc, xbc, xbc, dt_raw, dt_raw_t, dt_bias.reshape(1, -1), dt_bias.reshape(-1, 1),
      a.reshape(1, -1), a.reshape(-1, 1), init, extra)


def _ssd_out_kernel(y_ref, z_ref, ng_ref, w_ref, o_ref):
    z = z_ref[0].astype(jnp.float32)
    y = y_ref[0].astype(jnp.float32) * (z * jax.nn.sigmoid(z))
    ms = jnp.mean(y * y, axis=-1, keepdims=True)
    y = y * lax.rsqrt(ms + RMS_EPS) * ng_ref[...]
    o_ref[0] = jnp.dot(y.astype(jnp.bfloat16), w_ref[...], preferred_element_type=jnp.float32)


def _ssd_out(y, z, norm_g, out_w):
    b, length, di = y.shape
    d = out_w.shape[1]
    tm = min(length, ROW_TILE)
    row = lambda i, j: (i, j, 0)
    full2 = lambda i, j: (0, 0)
    return pl.pallas_call(
        _ssd_out_kernel,
        grid=(b, length // tm),
        in_specs=[pl.BlockSpec((1, tm, di), row), pl.BlockSpec((1, tm, di), row),
                  pl.BlockSpec((1, di), full2), pl.BlockSpec((di, d), full2)],
        out_specs=pl.BlockSpec((1, tm, d), row),
        out_shape=jax.ShapeDtypeStruct((b, length, d), jnp.float32),
        compiler_params=pltpu.CompilerParams(
            dimension_semantics=("parallel", "parallel"), vmem_limit_bytes=VMEM_LIMIT_BYTES),
        name="ssd_out",
    )(y, z, norm_g.reshape(1, di), out_w.astype(jnp.bfloat16))


def _state_to_kernel(s):
    b = s.shape[0]
    s = s.reshape(b, SSD_GROUPS, SSD_HEADS_PER_GROUP, SSD_HEAD_DIM, SSD_STATE)
    return jnp.transpose(s, (0, 1, 4, 2, 3)).reshape(b, SSD_GROUPS, SSD_STATE, SSD_GROUP_W)


def _state_from_kernel(s):
    b = s.shape[0]
    s = s.reshape(b, SSD_GROUPS, SSD_STATE, SSD_HEADS_PER_GROUP, SSD_HEAD_DIM)
    return jnp.transpose(s, (0, 1, 3, 4, 2)).reshape(b, SSD_HEADS, SSD_HEAD_DIM, SSD_STATE)


def _ssd_mixer_p(x, ng, sh, sc, init_f, init_b, in_w, conv_w, conv_b, dt_bias, a_log, d_skip, norm_g, out_w):
    z, xbc_raw, dt_raw, dt_raw_t = _ssd_in(x, ng, sh, sc, in_w)
    xbc = _dwconv_p(xbc_raw, conv_w, conv_b, silu=True)
    a = -jnp.exp(a_log)
    yf, s_f = _ssd_scan_p(xbc, dt_raw, dt_raw_t, dt_bias[0], a[0], _state_to_kernel(init_f), False, 0,
                          d_skip=d_skip)
    y, s_b = _ssd_scan_p(xbc, dt_raw, dt_raw_t, dt_bias[1], a[1], _state_to_kernel(init_b), True, 1, y_prev=yf)
    m = _ssd_out(y, z, norm_g, out_w)
    return m, _state_from_kernel(s_f), _state_from_kernel(s_b)


HY_MAX_BLOCK = 512
HY_CC = 128
HY_MAC_ELEMS = 8192
HY_HIDDEN = 64
HY_FEAT_ROWS = 64


def _odd_dft_tables(n):
    m = np.arange(n, dtype=np.int64)[:, None]
    f = np.arange(n // 2, dtype=np.int64)[None, :]
    ang = 2.0 * np.pi * (((2 * f + 1) * m) % (2 * n)).astype(np.float64) / (2 * n)
    return np.cos(ang), np.sin(ang)


def _hy_in_kernel(x_ref, ng_ref, sh_ref, sc_ref, w_ref, b_ref, o_ref):
    h = _modnorm(x_ref[0], ng_ref[...], sh_ref[0], sc_ref[0]).astype(jnp.bfloat16)
    o_ref[0] = jnp.dot(h, w_ref[...], preferred_element_type=jnp.float32) + b_ref[...]


def _hy_in(x, ng, sh, sc, in_w, in_b):
    b, length, d = x.shape
    n = in_w.shape[1]
    tm = min(length, ROW_TILE)
    row = lambda i, j: (i, j, 0)
    per_b = lambda i, j: (i, 0, 0)
    full2 = lambda i, j: (0, 0)
    return pl.pallas_call(
        _hy_in_kernel,
        grid=(b, length // tm),
        in_specs=[pl.BlockSpec((1, tm, d), row), pl.BlockSpec((1, d), full2),
                  pl.BlockSpec((1, 1, d), per_b), pl.BlockSpec((1, 1, d), per_b),
                  pl.BlockSpec((d, n), full2), pl.BlockSpec((1, n), full2)],
        out_specs=pl.BlockSpec((1, tm, n), row),
        out_shape=jax.ShapeDtypeStruct((b, length, n), jnp.float32),
        compiler_params=pltpu.CompilerParams(
            dimension_semantics=("parallel", "parallel"), vmem_limit_bytes=VMEM_LIMIT_BYTES),
        name="hy_in",
    )(x, ng.reshape(1, d), sh, sc, in_w.astype(jnp.bfloat16), in_b.reshape(1, n))


def _dot3(a, b):
    a_hi, a_lo = _split_bf16(a)
    b_hi, b_lo = _split_bf16(b)
    f32 = jnp.float32
    return (jnp.dot(a_hi, b_hi, preferred_element_type=f32) + jnp.dot(a_lo, b_hi, preferred_element_type=f32)
            + jnp.dot(a_hi, b_lo, preferred_element_type=f32))


def _hy_filter_kernel(w1t_ref, b1_ref, w2t_ref, b2_ref, w3t_ref, fr_ref, dl_ref, o_ref, *, length, blk):
    k = pl.program_id(0)
    q = (lax.broadcasted_iota(jnp.int32, (1, blk), 1) + k * blk)
    pos = jnp.abs(q - length).astype(jnp.float32)
    t = pos / float(length - 1)
    w = (2.0 * math.pi / length) * pos
    band = lax.broadcasted_iota(jnp.int32, (HY_BANDS, 1), 0).astype(jnp.float32)
    fb = 1e-4 + band * ((HY_BANDS - 1 - 1e-4) / (HY_BANDS - 1))
    z = jnp.concatenate([jnp.broadcast_to(t, (8, blk)), jnp.cos(fb * w), -jnp.sin(fb * w),
                         jnp.zeros((HY_FEAT_ROWS - 8 - 2 * HY_BANDS, blk), jnp.float32)], axis=0)
    h = jnp.sin(fr_ref[...] * (_dot3(w1t_ref[...], z) + b1_ref[...]))
    h = jnp.sin(fr_ref[...] * (_dot3(w2t_ref[...], h) + b2_ref[...]))
    kt = _dot3(w3t_ref[0], h)
    o_ref[0] = kt * jnp.exp(-t * dl_ref[...])


def _hy_filter(length, blk, f_w1, f_b1, f_w2, f_b2, f_w3, f_freq):
    d = f_w3.shape[1] // 2
    nk = 2 * length // blk
    w1t = jnp.concatenate([f_w1[0:1].T, jnp.zeros((HY_HIDDEN, 7), jnp.float32), f_w1[1:].T,
                           jnp.zeros((HY_HIDDEN, HY_FEAT_ROWS - 8 - 2 * HY_BANDS), jnp.float32)], axis=1)
    w3t = jnp.stack([f_w3[:, d:].T, f_w3[:, :d].T])
    deltas = jnp.abs(jnp.linspace(HY_MIN_DECAY, HY_MAX_DECAY, d, dtype=jnp.float32)).reshape(d, 1)
    col = lambda v: v.reshape(HY_HIDDEN, 1)
    full2 = lambda k: (0, 0)
    half = length // blk
    return pl.pallas_call(
        functools.partial(_hy_filter_kernel, length=length, blk=blk),
        grid=(nk,),
        in_specs=[pl.BlockSpec((HY_HIDDEN, HY_FEAT_ROWS), full2), pl.BlockSpec((HY_HIDDEN, 1), full2),
                  pl.BlockSpec((HY_HIDDEN, HY_HIDDEN), full2), pl.BlockSpec((HY_HIDDEN, 1), full2),
                  pl.BlockSpec((1, d, HY_HIDDEN), lambda k: (k // half, 0, 0)),
                  pl.BlockSpec((HY_HIDDEN, 1), full2), pl.BlockSpec((d, 1), full2)],
        out_specs=pl.BlockSpec((1, d, blk), lambda k: (k, 0, 0)),
        out_shape=jax.ShapeDtypeStruct((nk, d, blk), jnp.float32),
        compiler_params=pltpu.CompilerParams(
            dimension_semantics=("parallel",), vmem_limit_bytes=VMEM_LIMIT_BYTES),
        name="hy_filter",
    )(w1t, col(f_b1), f_w2.T, col(f_b2), w3t, col(f_freq), deltas)


def _hy_gspec_kernel(hi_ref, lo_ref, ft_ref, fb_ref, o_ref):
    o_ref[0] = _dot3(hi_ref[0], ft_ref[...]) + _dot3(lo_ref[0], fb_ref[...])


def _hy_gspec(kt):
    nk, d, blk = kt.shape
    cos, sin = _odd_dft_tables(2 * blk)
    top = np.concatenate([cos[:blk], -sin[:blk]], axis=1)
    bot = -np.concatenate([cos[blk:], -sin[blk:]], axis=1)
    bot[0] = 0.0
    tm = 512
    full2 = lambda e, i: (0, 0)
    return pl.pallas_call(
        _hy_gspec_kernel,
        grid=(nk - 1, d // tm),
        in_specs=[pl.BlockSpec((1, tm, blk), lambda e, i: (e + 1, i, 0)),
                  pl.BlockSpec((1, tm, blk), lambda e, i: (e, i, 0)),
                  pl.BlockSpec((blk, 2 * blk), full2), pl.BlockSpec((blk, 2 * blk), full2)],
        out_specs=pl.BlockSpec((1, tm, 2 * blk), lambda e, i: (e, i, 0)),
        out_shape=jax.ShapeDtypeStruct((nk - 1, d, 2 * blk), jnp.float32),
        compiler_params=pltpu.CompilerParams(
            dimension_semantics=("parallel", "parallel"), vmem_limit_bytes=VMEM_LIMIT_BYTES),
        name="hy_gspec",
    )(kt, kt, jnp.asarray(top, jnp.float32), jnp.asarray(bot, jnp.float32))


def _hy_conv_kernel(x0_ref, x1_ref, v_ref, g_ref, fb_ref, ff_ref, fi_ref, o_ref, lhs_ref, u_ref, y_ref, *, nb):
    cc, bsz = HY_CC, ff_ref.shape[0]
    mrows = HY_MAC_ELEMS // bsz
    for j in range(nb):
        sl = slice(j * bsz, (j + 1) * bsz)
        wj = v_ref[0, sl, :] * x1_ref[0, sl, :]
        lhs_ref[j * cc:(j + 1) * cc, :] = wj.T.astype(jnp.bfloat16)
    u_ref[...] = jnp.dot(lhs_ref[...], ff_ref[...], preferred_element_type=jnp.float32)

    def per_out_block(i, carry):
        def per_rows(rc, carry2):
            rows = pl.ds(pl.multiple_of(rc * mrows, mrows), mrows)
            acc_r = jnp.zeros((mrows, bsz), jnp.float32)
            acc_i = jnp.zeros((mrows, bsz), jnp.float32)
            for j in range(nb):
                e = i - j + (nb - 1)
                gr = g_ref[e, rows, 0:bsz]
                gi = g_ref[e, rows, bsz:2 * bsz]
                urows = pl.ds(pl.multiple_of(j * cc + rc * mrows, mrows), mrows)
                ur = u_ref[urows, 0:bsz]
                ui = u_ref[urows, bsz:2 * bsz]
                acc_r = acc_r + gr * ur - gi * ui
                acc_i = acc_i + gr * ui + gi * ur
            yrows = pl.ds(pl.multiple_of(i * cc + rc * mrows, mrows), mrows)
            y_ref[yrows, 0:bsz] = acc_r.astype(jnp.bfloat16)
            y_ref[yrows, bsz:2 * bsz] = acc_i.astype(jnp.bfloat16)
            return carry2
        return lax.fori_loop(0, cc // mrows, per_rows, carry)
    lax.fori_loop(0, nb, per_out_block, 0)

    yt = jnp.dot(y_ref[...], fi_ref[...], preferred_element_type=jnp.float32)
    for i in range(nb):
        sl = slice(i * bsz, (i + 1) * bsz)
        w = v_ref[0, sl, :] * x1_ref[0, sl, :]
        o_ref[0, sl, :] = (yt[i * cc:(i + 1) * cc, :].T + fb_ref[...] * w) * x0_ref[0, sl, :]


def _hy_conv(u, g, f_bias, blk):
    b, length, d3 = u.shape
    d = d3 // 3
    nb = length // blk
    ncb = d // HY_CC
    cos, sin = _odd_dft_tables(2 * blk)
    fwd = np.concatenate([cos[:blk], -sin[:blk]], axis=1)
    inv = (1.0 / blk) * np.concatenate([cos[:blk].T, -sin[:blk].T], axis=0)
    col = lambda off: pl.BlockSpec((1, length, HY_CC), lambda c, i, off=off: (i, 0, off + c))
    full2 = lambda c, i: (0, 0)
    return pl.pallas_call(
        functools.partial(_hy_conv_kernel, nb=nb),
        grid=(ncb, b),
        in_specs=[col(0), col(ncb), col(2 * ncb),
                  pl.BlockSpec((2 * nb - 1, HY_CC, 2 * blk), lambda c, i: (0, c, 0)),
                  pl.BlockSpec((1, HY_CC), lambda c, i: (0, c)),
                  pl.BlockSpec((blk, 2 * blk), full2), pl.BlockSpec((2 * blk, blk), full2)],
        out_specs=pl.BlockSpec((1, length, HY_CC), lambda c, i: (i, 0, c)),
        out_shape=jax.ShapeDtypeStruct((b, length, d), jnp.float32),
        scratch_shapes=[pltpu.VMEM((nb * HY_CC, blk), jnp.bfloat16),
                        pltpu.VMEM((nb * HY_CC, 2 * blk), jnp.float32),
                        pltpu.VMEM((nb * HY_CC, 2 * blk), jnp.bfloat16)],
        compiler_params=pltpu.CompilerParams(
            dimension_semantics=("parallel", "arbitrary"), vmem_limit_bytes=56 * 1024 * 1024),
        name="hy_conv",
    )(u, u, u, g, f_bias.reshape(1, d), jnp.asarray(fwd, jnp.bfloat16), jnp.asarray(inv, jnp.bfloat16))


def _hyena_mixer_p(x, ng, sh, sc, in_w, in_b, short_w, short_b, f_w1, f_b1, f_w2, f_b2, f_w3, f_freq, f_bias, out_w):
    length = x.shape[1]
    blk = min(HY_MAX_BLOCK, length)
    u = _dwconv_p(_hy_in(x, ng, sh, sc, in_w, in_b), short_w, short_b, silu=False)
    g = _hy_gspec(_hy_filter(length, blk, f_w1, f_b1, f_w2, f_b2, f_w3, f_freq))
    return _mm3(_hy_conv(u, g, f_bias, blk), out_w)


TOK_TILE = 256
MOE_ROWS = 512
SEG_CHUNK = 64
BF16_TILE_ROWS = 16
LANES = 128


def _split_bf16(w):
    hi = w.astype(jnp.bfloat16)
    lo = (w - hi.astype(jnp.float32)).astype(jnp.bfloat16)
    return hi, lo


def _moe_pre_kernel(x_ref, m_ref, g1_ref, ng_ref, sh_ref, sc_ref, wrh_ref, wrl_ref,
                    xo_ref, hpk_ref, lg_ref):
    x = x_ref[0] + g1_ref[0] * m_ref[0]
    xo_ref[0] = x
    ms = jnp.mean(x * x, axis=-1, keepdims=True)
    h = x * lax.rsqrt(ms + RMS_EPS) * ng_ref[...]
    h = h * (1.0 + sc_ref[0]) + sh_ref[0]
    h_hi = h.astype(jnp.bfloat16)
    h_lo = (h - h_hi.astype(jnp.float32)).astype(jnp.bfloat16)
    dn = (((1,), (1,)), ((), ()))
    lg = lax.dot_general(wrh_ref[...], h_hi, dn, preferred_element_type=jnp.float32)
    lg += lax.dot_general(wrh_ref[...], h_lo, dn, preferred_element_type=jnp.float32)
    lg += lax.dot_general(wrl_ref[...], h_hi, dn, preferred_element_type=jnp.float32)
    lg_ref[0] = lg
    half = h.shape[1] // 2
    wa = pltpu.bitcast(h_hi[:, :half].astype(jnp.float32), jnp.uint32) >> 16
    wb = pltpu.bitcast(h_hi[:, half:].astype(jnp.float32), jnp.uint32) & jnp.uint32(0xFFFF0000)
    hpk_ref[0] = wa | wb


def _moe_pre(x, m, g1, ng, sh, sc, w_router):
    b, length, d = x.shape
    tm = min(length, 512)
    wrh, wrl = _split_bf16(w_router.T)
    row = lambda i, j: (i, j, 0)
    per_b = lambda i, j: (i, 0, 0)
    full2 = lambda i, j: (0, 0)
    return pl.pallas_call(
        _moe_pre_kernel,
        grid=(b, length // tm),
        in_specs=[pl.BlockSpec((1, tm, d), row), pl.BlockSpec((1, tm, d), row),
                  pl.BlockSpec((1, 1, d), per_b), pl.BlockSpec((1, d), full2),
                  pl.BlockSpec((1, 1, d), per_b), pl.BlockSpec((1, 1, d), per_b),
                  pl.BlockSpec((N_EXPERTS, d), full2), pl.BlockSpec((N_EXPERTS, d), full2)],
        out_specs=[pl.BlockSpec((1, tm, d), row), pl.BlockSpec((1, tm, d // 2), row),
                   pl.BlockSpec((1, N_EXPERTS, tm), lambda i, j: (i, 0, j))],
        out_shape=[jax.ShapeDtypeStruct((b, length, d), jnp.float32),
                   jax.ShapeDtypeStruct((b, length, d // 2), jnp.uint32),
                   jax.ShapeDtypeStruct((b, N_EXPERTS, length), jnp.float32)],
        compiler_params=pltpu.CompilerParams(
            dimension_semantics=("parallel", "parallel"), vmem_limit_bytes=VMEM_LIMIT_BYTES),
        name="moe_pre",
    )(x, m, g1, ng.reshape(1, d), sh, sc, wrh, wrl)


def _moe_ffn_kernel(idx_ref, nidx_ref, h_hbm, gate_ref, wg_ref, wu_ref, wd_ref, y_ref,
                    xe_ref, wgb, wub, wdb, sem):
    nblk = pl.num_programs(1)
    step = pl.program_id(0) * nblk + pl.program_id(1)
    last = pl.num_programs(0) * nblk - 1

    def issue(ids_ref, slot):
        base = slot * MOE_ROWS
        for c in range(MOE_ROWS):
            pltpu.make_async_copy(h_hbm.at[pl.ds(ids_ref[0, 0, c], 1)], xe_ref.at[pl.ds(base + c, 1)],
                                  sem.at[slot]).start()

    @pl.when(step == 0)
    def _():
        issue(idx_ref, 0)

    @pl.when(step < last)
    def _():
        issue(nidx_ref, (step + 1) % 2)

    @pl.when(pl.program_id(1) == 0)
    def _():
        wgb[...] = wg_ref[0].astype(jnp.bfloat16)
        wub[...] = wu_ref[0].astype(jnp.bfloat16)
        wdb[...] = wd_ref[0].astype(jnp.bfloat16)

    slot = step % 2
    rows = pl.ds(pl.multiple_of(slot * MOE_ROWS, MOE_ROWS), MOE_ROWS)
    pltpu.make_async_copy(h_hbm.at[pl.ds(0, MOE_ROWS)], xe_ref.at[rows], sem.at[slot]).wait()
    half = wgb.shape[0] // 2
    w = xe_ref[rows, :]
    xa = pltpu.bitcast(w << 16, jnp.float32).astype(jnp.bfloat16)
    xb = pltpu.bitcast(w & jnp.uint32(0xFFFF0000), jnp.float32).astype(jnp.bfloat16)
    hg = jnp.dot(xa, wgb[:half], preferred_element_type=jnp.float32)
    hg += jnp.dot(xb, wgb[half:], preferred_element_type=jnp.float32)
    hu = jnp.dot(xa, wub[:half], preferred_element_type=jnp.float32)
    hu += jnp.dot(xb, wub[half:], preferred_element_type=jnp.float32)
    hid = (hg * jax.nn.sigmoid(hg) * hu).astype(jnp.bfloat16)
    y = jnp.dot(hid, wdb[...], preferred_element_type=jnp.float32)
    y_ref[0] = (y * gate_ref[0]).astype(jnp.bfloat16)


def _moe_ffn(hpk, grow, gate, w_gate, w_up, w_down):
    e, r = grow.shape
    d, f = w_gate.shape[1], w_gate.shape[2]
    nblk = r // MOE_ROWS
    nsteps = e * nblk
    wspec = lambda shp: pl.BlockSpec((1,) + shp, lambda i, j: (i, 0, 0))
    ids = grow.reshape(nsteps, 1, MOE_ROWS)
    smem_ids = lambda off: pl.BlockSpec(
        (1, 1, MOE_ROWS), lambda i, j: (jnp.minimum(i * nblk + j + off, nsteps - 1), 0, 0), memory_space=pltpu.SMEM)
    return pl.pallas_call(
        _moe_ffn_kernel,
        grid=(e, nblk),
        in_specs=[smem_ids(0), smem_ids(1),
                  pl.BlockSpec(memory_space=pltpu.HBM),
                  pl.BlockSpec((1, MOE_ROWS, 1), lambda i, j: (i, j, 0)),
                  wspec((d, f)), wspec((d, f)), wspec((f, d))],
        out_specs=pl.BlockSpec((1, MOE_ROWS, d), lambda i, j: (i, j, 0)),
        out_shape=jax.ShapeDtypeStruct((e, r, d), jnp.bfloat16),
        scratch_shapes=[pltpu.VMEM((2 * MOE_ROWS, d // 2), jnp.uint32),
                        pltpu.VMEM((d, f), jnp.bfloat16), pltpu.VMEM((d, f), jnp.bfloat16),
                        pltpu.VMEM((f, d), jnp.bfloat16),
                        pltpu.SemaphoreType.DMA((2,))],
        compiler_params=pltpu.CompilerParams(
            dimension_semantics=("arbitrary", "arbitrary"), vmem_limit_bytes=VMEM_LIMIT_BYTES),
        name="moe_ffn",
    )(ids, ids, hpk, gate, w_gate, w_up, w_down)


def _moe_comb_kernel(cs_ref, x_ref, g2_ref, y_ref, idx_ref, o_ref, ycat, acc, *, cap, ch, ntile):
    b = pl.program_id(0)
    t = pl.program_id(1)
    base = t * TOK_TILE
    sub = lax.broadcasted_iota(jnp.int32, (TOK_TILE, LANES), 0) + base
    if ntile == 1:
        for e in range(N_EXPERTS):
            ycat[e * ch:(e + 1) * ch, :] = y_ref[e, 0:ch, :]
        v = idx_ref[0]
        tiles = [(v[:, p * LANES:(p + 1) * LANES] == sub).astype(jnp.bfloat16)
                 for p in range(N_EXPERTS * ch // LANES)]
        acc[...] = jnp.dot(jnp.concatenate(tiles, axis=1), ycat[...], preferred_element_type=jnp.float32)
    else:
        lane = lax.broadcasted_iota(jnp.int32, (1, LANES), 1)
        per = LANES // ch
        sts = []
        for e in range(N_EXPERTS):
            s0 = cs_ref[(b * N_EXPERTS + e) * (ntile + 1) + t]
            st = jnp.minimum((s0 // BF16_TILE_ROWS) * BF16_TILE_ROWS, cap - ch)
            st = pl.multiple_of(st, BF16_TILE_ROWS)
            sts.append(st)
            ycat[e * ch:(e + 1) * ch, :] = y_ref[e, pl.ds(st, ch), :]
        tiles = []
        for p in range(N_EXPERTS // per):
            v = None
            for q in range(per):
                e = p * per + q
                r = pltpu.roll(idx_ref[0, e:e + 1, :], (2 * cap - sts[e] + q * ch) % cap, 1)[:, :LANES]
                v = r if v is None else jnp.where(lane >= q * ch, r, v)
            tiles.append((v == sub).astype(jnp.bfloat16))
        acc[...] = jnp.dot(jnp.concatenate(tiles, axis=1), ycat[...], preferred_element_type=jnp.float32)
        sub_c = lax.broadcasted_iota(jnp.int32, (TOK_TILE, ch), 0) + base
        lane_c = lax.broadcasted_iota(jnp.int32, (1, ch), 1)
        for e in range(N_EXPERTS):
            s1 = cs_ref[(b * N_EXPERTS + e) * (ntile + 1) + t + 1]
            first_end = sts[e] + ch
            n_extra = jnp.maximum(s1 - first_end + ch - 1, 0) // ch

            def extra(q, carry, e=e, first_end=first_end):
                lo = first_end + q * ch
                stq = pl.multiple_of(jnp.minimum(lo, cap - ch), BF16_TILE_ROWS)
                r = pltpu.roll(idx_ref[0, e:e + 1, :], (2 * cap - stq) % cap, 1)[:, :ch]
                hit = (r == sub_c) & (lane_c + stq >= lo)
                acc[...] += jnp.dot(hit.astype(jnp.bfloat16), y_ref[e, pl.ds(stq, ch), :],
                                    preferred_element_type=jnp.float32)
                return carry
            lax.fori_loop(0, n_extra, extra, 0)
    o_ref[0] = x_ref[0] + g2_ref[0] * acc[...]


def _moe_combine(x, g2, y, idx, cs):
    b, length, d = x.shape
    cap = idx.shape[2]
    ntile = length // TOK_TILE
    ch = min(SEG_CHUNK, cap)
    if ntile == 1:
        idx_in = idx.reshape(b, 1, N_EXPERTS * cap)
        idx_spec = pl.BlockSpec((1, 1, N_EXPERTS * cap), lambda i, j, c: (i, 0, 0))
    else:
        idx_in = idx
        idx_spec = pl.BlockSpec((1, N_EXPERTS, cap), lambda i, j, c: (i, 0, 0))
    grid_spec = pltpu.PrefetchScalarGridSpec(
        num_scalar_prefetch=1,
        grid=(b, ntile),
        in_specs=[pl.BlockSpec((1, TOK_TILE, d), lambda i, j, c: (i, j, 0)),
                  pl.BlockSpec((1, 1, d), lambda i, j, c: (i, 0, 0)),
                  pl.BlockSpec((N_EXPERTS, cap, d), lambda i, j, c: (0, i, 0)),
                  idx_spec],
        out_specs=pl.BlockSpec((1, TOK_TILE, d), lambda i, j, c: (i, j, 0)),
        scratch_shapes=[pltpu.VMEM((N_EXPERTS * ch, d), jnp.bfloat16),
                        pltpu.VMEM((TOK_TILE, d), jnp.float32)])
    return pl.pallas_call(
        functools.partial(_moe_comb_kernel, cap=cap, ch=ch, ntile=ntile),
        grid_spec=grid_spec,
        out_shape=jax.ShapeDtypeStruct((b, length, d), jnp.float32),
        compiler_params=pltpu.CompilerParams(
            dimension_semantics=("arbitrary", "arbitrary"), vmem_limit_bytes=56 * 1024 * 1024),
        name="moe_combine",
    )(cs.reshape(-1).astype(jnp.int32), x, g2, y, idx_in)


def _moe_block(x, m, g1, ng, sh, sc, g2, w_router, w_gate, w_up, w_down):
    b, length, d = x.shape
    cap = EC_FACTOR * length // N_EXPERTS
    x1, hpk, lg = _moe_pre(x, m, g1, ng, sh, sc, w_router)
    aff = jax.nn.softmax(lg, axis=1)
    _, idx = lax.top_k(aff, cap)
    idx = jnp.sort(idx, axis=-1)
    gate = jnp.take_along_axis(aff, idx, axis=-1)
    ntile = length // TOK_TILE
    bounds = jnp.arange(ntile + 1, dtype=jnp.int32) * TOK_TILE
    cs = jnp.sum(idx[:, :, :, None] < bounds, axis=2, dtype=jnp.int32)
    grow = idx + (jnp.arange(b, dtype=jnp.int32) * length)[:, None, None]
    grow = jnp.swapaxes(grow, 0, 1).reshape(N_EXPERTS, b * cap)
    gate_e = jnp.swapaxes(gate, 0, 1).reshape(N_EXPERTS, b * cap, 1)
    y = _moe_ffn(hpk.reshape(b * length, d // 2), grow, gate_e, w_gate, w_up, w_down)
    return _moe_combine(x1, g2, y, idx, cs)


def kernel(x_prompt, x_sample, state_ssd, c, c_ctx, norm_g, ada_w, ada_b, hy_in_w, hy_in_b, hy_short_w, hy_short_b, hy_f_w1, hy_f_b1, hy_f_w2, hy_f_b2, hy_f_w3, hy_f_freq, hy_f_bias, hy_out_w, ssd_in_w, ssd_conv_w, ssd_conv_b, ssd_dt_bias, ssd_A_log, ssd_D, ssd_norm_g, ssd_out_w, moe_router, moe_w_gate, moe_w_up, moe_w_down, final_norm_g):
    rows = x_sample.shape[1] // GRID_W
    xp = x_prompt
    xs = x_sample + _sincos_2d(rows, GRID_W, D_MODEL)[None]
    new_ssd = []
    for i in range(DEPTH):
        sh1p, sc1p, g1p, sh2p, sc2p, g2p = _adaln(c_ctx[None, :], ada_w[i], ada_b[i])
        sh1s, sc1s, g1s, sh2s, sc2s, g2s = _adaln(c, ada_w[i], ada_b[i])
        j = i // N_MIXERS
        bp = (xp.shape[0], 1, D_MODEL)
        if i % N_MIXERS == 0:
            hy = (hy_in_w[j], hy_in_b[j], hy_short_w[j], hy_short_b[j], hy_f_w1[j], hy_f_b1[j],
                  hy_f_w2[j], hy_f_b2[j], hy_f_w3[j], hy_f_freq[j], hy_f_bias[j], hy_out_w[j])
            mp = _hyena_mixer_p(xp, norm_g[i, 0], jnp.broadcast_to(sh1p, bp), jnp.broadcast_to(sc1p, bp), *hy)
            ms = _hyena_mixer_p(xs, norm_g[i, 0], sh1s, sc1s, *hy)
        else:
            sp = (ssd_in_w[j], ssd_conv_w[j], ssd_conv_b[j], ssd_dt_bias[j], ssd_A_log[j],
                  ssd_D[j], ssd_norm_g[j], ssd_out_w[j])
            zeros = jnp.zeros((xp.shape[0], SSD_HEADS, SSD_HEAD_DIM, SSD_STATE), jnp.float32)
            mp, s_f, s_b = _ssd_mixer_p(xp, norm_g[i, 0], jnp.broadcast_to(sh1p, bp),
                                        jnp.broadcast_to(sc1p, bp), zeros, zeros, *sp)
            new_ssd.append(jnp.stack([s_f, s_b], axis=1))
            ms, _, _ = _ssd_mixer_p(xs, norm_g[i, 0], sh1s, sc1s, state_ssd[:, j, 0], state_ssd[:, j, 1], *sp)
        moe = (moe_router[i], moe_w_gate[i], moe_w_up[i], moe_w_down[i])
        xp = _moe_block(xp, mp, jnp.broadcast_to(g1p, bp), norm_g[i, 1], jnp.broadcast_to(sh2p, bp),
                        jnp.broadcast_to(sc2p, bp), jnp.broadcast_to(g2p, bp), *moe)
        xs = _moe_block(xs, ms, g1s, norm_g[i, 1], sh2s, sc2s, g2s, *moe)
    y_prompt = _rmsnorm(xp, final_norm_g)
    y_sample = _rmsnorm(xs, final_norm_g)
    new_state_ssd = jnp.stack(new_ssd, axis=1)
    return (y_prompt, y_sample, new_state_ssd)
```

```python
import functools
import math

import jax
import jax.numpy as jnp
import numpy as np
from jax import lax
from jax.experimental import pallas as pl
from jax.experimental.pallas import tpu as pltpu

D_MODEL = 1024
DEPTH = 2
GRID_W = 64
N_MIXERS = 2
RMS_EPS = 1e-6
HY_EMB = 33
HY_BANDS = (HY_EMB - 1) // 2
HY_SHORT_DECAY_FRAC = 0.3
HY_LONG_DECAY_FRAC = 1.5
HY_DECAY_TARGET = 1e-2
HY_MAX_DECAY = math.log(HY_DECAY_TARGET) / HY_SHORT_DECAY_FRAC
HY_MIN_DECAY = math.log(HY_DECAY_TARGET) / HY_LONG_DECAY_FRAC
SSD_D_INNER = 2 * D_MODEL
SSD_HEAD_DIM = 64
SSD_HEADS = SSD_D_INNER // SSD_HEAD_DIM
SSD_GROUPS = 4
SSD_STATE = 128
SSD_CHUNK = 128
SSD_XBC = SSD_D_INNER + 2 * SSD_GROUPS * SSD_STATE
N_EXPERTS = 16
EC_FACTOR = 2

VMEM_LIMIT_BYTES = 48 * 1024 * 1024


def _mm_kernel(a_ref, b_ref, o_ref, acc_ref):
    @pl.when(pl.program_id(2) == 0)
    def _():
        acc_ref[...] = jnp.zeros_like(acc_ref)

    acc_ref[...] += jnp.dot(a_ref[...].astype(jnp.bfloat16), b_ref[...],
                            preferred_element_type=jnp.float32)

    @pl.when(pl.program_id(2) == pl.num_programs(2) - 1)
    def _():
        o_ref[...] = acc_ref[...]


def _pick(n, pref):
    for t in pref:
        if n % t == 0:
            return t
    return n


def _mm(a, b):
    m, k = a.shape
    n = b.shape[1]
    mp = -(-m // 8) * 8
    if mp != m:
        a = jnp.pad(a, ((0, mp - m), (0, 0)))
    tm = _pick(mp, (512, 256, 128, 64, 32, 16, 8))
    tn = _pick(n, (512, 256, 128))
    tk = _pick(k, (1024, 512, 256, 128))
    out = pl.pallas_call(
        _mm_kernel,
        grid=(mp // tm, n // tn, k // tk),
        in_specs=[pl.BlockSpec((tm, tk), lambda i, j, l: (i, l)),
                  pl.BlockSpec((tk, tn), lambda i, j, l: (l, j))],
        out_specs=pl.BlockSpec((tm, tn), lambda i, j, l: (i, j)),
        out_shape=jax.ShapeDtypeStruct((mp, n), jnp.float32),
        scratch_shapes=[pltpu.VMEM((tm, tn), jnp.float32)],
        compiler_params=pltpu.CompilerParams(
            dimension_semantics=("parallel", "parallel", "arbitrary"),
            vmem_limit_bytes=VMEM_LIMIT_BYTES),
        name="mm",
    )(a, b.astype(jnp.bfloat16))
    return out[:m]


def _mm3(a, b):
    lead = a.shape[:-1]
    return _mm(a.reshape(-1, a.shape[-1]), b).reshape(*lead, b.shape[1])


def _bmm_kernel(a_ref, b_ref, o_ref):
    o_ref[0] = jnp.dot(a_ref[0].astype(jnp.bfloat16), b_ref[0],
                       preferred_element_type=jnp.float32)


def _bmm(a, b):
    e, m, k = a.shape
    n = b.shape[2]
    tm = _pick(m, (512, 256, 128, 64, 32, 16, 8))
    tn = _pick(n, (512, 256, 128))
    return pl.pallas_call(
        _bmm_kernel,
        grid=(e, m // tm, n // tn),
        in_specs=[pl.BlockSpec((1, tm, k), lambda g, i, j: (g, i, 0)),
                  pl.BlockSpec((1, k, tn), lambda g, i, j: (g, 0, j))],
        out_specs=pl.BlockSpec((1, tm, tn), lambda g, i, j: (g, i, j)),
        out_shape=jax.ShapeDtypeStruct((e, m, n), jnp.float32),
        compiler_params=pltpu.CompilerParams(
            dimension_semantics=("parallel", "parallel", "parallel"),
            vmem_limit_bytes=VMEM_LIMIT_BYTES),
        name="bmm",
    )(a, b.astype(jnp.bfloat16))


def _rmsnorm(x, g):
    y = x * lax.rsqrt(jnp.mean(x * x, axis=-1, keepdims=True) + RMS_EPS)
    return y * g


def _adaln(cond, ada_w, ada_b):
    m = jnp.dot(jax.nn.silu(cond), ada_w, precision=lax.Precision.HIGHEST) + ada_b
    return jnp.split(m[:, None, :], 6, axis=-1)


def _modulate(h, shift, scale):
    return h * (1 + scale) + shift


def _dwconv(x, w, b):
    k, ch = w.shape
    y = lax.conv_general_dilated(x, w[:, None, :], window_strides=(1,),
                                 padding=[(k // 2, k // 2)],
                                 dimension_numbers=('NWC', 'WIO', 'NWC'),
                                 feature_group_count=ch,
                                 precision=lax.Precision.HIGHEST)
    return y + b


def _sincos_2d(rows, cols, d):
    q = d // 4
    omega = 1.0 / (10000.0 ** (jnp.arange(q, dtype=jnp.float32) / q))
    t = jnp.arange(rows * cols)
    er = (t // cols).astype(jnp.float32)[:, None] * omega[None, :]
    ec = (t % cols).astype(jnp.float32)[:, None] * omega[None, :]
    return jnp.concatenate([jnp.sin(er), jnp.cos(er), jnp.sin(ec), jnp.cos(ec)], axis=-1)


def _hyena_filters(length, f_w1, f_b1, f_w2, f_b2, f_w3, f_freq):
    hp = lax.Precision.HIGHEST
    t = jnp.linspace(0.0, 1.0, length, dtype=jnp.float32)[:, None]
    w = 2.0 * math.pi * jnp.arange(length, dtype=jnp.float32)[:, None] / length
    f = jnp.linspace(1e-4, HY_BANDS - 1, HY_BANDS, dtype=jnp.float32)[None, :]
    z = jnp.concatenate([t, jnp.cos(f * w), -jnp.sin(f * w)], axis=-1)
    h = jnp.sin(f_freq * (jnp.dot(z, f_w1, precision=hp) + f_b1))
    h = jnp.sin(f_freq * (jnp.dot(h, f_w2, precision=hp) + f_b2))
    h = jnp.dot(h, f_w3, precision=hp)
    deltas = jnp.linspace(HY_MIN_DECAY, HY_MAX_DECAY, D_MODEL, dtype=jnp.float32)
    window = jnp.exp(-t * jnp.abs(deltas)[None, :])
    return h[:, :D_MODEL] * window, h[:, D_MODEL:] * window


def _bidir_long_conv(u, h_fwd, h_bwd, bias):
    b, length, ch = u.shape
    n = 2 * length
    k = jnp.concatenate([h_fwd, jnp.zeros((1, ch), jnp.float32), h_bwd[1:][::-1]], axis=0)
    kf = jnp.fft.rfft(k, n=n, axis=0)
    uf = jnp.fft.rfft(u, n=n, axis=1)
    y = jnp.fft.irfft(uf * kf[None], n=n, axis=1)[:, :length]
    return y + u * bias


def _hyena_mixer(h, in_w, in_b, short_w, short_b, f_w1, f_b1, f_w2, f_b2, f_w3, f_freq, f_bias, out_w):
    length = h.shape[1]
    u = _dwconv(_mm3(h, in_w) + in_b, short_w, short_b)
    x0 = u[..., :D_MODEL]
    x1 = u[..., D_MODEL:2 * D_MODEL]
    v = u[..., 2 * D_MODEL:]
    h_fwd, h_bwd = _hyena_filters(length, f_w1, f_b1, f_w2, f_b2, f_w3, f_freq)
    v = _bidir_long_conv(v * x1, h_fwd, h_bwd, f_bias)
    return _mm3(v * x0, out_w)


def _ssd_scan(x, dt, a, bm, cm, init):
    hp = lax.Precision.HIGHEST
    b, length = x.shape[:2]
    nc = length // SSD_CHUNK
    q, g, r = SSD_CHUNK, SSD_GROUPS, SSD_HEADS // SSD_GROUPS
    x = x.reshape(b, nc, q, g, r, SSD_HEAD_DIM)
    dt = dt.reshape(b, nc, q, g, r)
    bm = bm.reshape(b, nc, q, g, SSD_STATE)
    cm = cm.reshape(b, nc, q, g, SSD_STATE)
    a_cum = jnp.cumsum(dt * a.reshape(g, r), axis=2)
    xdt = x * dt[..., None]
    seg = a_cum[:, :, :, None] - a_cum[:, :, None, :]
    causal = jnp.tril(jnp.ones((q, q), dtype=bool))[None, None, :, :, None, None]
    decay = jnp.exp(jnp.where(causal, seg, -jnp.inf))
    cb = jnp.einsum('bclgn,bcsgn->bclsg', cm, bm, precision=hp)
    y_diag = jnp.einsum('bclsgr,bcsgrp->bclgrp', cb[..., None] * decay, xdt, precision=hp)
    decay_end = jnp.exp(a_cum[:, :, -1:] - a_cum)
    chunk_states = jnp.einsum('bcsgn,bcsgrp->bcgrpn', bm, xdt * decay_end[..., None], precision=hp)
    chunk_decay = jnp.exp(a_cum[:, :, -1])

    def step(s, inp):
        st, dec = inp
        return s * dec[..., None, None] + st, s

    init_g = init.astype(jnp.float32).reshape(b, g, r, SSD_HEAD_DIM, SSD_STATE)
    final, starts = lax.scan(step, init_g, (jnp.moveaxis(chunk_states, 1, 0), jnp.moveaxis(chunk_decay, 1, 0)))
    starts = jnp.moveaxis(starts, 0, 1)
    y_off = jnp.einsum('bclgn,bcgrpn->bclgrp', cm, starts, precision=hp) * jnp.exp(a_cum)[..., None]
    y = (y_diag + y_off).reshape(b, length, SSD_HEADS, SSD_HEAD_DIM)
    return y, final.reshape(b, SSD_HEADS, SSD_HEAD_DIM, SSD_STATE)


def _flip(t):
    return jnp.flip(t, axis=1)


def _ssd_mixer(h, init_f, init_b, in_w, conv_w, conv_b, dt_bias, a_log, d_skip, norm_g, out_w):
    b, length, _ = h.shape
    proj = _mm3(h, in_w)
    z = proj[..., :SSD_D_INNER]
    xbc = jax.nn.silu(_dwconv(proj[..., SSD_D_INNER:SSD_D_INNER + SSD_XBC], conv_w, conv_b))
    dt_raw = proj[..., SSD_D_INNER + SSD_XBC:]
    gn = SSD_GROUPS * SSD_STATE
    xh = xbc[..., :SSD_D_INNER].reshape(b, length, SSD_HEADS, SSD_HEAD_DIM)
    bm = xbc[..., SSD_D_INNER:SSD_D_INNER + gn].reshape(b, length, SSD_GROUPS, SSD_STATE)
    cm = xbc[..., SSD_D_INNER + gn:].reshape(b, length, SSD_GROUPS, SSD_STATE)
    dt = jax.nn.softplus(dt_raw.reshape(b, length, 2, SSD_HEADS) + dt_bias)
    a = -jnp.exp(a_log)
    y_f, s_f = _ssd_scan(xh, dt[:, :, 0], a[0], bm, cm, init_f)
    y_b, s_b = _ssd_scan(_flip(xh), _flip(dt[:, :, 1]), a[1], _flip(bm), _flip(cm), init_b)
    y = y_f + _flip(y_b) + d_skip[:, None] * xh
    y = y.reshape(b, length, SSD_D_INNER) * jax.nn.silu(z)
    y = _rmsnorm(y, norm_g)
    return _mm3(y, out_w), s_f, s_b


SSD_GN = SSD_GROUPS * SSD_STATE
SSD_GROUP_W = SSD_D_INNER // SSD_GROUPS
SSD_HEADS_PER_GROUP = SSD_HEADS // SSD_GROUPS
ROW_TILE = 256


def _modnorm(x, ng, sh, sc):
    ms = jnp.mean(x * x, axis=-1, keepdims=True)
    return (x * lax.rsqrt(ms + RMS_EPS) * ng) * (1.0 + sc) + sh


HALO = 8
CONV_COLS = 512


def _halo_rows(xm_ref, xp_ref, xn_ref, ng_ref, sh_ref, sc_ref):
    j = pl.program_id(1)
    xa = jnp.concatenate([xp_ref[0], xm_ref[0], xn_ref[0]], axis=0)
    h = _modnorm(xa, ng_ref[...], sh_ref[0], sc_ref[0]).astype(jnp.bfloat16)
    tm = xm_ref.shape[1]
    r = lax.broadcasted_iota(jnp.int32, (tm + 2 * HALO, 1), 0)
    valid = ((r >= HALO) | (j > 0)) & ((r < tm + HALO) | (j < pl.num_programs(1) - 1))
    return h, valid


def _conv_rows(pad_ref, cw_ref, cb_ref, o_ref, tm, silu):
    taps = cw_ref.shape[0]
    ncol = pad_ref.shape[1]
    for c0 in range(0, ncol, CONV_COLS):
        cols = slice(c0, c0 + CONV_COLS)
        acc = cb_ref[:, cols] + jnp.zeros((tm, CONV_COLS), jnp.float32)
        for k in range(taps):
            off = HALO + k - taps // 2
            acc = acc + cw_ref[k:k + 1, cols] * pad_ref[off:off + tm, cols]
        if silu:
            acc = acc * jax.nn.sigmoid(acc)
        o_ref[0, :, cols] = acc


def _halo_specs(length, tm, d):
    nh = length // HALO
    per = tm // HALO
    main = pl.BlockSpec((1, tm, d), lambda i, j: (i, j, 0))
    prev = pl.BlockSpec((1, HALO, d), lambda i, j: (i, jnp.maximum(j * per - 1, 0), 0))
    nxt = pl.BlockSpec((1, HALO, d), lambda i, j: (i, jnp.minimum((j + 1) * per, nh - 1), 0))
    return [main, prev, nxt]


def _ssd_in_kernel(xm_ref, xp_ref, xn_ref, ng_ref, sh_ref, sc_ref, w_ref, wdt_ref, wdtt_ref, cw_ref, cb_ref,
                   z_ref, xbc_ref, dt_ref, dtt_ref, pad_ref):
    tm = xm_ref.shape[1]
    h, valid = _halo_rows(xm_ref, xp_ref, xn_ref, ng_ref, sh_ref, sc_ref)
    zx = jnp.dot(h, w_ref[...], preferred_element_type=jnp.float32)
    z_ref[0] = zx[HALO:HALO + tm, :SSD_D_INNER].astype(jnp.bfloat16)
    pad_ref[...] = jnp.where(valid, zx[:, SSD_D_INNER:], 0.0)
    _conv_rows(pad_ref, cw_ref, cb_ref, xbc_ref, tm, silu=True)
    hm = h[HALO:HALO + tm]
    dt_ref[0] = jnp.dot(hm, wdt_ref[...], preferred_element_type=jnp.float32)
    dtt_ref[0] = lax.dot_general(wdtt_ref[...], hm, (((1,), (1,)), ((), ())),
                                 preferred_element_type=jnp.float32)


def _ssd_in(x, ng, sh, sc, in_w, conv_w, conv_b):
    b, length, d = x.shape
    tm = min(length, ROW_TILE)
    nzx = SSD_D_INNER + SSD_XBC
    w = in_w[:, :nzx].astype(jnp.bfloat16)
    wdt = in_w[:, nzx:]
    wdt_p = jnp.pad(wdt, ((0, 0), (0, LANES - 2 * SSD_HEADS))).astype(jnp.bfloat16)
    wdt_t = wdt.T.astype(jnp.bfloat16)
    taps = conv_w.shape[0]
    row = lambda i, j: (i, j, 0)
    per_b = lambda i, j: (i, 0, 0)
    full2 = lambda i, j: (0, 0)
    return pl.pallas_call(
        _ssd_in_kernel,
        grid=(b, length // tm),
        in_specs=_halo_specs(length, tm, d) + [
            pl.BlockSpec((1, d), full2), pl.BlockSpec((1, 1, d), per_b), pl.BlockSpec((1, 1, d), per_b),
            pl.BlockSpec((d, nzx), full2), pl.BlockSpec((d, LANES), full2),
            pl.BlockSpec((2 * SSD_HEADS, d), full2),
            pl.BlockSpec((taps, SSD_XBC), full2), pl.BlockSpec((1, SSD_XBC), full2)],
        out_specs=[pl.BlockSpec((1, tm, SSD_D_INNER), row), pl.BlockSpec((1, tm, SSD_XBC), row),
                   pl.BlockSpec((1, tm, LANES), row),
                   pl.BlockSpec((1, 2 * SSD_HEADS, tm), lambda i, j: (i, 0, j))],
        out_shape=[jax.ShapeDtypeStruct((b, length, SSD_D_INNER), jnp.bfloat16),
                   jax.ShapeDtypeStruct((b, length, SSD_XBC), jnp.float32),
                   jax.ShapeDtypeStruct((b, length, LANES), jnp.float32),
                   jax.ShapeDtypeStruct((b, 2 * SSD_HEADS, length), jnp.float32)],
        scratch_shapes=[pltpu.VMEM((tm + 2 * HALO, SSD_XBC), jnp.float32)],
        compiler_params=pltpu.CompilerParams(
            dimension_semantics=("parallel", "parallel"), vmem_limit_bytes=56 * 1024 * 1024),
        name="ssd_in",
    )(x, x, x, ng.reshape(1, d), sh, sc, w, wdt_p, wdt_t, conv_w, conv_b.reshape(1, SSD_XBC))


def _split3_bf16(v):
    p1 = v.astype(jnp.bfloat16)
    r1 = v - p1.astype(jnp.float32)
    p2 = r1.astype(jnp.bfloat16)
    p3 = (r1 - p2.astype(jnp.float32)).astype(jnp.bfloat16)
    return p1, p2, p3


def _softplus(v):
    return jnp.maximum(v, 0.0) + jnp.log1p(jnp.exp(-jnp.abs(v)))


def _expand_heads(cols, g):
    q = cols.shape[0]
    lane = lax.broadcasted_iota(jnp.int32, (q, LANES), 1)
    tiles = []
    for k in range(SSD_HEADS_PER_GROUP // 2):
        ha = g * SSD_HEADS_PER_GROUP + 2 * k
        ca = jnp.broadcast_to(cols[:, ha:ha + 1], (q, LANES))
        cb = jnp.broadcast_to(cols[:, ha + 1:ha + 2], (q, LANES))
        tiles.append(jnp.where(lane < SSD_HEAD_DIM, ca, cb))
    return jnp.concatenate(tiles, axis=1)


def _ssd_scan_kernel(x_ref, b_ref, c_ref, dt_ref, dtt_ref, dtb_ref, dtbt_ref, a_ref, at_ref, init_ref, extra_ref,
                     y_ref, fin_ref, st_ref, *, reverse, hoff, add_prev):
    ci = pl.program_id(1)

    @pl.when(ci == 0)
    def _():
        st_ref[...] = init_ref[0]

    q = SSD_CHUNK
    f32, bf16 = jnp.float32, jnp.bfloat16
    dt = _softplus(dt_ref[0][:, hoff:hoff + SSD_HEADS] + dtb_ref[...])
    dtt = _softplus(dtt_ref[0][hoff:hoff + SSD_HEADS, :] + dtbt_ref[...])
    ri = lax.broadcasted_iota(jnp.int32, (q, q), 0)
    cj = lax.broadcasted_iota(jnp.int32, (q, q), 1)
    keep = (cj >= ri) if reverse else (cj <= ri)
    tri = keep.astype(bf16)
    tri_t = ((ri >= cj) if reverse else (ri <= cj)).astype(bf16)
    acum = sum(jnp.dot(tri, p, preferred_element_type=f32) for p in _split3_bf16(dt * a_ref[...]))
    acum_t = sum(jnp.dot(p, tri_t, preferred_element_type=f32) for p in _split3_bf16(dtt * at_ref[...]))
    end = 0 if reverse else q - 1
    a_end = acum[end:end + 1, :]
    eacum = jnp.exp(acum)
    dec_end = jnp.exp(a_end - acum)
    lane = lax.broadcasted_iota(jnp.int32, (q, LANES), 1)
    for g in range(SSD_GROUPS):
        cg = c_ref[0][:, g * SSD_STATE:(g + 1) * SSD_STATE]
        bg = b_ref[0][:, g * SSD_STATE:(g + 1) * SSD_STATE]
        cg16 = cg.astype(bf16)
        cb = lax.dot_general(cg16, bg.astype(bf16), (((1,), (1,)), ((), ())), preferred_element_type=f32)
        xg = x_ref[0][:, g * SSD_GROUP_W:(g + 1) * SSD_GROUP_W]
        xdt = xg * _expand_heads(dt, g)
        xdt16 = xdt.astype(bf16)
        eac_x = _expand_heads(eacum, g)
        yd = []
        for k in range(SSD_HEADS_PER_GROUP // 2):
            xp = xdt16[:, k * LANES:(k + 1) * LANES]
            ys = []
            for hh in range(2):
                h = g * SSD_HEADS_PER_GROUP + 2 * k + hh
                seg = acum[:, h:h + 1] - acum_t[h:h + 1, :]
                lmat = jnp.exp(jnp.where(keep, seg, -jnp.inf))
                ys.append(jnp.dot((cb * lmat).astype(bf16), xp, preferred_element_type=f32))
            yd.append(jnp.where(lane < SSD_HEAD_DIM, ys[0], ys[1]))
        st = st_ref[g]
        y_off = jnp.dot(cg16, st.astype(bf16), preferred_element_type=f32) * eac_x
        cols = slice(g * SSD_GROUP_W, (g + 1) * SSD_GROUP_W)
        if add_prev:
            other = extra_ref[0, :, cols].astype(f32)
        else:
            other = extra_ref[:, cols] * xg
        y_ref[0, :, cols] = (jnp.concatenate(yd, axis=1) + y_off + other).astype(bf16)
        xdd16 = (xdt * _expand_heads(dec_end, g)).astype(bf16)
        st_ref[g] = st * eac_x[end:end + 1, :] + jnp.dot(bg.T.astype(bf16), xdd16, preferred_element_type=f32)

    @pl.when(ci == pl.num_programs(1) - 1)
    def _():
        fin_ref[0] = st_ref[...]


def _ssd_scan_p(xbc, dt_raw, dt_raw_t, dt_bias, a, init, reverse, direction, y_prev=None, d_skip=None):
    b, length, _ = xbc.shape
    nc = length // SSD_CHUNK
    q = SSD_CHUNK
    cidx = (lambda j: nc - 1 - j) if reverse else (lambda j: j)
    nb = SSD_D_INNER // SSD_GN
    hoff = direction * SSD_HEADS
    full2 = lambda i, j: (0, 0)
    st_shape = (SSD_GROUPS, SSD_STATE, SSD_GROUP_W)
    add_prev = y_prev is not None
    if add_prev:
        extra = y_prev
        extra_spec = pl.BlockSpec((1, q, SSD_D_INNER), lambda i, j: (i, cidx(j), 0))
    else:
        extra = jnp.repeat(d_skip, SSD_HEAD_DIM).reshape(1, SSD_D_INNER)
        extra_spec = pl.BlockSpec((1, SSD_D_INNER), full2)
    return pl.pallas_call(
        functools.partial(_ssd_scan_kernel, reverse=reverse, hoff=hoff, add_prev=add_prev),
        grid=(b, nc),
        in_specs=[pl.BlockSpec((1, q, SSD_D_INNER), lambda i, j: (i, cidx(j), 0)),
                  pl.BlockSpec((1, q, SSD_GN), lambda i, j: (i, cidx(j), nb)),
                  pl.BlockSpec((1, q, SSD_GN), lambda i, j: (i, cidx(j), nb + 1)),
                  pl.BlockSpec((1, q, LANES), lambda i, j: (i, cidx(j), 0)),
                  pl.BlockSpec((1, 2 * SSD_HEADS, q), lambda i, j: (i, 0, cidx(j))),
                  pl.BlockSpec((1, SSD_HEADS), full2), pl.BlockSpec((SSD_HEADS, 1), full2),
                  pl.BlockSpec((1, SSD_HEADS), full2), pl.BlockSpec((SSD_HEADS, 1), full2),
                  pl.BlockSpec((1,) + st_shape, lambda i, j: (i, 0, 0, 0)),
                  extra_spec],
        out_specs=[pl.BlockSpec((1, q, SSD_D_INNER), lambda i, j: (i, cidx(j), 0)),
                   pl.BlockSpec((1,) + st_shape, lambda i, j: (i, 0, 0, 0))],
        out_shape=[jax.ShapeDtypeStruct((b, length, SSD_D_INNER), jnp.bfloat16),
                   jax.ShapeDtypeStruct((b,) + st_shape, jnp.float32)],
        scratch_shapes=[pltpu.VMEM(st_shape, jnp.float32)],
        compiler_params=pltpu.CompilerParams(
            dimension_semantics=("parallel", "arbitrary"), vmem_limit_bytes=VMEM_LIMIT_BYTES),
        name="ssd_scan",
    )(xbc, xbc, xbc, dt_raw, dt_raw_t, dt_bias.reshape(1, -1), dt_bias.reshape(-1, 1),
      a.reshape(1, -1), a.reshape(-1, 1), init, extra)


def _ssd_out_kernel(y_ref, z_ref, ng_ref, w_ref, o_ref):
    z = z_ref[0].astype(jnp.float32)
    y = y_ref[0].astype(jnp.float32) * (z * jax.nn.sigmoid(z))
    ms = jnp.mean(y * y, axis=-1, keepdims=True)
    y = y * lax.rsqrt(ms + RMS_EPS) * ng_ref[...]
    o_ref[0] = jnp.dot(y.astype(jnp.bfloat16), w_ref[...], preferred_element_type=jnp.float32)


def _ssd_out(y, z, norm_g, out_w):
    b, length, di = y.shape
    d = out_w.shape[1]
    tm = min(length, ROW_TILE)
    row = lambda i, j: (i, j, 0)
    full2 = lambda i, j: (0, 0)
    return pl.pallas_call(
        _ssd_out_kernel,
        grid=(b, length // tm),
        in_specs=[pl.BlockSpec((1, tm, di), row), pl.BlockSpec((1, tm, di), row),
                  pl.BlockSpec((1, di), full2), pl.BlockSpec((di, d), full2)],
        out_specs=pl.BlockSpec((1, tm, d), row),
        out_shape=jax.ShapeDtypeStruct((b, length, d), jnp.float32),
        compiler_params=pltpu.CompilerParams(
            dimension_semantics=("parallel", "parallel"), vmem_limit_bytes=VMEM_LIMIT_BYTES),
        name="ssd_out",
    )(y, z, norm_g.reshape(1, di), out_w.astype(jnp.bfloat16))


def _state_to_kernel(s):
    b = s.shape[0]
    s = s.reshape(b, SSD_GROUPS, SSD_HEADS_PER_GROUP, SSD_HEAD_DIM, SSD_STATE)
    return jnp.transpose(s, (0, 1, 4, 2, 3)).reshape(b, SSD_GROUPS, SSD_STATE, SSD_GROUP_W)


def _state_from_kernel(s):
    b = s.shape[0]
    s = s.reshape(b, SSD_GROUPS, SSD_STATE, SSD_HEADS_PER_GROUP, SSD_HEAD_DIM)
    return jnp.transpose(s, (0, 1, 3, 4, 2)).reshape(b, SSD_HEADS, SSD_HEAD_DIM, SSD_STATE)


def _ssd_mixer_p(x, ng, sh, sc, init_f, init_b, in_w, conv_w, conv_b, dt_bias, a_log, d_skip, norm_g, out_w):
    z, xbc, dt_raw, dt_raw_t = _ssd_in(x, ng, sh, sc, in_w, conv_w, conv_b)
    a = -jnp.exp(a_log)
    yf, s_f = _ssd_scan_p(xbc, dt_raw, dt_raw_t, dt_bias[0], a[0], _state_to_kernel(init_f), False, 0,
                          d_skip=d_skip)
    y, s_b = _ssd_scan_p(xbc, dt_raw, dt_raw_t, dt_bias[1], a[1], _state_to_kernel(init_b), True, 1, y_prev=yf)
    m = _ssd_out(y, z, norm_g, out_w)
    return m, _state_from_kernel(s_f), _state_from_kernel(s_b)


HY_MAX_BLOCK = 512
HY_CC = 128
HY_MAC_ELEMS = 8192
HY_HIDDEN = 64
HY_FEAT_ROWS = 64


def _odd_dft_tables(n):
    m = np.arange(n, dtype=np.int64)[:, None]
    f = np.arange(n // 2, dtype=np.int64)[None, :]
    ang = 2.0 * np.pi * (((2 * f + 1) * m) % (2 * n)).astype(np.float64) / (2 * n)
    return np.cos(ang), np.sin(ang)


def _hy_in_kernel(xm_ref, xp_ref, xn_ref, ng_ref, sh_ref, sc_ref, w_ref, b_ref, cw_ref, cb_ref, o_ref, pad_ref):
    tm = xm_ref.shape[1]
    h, valid = _halo_rows(xm_ref, xp_ref, xn_ref, ng_ref, sh_ref, sc_ref)
    u = jnp.dot(h, w_ref[...], preferred_element_type=jnp.float32) + b_ref[...]
    pad_ref[...] = jnp.where(valid, u, 0.0)
    _conv_rows(pad_ref, cw_ref, cb_ref, o_ref, tm, silu=False)


def _hy_in(x, ng, sh, sc, in_w, in_b, short_w, short_b):
    b, length, d = x.shape
    n = in_w.shape[1]
    tm = min(length, ROW_TILE)
    taps = short_w.shape[0]
    per_b = lambda i, j: (i, 0, 0)
    full2 = lambda i, j: (0, 0)
    return pl.pallas_call(
        _hy_in_kernel,
        grid=(b, length // tm),
        in_specs=_halo_specs(length, tm, d) + [
            pl.BlockSpec((1, d), full2), pl.BlockSpec((1, 1, d), per_b), pl.BlockSpec((1, 1, d), per_b),
            pl.BlockSpec((d, n), full2), pl.BlockSpec((1, n), full2),
            pl.BlockSpec((taps, n), full2), pl.BlockSpec((1, n), full2)],
        out_specs=pl.BlockSpec((1, tm, n), lambda i, j: (i, j, 0)),
        out_shape=jax.ShapeDtypeStruct((b, length, n), jnp.float32),
        scratch_shapes=[pltpu.VMEM((tm + 2 * HALO, n), jnp.float32)],
        compiler_params=pltpu.CompilerParams(
            dimension_semantics=("parallel", "parallel"), vmem_limit_bytes=VMEM_LIMIT_BYTES),
        name="hy_in",
    )(x, x, x, ng.reshape(1, d), sh, sc, in_w.astype(jnp.bfloat16), in_b.reshape(1, n), short_w, short_b.reshape(1, n))


def _dot3(a, b):
    a_hi, a_lo = _split_bf16(a)
    b_hi, b_lo = _split_bf16(b)
    f32 = jnp.float32
    return (jnp.dot(a_hi, b_hi, preferred_element_type=f32) + jnp.dot(a_lo, b_hi, preferred_element_type=f32)
            + jnp.dot(a_hi, b_lo, preferred_element_type=f32))


def _hy_filter_kernel(w1t_ref, b1_ref, w2t_ref, b2_ref, w3t_ref, fr_ref, dl_ref, o_ref, *, length, blk):
    k = pl.program_id(0)
    q = (lax.broadcasted_iota(jnp.int32, (1, blk), 1) + k * blk)
    pos = jnp.abs(q - length).astype(jnp.float32)
    t = pos / float(length - 1)
    w = (2.0 * math.pi / length) * pos
    band = lax.broadcasted_iota(jnp.int32, (HY_BANDS, 1), 0).astype(jnp.float32)
    fb = 1e-4 + band * ((HY_BANDS - 1 - 1e-4) / (HY_BANDS - 1))
    z = jnp.concatenate([jnp.broadcast_to(t, (8, blk)), jnp.cos(fb * w), -jnp.sin(fb * w),
                         jnp.zeros((HY_FEAT_ROWS - 8 - 2 * HY_BANDS, blk), jnp.float32)], axis=0)
    h = jnp.sin(fr_ref[...] * (_dot3(w1t_ref[...], z) + b1_ref[...]))
    h = jnp.sin(fr_ref[...] * (_dot3(w2t_ref[...], h) + b2_ref[...]))
    kt = _dot3(w3t_ref[0], h)
    o_ref[0] = kt * jnp.exp(-t * dl_ref[...])


def _hy_filter(length, blk, f_w1, f_b1, f_w2, f_b2, f_w3, f_freq):
    d = f_w3.shape[1] // 2
    nk = 2 * length // blk
    w1t = jnp.concatenate([f_w1[0:1].T, jnp.zeros((HY_HIDDEN, 7), jnp.float32), f_w1[1:].T,
                           jnp.zeros((HY_HIDDEN, HY_FEAT_ROWS - 8 - 2 * HY_BANDS), jnp.float32)], axis=1)
    w3t = jnp.stack([f_w3[:, d:].T, f_w3[:, :d].T])
    deltas = jnp.abs(jnp.linspace(HY_MIN_DECAY, HY_MAX_DECAY, d, dtype=jnp.float32)).reshape(d, 1)
    col = lambda v: v.reshape(HY_HIDDEN, 1)
    full2 = lambda k: (0, 0)
    half = length // blk
    return pl.pallas_call(
        functools.partial(_hy_filter_kernel, length=length, blk=blk),
        grid=(nk,),
        in_specs=[pl.BlockSpec((HY_HIDDEN, HY_FEAT_ROWS), full2), pl.BlockSpec((HY_HIDDEN, 1), full2),
                  pl.BlockSpec((HY_HIDDEN, HY_HIDDEN), full2), pl.BlockSpec((HY_HIDDEN, 1), full2),
                  pl.BlockSpec((1, d, HY_HIDDEN), lambda k: (k // half, 0, 0)),
                  pl.BlockSpec((HY_HIDDEN, 1), full2), pl.BlockSpec((d, 1), full2)],
        out_specs=pl.BlockSpec((1, d, blk), lambda k: (k, 0, 0)),
        out_shape=jax.ShapeDtypeStruct((nk, d, blk), jnp.float32),
        compiler_params=pltpu.CompilerParams(
            dimension_semantics=("parallel",), vmem_limit_bytes=VMEM_LIMIT_BYTES),
        name="hy_filter",
    )(w1t, col(f_b1), f_w2.T, col(f_b2), w3t, col(f_freq), deltas)


def _hy_gspec_kernel(hi_ref, lo_ref, ft_ref, fb_ref, o_ref):
    o_ref[0] = _dot3(hi_ref[0], ft_ref[...]) + _dot3(lo_ref[0], fb_ref[...])


def _hy_gspec(kt):
    nk, d, blk = kt.shape
    cos, sin = _odd_dft_tables(2 * blk)
    top = np.concatenate([cos[:blk], -sin[:blk]], axis=1)
    bot = -np.concatenate([cos[blk:], -sin[blk:]], axis=1)
    bot[0] = 0.0
    tm = 512
    full2 = lambda e, i: (0, 0)
    return pl.pallas_call(
        _hy_gspec_kernel,
        grid=(nk - 1, d // tm),
        in_specs=[pl.BlockSpec((1, tm, blk), lambda e, i: (e + 1, i, 0)),
                  pl.BlockSpec((1, tm, blk), lambda e, i: (e, i, 0)),
                  pl.BlockSpec((blk, 2 * blk), full2), pl.BlockSpec((blk, 2 * blk), full2)],
        out_specs=pl.BlockSpec((1, tm, 2 * blk), lambda e, i: (e, i, 0)),
        out_shape=jax.ShapeDtypeStruct((nk - 1, d, 2 * blk), jnp.float32),
        compiler_params=pltpu.CompilerParams(
            dimension_semantics=("parallel", "parallel"), vmem_limit_bytes=VMEM_LIMIT_BYTES),
        name="hy_gspec",
    )(kt, kt, jnp.asarray(top, jnp.float32), jnp.asarray(bot, jnp.float32))


def _hy_conv_kernel(x0_ref, x1_ref, v_ref, g_ref, fb_ref, ff_ref, fi_ref, o_ref, lhs_ref, u_ref, y_ref, *, nb):
    cc, bsz = HY_CC, ff_ref.shape[0]
    mrows = HY_MAC_ELEMS // bsz
    for j in range(nb):
        sl = slice(j * bsz, (j + 1) * bsz)
        wj = v_ref[0, sl, :] * x1_ref[0, sl, :]
        lhs_ref[j * cc:(j + 1) * cc, :] = wj.T.astype(jnp.bfloat16)
    u_ref[...] = jnp.dot(lhs_ref[...], ff_ref[...], preferred_element_type=jnp.float32)

    def per_out_block(i, carry):
        def per_rows(rc, carry2):
            rows = pl.ds(pl.multiple_of(rc * mrows, mrows), mrows)
            acc_r = jnp.zeros((mrows, bsz), jnp.float32)
            acc_i = jnp.zeros((mrows, bsz), jnp.float32)
            for j in range(nb):
                e = i - j + (nb - 1)
                gr = g_ref[e, rows, 0:bsz]
                gi = g_ref[e, rows, bsz:2 * bsz]
                urows = pl.ds(pl.multiple_of(j * cc + rc * mrows, mrows), mrows)
                ur = u_ref[urows, 0:bsz]
                ui = u_ref[urows, bsz:2 * bsz]
                acc_r = acc_r + gr * ur - gi * ui
                acc_i = acc_i + gr * ui + gi * ur
            yrows = pl.ds(pl.multiple_of(i * cc + rc * mrows, mrows), mrows)
            y_ref[yrows, 0:bsz] = acc_r.astype(jnp.bfloat16)
            y_ref[yrows, bsz:2 * bsz] = acc_i.astype(jnp.bfloat16)
            return carry2
        return lax.fori_loop(0, cc // mrows, per_rows, carry)
    lax.fori_loop(0, nb, per_out_block, 0)

    yt = jnp.dot(y_ref[...], fi_ref[...], preferred_element_type=jnp.float32)
    for i in range(nb):
        sl = slice(i * bsz, (i + 1) * bsz)
        w = v_ref[0, sl, :] * x1_ref[0, sl, :]
        o_ref[0, sl, :] = (yt[i * cc:(i + 1) * cc, :].T + fb_ref[...] * w) * x0_ref[0, sl, :]


def _hy_conv(u, g, f_bias, blk):
    b, length, d3 = u.shape
    d = d3 // 3
    nb = length // blk
    ncb = d // HY_CC
    cos, sin = _odd_dft_tables(2 * blk)
    fwd = np.concatenate([cos[:blk], -sin[:blk]], axis=1)
    inv = (1.0 / blk) * np.concatenate([cos[:blk].T, -sin[:blk].T], axis=0)
    col = lambda off: pl.BlockSpec((1, length, HY_CC), lambda c, i, off=off: (i, 0, off + c))
    full2 = lambda c, i: (0, 0)
    return pl.pallas_call(
        functools.partial(_hy_conv_kernel, nb=nb),
        grid=(ncb, b),
        in_specs=[col(0), col(ncb), col(2 * ncb),
                  pl.BlockSpec((2 * nb - 1, HY_CC, 2 * blk), lambda c, i: (0, c, 0)),
                  pl.BlockSpec((1, HY_CC), lambda c, i: (0, c)),
                  pl.BlockSpec((blk, 2 * blk), full2), pl.BlockSpec((2 * blk, blk), full2)],
        out_specs=pl.BlockSpec((1, length, HY_CC), lambda c, i: (i, 0, c)),
        out_shape=jax.ShapeDtypeStruct((b, length, d), jnp.float32),
        scratch_shapes=[pltpu.VMEM((nb * HY_CC, blk), jnp.bfloat16),
                        pltpu.VMEM((nb * HY_CC, 2 * blk), jnp.float32),
                        pltpu.VMEM((nb * HY_CC, 2 * blk), jnp.bfloat16)],
        compiler_params=pltpu.CompilerParams(
            dimension_semantics=("parallel", "arbitrary"), vmem_limit_bytes=56 * 1024 * 1024),
        name="hy_conv",
    )(u, u, u, g, f_bias.reshape(1, d), jnp.asarray(fwd, jnp.bfloat16), jnp.asarray(inv, jnp.bfloat16))


def _hyena_mixer_p(x, ng, sh, sc, in_w, in_b, short_w, short_b, f_w1, f_b1, f_w2, f_b2, f_w3, f_freq, f_bias, out_w):
    length = x.shape[1]
    blk = min(HY_MAX_BLOCK, length)
    u = _hy_in(x, ng, sh, sc, in_w, in_b, short_w, short_b)
    g = _hy_gspec(_hy_filter(length, blk, f_w1, f_b1, f_w2, f_b2, f_w3, f_freq))
    return _mm3(_hy_conv(u, g, f_bias, blk), out_w)


TOK_TILE = 256
MOE_ROWS = 512
SEG_CHUNK = 64
BF16_TILE_ROWS = 16
LANES = 128


def _split_bf16(w):
    hi = w.astype(jnp.bfloat16)
    lo = (w - hi.astype(jnp.float32)).astype(jnp.bfloat16)
    return hi, lo


def _moe_pre_kernel(x_ref, m_ref, g1_ref, ng_ref, sh_ref, sc_ref, wrh_ref, wrl_ref,
                    xo_ref, hpk_ref, lg_ref):
    x = x_ref[0] + g1_ref[0] * m_ref[0]
    xo_ref[0] = x
    ms = jnp.mean(x * x, axis=-1, keepdims=True)
    h = x * lax.rsqrt(ms + RMS_EPS) * ng_ref[...]
    h = h * (1.0 + sc_ref[0]) + sh_ref[0]
    h_hi = h.astype(jnp.bfloat16)
    h_lo = (h - h_hi.astype(jnp.float32)).astype(jnp.bfloat16)
    dn = (((1,), (1,)), ((), ()))
    lg = lax.dot_general(wrh_ref[...], h_hi, dn, preferred_element_type=jnp.float32)
    lg += lax.dot_general(wrh_ref[...], h_lo, dn, preferred_element_type=jnp.float32)
    lg += lax.dot_general(wrl_ref[...], h_hi, dn, preferred_element_type=jnp.float32)
    lg_ref[0] = lg
    half = h.shape[1] // 2
    wa = pltpu.bitcast(h_hi[:, :half].astype(jnp.float32), jnp.uint32) >> 16
    wb = pltpu.bitcast(h_hi[:, half:].astype(jnp.float32), jnp.uint32) & jnp.uint32(0xFFFF0000)
    hpk_ref[0] = wa | wb


def _moe_pre(x, m, g1, ng, sh, sc, w_router):
    b, length, d = x.shape
    tm = min(length, 512)
    wrh, wrl = _split_bf16(w_router.T)
    row = lambda i, j: (i, j, 0)
    per_b = lambda i, j: (i, 0, 0)
    full2 = lambda i, j: (0, 0)
    return pl.pallas_call(
        _moe_pre_kernel,
        grid=(b, length // tm),
        in_specs=[pl.BlockSpec((1, tm, d), row), pl.BlockSpec((1, tm, d), row),
                  pl.BlockSpec((1, 1, d), per_b), pl.BlockSpec((1, d), full2),
                  pl.BlockSpec((1, 1, d), per_b), pl.BlockSpec((1, 1, d), per_b),
                  pl.BlockSpec((N_EXPERTS, d), full2), pl.BlockSpec((N_EXPERTS, d), full2)],
        out_specs=[pl.BlockSpec((1, tm, d), row), pl.BlockSpec((1, tm, d // 2), row),
                   pl.BlockSpec((1, N_EXPERTS, tm), lambda i, j: (i, 0, j))],
        out_shape=[jax.ShapeDtypeStruct((b, length, d), jnp.float32),
                   jax.ShapeDtypeStruct((b, length, d // 2), jnp.uint32),
                   jax.ShapeDtypeStruct((b, N_EXPERTS, length), jnp.float32)],
        compiler_params=pltpu.CompilerParams(
            dimension_semantics=("parallel", "parallel"), vmem_limit_bytes=VMEM_LIMIT_BYTES),
        name="moe_pre",
    )(x, m, g1, ng.reshape(1, d), sh, sc, wrh, wrl)


def _moe_ffn_kernel(idx_ref, nidx_ref, h_hbm, gate_ref, wg_ref, wu_ref, wd_ref, y_ref,
                    xe_ref, wgb, wub, wdb, sem):
    nblk = pl.num_programs(1)
    step = pl.program_id(0) * nblk + pl.program_id(1)
    last = pl.num_programs(0) * nblk - 1

    def issue(ids_ref, slot):
        base = slot * MOE_ROWS
        for c in range(MOE_ROWS):
            pltpu.make_async_copy(h_hbm.at[pl.ds(ids_ref[0, 0, c], 1)], xe_ref.at[pl.ds(base + c, 1)],
                                  sem.at[slot]).start()

    @pl.when(step == 0)
    def _():
        issue(idx_ref, 0)

    for parity in range(2):
        @pl.when((step < last) & (step % 2 == parity))
        def _(parity=parity):
            issue(nidx_ref, 1 - parity)

    @pl.when(pl.program_id(1) == 0)
    def _():
        wgb[...] = wg_ref[0].astype(jnp.bfloat16)
        wub[...] = wu_ref[0].astype(jnp.bfloat16)
        wdb[...] = wd_ref[0].astype(jnp.bfloat16)

    slot = step % 2
    rows = pl.ds(pl.multiple_of(slot * MOE_ROWS, MOE_ROWS), MOE_ROWS)
    pltpu.make_async_copy(h_hbm.at[pl.ds(0, MOE_ROWS)], xe_ref.at[rows], sem.at[slot]).wait()
    half = wgb.shape[0] // 2
    w = xe_ref[rows, :]
    xa = pltpu.bitcast(w << 16, jnp.float32).astype(jnp.bfloat16)
    xb = pltpu.bitcast(w & jnp.uint32(0xFFFF0000), jnp.float32).astype(jnp.bfloat16)
    hg = jnp.dot(xa, wgb[:half], preferred_element_type=jnp.float32)
    hg += jnp.dot(xb, wgb[half:], preferred_element_type=jnp.float32)
    hu = jnp.dot(xa, wub[:half], preferred_element_type=jnp.float32)
    hu += jnp.dot(xb, wub[half:], preferred_element_type=jnp.float32)
    hid = (hg * jax.nn.sigmoid(hg) * hu).astype(jnp.bfloat16)
    y = jnp.dot(hid, wdb[...], preferred_element_type=jnp.float32)
    y_ref[0] = (y * gate_ref[0]).astype(jnp.bfloat16)


def _moe_ffn(hpk, grow, gate, w_gate, w_up, w_down):
    e, r = grow.shape
    d, f = w_gate.shape[1], w_gate.shape[2]
    nblk = r // MOE_ROWS
    nsteps = e * nblk
    wspec = lambda shp: pl.BlockSpec((1,) + shp, lambda i, j: (i, 0, 0))
    ids = grow.reshape(nsteps, 1, MOE_ROWS)
    smem_ids = lambda off: pl.BlockSpec(
        (1, 1, MOE_ROWS), lambda i, j: (jnp.minimum(i * nblk + j + off, nsteps - 1), 0, 0), memory_space=pltpu.SMEM)
    return pl.pallas_call(
        _moe_ffn_kernel,
        grid=(e, nblk),
        in_specs=[smem_ids(0), smem_ids(1),
                  pl.BlockSpec(memory_space=pltpu.HBM),
                  pl.BlockSpec((1, MOE_ROWS, 1), lambda i, j: (i, j, 0)),
                  wspec((d, f)), wspec((d, f)), wspec((f, d))],
        out_specs=pl.BlockSpec((1, MOE_ROWS, d), lambda i, j: (i, j, 0)),
        out_shape=jax.ShapeDtypeStruct((e, r, d), jnp.bfloat16),
        scratch_shapes=[pltpu.VMEM((2 * MOE_ROWS, d // 2), jnp.uint32),
                        pltpu.VMEM((d, f), jnp.bfloat16), pltpu.VMEM((d, f), jnp.bfloat16),
                        pltpu.VMEM((f, d), jnp.bfloat16),
                        pltpu.SemaphoreType.DMA((2,))],
        compiler_params=pltpu.CompilerParams(
            dimension_semantics=("arbitrary", "arbitrary"), vmem_limit_bytes=VMEM_LIMIT_BYTES),
        name="moe_ffn",
    )(ids, ids, hpk, gate, w_gate, w_up, w_down)


def _moe_comb_kernel(cs_ref, x_ref, g2_ref, y_ref, idx_ref, o_ref, ycat, acc, *, cap, ch, ntile):
    b = pl.program_id(0)
    t = pl.program_id(1)
    base = t * TOK_TILE
    sub = lax.broadcasted_iota(jnp.int32, (TOK_TILE, LANES), 0) + base
    if ntile == 1:
        for e in range(N_EXPERTS):
            ycat[e * ch:(e + 1) * ch, :] = y_ref[e, 0:ch, :]
        v = idx_ref[0]
        tiles = [(v[:, p * LANES:(p + 1) * LANES] == sub).astype(jnp.bfloat16)
                 for p in range(N_EXPERTS * ch // LANES)]
        acc[...] = jnp.dot(jnp.concatenate(tiles, axis=1), ycat[...], preferred_element_type=jnp.float32)
    else:
        lane = lax.broadcasted_iota(jnp.int32, (1, LANES), 1)
        per = LANES // ch
        sts = []
        for e in range(N_EXPERTS):
            s0 = cs_ref[(b * N_EXPERTS + e) * (ntile + 1) + t]
            st = jnp.minimum((s0 // BF16_TILE_ROWS) * BF16_TILE_ROWS, cap - ch)
            st = pl.multiple_of(st, BF16_TILE_ROWS)
            sts.append(st)
            ycat[e * ch:(e + 1) * ch, :] = y_ref[e, pl.ds(st, ch), :]
        tiles = []
        for p in range(N_EXPERTS // per):
            v = None
            for q in range(per):
                e = p * per + q
                r = pltpu.roll(idx_ref[0, e:e + 1, :], (2 * cap - sts[e] + q * ch) % cap, 1)[:, :LANES]
                v = r if v is None else jnp.where(lane >= q * ch, r, v)
            tiles.append((v == sub).astype(jnp.bfloat16))
        acc[...] = jnp.dot(jnp.concatenate(tiles, axis=1), ycat[...], preferred_element_type=jnp.float32)
        sub_c = lax.broadcasted_iota(jnp.int32, (TOK_TILE, ch), 0) + base
        lane_c = lax.broadcasted_iota(jnp.int32, (1, ch), 1)
        for e in range(N_EXPERTS):
            s1 = cs_ref[(b * N_EXPERTS + e) * (ntile + 1) + t + 1]
            first_end = sts[e] + ch
            n_extra = jnp.maximum(s1 - first_end + ch - 1, 0) // ch

            def extra(q, carry, e=e, first_end=first_end):
                lo = first_end + q * ch
                stq = pl.multiple_of(jnp.minimum(lo, cap - ch), BF16_TILE_ROWS)
                r = pltpu.roll(idx_ref[0, e:e + 1, :], (2 * cap - stq) % cap, 1)[:, :ch]
                hit = (r == sub_c) & (lane_c + stq >= lo)
                acc[...] += jnp.dot(hit.astype(jnp.bfloat16), y_ref[e, pl.ds(stq, ch), :],
                                    preferred_element_type=jnp.float32)
                return carry
            lax.fori_loop(0, n_extra, extra, 0)
    o_ref[0] = x_ref[0] + g2_ref[0] * acc[...]


def _moe_combine(x, g2, y, idx, cs):
    b, length, d = x.shape
    cap = idx.shape[2]
    ntile = length // TOK_TILE
    ch = min(SEG_CHUNK, cap)
    if ntile == 1:
        idx_in = idx.reshape(b, 1, N_EXPERTS * cap)
        idx_spec = pl.BlockSpec((1, 1, N_EXPERTS * cap), lambda i, j, c: (i, 0, 0))
    else:
        idx_in = idx
        idx_spec = pl.BlockSpec((1, N_EXPERTS, cap), lambda i, j, c: (i, 0, 0))
    grid_spec = pltpu.PrefetchScalarGridSpec(
        num_scalar_prefetch=1,
        grid=(b, ntile),
        in_specs=[pl.BlockSpec((1, TOK_TILE, d), lambda i, j, c: (i, j, 0)),
                  pl.BlockSpec((1, 1, d), lambda i, j, c: (i, 0, 0)),
                  pl.BlockSpec((N_EXPERTS, cap, d), lambda i, j, c: (0, i, 0)),
                  idx_spec],
        out_specs=pl.BlockSpec((1, TOK_TILE, d), lambda i, j, c: (i, j, 0)),
        scratch_shapes=[pltpu.VMEM((N_EXPERTS * ch, d), jnp.bfloat16),
                        pltpu.VMEM((TOK_TILE, d), jnp.float32)])
    return pl.pallas_call(
        functools.partial(_moe_comb_kernel, cap=cap, ch=ch, ntile=ntile),
        grid_spec=grid_spec,
        out_shape=jax.ShapeDtypeStruct((b, length, d), jnp.float32),
        compiler_params=pltpu.CompilerParams(
            dimension_semantics=("arbitrary", "arbitrary"), vmem_limit_bytes=56 * 1024 * 1024),
        name="moe_combine",
    )(cs.reshape(-1).astype(jnp.int32), x, g2, y, idx_in)


def _moe_block(x, m, g1, ng, sh, sc, g2, w_router, w_gate, w_up, w_down):
    b, length, d = x.shape
    cap = EC_FACTOR * length // N_EXPERTS
    x1, hpk, lg = _moe_pre(x, m, g1, ng, sh, sc, w_router)
    aff = jax.nn.softmax(lg, axis=1)
    _, idx = lax.top_k(aff, cap)
    idx = jnp.sort(idx, axis=-1)
    gate = jnp.take_along_axis(aff, idx, axis=-1)
    ntile = length // TOK_TILE
    bounds = jnp.arange(ntile + 1, dtype=jnp.int32) * TOK_TILE
    cs = jnp.sum(idx[:, :, :, None] < bounds, axis=2, dtype=jnp.int32)
    grow = idx + (jnp.arange(b, dtype=jnp.int32) * length)[:, None, None]
    grow = jnp.swapaxes(grow, 0, 1).reshape(N_EXPERTS, b * cap)
    gate_e = jnp.swapaxes(gate, 0, 1).reshape(N_EXPERTS, b * cap, 1)
    y = _moe_ffn(hpk.reshape(b * length, d // 2), grow, gate_e, w_gate, w_up, w_down)
    return _moe_combine(x1, g2, y, idx, cs)


def kernel(x_prompt, x_sample, state_ssd, c, c_ctx, norm_g, ada_w, ada_b, hy_in_w, hy_in_b, hy_short_w, hy_short_b, hy_f_w1, hy_f_b1, hy_f_w2, hy_f_b2, hy_f_w3, hy_f_freq, hy_f_bias, hy_out_w, ssd_in_w, ssd_conv_w, ssd_conv_b, ssd_dt_bias, ssd_A_log, ssd_D, ssd_norm_g, ssd_out_w, moe_router, moe_w_gate, moe_w_up, moe_w_down, final_norm_g):
    rows = x_sample.shape[1] // GRID_W
    xp = x_prompt
    xs = x_sample + _sincos_2d(rows, GRID_W, D_MODEL)[None]
    new_ssd = []
    for i in range(DEPTH):
        sh1p, sc1p, g1p, sh2p, sc2p, g2p = _adaln(c_ctx[None, :], ada_w[i], ada_b[i])
        sh1s, sc1s, g1s, sh2s, sc2s, g2s = _adaln(c, ada_w[i], ada_b[i])
        j = i // N_MIXERS
        bp = (xp.shape[0], 1, D_MODEL)
        if i % N_MIXERS == 0:
            hy = (hy_in_w[j], hy_in_b[j], hy_short_w[j], hy_short_b[j], hy_f_w1[j], hy_f_b1[j],
                  hy_f_w2[j], hy_f_b2[j], hy_f_w3[j], hy_f_freq[j], hy_f_bias[j], hy_out_w[j])
            mp = _hyena_mixer_p(xp, norm_g[i, 0], jnp.broadcast_to(sh1p, bp), jnp.broadcast_to(sc1p, bp), *hy)
            ms = _hyena_mixer_p(xs, norm_g[i, 0], sh1s, sc1s, *hy)
        else:
            sp = (ssd_in_w[j], ssd_conv_w[j], ssd_conv_b[j], ssd_dt_bias[j], ssd_A_log[j],
                  ssd_D[j], ssd_norm_g[j], ssd_out_w[j])
            zeros = jnp.zeros((xp.shape[0], SSD_HEADS, SSD_HEAD_DIM, SSD_STATE), jnp.float32)
            mp, s_f, s_b = _ssd_mixer_p(xp, norm_g[i, 0], jnp.broadcast_to(sh1p, bp),
                                        jnp.broadcast_to(sc1p, bp), zeros, zeros, *sp)
            new_ssd.append(jnp.stack([s_f, s_b], axis=1))
            ms, _, _ = _ssd_mixer_p(xs, norm_g[i, 0], sh1s, sc1s, state_ssd[:, j, 0], state_ssd[:, j, 1], *sp)
        moe = (moe_router[i], moe_w_gate[i], moe_w_up[i], moe_w_down[i])
        xp = _moe_block(xp, mp, jnp.broadcast_to(g1p, bp), norm_g[i, 1], jnp.broadcast_to(sh2p, bp),
                        jnp.broadcast_to(sc2p, bp), jnp.broadcast_to(g2p, bp), *moe)
        xs = _moe_block(xs, ms, g1s, norm_g[i, 1], sh2s, sc2s, g2s, *moe)
    y_prompt = _rmsnorm(xp, final_norm_g)
    y_sample = _rmsnorm(xs, final_norm_g)
    new_state_ssd = jnp.stack(new_ssd, axis=1)
    return (y_prompt, y_sample, new_state_ssd)
```

```python
import functools
import math

import jax
import jax.numpy as jnp
import numpy as np
from jax import lax
from jax.experimental import pallas as pl
from jax.experimental.pallas import tpu as pltpu

D_MODEL = 1024
DEPTH = 2
GRID_W = 64
N_MIXERS = 2
RMS_EPS = 1e-6
HY_EMB = 33
HY_BANDS = (HY_EMB - 1) // 2
HY_SHORT_DECAY_FRAC = 0.3
HY_LONG_DECAY_FRAC = 1.5
HY_DECAY_TARGET = 1e-2
HY_MAX_DECAY = math.log(HY_DECAY_TARGET) / HY_SHORT_DECAY_FRAC
HY_MIN_DECAY = math.log(HY_DECAY_TARGET) / HY_LONG_DECAY_FRAC
SSD_D_INNER = 2 * D_MODEL
SSD_HEAD_DIM = 64
SSD_HEADS = SSD_D_INNER // SSD_HEAD_DIM
SSD_GROUPS = 4
SSD_STATE = 128
SSD_CHUNK = 128
SSD_XBC = SSD_D_INNER + 2 * SSD_GROUPS * SSD_STATE
N_EXPERTS = 16
EC_FACTOR = 2

VMEM_LIMIT_BYTES = 48 * 1024 * 1024


def _mm_kernel(a_ref, b_ref, o_ref, acc_ref):
    @pl.when(pl.program_id(2) == 0)
    def _():
        acc_ref[...] = jnp.zeros_like(acc_ref)

    acc_ref[...] += jnp.dot(a_ref[...].astype(jnp.bfloat16), b_ref[...],
                            preferred_element_type=jnp.float32)

    @pl.when(pl.program_id(2) == pl.num_programs(2) - 1)
    def _():
        o_ref[...] = acc_ref[...]


def _pick(n, pref):
    for t in pref:
        if n % t == 0:
            return t
    return n


def _mm(a, b):
    m, k = a.shape
    n = b.shape[1]
    mp = -(-m // 8) * 8
    if mp != m:
        a = jnp.pad(a, ((0, mp - m), (0, 0)))
    tm = _pick(mp, (512, 256, 128, 64, 32, 16, 8))
    tn = _pick(n, (512, 256, 128))
    tk = _pick(k, (1024, 512, 256, 128))
    out = pl.pallas_call(
        _mm_kernel,
        grid=(mp // tm, n // tn, k // tk),
        in_specs=[pl.BlockSpec((tm, tk), lambda i, j, l: (i, l)),
                  pl.BlockSpec((tk, tn), lambda i, j, l: (l, j))],
        out_specs=pl.BlockSpec((tm, tn), lambda i, j, l: (i, j)),
        out_shape=jax.ShapeDtypeStruct((mp, n), jnp.float32),
        scratch_shapes=[pltpu.VMEM((tm, tn), jnp.float32)],
        compiler_params=pltpu.CompilerParams(
            dimension_semantics=("parallel", "parallel", "arbitrary"),
            vmem_limit_bytes=VMEM_LIMIT_BYTES),
        name="mm",
    )(a, b.astype(jnp.bfloat16))
    return out[:m]


def _mm3(a, b):
    lead = a.shape[:-1]
    return _mm(a.reshape(-1, a.shape[-1]), b).reshape(*lead, b.shape[1])


def _bmm_kernel(a_ref, b_ref, o_ref):
    o_ref[0] = jnp.dot(a_ref[0].astype(jnp.bfloat16), b_ref[0],
                       preferred_element_type=jnp.float32)


def _bmm(a, b):
    e, m, k = a.shape
    n = b.shape[2]
    tm = _pick(m, (512, 256, 128, 64, 32, 16, 8))
    tn = _pick(n, (512, 256, 128))
    return pl.pallas_call(
        _bmm_kernel,
        grid=(e, m // tm, n // tn),
        in_specs=[pl.BlockSpec((1, tm, k), lambda g, i, j: (g, i, 0)),
                  pl.BlockSpec((1, k, tn), lambda g, i, j: (g, 0, j))],
        out_specs=pl.BlockSpec((1, tm, tn), lambda g, i, j: (g, i, j)),
        out_shape=jax.ShapeDtypeStruct((e, m, n), jnp.float32),
        compiler_params=pltpu.CompilerParams(
            dimension_semantics=("parallel", "parallel", "parallel"),
            vmem_limit_bytes=VMEM_LIMIT_BYTES),
        name="bmm",
    )(a, b.astype(jnp.bfloat16))


def _rmsnorm(x, g):
    y = x * lax.rsqrt(jnp.mean(x * x, axis=-1, keepdims=True) + RMS_EPS)
    return y * g


def _adaln(cond, ada_w, ada_b):
    m = jnp.dot(jax.nn.silu(cond), ada_w, precision=lax.Precision.HIGHEST) + ada_b
    return jnp.split(m[:, None, :], 6, axis=-1)


def _modulate(h, shift, scale):
    return h * (1 + scale) + shift


def _dwconv(x, w, b):
    k, ch = w.shape
    y = lax.conv_general_dilated(x, w[:, None, :], window_strides=(1,),
                                 padding=[(k // 2, k // 2)],
                                 dimension_numbers=('NWC', 'WIO', 'NWC'),
                                 feature_group_count=ch,
                                 precision=lax.Precision.HIGHEST)
    return y + b


def _sincos_2d(rows, cols, d):
    q = d // 4
    omega = 1.0 / (10000.0 ** (jnp.arange(q, dtype=jnp.float32) / q))
    t = jnp.arange(rows * cols)
    er = (t // cols).astype(jnp.float32)[:, None] * omega[None, :]
    ec = (t % cols).astype(jnp.float32)[:, None] * omega[None, :]
    return jnp.concatenate([jnp.sin(er), jnp.cos(er), jnp.sin(ec), jnp.cos(ec)], axis=-1)


def _hyena_filters(length, f_w1, f_b1, f_w2, f_b2, f_w3, f_freq):
    hp = lax.Precision.HIGHEST
    t = jnp.linspace(0.0, 1.0, length, dtype=jnp.float32)[:, None]
    w = 2.0 * math.pi * jnp.arange(length, dtype=jnp.float32)[:, None] / length
    f = jnp.linspace(1e-4, HY_BANDS - 1, HY_BANDS, dtype=jnp.float32)[None, :]
    z = jnp.concatenate([t, jnp.cos(f * w), -jnp.sin(f * w)], axis=-1)
    h = jnp.sin(f_freq * (jnp.dot(z, f_w1, precision=hp) + f_b1))
    h = jnp.sin(f_freq * (jnp.dot(h, f_w2, precision=hp) + f_b2))
    h = jnp.dot(h, f_w3, precision=hp)
    deltas = jnp.linspace(HY_MIN_DECAY, HY_MAX_DECAY, D_MODEL, dtype=jnp.float32)
    window = jnp.exp(-t * jnp.abs(deltas)[None, :])
    return h[:, :D_MODEL] * window, h[:, D_MODEL:] * window


def _bidir_long_conv(u, h_fwd, h_bwd, bias):
    b, length, ch = u.shape
    n = 2 * length
    k = jnp.concatenate([h_fwd, jnp.zeros((1, ch), jnp.float32), h_bwd[1:][::-1]], axis=0)
    kf = jnp.fft.rfft(k, n=n, axis=0)
    uf = jnp.fft.rfft(u, n=n, axis=1)
    y = jnp.fft.irfft(uf * kf[None], n=n, axis=1)[:, :length]
    return y + u * bias


def _hyena_mixer(h, in_w, in_b, short_w, short_b, f_w1, f_b1, f_w2, f_b2, f_w3, f_freq, f_bias, out_w):
    length = h.shape[1]
    u = _dwconv(_mm3(h, in_w) + in_b, short_w, short_b)
    x0 = u[..., :D_MODEL]
    x1 = u[..., D_MODEL:2 * D_MODEL]
    v = u[..., 2 * D_MODEL:]
    h_fwd, h_bwd = _hyena_filters(length, f_w1, f_b1, f_w2, f_b2, f_w3, f_freq)
    v = _bidir_long_conv(v * x1, h_fwd, h_bwd, f_bias)
    return _mm3(v * x0, out_w)


def _ssd_scan(x, dt, a, bm, cm, init):
    hp = lax.Precision.HIGHEST
    b, length = x.shape[:2]
    nc = length // SSD_CHUNK
    q, g, r = SSD_CHUNK, SSD_GROUPS, SSD_HEADS // SSD_GROUPS
    x = x.reshape(b, nc, q, g, r, SSD_HEAD_DIM)
    dt = dt.reshape(b, nc, q, g, r)
    bm = bm.reshape(b, nc, q, g, SSD_STATE)
    cm = cm.reshape(b, nc, q, g, SSD_STATE)
    a_cum = jnp.cumsum(dt * a.reshape(g, r), axis=2)
    xdt = x * dt[..., None]
    seg = a_cum[:, :, :, None] - a_cum[:, :, None, :]
    causal = jnp.tril(jnp.ones((q, q), dtype=bool))[None, None, :, :, None, None]
    decay = jnp.exp(jnp.where(causal, seg, -jnp.inf))
    cb = jnp.einsum('bclgn,bcsgn->bclsg', cm, bm, precision=hp)
    y_diag = jnp.einsum('bclsgr,bcsgrp->bclgrp', cb[..., None] * decay, xdt, precision=hp)
    decay_end = jnp.exp(a_cum[:, :, -1:] - a_cum)
    chunk_states = jnp.einsum('bcsgn,bcsgrp->bcgrpn', bm, xdt * decay_end[..., None], precision=hp)
    chunk_decay = jnp.exp(a_cum[:, :, -1])

    def step(s, inp):
        st, dec = inp
        return s * dec[..., None, None] + st, s

    init_g = init.astype(jnp.float32).reshape(b, g, r, SSD_HEAD_DIM, SSD_STATE)
    final, starts = lax.scan(step, init_g, (jnp.moveaxis(chunk_states, 1, 0), jnp.moveaxis(chunk_decay, 1, 0)))
    starts = jnp.moveaxis(starts, 0, 1)
    y_off = jnp.einsum('bclgn,bcgrpn->bclgrp', cm, starts, precision=hp) * jnp.exp(a_cum)[..., None]
    y = (y_diag + y_off).reshape(b, length, SSD_HEADS, SSD_HEAD_DIM)
    return y, final.reshape(b, SSD_HEADS, SSD_HEAD_DIM, SSD_STATE)


def _flip(t):
    return jnp.flip(t, axis=1)


def _ssd_mixer(h, init_f, init_b, in_w, conv_w, conv_b, dt_bias, a_log, d_skip, norm_g, out_w):
    b, length, _ = h.shape
    proj = _mm3(h, in_w)
    z = proj[..., :SSD_D_INNER]
    xbc = jax.nn.silu(_dwconv(proj[..., SSD_D_INNER:SSD_D_INNER + SSD_XBC], conv_w, conv_b))
    dt_raw = proj[..., SSD_D_INNER + SSD_XBC:]
    gn = SSD_GROUPS * SSD_STATE
    xh = xbc[..., :SSD_D_INNER].reshape(b, length, SSD_HEADS, SSD_HEAD_DIM)
    bm = xbc[..., SSD_D_INNER:SSD_D_INNER + gn].reshape(b, length, SSD_GROUPS, SSD_STATE)
    cm = xbc[..., SSD_D_INNER + gn:].reshape(b, length, SSD_GROUPS, SSD_STATE)
    dt = jax.nn.softplus(dt_raw.reshape(b, length, 2, SSD_HEADS) + dt_bias)
    a = -jnp.exp(a_log)
    y_f, s_f = _ssd_scan(xh, dt[:, :, 0], a[0], bm, cm, init_f)
    y_b, s_b = _ssd_scan(_flip(xh), _flip(dt[:, :, 1]), a[1], _flip(bm), _flip(cm), init_b)
    y = y_f + _flip(y_b) + d_skip[:, None] * xh
    y = y.reshape(b, length, SSD_D_INNER) * jax.nn.silu(z)
    y = _rmsnorm(y, norm_g)
    return _mm3(y, out_w), s_f, s_b


SSD_GN = SSD_GROUPS * SSD_STATE
SSD_GROUP_W = SSD_D_INNER // SSD_GROUPS
SSD_HEADS_PER_GROUP = SSD_HEADS // SSD_GROUPS
ROW_TILE = 256


def _modnorm(x, ng, sh, sc):
    ms = jnp.mean(x * x, axis=-1, keepdims=True)
    return (x * lax.rsqrt(ms + RMS_EPS) * ng) * (1.0 + sc) + sh


HALO = 8
CONV_COLS = 512


def _halo_rows(xm_ref, xp_ref, xn_ref, ng_ref, sh_ref, sc_ref):
    j = pl.program_id(1)
    xa = jnp.concatenate([xp_ref[0], xm_ref[0], xn_ref[0]], axis=0)
    h = _modnorm(xa, ng_ref[...], sh_ref[0], sc_ref[0]).astype(jnp.bfloat16)
    tm = xm_ref.shape[1]
    r = lax.broadcasted_iota(jnp.int32, (tm + 2 * HALO, 1), 0)
    valid = ((r >= HALO) | (j > 0)) & ((r < tm + HALO) | (j < pl.num_programs(1) - 1))
    return h, valid


def _conv_rows(pad_ref, cw_ref, cb_ref, o_ref, tm, silu):
    taps = cw_ref.shape[0]
    ncol = pad_ref.shape[1]
    for c0 in range(0, ncol, CONV_COLS):
        cols = slice(c0, c0 + CONV_COLS)
        acc = cb_ref[:, cols] + jnp.zeros((tm, CONV_COLS), jnp.float32)
        for k in range(taps):
            off = HALO + k - taps // 2
            acc = acc + cw_ref[k:k + 1, cols] * pad_ref[off:off + tm, cols]
        if silu:
            acc = acc * jax.nn.sigmoid(acc)
        o_ref[0, :, cols] = acc


def _halo_specs(length, tm, d):
    nh = length // HALO
    per = tm // HALO
    main = pl.BlockSpec((1, tm, d), lambda i, j: (i, j, 0))
    prev = pl.BlockSpec((1, HALO, d), lambda i, j: (i, jnp.maximum(j * per - 1, 0), 0))
    nxt = pl.BlockSpec((1, HALO, d), lambda i, j: (i, jnp.minimum((j + 1) * per, nh - 1), 0))
    return [main, prev, nxt]


def _ssd_in_kernel(xm_ref, xp_ref, xn_ref, ng_ref, sh_ref, sc_ref, w_ref, wdt_ref, wdtt_ref, cw_ref, cb_ref,
                   z_ref, xbc_ref, dt_ref, dtt_ref, pad_ref):
    tm = xm_ref.shape[1]
    h, valid = _halo_rows(xm_ref, xp_ref, xn_ref, ng_ref, sh_ref, sc_ref)
    zx = jnp.dot(h, w_ref[...], preferred_element_type=jnp.float32)
    z_ref[0] = zx[HALO:HALO + tm, :SSD_D_INNER].astype(jnp.bfloat16)
    pad_ref[...] = jnp.where(valid, zx[:, SSD_D_INNER:], 0.0)
    _conv_rows(pad_ref, cw_ref, cb_ref, xbc_ref, tm, silu=True)
    hm = h[HALO:HALO + tm]
    dt_ref[0] = jnp.dot(hm, wdt_ref[...], preferred_element_type=jnp.float32)
    dtt_ref[0] = lax.dot_general(wdtt_ref[...], hm, (((1,), (1,)), ((), ())),
                                 preferred_element_type=jnp.float32)


def _ssd_in(x, ng, sh, sc, in_w, conv_w, conv_b):
    b, length, d = x.shape
    tm = min(length, ROW_TILE)
    nzx = SSD_D_INNER + SSD_XBC
    w = in_w[:, :nzx].astype(jnp.bfloat16)
    wdt = in_w[:, nzx:]
    wdt_p = jnp.pad(wdt, ((0, 0), (0, LANES - 2 * SSD_HEADS))).astype(jnp.bfloat16)
    wdt_t = wdt.T.astype(jnp.bfloat16)
    taps = conv_w.shape[0]
    row = lambda i, j: (i, j, 0)
    per_b = lambda i, j: (i, 0, 0)
    full2 = lambda i, j: (0, 0)
    return pl.pallas_call(
        _ssd_in_kernel,
        grid=(b, length // tm),
        in_specs=_halo_specs(length, tm, d) + [
            pl.BlockSpec((1, d), full2), pl.BlockSpec((1, 1, d), per_b), pl.BlockSpec((1, 1, d), per_b),
            pl.BlockSpec((d, nzx), full2), pl.BlockSpec((d, LANES), full2),
            pl.BlockSpec((2 * SSD_HEADS, d), full2),
            pl.BlockSpec((taps, SSD_XBC), full2), pl.BlockSpec((1, SSD_XBC), full2)],
        out_specs=[pl.BlockSpec((1, tm, SSD_D_INNER), row), pl.BlockSpec((1, tm, SSD_XBC), row),
                   pl.BlockSpec((1, tm, LANES), row),
                   pl.BlockSpec((1, 2 * SSD_HEADS, tm), lambda i, j: (i, 0, j))],
        out_shape=[jax.ShapeDtypeStruct((b, length, SSD_D_INNER), jnp.bfloat16),
                   jax.ShapeDtypeStruct((b, length, SSD_XBC), jnp.float32),
                   jax.ShapeDtypeStruct((b, length, LANES), jnp.float32),
                   jax.ShapeDtypeStruct((b, 2 * SSD_HEADS, length), jnp.float32)],
        scratch_shapes=[pltpu.VMEM((tm + 2 * HALO, SSD_XBC), jnp.float32)],
        compiler_params=pltpu.CompilerParams(
            dimension_semantics=("parallel", "parallel"), vmem_limit_bytes=56 * 1024 * 1024),
        name="ssd_in",
    )(x, x, x, ng.reshape(1, d), sh, sc, w, wdt_p, wdt_t, conv_w, conv_b.reshape(1, SSD_XBC))


def _split3_bf16(v):
    p1 = v.astype(jnp.bfloat16)
    r1 = v - p1.astype(jnp.float32)
    p2 = r1.astype(jnp.bfloat16)
    p3 = (r1 - p2.astype(jnp.float32)).astype(jnp.bfloat16)
    return p1, p2, p3


def _softplus(v):
    return jnp.maximum(v, 0.0) + jnp.log1p(jnp.exp(-jnp.abs(v)))


def _expand_heads(cols, g, hoff):
    q = cols.shape[0]
    lane = lax.broadcasted_iota(jnp.int32, (q, LANES), 1)
    tiles = []
    for k in range(SSD_HEADS_PER_GROUP // 2):
        ha = hoff + g * SSD_HEADS_PER_GROUP + 2 * k
        tiles.append(jnp.take_along_axis(cols, jnp.where(lane < SSD_HEAD_DIM, ha, ha + 1), axis=1))
    return jnp.concatenate(tiles, axis=1)


def _ssd_scan_kernel(x_ref, b_ref, c_ref, dt_ref, dtt_ref, dtb_ref, dtbt_ref, a_ref, at_ref, init_ref, extra_ref,
                     y_ref, fin_ref, st_ref, *, reverse, hoff, add_prev):
    ci = pl.program_id(1)

    @pl.when(ci == 0)
    def _():
        st_ref[...] = init_ref[0]

    q = SSD_CHUNK
    f32, bf16 = jnp.float32, jnp.bfloat16
    dt = _softplus(dt_ref[0] + dtb_ref[...])
    dtt = _softplus(dtt_ref[0][hoff:hoff + SSD_HEADS, :] + dtbt_ref[...])
    ri = lax.broadcasted_iota(jnp.int32, (q, q), 0)
    cj = lax.broadcasted_iota(jnp.int32, (q, q), 1)
    keep = (cj >= ri) if reverse else (cj <= ri)
    tri = keep.astype(bf16)
    tri_t = ((ri >= cj) if reverse else (ri <= cj)).astype(bf16)
    acum = sum(jnp.dot(tri, p, preferred_element_type=f32) for p in _split3_bf16(dt * a_ref[...]))
    acum_t = sum(jnp.dot(p, tri_t, preferred_element_type=f32) for p in _split3_bf16(dtt * at_ref[...]))
    end = 0 if reverse else q - 1
    a_end = acum[end:end + 1, :]
    eacum = jnp.exp(acum)
    dt_dec_end = dt * jnp.exp(a_end - acum)
    lane = lax.broadcasted_iota(jnp.int32, (q, LANES), 1)
    for g in range(SSD_GROUPS):
        cg = c_ref[0][:, g * SSD_STATE:(g + 1) * SSD_STATE]
        bg = b_ref[0][:, g * SSD_STATE:(g + 1) * SSD_STATE]
        cg16 = cg.astype(bf16)
        cb = lax.dot_general(cg16, bg.astype(bf16), (((1,), (1,)), ((), ())), preferred_element_type=f32)
        xg = x_ref[0][:, g * SSD_GROUP_W:(g + 1) * SSD_GROUP_W]
        xg16 = xg.astype(bf16)
        eac_x = _expand_heads(eacum, g, hoff)
        yd = []
        for k in range(SSD_HEADS_PER_GROUP // 2):
            xp = xg16[:, k * LANES:(k + 1) * LANES]
            ys = []
            for hh in range(2):
                h = g * SSD_HEADS_PER_GROUP + 2 * k + hh
                seg = acum[:, hoff + h:hoff + h + 1] - acum_t[h:h + 1, :]
                lmat = jnp.exp(jnp.where(keep, seg, -jnp.inf)) * dtt[h:h + 1, :]
                ys.append(jnp.dot((cb * lmat).astype(bf16), xp, preferred_element_type=f32))
            yd.append(jnp.where(lane < SSD_HEAD_DIM, ys[0], ys[1]))
        st = st_ref[g]
        y_off = jnp.dot(cg16, st.astype(bf16), preferred_element_type=f32) * eac_x
        cols = slice(g * SSD_GROUP_W, (g + 1) * SSD_GROUP_W)
        if add_prev:
            other = extra_ref[0, :, cols].astype(f32)
        else:
            other = extra_ref[:, cols] * xg
        y_ref[0, :, cols] = (jnp.concatenate(yd, axis=1) + y_off + other).astype(bf16)
        xdd16 = (xg * _expand_heads(dt_dec_end, g, hoff)).astype(bf16)
        st_ref[g] = st * eac_x[end:end + 1, :] + jnp.dot(bg.T.astype(bf16), xdd16, preferred_element_type=f32)

    @pl.when(ci == pl.num_programs(1) - 1)
    def _():
        fin_ref[0] = st_ref[...]


def _ssd_scan_p(xbc, dt_raw, dt_raw_t, dt_bias, a, init, reverse, direction, y_prev=None, d_skip=None):
    b, length, _ = xbc.shape
    nc = length // SSD_CHUNK
    q = SSD_CHUNK
    cidx = (lambda j: nc - 1 - j) if reverse else (lambda j: j)
    nb = SSD_D_INNER // SSD_GN
    hoff = direction * SSD_HEADS
    full2 = lambda i, j: (0, 0)
    st_shape = (SSD_GROUPS, SSD_STATE, SSD_GROUP_W)
    lanes = lambda v: jnp.pad(v, (hoff, LANES - hoff - SSD_HEADS)).reshape(1, LANES)
    add_prev = y_prev is not None
    if add_prev:
        extra = y_prev
        extra_spec = pl.BlockSpec((1, q, SSD_D_INNER), lambda i, j: (i, cidx(j), 0))
    else:
        extra = jnp.repeat(d_skip, SSD_HEAD_DIM).reshape(1, SSD_D_INNER)
        extra_spec = pl.BlockSpec((1, SSD_D_INNER), full2)
    return pl.pallas_call(
        functools.partial(_ssd_scan_kernel, reverse=reverse, hoff=hoff, add_prev=add_prev),
        grid=(b, nc),
        in_specs=[pl.BlockSpec((1, q, SSD_D_INNER), lambda i, j: (i, cidx(j), 0)),
                  pl.BlockSpec((1, q, SSD_GN), lambda i, j: (i, cidx(j), nb)),
                  pl.BlockSpec((1, q, SSD_GN), lambda i, j: (i, cidx(j), nb + 1)),
                  pl.BlockSpec((1, q, LANES), lambda i, j: (i, cidx(j), 0)),
                  pl.BlockSpec((1, 2 * SSD_HEADS, q), lambda i, j: (i, 0, cidx(j))),
                  pl.BlockSpec((1, LANES), full2), pl.BlockSpec((SSD_HEADS, 1), full2),
                  pl.BlockSpec((1, LANES), full2), pl.BlockSpec((SSD_HEADS, 1), full2),
                  pl.BlockSpec((1,) + st_shape, lambda i, j: (i, 0, 0, 0)),
                  extra_spec],
        out_specs=[pl.BlockSpec((1, q, SSD_D_INNER), lambda i, j: (i, cidx(j), 0)),
                   pl.BlockSpec((1,) + st_shape, lambda i, j: (i, 0, 0, 0))],
        out_shape=[jax.ShapeDtypeStruct((b, length, SSD_D_INNER), jnp.bfloat16),
                   jax.ShapeDtypeStruct((b,) + st_shape, jnp.float32)],
        scratch_shapes=[pltpu.VMEM(st_shape, jnp.float32)],
        compiler_params=pltpu.CompilerParams(
            dimension_semantics=("parallel", "arbitrary"), vmem_limit_bytes=VMEM_LIMIT_BYTES),
        name="ssd_scan",
    )(xbc, xbc, xbc, dt_raw, dt_raw_t, lanes(dt_bias), dt_bias.reshape(-1, 1),
      lanes(a), a.reshape(-1, 1), init, extra)


def _ssd_out_kernel(y_ref, z_ref, ng_ref, w_ref, o_ref):
    z = z_ref[0].astype(jnp.float32)
    y = y_ref[0].astype(jnp.float32) * (z * jax.nn.sigmoid(z))
    ms = jnp.mean(y * y, axis=-1, keepdims=True)
    y = y * lax.rsqrt(ms + RMS_EPS) * ng_ref[...]
    o_ref[0] = jnp.dot(y.astype(jnp.bfloat16), w_ref[...], preferred_element_type=jnp.float32)


def _ssd_out(y, z, norm_g, out_w):
    b, length, di = y.shape
    d = out_w.shape[1]
    tm = min(length, ROW_TILE)
    row = lambda i, j: (i, j, 0)
    full2 = lambda i, j: (0, 0)
    return pl.pallas_call(
        _ssd_out_kernel,
        grid=(b, length // tm),
        in_specs=[pl.BlockSpec((1, tm, di), row), pl.BlockSpec((1, tm, di), row),
                  pl.BlockSpec((1, di), full2), pl.BlockSpec((di, d), full2)],
        out_specs=pl.BlockSpec((1, tm, d), row),
        out_shape=jax.ShapeDtypeStruct((b, length, d), jnp.float32),
        compiler_params=pltpu.CompilerParams(
            dimension_semantics=("parallel", "parallel"), vmem_limit_bytes=VMEM_LIMIT_BYTES),
        name="ssd_out",
    )(y, z, norm_g.reshape(1, di), out_w.astype(jnp.bfloat16))


def _state_to_kernel(s):
    b = s.shape[0]
    s = s.reshape(b, SSD_GROUPS, SSD_HEADS_PER_GROUP, SSD_HEAD_DIM, SSD_STATE)
    return jnp.transpose(s, (0, 1, 4, 2, 3)).reshape(b, SSD_GROUPS, SSD_STATE, SSD_GROUP_W)


def _state_from_kernel(s):
    b = s.shape[0]
    s = s.reshape(b, SSD_GROUPS, SSD_STATE, SSD_HEADS_PER_GROUP, SSD_HEAD_DIM)
    return jnp.transpose(s, (0, 1, 3, 4, 2)).reshape(b, SSD_HEADS, SSD_HEAD_DIM, SSD_STATE)


def _ssd_mixer_p(x, ng, sh, sc, init_f, init_b, in_w, conv_w, conv_b, dt_bias, a_log, d_skip, norm_g, out_w):
    z, xbc, dt_raw, dt_raw_t = _ssd_in(x, ng, sh, sc, in_w, conv_w, conv_b)
    a = -jnp.exp(a_log)
    yf, s_f = _ssd_scan_p(xbc, dt_raw, dt_raw_t, dt_bias[0], a[0], _state_to_kernel(init_f), False, 0,
                          d_skip=d_skip)
    y, s_b = _ssd_scan_p(xbc, dt_raw, dt_raw_t, dt_bias[1], a[1], _state_to_kernel(init_b), True, 1, y_prev=yf)
    m = _ssd_out(y, z, norm_g, out_w)
    return m, _state_from_kernel(s_f), _state_from_kernel(s_b)


HY_MAX_BLOCK = 512
HY_CC = 128
HY_MAC_ELEMS = 8192
HY_HIDDEN = 64
HY_FEAT_ROWS = 64


def _odd_dft_tables(n):
    m = np.arange(n, dtype=np.int64)[:, None]
    f = np.arange(n // 2, dtype=np.int64)[None, :]
    ang = 2.0 * np.pi * (((2 * f + 1) * m) % (2 * n)).astype(np.float64) / (2 * n)
    return np.cos(ang), np.sin(ang)


def _hy_in_kernel(xm_ref, xp_ref, xn_ref, ng_ref, sh_ref, sc_ref, w_ref, b_ref, cw_ref, cb_ref, o_ref, pad_ref):
    tm = xm_ref.shape[1]
    h, valid = _halo_rows(xm_ref, xp_ref, xn_ref, ng_ref, sh_ref, sc_ref)
    u = jnp.dot(h, w_ref[...], preferred_element_type=jnp.float32) + b_ref[...]
    pad_ref[...] = jnp.where(valid, u, 0.0)
    _conv_rows(pad_ref, cw_ref, cb_ref, o_ref, tm, silu=False)


def _hy_in(x, ng, sh, sc, in_w, in_b, short_w, short_b):
    b, length, d = x.shape
    n = in_w.shape[1]
    tm = min(length, ROW_TILE)
    taps = short_w.shape[0]
    per_b = lambda i, j: (i, 0, 0)
    full2 = lambda i, j: (0, 0)
    return pl.pallas_call(
        _hy_in_kernel,
        grid=(b, length // tm),
        in_specs=_halo_specs(length, tm, d) + [
            pl.BlockSpec((1, d), full2), pl.BlockSpec((1, 1, d), per_b), pl.BlockSpec((1, 1, d), per_b),
            pl.BlockSpec((d, n), full2), pl.BlockSpec((1, n), full2),
            pl.BlockSpec((taps, n), full2), pl.BlockSpec((1, n), full2)],
        out_specs=pl.BlockSpec((1, tm, n), lambda i, j: (i, j, 0)),
        out_shape=jax.ShapeDtypeStruct((b, length, n), jnp.float32),
        scratch_shapes=[pltpu.VMEM((tm + 2 * HALO, n), jnp.float32)],
        compiler_params=pltpu.CompilerParams(
            dimension_semantics=("parallel", "parallel"), vmem_limit_bytes=VMEM_LIMIT_BYTES),
        name="hy_in",
    )(x, x, x, ng.reshape(1, d), sh, sc, in_w.astype(jnp.bfloat16), in_b.reshape(1, n), short_w, short_b.reshape(1, n))


def _dot3(a, b):
    a_hi, a_lo = _split_bf16(a)
    b_hi, b_lo = _split_bf16(b)
    f32 = jnp.float32
    return (jnp.dot(a_hi, b_hi, preferred_element_type=f32) + jnp.dot(a_lo, b_hi, preferred_element_type=f32)
            + jnp.dot(a_hi, b_lo, preferred_element_type=f32))


def _hy_filter_kernel(w1t_ref, b1_ref, w2t_ref, b2_ref, w3t_ref, fr_ref, dl_ref, o_ref, *, length, blk):
    k = pl.program_id(0)
    q = (lax.broadcasted_iota(jnp.int32, (1, blk), 1) + k * blk)
    pos = jnp.abs(q - length).astype(jnp.float32)
    t = pos / float(length - 1)
    w = (2.0 * math.pi / length) * pos
    band = lax.broadcasted_iota(jnp.int32, (HY_BANDS, 1), 0).astype(jnp.float32)
    fb = 1e-4 + band * ((HY_BANDS - 1 - 1e-4) / (HY_BANDS - 1))
    z = jnp.concatenate([jnp.broadcast_to(t, (8, blk)), jnp.cos(fb * w), -jnp.sin(fb * w),
                         jnp.zeros((HY_FEAT_ROWS - 8 - 2 * HY_BANDS, blk), jnp.float32)], axis=0)
    h = jnp.sin(fr_ref[...] * (_dot3(w1t_ref[...], z) + b1_ref[...]))
    h = jnp.sin(fr_ref[...] * (_dot3(w2t_ref[...], h) + b2_ref[...]))
    kt = _dot3(w3t_ref[0], h)
    o_ref[0] = kt * jnp.exp(-t * dl_ref[...])


def _hy_filter(length, blk, f_w1, f_b1, f_w2, f_b2, f_w3, f_freq):
    d = f_w3.shape[1] // 2
    nk = 2 * length // blk
    w1t = jnp.concatenate([f_w1[0:1].T, jnp.zeros((HY_HIDDEN, 7), jnp.float32), f_w1[1:].T,
                           jnp.zeros((HY_HIDDEN, HY_FEAT_ROWS - 8 - 2 * HY_BANDS), jnp.float32)], axis=1)
    w3t = jnp.stack([f_w3[:, d:].T, f_w3[:, :d].T])
    deltas = jnp.abs(jnp.linspace(HY_MIN_DECAY, HY_MAX_DECAY, d, dtype=jnp.float32)).reshape(d, 1)
    col = lambda v: v.reshape(HY_HIDDEN, 1)
    full2 = lambda k: (0, 0)
    half = length // blk
    return pl.pallas_call(
        functools.partial(_hy_filter_kernel, length=length, blk=blk),
        grid=(nk,),
        in_specs=[pl.BlockSpec((HY_HIDDEN, HY_FEAT_ROWS), full2), pl.BlockSpec((HY_HIDDEN, 1), full2),
                  pl.BlockSpec((HY_HIDDEN, HY_HIDDEN), full2), pl.BlockSpec((HY_HIDDEN, 1), full2),
                  pl.BlockSpec((1, d, HY_HIDDEN), lambda k: (k // half, 0, 0)),
                  pl.BlockSpec((HY_HIDDEN, 1), full2), pl.BlockSpec((d, 1), full2)],
        out_specs=pl.BlockSpec((1, d, blk), lambda k: (k, 0, 0)),
        out_shape=jax.ShapeDtypeStruct((nk, d, blk), jnp.float32),
        compiler_params=pltpu.CompilerParams(
            dimension_semantics=("parallel",), vmem_limit_bytes=VMEM_LIMIT_BYTES),
        name="hy_filter",
    )(w1t, col(f_b1), f_w2.T, col(f_b2), w3t, col(f_freq), deltas)


def _hy_gspec_kernel(hi_ref, lo_ref, ft_ref, fb_ref, o_ref):
    o_ref[0] = _dot3(hi_ref[0], ft_ref[...]) + _dot3(lo_ref[0], fb_ref[...])


def _hy_gspec(kt):
    nk, d, blk = kt.shape
    cos, sin = _odd_dft_tables(2 * blk)
    top = np.concatenate([cos[:blk], -sin[:blk]], axis=1)
    bot = -np.concatenate([cos[blk:], -sin[blk:]], axis=1)
    bot[0] = 0.0
    tm = 512
    full2 = lambda e, i: (0, 0)
    return pl.pallas_call(
        _hy_gspec_kernel,
        grid=(nk - 1, d // tm),
        in_specs=[pl.BlockSpec((1, tm, blk), lambda e, i: (e + 1, i, 0)),
                  pl.BlockSpec((1, tm, blk), lambda e, i: (e, i, 0)),
                  pl.BlockSpec((blk, 2 * blk), full2), pl.BlockSpec((blk, 2 * blk), full2)],
        out_specs=pl.BlockSpec((1, tm, 2 * blk), lambda e, i: (e, i, 0)),
        out_shape=jax.ShapeDtypeStruct((nk - 1, d, 2 * blk), jnp.float32),
        compiler_params=pltpu.CompilerParams(
            dimension_semantics=("parallel", "parallel"), vmem_limit_bytes=VMEM_LIMIT_BYTES),
        name="hy_gspec",
    )(kt, kt, jnp.asarray(top, jnp.float32), jnp.asarray(bot, jnp.float32))


def _hy_conv_kernel(x0_ref, x1_ref, v_ref, g_ref, fb_ref, ff_ref, fi_ref, o_ref, lhs_ref, u_ref, y_ref, *, nb):
    cc, bsz = HY_CC, ff_ref.shape[0]
    mrows = HY_MAC_ELEMS // bsz
    for j in range(nb):
        sl = slice(j * bsz, (j + 1) * bsz)
        wj = v_ref[0, sl, :] * x1_ref[0, sl, :]
        lhs_ref[j * cc:(j + 1) * cc, :] = wj.T.astype(jnp.bfloat16)
    u_ref[...] = jnp.dot(lhs_ref[...], ff_ref[...], preferred_element_type=jnp.float32)

    def per_out_block(i, carry):
        def per_rows(rc, carry2):
            rows = pl.ds(pl.multiple_of(rc * mrows, mrows), mrows)
            acc_r = jnp.zeros((mrows, bsz), jnp.float32)
            acc_i = jnp.zeros((mrows, bsz), jnp.float32)
            for j in range(nb):
                e = i - j + (nb - 1)
                gr = g_ref[e, rows, 0:bsz]
                gi = g_ref[e, rows, bsz:2 * bsz]
                urows = pl.ds(pl.multiple_of(j * cc + rc * mrows, mrows), mrows)
                ur = u_ref[urows, 0:bsz]
                ui = u_ref[urows, bsz:2 * bsz]
                acc_r = acc_r + gr * ur - gi * ui
                acc_i = acc_i + gr * ui + gi * ur
            yrows = pl.ds(pl.multiple_of(i * cc + rc * mrows, mrows), mrows)
            y_ref[yrows, 0:bsz] = acc_r.astype(jnp.bfloat16)
            y_ref[yrows, bsz:2 * bsz] = acc_i.astype(jnp.bfloat16)
            return carry2
        return lax.fori_loop(0, cc // mrows, per_rows, carry)
    lax.fori_loop(0, nb, per_out_block, 0)

    yt = jnp.dot(y_ref[...], fi_ref[...], preferred_element_type=jnp.float32)
    for i in range(nb):
        sl = slice(i * bsz, (i + 1) * bsz)
        w = v_ref[0, sl, :] * x1_ref[0, sl, :]
        o_ref[0, sl, :] = (yt[i * cc:(i + 1) * cc, :].T + fb_ref[...] * w) * x0_ref[0, sl, :]


def _hy_conv(u, g, f_bias, blk):
    b, length, d3 = u.shape
    d = d3 // 3
    nb = length // blk
    ncb = d // HY_CC
    cos, sin = _odd_dft_tables(2 * blk)
    fwd = np.concatenate([cos[:blk], -sin[:blk]], axis=1)
    inv = (1.0 / blk) * np.concatenate([cos[:blk].T, -sin[:blk].T], axis=0)
    col = lambda off: pl.BlockSpec((1, length, HY_CC), lambda c, i, off=off: (i, 0, off + c))
    full2 = lambda c, i: (0, 0)
    return pl.pallas_call(
        functools.partial(_hy_conv_kernel, nb=nb),
        grid=(ncb, b),
        in_specs=[col(0), col(ncb), col(2 * ncb),
                  pl.BlockSpec((2 * nb - 1, HY_CC, 2 * blk), lambda c, i: (0, c, 0)),
                  pl.BlockSpec((1, HY_CC), lambda c, i: (0, c)),
                  pl.BlockSpec((blk, 2 * blk), full2), pl.BlockSpec((2 * blk, blk), full2)],
        out_specs=pl.BlockSpec((1, length, HY_CC), lambda c, i: (i, 0, c)),
        out_shape=jax.ShapeDtypeStruct((b, length, d), jnp.float32),
        scratch_shapes=[pltpu.VMEM((nb * HY_CC, blk), jnp.bfloat16),
                        pltpu.VMEM((nb * HY_CC, 2 * blk), jnp.float32),
                        pltpu.VMEM((nb * HY_CC, 2 * blk), jnp.bfloat16)],
        compiler_params=pltpu.CompilerParams(
            dimension_semantics=("parallel", "arbitrary"), vmem_limit_bytes=56 * 1024 * 1024),
        name="hy_conv",
    )(u, u, u, g, f_bias.reshape(1, d), jnp.asarray(fwd, jnp.bfloat16), jnp.asarray(inv, jnp.bfloat16))


def _hyena_mixer_p(x, ng, sh, sc, in_w, in_b, short_w, short_b, f_w1, f_b1, f_w2, f_b2, f_w3, f_freq, f_bias, out_w):
    length = x.shape[1]
    blk = min(HY_MAX_BLOCK, length)
    u = _hy_in(x, ng, sh, sc, in_w, in_b, short_w, short_b)
    g = _hy_gspec(_hy_filter(length, blk, f_w1, f_b1, f_w2, f_b2, f_w3, f_freq))
    return _mm3(_hy_conv(u, g, f_bias, blk), out_w)


TOK_TILE = 256
MOE_ROWS = 512
SEG_CHUNK = 64
BF16_TILE_ROWS = 16
LANES = 128


def _split_bf16(w):
    hi = w.astype(jnp.bfloat16)
    lo = (w - hi.astype(jnp.float32)).astype(jnp.bfloat16)
    return hi, lo


def _moe_pre_kernel(x_ref, m_ref, g1_ref, ng_ref, sh_ref, sc_ref, wrh_ref, wrl_ref,
                    xo_ref, hpk_ref, lg_ref):
    x = x_ref[0] + g1_ref[0] * m_ref[0]
    xo_ref[0] = x
    ms = jnp.mean(x * x, axis=-1, keepdims=True)
    h = x * lax.rsqrt(ms + RMS_EPS) * ng_ref[...]
    h = h * (1.0 + sc_ref[0]) + sh_ref[0]
    h_hi = h.astype(jnp.bfloat16)
    h_lo = (h - h_hi.astype(jnp.float32)).astype(jnp.bfloat16)
    dn = (((1,), (1,)), ((), ()))
    lg = lax.dot_general(wrh_ref[...], h_hi, dn, preferred_element_type=jnp.float32)
    lg += lax.dot_general(wrh_ref[...], h_lo, dn, preferred_element_type=jnp.float32)
    lg += lax.dot_general(wrl_ref[...], h_hi, dn, preferred_element_type=jnp.float32)
    lg_ref[0] = lg
    half = h.shape[1] // 2
    wa = pltpu.bitcast(h_hi[:, :half].astype(jnp.float32), jnp.uint32) >> 16
    wb = pltpu.bitcast(h_hi[:, half:].astype(jnp.float32), jnp.uint32) & jnp.uint32(0xFFFF0000)
    hpk_ref[0] = wa | wb


def _moe_pre(x, m, g1, ng, sh, sc, w_router):
    b, length, d = x.shape
    tm = min(length, 512)
    wrh, wrl = _split_bf16(w_router.T)
    row = lambda i, j: (i, j, 0)
    per_b = lambda i, j: (i, 0, 0)
    full2 = lambda i, j: (0, 0)
    return pl.pallas_call(
        _moe_pre_kernel,
        grid=(b, length // tm),
        in_specs=[pl.BlockSpec((1, tm, d), row), pl.BlockSpec((1, tm, d), row),
                  pl.BlockSpec((1, 1, d), per_b), pl.BlockSpec((1, d), full2),
                  pl.BlockSpec((1, 1, d), per_b), pl.BlockSpec((1, 1, d), per_b),
                  pl.BlockSpec((N_EXPERTS, d), full2), pl.BlockSpec((N_EXPERTS, d), full2)],
        out_specs=[pl.BlockSpec((1, tm, d), row), pl.BlockSpec((1, tm, d // 2), row),
                   pl.BlockSpec((1, N_EXPERTS, tm), lambda i, j: (i, 0, j))],
        out_shape=[jax.ShapeDtypeStruct((b, length, d), jnp.float32),
                   jax.ShapeDtypeStruct((b, length, d // 2), jnp.uint32),
                   jax.ShapeDtypeStruct((b, N_EXPERTS, length), jnp.float32)],
        compiler_params=pltpu.CompilerParams(
            dimension_semantics=("parallel", "parallel"), vmem_limit_bytes=VMEM_LIMIT_BYTES),
        name="moe_pre",
    )(x, m, g1, ng.reshape(1, d), sh, sc, wrh, wrl)


def _moe_ffn_kernel(idx_ref, nidx_ref, h_hbm, gate_ref, wg_ref, wu_ref, wd_ref, y_ref,
                    xe_ref, wgb, wub, wdb, sem):
    nblk = pl.num_programs(1)
    step = pl.program_id(0) * nblk + pl.program_id(1)
    last = pl.num_programs(0) * nblk - 1

    def issue(ids_ref, slot):
        base = slot * MOE_ROWS
        for c in range(MOE_ROWS):
            pltpu.make_async_copy(h_hbm.at[pl.ds(ids_ref[0, 0, c], 1)], xe_ref.at[pl.ds(base + c, 1)],
                                  sem.at[slot]).start()

    @pl.when(step == 0)
    def _():
        issue(idx_ref, 0)

    for parity in range(2):
        @pl.when((step < last) & (step % 2 == parity))
        def _(parity=parity):
            issue(nidx_ref, 1 - parity)

    @pl.when(pl.program_id(1) == 0)
    def _():
        wgb[...] = wg_ref[0, 0].astype(jnp.bfloat16)
        wub[...] = wu_ref[0, 0].astype(jnp.bfloat16)
        wdb[...] = wd_ref[0, 0].astype(jnp.bfloat16)

    slot = step % 2
    rows = pl.ds(pl.multiple_of(slot * MOE_ROWS, MOE_ROWS), MOE_ROWS)
    pltpu.make_async_copy(h_hbm.at[pl.ds(0, MOE_ROWS)], xe_ref.at[rows], sem.at[slot]).wait()
    half = wgb.shape[0] // 2
    w = xe_ref[rows, :]
    xa = pltpu.bitcast(w << 16, jnp.float32).astype(jnp.bfloat16)
    xb = pltpu.bitcast(w & jnp.uint32(0xFFFF0000), jnp.float32).astype(jnp.bfloat16)
    hg = jnp.dot(xa, wgb[:half], preferred_element_type=jnp.float32)
    hg += jnp.dot(xb, wgb[half:], preferred_element_type=jnp.float32)
    hu = jnp.dot(xa, wub[:half], preferred_element_type=jnp.float32)
    hu += jnp.dot(xb, wub[half:], preferred_element_type=jnp.float32)
    hid = (hg * jax.nn.sigmoid(hg) * hu).astype(jnp.bfloat16)
    y = jnp.dot(hid, wdb[...], preferred_element_type=jnp.float32)
    y_ref[0] = (y * gate_ref[0]).astype(jnp.bfloat16)


def _moe_ffn(hpk, grow, gate, w_gate, w_up, w_down, layer):
    e, r = grow.shape
    d, f = w_gate.shape[2], w_gate.shape[3]
    nblk = r // MOE_ROWS
    nsteps = e * nblk
    wspec = lambda shp: pl.BlockSpec((1, 1) + shp, lambda i, j: (layer, i, 0, 0))
    ids = grow.reshape(nsteps, 1, MOE_ROWS)
    smem_ids = lambda off: pl.BlockSpec(
        (1, 1, MOE_ROWS), lambda i, j: (jnp.minimum(i * nblk + j + off, nsteps - 1), 0, 0), memory_space=pltpu.SMEM)
    return pl.pallas_call(
        _moe_ffn_kernel,
        grid=(e, nblk),
        in_specs=[smem_ids(0), smem_ids(1),
                  pl.BlockSpec(memory_space=pltpu.HBM),
                  pl.BlockSpec((1, MOE_ROWS, 1), lambda i, j: (i, j, 0)),
                  wspec((d, f)), wspec((d, f)), wspec((f, d))],
        out_specs=pl.BlockSpec((1, MOE_ROWS, d), lambda i, j: (i, j, 0)),
        out_shape=jax.ShapeDtypeStruct((e, r, d), jnp.bfloat16),
        scratch_shapes=[pltpu.VMEM((2 * MOE_ROWS, d // 2), jnp.uint32),
                        pltpu.VMEM((d, f), jnp.bfloat16), pltpu.VMEM((d, f), jnp.bfloat16),
                        pltpu.VMEM((f, d), jnp.bfloat16),
                        pltpu.SemaphoreType.DMA((2,))],
        compiler_params=pltpu.CompilerParams(
            dimension_semantics=("arbitrary", "arbitrary"), vmem_limit_bytes=VMEM_LIMIT_BYTES),
        name="moe_ffn",
    )(ids, ids, hpk, gate, w_gate, w_up, w_down)


def _moe_comb_kernel(cs_ref, x_ref, g2_ref, y_ref, idx_ref, fg_ref, o_ref, ycat, acc, *, cap, ch, ntile, final_norm):
    b = pl.program_id(0)
    t = pl.program_id(1)
    base = t * TOK_TILE
    sub = lax.broadcasted_iota(jnp.int32, (TOK_TILE, LANES), 0) + base
    if ntile == 1:
        for e in range(N_EXPERTS):
            ycat[e * ch:(e + 1) * ch, :] = y_ref[e, 0:ch, :]
        v = idx_ref[0]
        tiles = [(v[:, p * LANES:(p + 1) * LANES] == sub).astype(jnp.bfloat16)
                 for p in range(N_EXPERTS * ch // LANES)]
        acc[...] = jnp.dot(jnp.concatenate(tiles, axis=1), ycat[...], preferred_element_type=jnp.float32)
    else:
        lane = lax.broadcasted_iota(jnp.int32, (1, LANES), 1)
        per = LANES // ch
        sts = []
        for e in range(N_EXPERTS):
            s0 = cs_ref[(b * N_EXPERTS + e) * (ntile + 1) + t]
            st = jnp.minimum((s0 // BF16_TILE_ROWS) * BF16_TILE_ROWS, cap - ch)
            st = pl.multiple_of(st, BF16_TILE_ROWS)
            sts.append(st)
            ycat[e * ch:(e + 1) * ch, :] = y_ref[e, pl.ds(st, ch), :]
        tiles = []
        for p in range(N_EXPERTS // per):
            v = None
            for q in range(per):
                e = p * per + q
                r = pltpu.roll(idx_ref[0, e:e + 1, :], (2 * cap - sts[e] + q * ch) % cap, 1)[:, :LANES]
                v = r if v is None else jnp.where(lane >= q * ch, r, v)
            tiles.append((v == sub).astype(jnp.bfloat16))
        acc[...] = jnp.dot(jnp.concatenate(tiles, axis=1), ycat[...], preferred_element_type=jnp.float32)
        sub_c = lax.broadcasted_iota(jnp.int32, (TOK_TILE, ch), 0) + base
        lane_c = lax.broadcasted_iota(jnp.int32, (1, ch), 1)
        for e in range(N_EXPERTS):
            s1 = cs_ref[(b * N_EXPERTS + e) * (ntile + 1) + t + 1]
            first_end = sts[e] + ch
            n_extra = jnp.maximum(s1 - first_end + ch - 1, 0) // ch

            def extra(q, carry, e=e, first_end=first_end):
                lo = first_end + q * ch
                stq = pl.multiple_of(jnp.minimum(lo, cap - ch), BF16_TILE_ROWS)
                r = pltpu.roll(idx_ref[0, e:e + 1, :], (2 * cap - stq) % cap, 1)[:, :ch]
                hit = (r == sub_c) & (lane_c + stq >= lo)
                acc[...] += jnp.dot(hit.astype(jnp.bfloat16), y_ref[e, pl.ds(stq, ch), :],
                                    preferred_element_type=jnp.float32)
                return carry
            lax.fori_loop(0, n_extra, extra, 0)
    out = x_ref[0] + g2_ref[0] * acc[...]
    if final_norm:
        ms = jnp.mean(out * out, axis=-1, keepdims=True)
        out = out * lax.rsqrt(ms + RMS_EPS) * fg_ref[...]
    o_ref[0] = out


def _moe_combine(x, g2, y, idx, cs, final_g=None):
    b, length, d = x.shape
    final_norm = final_g is not None
    fg = (final_g if final_norm else jnp.ones((d,), jnp.float32)).reshape(1, d)
    cap = idx.shape[2]
    ntile = length // TOK_TILE
    ch = min(SEG_CHUNK, cap)
    if ntile == 1:
        idx_in = idx.reshape(b, 1, N_EXPERTS * cap)
        idx_spec = pl.BlockSpec((1, 1, N_EXPERTS * cap), lambda i, j, c: (i, 0, 0))
    else:
        idx_in = idx
        idx_spec = pl.BlockSpec((1, N_EXPERTS, cap), lambda i, j, c: (i, 0, 0))
    grid_spec = pltpu.PrefetchScalarGridSpec(
        num_scalar_prefetch=1,
        grid=(b, ntile),
        in_specs=[pl.BlockSpec((1, TOK_TILE, d), lambda i, j, c: (i, j, 0)),
                  pl.BlockSpec((1, 1, d), lambda i, j, c: (i, 0, 0)),
                  pl.BlockSpec((N_EXPERTS, cap, d), lambda i, j, c: (0, i, 0)),
                  idx_spec,
                  pl.BlockSpec((1, d), lambda i, j, c: (0, 0))],
        out_specs=pl.BlockSpec((1, TOK_TILE, d), lambda i, j, c: (i, j, 0)),
        scratch_shapes=[pltpu.VMEM((N_EXPERTS * ch, d), jnp.bfloat16),
                        pltpu.VMEM((TOK_TILE, d), jnp.float32)])
    return pl.pallas_call(
        functools.partial(_moe_comb_kernel, cap=cap, ch=ch, ntile=ntile, final_norm=final_norm),
        grid_spec=grid_spec,
        out_shape=jax.ShapeDtypeStruct((b, length, d), jnp.float32),
        compiler_params=pltpu.CompilerParams(
            dimension_semantics=("arbitrary", "arbitrary"), vmem_limit_bytes=56 * 1024 * 1024),
        name="moe_combine",
    )(cs.reshape(-1).astype(jnp.int32), x, g2, y, idx_in, fg)


def _moe_block(x, m, g1, ng, sh, sc, g2, w_router, w_gate, w_up, w_down, layer, final_g=None):
    b, length, d = x.shape
    cap = EC_FACTOR * length // N_EXPERTS
    x1, hpk, lg = _moe_pre(x, m, g1, ng, sh, sc, w_router)
    aff = jax.nn.softmax(lg, axis=1)
    _, idx = lax.top_k(aff, cap)
    idx = jnp.sort(idx, axis=-1)
    gate = jnp.take_along_axis(aff, idx, axis=-1)
    ntile = length // TOK_TILE
    bounds = jnp.arange(ntile + 1, dtype=jnp.int32) * TOK_TILE
    cs = jnp.sum(idx[:, :, :, None] < bounds, axis=2, dtype=jnp.int32)
    grow = idx + (jnp.arange(b, dtype=jnp.int32) * length)[:, None, None]
    grow = jnp.swapaxes(grow, 0, 1).reshape(N_EXPERTS, b * cap)
    gate_e = jnp.swapaxes(gate, 0, 1).reshape(N_EXPERTS, b * cap, 1)
    y = _moe_ffn(hpk.reshape(b * length, d // 2), grow, gate_e, w_gate, w_up, w_down, layer)
    return _moe_combine(x1, g2, y, idx, cs, final_g)


def kernel(x_prompt, x_sample, state_ssd, c, c_ctx, norm_g, ada_w, ada_b, hy_in_w, hy_in_b, hy_short_w, hy_short_b, hy_f_w1, hy_f_b1, hy_f_w2, hy_f_b2, hy_f_w3, hy_f_freq, hy_f_bias, hy_out_w, ssd_in_w, ssd_conv_w, ssd_conv_b, ssd_dt_bias, ssd_A_log, ssd_D, ssd_norm_g, ssd_out_w, moe_router, moe_w_gate, moe_w_up, moe_w_down, final_norm_g):
    rows = x_sample.shape[1] // GRID_W
    xp = x_prompt
    xs = x_sample + _sincos_2d(rows, GRID_W, D_MODEL)[None]
    new_ssd = []
    for i in range(DEPTH):
        sh1p, sc1p, g1p, sh2p, sc2p, g2p = _adaln(c_ctx[None, :], ada_w[i], ada_b[i])
        sh1s, sc1s, g1s, sh2s, sc2s, g2s = _adaln(c, ada_w[i], ada_b[i])
        j = i // N_MIXERS
        bp = (xp.shape[0], 1, D_MODEL)
        if i % N_MIXERS == 0:
            hy = (hy_in_w[j], hy_in_b[j], hy_short_w[j], hy_short_b[j], hy_f_w1[j], hy_f_b1[j],
                  hy_f_w2[j], hy_f_b2[j], hy_f_w3[j], hy_f_freq[j], hy_f_bias[j], hy_out_w[j])
            mp = _hyena_mixer_p(xp, norm_g[i, 0], jnp.broadcast_to(sh1p, bp), jnp.broadcast_to(sc1p, bp), *hy)
            ms = _hyena_mixer_p(xs, norm_g[i, 0], sh1s, sc1s, *hy)
        else:
            sp = (ssd_in_w[j], ssd_conv_w[j], ssd_conv_b[j], ssd_dt_bias[j], ssd_A_log[j],
                  ssd_D[j], ssd_norm_g[j], ssd_out_w[j])
            zeros = jnp.zeros((xp.shape[0], SSD_HEADS, SSD_HEAD_DIM, SSD_STATE), jnp.float32)
            mp, s_f, s_b = _ssd_mixer_p(xp, norm_g[i, 0], jnp.broadcast_to(sh1p, bp),
                                        jnp.broadcast_to(sc1p, bp), zeros, zeros, *sp)
            new_ssd.append(jnp.stack([s_f, s_b], axis=1))
            ms, _, _ = _ssd_mixer_p(xs, norm_g[i, 0], sh1s, sc1s, state_ssd[:, j, 0], state_ssd[:, j, 1], *sp)
        moe = (moe_router[i], moe_w_gate, moe_w_up, moe_w_down, i, final_norm_g if i == DEPTH - 1 else None)
        xp = _moe_block(xp, mp, jnp.broadcast_to(g1p, bp), norm_g[i, 1], jnp.broadcast_to(sh2p, bp),
                        jnp.broadcast_to(sc2p, bp), jnp.broadcast_to(g2p, bp), *moe)
        xs = _moe_block(xs, ms, g1s, norm_g[i, 1], sh2s, sc2s, g2s, *moe)
    new_state_ssd = jnp.stack(new_ssd, axis=1)
    return (xp, xs, new_state_ssd)
```

```python
import functools
import math

import jax
import jax.numpy as jnp
import numpy as np
from jax import lax
from jax.experimental import pallas as pl
from jax.experimental.pallas import tpu as pltpu

D_MODEL = 1024
DEPTH = 2
GRID_W = 64
N_MIXERS = 2
RMS_EPS = 1e-6
HY_EMB = 33
HY_BANDS = (HY_EMB - 1) // 2
HY_SHORT_DECAY_FRAC = 0.3
HY_LONG_DECAY_FRAC = 1.5
HY_DECAY_TARGET = 1e-2
HY_MAX_DECAY = math.log(HY_DECAY_TARGET) / HY_SHORT_DECAY_FRAC
HY_MIN_DECAY = math.log(HY_DECAY_TARGET) / HY_LONG_DECAY_FRAC
SSD_D_INNER = 2 * D_MODEL
SSD_HEAD_DIM = 64
SSD_HEADS = SSD_D_INNER // SSD_HEAD_DIM
SSD_GROUPS = 4
SSD_STATE = 128
SSD_CHUNK = 128
SSD_XBC = SSD_D_INNER + 2 * SSD_GROUPS * SSD_STATE
N_EXPERTS = 16
EC_FACTOR = 2

VMEM_LIMIT_BYTES = 48 * 1024 * 1024


def _mm_kernel(a_ref, b_ref, o_ref, acc_ref):
    @pl.when(pl.program_id(2) == 0)
    def _():
        acc_ref[...] = jnp.zeros_like(acc_ref)

    acc_ref[...] += jnp.dot(a_ref[...].astype(jnp.bfloat16), b_ref[...],
                            preferred_element_type=jnp.float32)

    @pl.when(pl.program_id(2) == pl.num_programs(2) - 1)
    def _():
        o_ref[...] = acc_ref[...]


def _pick(n, pref):
    for t in pref:
        if n % t == 0:
            return t
    return n


def _mm(a, b):
    m, k = a.shape
    n = b.shape[1]
    mp = -(-m // 8) * 8
    if mp != m:
        a = jnp.pad(a, ((0, mp - m), (0, 0)))
    tm = _pick(mp, (512, 256, 128, 64, 32, 16, 8))
    tn = _pick(n, (512, 256, 128))
    tk = _pick(k, (1024, 512, 256, 128))
    out = pl.pallas_call(
        _mm_kernel,
        grid=(mp // tm, n // tn, k // tk),
        in_specs=[pl.BlockSpec((tm, tk), lambda i, j, l: (i, l)),
                  pl.BlockSpec((tk, tn), lambda i, j, l: (l, j))],
        out_specs=pl.BlockSpec((tm, tn), lambda i, j, l: (i, j)),
        out_shape=jax.ShapeDtypeStruct((mp, n), jnp.float32),
        scratch_shapes=[pltpu.VMEM((tm, tn), jnp.float32)],
        compiler_params=pltpu.CompilerParams(
            dimension_semantics=("parallel", "parallel", "arbitrary"),
            vmem_limit_bytes=VMEM_LIMIT_BYTES),
        name="mm",
    )(a, b.astype(jnp.bfloat16))
    return out[:m]


def _mm3(a, b):
    lead = a.shape[:-1]
    return _mm(a.reshape(-1, a.shape[-1]), b).reshape(*lead, b.shape[1])


def _bmm_kernel(a_ref, b_ref, o_ref):
    o_ref[0] = jnp.dot(a_ref[0].astype(jnp.bfloat16), b_ref[0],
                       preferred_element_type=jnp.float32)


def _bmm(a, b):
    e, m, k = a.shape
    n = b.shape[2]
    tm = _pick(m, (512, 256, 128, 64, 32, 16, 8))
    tn = _pick(n, (512, 256, 128))
    return pl.pallas_call(
        _bmm_kernel,
        grid=(e, m // tm, n // tn),
        in_specs=[pl.BlockSpec((1, tm, k), lambda g, i, j: (g, i, 0)),
                  pl.BlockSpec((1, k, tn), lambda g, i, j: (g, 0, j))],
        out_specs=pl.BlockSpec((1, tm, tn), lambda g, i, j: (g, i, j)),
        out_shape=jax.ShapeDtypeStruct((e, m, n), jnp.float32),
        compiler_params=pltpu.CompilerParams(
            dimension_semantics=("parallel", "parallel", "parallel"),
            vmem_limit_bytes=VMEM_LIMIT_BYTES),
        name="bmm",
    )(a, b.astype(jnp.bfloat16))


def _rmsnorm(x, g):
    y = x * lax.rsqrt(jnp.mean(x * x, axis=-1, keepdims=True) + RMS_EPS)
    return y * g


def _adaln(cond, ada_w, ada_b):
    m = jnp.dot(jax.nn.silu(cond), ada_w, precision=lax.Precision.HIGHEST) + ada_b
    return jnp.split(m[:, None, :], 6, axis=-1)


def _modulate(h, shift, scale):
    return h * (1 + scale) + shift


def _dwconv(x, w, b):
    k, ch = w.shape
    y = lax.conv_general_dilated(x, w[:, None, :], window_strides=(1,),
                                 padding=[(k // 2, k // 2)],
                                 dimension_numbers=('NWC', 'WIO', 'NWC'),
                                 feature_group_count=ch,
                                 precision=lax.Precision.HIGHEST)
    return y + b


def _sincos_2d(rows, cols, d):
    q = d // 4
    omega = 1.0 / (10000.0 ** (jnp.arange(q, dtype=jnp.float32) / q))
    t = jnp.arange(rows * cols)
    er = (t // cols).astype(jnp.float32)[:, None] * omega[None, :]
    ec = (t % cols).astype(jnp.float32)[:, None] * omega[None, :]
    return jnp.concatenate([jnp.sin(er), jnp.cos(er), jnp.sin(ec), jnp.cos(ec)], axis=-1)


def _hyena_filters(length, f_w1, f_b1, f_w2, f_b2, f_w3, f_freq):
    hp = lax.Precision.HIGHEST
    t = jnp.linspace(0.0, 1.0, length, dtype=jnp.float32)[:, None]
    w = 2.0 * math.pi * jnp.arange(length, dtype=jnp.float32)[:, None] / length
    f = jnp.linspace(1e-4, HY_BANDS - 1, HY_BANDS, dtype=jnp.float32)[None, :]
    z = jnp.concatenate([t, jnp.cos(f * w), -jnp.sin(f * w)], axis=-1)
    h = jnp.sin(f_freq * (jnp.dot(z, f_w1, precision=hp) + f_b1))
    h = jnp.sin(f_freq * (jnp.dot(h, f_w2, precision=hp) + f_b2))
    h = jnp.dot(h, f_w3, precision=hp)
    deltas = jnp.linspace(HY_MIN_DECAY, HY_MAX_DECAY, D_MODEL, dtype=jnp.float32)
    window = jnp.exp(-t * jnp.abs(deltas)[None, :])
    return h[:, :D_MODEL] * window, h[:, D_MODEL:] * window


def _bidir_long_conv(u, h_fwd, h_bwd, bias):
    b, length, ch = u.shape
    n = 2 * length
    k = jnp.concatenate([h_fwd, jnp.zeros((1, ch), jnp.float32), h_bwd[1:][::-1]], axis=0)
    kf = jnp.fft.rfft(k, n=n, axis=0)
    uf = jnp.fft.rfft(u, n=n, axis=1)
    y = jnp.fft.irfft(uf * kf[None], n=n, axis=1)[:, :length]
    return y + u * bias


def _hyena_mixer(h, in_w, in_b, short_w, short_b, f_w1, f_b1, f_w2, f_b2, f_w3, f_freq, f_bias, out_w):
    length = h.shape[1]
    u = _dwconv(_mm3(h, in_w) + in_b, short_w, short_b)
    x0 = u[..., :D_MODEL]
    x1 = u[..., D_MODEL:2 * D_MODEL]
    v = u[..., 2 * D_MODEL:]
    h_fwd, h_bwd = _hyena_filters(length, f_w1, f_b1, f_w2, f_b2, f_w3, f_freq)
    v = _bidir_long_conv(v * x1, h_fwd, h_bwd, f_bias)
    return _mm3(v * x0, out_w)


def _ssd_scan(x, dt, a, bm, cm, init):
    hp = lax.Precision.HIGHEST
    b, length = x.shape[:2]
    nc = length // SSD_CHUNK
    q, g, r = SSD_CHUNK, SSD_GROUPS, SSD_HEADS // SSD_GROUPS
    x = x.reshape(b, nc, q, g, r, SSD_HEAD_DIM)
    dt = dt.reshape(b, nc, q, g, r)
    bm = bm.reshape(b, nc, q, g, SSD_STATE)
    cm = cm.reshape(b, nc, q, g, SSD_STATE)
    a_cum = jnp.cumsum(dt * a.reshape(g, r), axis=2)
    xdt = x * dt[..., None]
    seg = a_cum[:, :, :, None] - a_cum[:, :, None, :]
    causal = jnp.tril(jnp.ones((q, q), dtype=bool))[None, None, :, :, None, None]
    decay = jnp.exp(jnp.where(causal, seg, -jnp.inf))
    cb = jnp.einsum('bclgn,bcsgn->bclsg', cm, bm, precision=hp)
    y_diag = jnp.einsum('bclsgr,bcsgrp->bclgrp', cb[..., None] * decay, xdt, precision=hp)
    decay_end = jnp.exp(a_cum[:, :, -1:] - a_cum)
    chunk_states = jnp.einsum('bcsgn,bcsgrp->bcgrpn', bm, xdt * decay_end[..., None], precision=hp)
    chunk_decay = jnp.exp(a_cum[:, :, -1])

    def step(s, inp):
        st, dec = inp
        return s * dec[..., None, None] + st, s

    init_g = init.astype(jnp.float32).reshape(b, g, r, SSD_HEAD_DIM, SSD_STATE)
    final, starts = lax.scan(step, init_g, (jnp.moveaxis(chunk_states, 1, 0), jnp.moveaxis(chunk_decay, 1, 0)))
    starts = jnp.moveaxis(starts, 0, 1)
    y_off = jnp.einsum('bclgn,bcgrpn->bclgrp', cm, starts, precision=hp) * jnp.exp(a_cum)[..., None]
    y = (y_diag + y_off).reshape(b, length, SSD_HEADS, SSD_HEAD_DIM)
    return y, final.reshape(b, SSD_HEADS, SSD_HEAD_DIM, SSD_STATE)


def _flip(t):
    return jnp.flip(t, axis=1)


def _ssd_mixer(h, init_f, init_b, in_w, conv_w, conv_b, dt_bias, a_log, d_skip, norm_g, out_w):
    b, length, _ = h.shape
    proj = _mm3(h, in_w)
    z = proj[..., :SSD_D_INNER]
    xbc = jax.nn.silu(_dwconv(proj[..., SSD_D_INNER:SSD_D_INNER + SSD_XBC], conv_w, conv_b))
    dt_raw = proj[..., SSD_D_INNER + SSD_XBC:]
    gn = SSD_GROUPS * SSD_STATE
    xh = xbc[..., :SSD_D_INNER].reshape(b, length, SSD_HEADS, SSD_HEAD_DIM)
    bm = xbc[..., SSD_D_INNER:SSD_D_INNER + gn].reshape(b, length, SSD_GROUPS, SSD_STATE)
    cm = xbc[..., SSD_D_INNER + gn:].reshape(b, length, SSD_GROUPS, SSD_STATE)
    dt = jax.nn.softplus(dt_raw.reshape(b, length, 2, SSD_HEADS) + dt_bias)
    a = -jnp.exp(a_log)
    y_f, s_f = _ssd_scan(xh, dt[:, :, 0], a[0], bm, cm, init_f)
    y_b, s_b = _ssd_scan(_flip(xh), _flip(dt[:, :, 1]), a[1], _flip(bm), _flip(cm), init_b)
    y = y_f + _flip(y_b) + d_skip[:, None] * xh
    y = y.reshape(b, length, SSD_D_INNER) * jax.nn.silu(z)
    y = _rmsnorm(y, norm_g)
    return _mm3(y, out_w), s_f, s_b


SSD_GN = SSD_GROUPS * SSD_STATE
SSD_GROUP_W = SSD_D_INNER // SSD_GROUPS
SSD_HEADS_PER_GROUP = SSD_HEADS // SSD_GROUPS
ROW_TILE = 256


def _modnorm(x, ng, sh, sc):
    ms = jnp.mean(x * x, axis=-1, keepdims=True)
    return (x * lax.rsqrt(ms + RMS_EPS) * ng) * (1.0 + sc) + sh


HALO = 8
CONV_COLS = 512


def _halo_rows(xm_ref, xp_ref, xn_ref, ng_ref, sh_ref, sc_ref):
    j = pl.program_id(1)
    xa = jnp.concatenate([xp_ref[0], xm_ref[0], xn_ref[0]], axis=0)
    h = _modnorm(xa, ng_ref[...], sh_ref[0], sc_ref[0]).astype(jnp.bfloat16)
    tm = xm_ref.shape[1]
    r = lax.broadcasted_iota(jnp.int32, (tm + 2 * HALO, 1), 0)
    valid = ((r >= HALO) | (j > 0)) & ((r < tm + HALO) | (j < pl.num_programs(1) - 1))
    return h, valid


def _conv_rows(pad_ref, cw_ref, cb_ref, o_ref, tm, silu):
    taps = cw_ref.shape[0]
    ncol = pad_ref.shape[1]
    for c0 in range(0, ncol, CONV_COLS):
        cols = slice(c0, c0 + CONV_COLS)
        acc = cb_ref[:, cols] + jnp.zeros((tm, CONV_COLS), jnp.float32)
        for k in range(taps):
            off = HALO + k - taps // 2
            acc = acc + cw_ref[k:k + 1, cols] * pad_ref[off:off + tm, cols]
        if silu:
            acc = acc * jax.nn.sigmoid(acc)
        o_ref[0, :, cols] = acc.astype(o_ref.dtype)


def _halo_specs(length, tm, d):
    nh = length // HALO
    per = tm // HALO
    main = pl.BlockSpec((1, tm, d), lambda i, j: (i, j, 0))
    prev = pl.BlockSpec((1, HALO, d), lambda i, j: (i, jnp.maximum(j * per - 1, 0), 0))
    nxt = pl.BlockSpec((1, HALO, d), lambda i, j: (i, jnp.minimum((j + 1) * per, nh - 1), 0))
    return [main, prev, nxt]


def _ssd_in_kernel(xm_ref, xp_ref, xn_ref, ng_ref, sh_ref, sc_ref, w_ref, wdt_ref, wdtt_ref, cw_ref, cb_ref,
                   z_ref, xbc_ref, dt_ref, dtt_ref, pad_ref):
    tm = xm_ref.shape[1]
    h, valid = _halo_rows(xm_ref, xp_ref, xn_ref, ng_ref, sh_ref, sc_ref)
    zx = jnp.dot(h, w_ref[...], preferred_element_type=jnp.float32)
    z_ref[0] = zx[HALO:HALO + tm, :SSD_D_INNER].astype(jnp.bfloat16)
    pad_ref[...] = jnp.where(valid, zx[:, SSD_D_INNER:], 0.0)
    _conv_rows(pad_ref, cw_ref, cb_ref, xbc_ref, tm, silu=True)
    hm = h[HALO:HALO + tm]
    dt_ref[0] = jnp.dot(hm, wdt_ref[...], preferred_element_type=jnp.float32)
    dtt_ref[0] = lax.dot_general(wdtt_ref[...], hm, (((1,), (1,)), ((), ())),
                                 preferred_element_type=jnp.float32)


def _ssd_in(x, ng, sh, sc, in_w, conv_w, conv_b):
    b, length, d = x.shape
    tm = min(length, ROW_TILE)
    nzx = SSD_D_INNER + SSD_XBC
    w = in_w[:, :nzx].astype(jnp.bfloat16)
    wdt = in_w[:, nzx:]
    wdt_p = jnp.pad(wdt, ((0, 0), (0, LANES - 2 * SSD_HEADS))).astype(jnp.bfloat16)
    wdt_t = wdt.T.astype(jnp.bfloat16)
    taps = conv_w.shape[0]
    row = lambda i, j: (i, j, 0)
    per_b = lambda i, j: (i, 0, 0)
    full2 = lambda i, j: (0, 0)
    return pl.pallas_call(
        _ssd_in_kernel,
        grid=(b, length // tm),
        in_specs=_halo_specs(length, tm, d) + [
            pl.BlockSpec((1, d), full2), pl.BlockSpec((1, 1, d), per_b), pl.BlockSpec((1, 1, d), per_b),
            pl.BlockSpec((d, nzx), full2), pl.BlockSpec((d, LANES), full2),
            pl.BlockSpec((2 * SSD_HEADS, d), full2),
            pl.BlockSpec((taps, SSD_XBC), full2), pl.BlockSpec((1, SSD_XBC), full2)],
        out_specs=[pl.BlockSpec((1, tm, SSD_D_INNER), row), pl.BlockSpec((1, tm, SSD_XBC), row),
                   pl.BlockSpec((1, tm, LANES), row),
                   pl.BlockSpec((1, 2 * SSD_HEADS, tm), lambda i, j: (i, 0, j))],
        out_shape=[jax.ShapeDtypeStruct((b, length, SSD_D_INNER), jnp.bfloat16),
                   jax.ShapeDtypeStruct((b, length, SSD_XBC), jnp.bfloat16),
                   jax.ShapeDtypeStruct((b, length, LANES), jnp.float32),
                   jax.ShapeDtypeStruct((b, 2 * SSD_HEADS, length), jnp.float32)],
        scratch_shapes=[pltpu.VMEM((tm + 2 * HALO, SSD_XBC), jnp.float32)],
        compiler_params=pltpu.CompilerParams(
            dimension_semantics=("parallel", "parallel"), vmem_limit_bytes=56 * 1024 * 1024),
        name="ssd_in",
    )(x, x, x, ng.reshape(1, d), sh, sc, w, wdt_p, wdt_t, conv_w, conv_b.reshape(1, SSD_XBC))


def _split3_bf16(v):
    p1 = v.astype(jnp.bfloat16)
    r1 = v - p1.astype(jnp.float32)
    p2 = r1.astype(jnp.bfloat16)
    p3 = (r1 - p2.astype(jnp.float32)).astype(jnp.bfloat16)
    return p1, p2, p3


def _softplus(v):
    return jnp.maximum(v, 0.0) + jnp.log1p(jnp.exp(-jnp.abs(v)))


def _expand_heads(cols, g, hoff):
    q = cols.shape[0]
    lane = lax.broadcasted_iota(jnp.int32, (q, LANES), 1)
    tiles = []
    for k in range(SSD_HEADS_PER_GROUP // 2):
        ha = hoff + g * SSD_HEADS_PER_GROUP + 2 * k
        tiles.append(jnp.take_along_axis(cols, jnp.where(lane < SSD_HEAD_DIM, ha, ha + 1), axis=1))
    return jnp.concatenate(tiles, axis=1)


def _ssd_scan_kernel(x_ref, b_ref, c_ref, dt_ref, dtt_ref, dtb_ref, dtbt_ref, a_ref, at_ref, init_ref, extra_ref,
                     y_ref, fin_ref, st_ref, *, reverse, hoff, add_prev):
    ci = pl.program_id(1)

    @pl.when(ci == 0)
    def _():
        st_ref[...] = init_ref[0]

    q = SSD_CHUNK
    f32, bf16 = jnp.float32, jnp.bfloat16
    dt = _softplus(dt_ref[0] + dtb_ref[...])
    dtt = _softplus(dtt_ref[0][hoff:hoff + SSD_HEADS, :] + dtbt_ref[...])
    ri = lax.broadcasted_iota(jnp.int32, (q, q), 0)
    cj = lax.broadcasted_iota(jnp.int32, (q, q), 1)
    keep = (cj >= ri) if reverse else (cj <= ri)
    tri = keep.astype(bf16)
    tri_t = ((ri >= cj) if reverse else (ri <= cj)).astype(bf16)
    acum = sum(jnp.dot(tri, p, preferred_element_type=f32) for p in _split3_bf16(dt * a_ref[...]))
    acum_t = sum(jnp.dot(p, tri_t, preferred_element_type=f32) for p in _split3_bf16(dtt * at_ref[...]))
    end = 0 if reverse else q - 1
    a_end = acum[end:end + 1, :]
    eacum = jnp.exp(acum)
    dt_dec_end = dt * jnp.exp(a_end - acum)
    lane = lax.broadcasted_iota(jnp.int32, (q, LANES), 1)
    for g in range(SSD_GROUPS):
        cg = c_ref[0][:, g * SSD_STATE:(g + 1) * SSD_STATE]
        bg = b_ref[0][:, g * SSD_STATE:(g + 1) * SSD_STATE]
        cg16 = cg.astype(bf16)
        cb = lax.dot_general(cg16, bg.astype(bf16), (((1,), (1,)), ((), ())), preferred_element_type=f32)
        xg16 = x_ref[0][:, g * SSD_GROUP_W:(g + 1) * SSD_GROUP_W].astype(bf16)
        xg = xg16.astype(f32)
        eac_x = _expand_heads(eacum, g, hoff)
        yd = []
        for k in range(SSD_HEADS_PER_GROUP // 2):
            xp = xg16[:, k * LANES:(k + 1) * LANES]
            ys = []
            for hh in range(2):
                h = g * SSD_HEADS_PER_GROUP + 2 * k + hh
                seg = acum[:, hoff + h:hoff + h + 1] - acum_t[h:h + 1, :]
                lmat = jnp.exp(jnp.where(keep, seg, -jnp.inf)) * dtt[h:h + 1, :]
                ys.append(jnp.dot((cb * lmat).astype(bf16), xp, preferred_element_type=f32))
            yd.append(jnp.where(lane < SSD_HEAD_DIM, ys[0], ys[1]))
        st = st_ref[g]
        y_off = jnp.dot(cg16, st.astype(bf16), preferred_element_type=f32) * eac_x
        cols = slice(g * SSD_GROUP_W, (g + 1) * SSD_GROUP_W)
        if add_prev:
            other = extra_ref[0, :, cols].astype(f32)
        else:
            other = extra_ref[:, cols] * xg
        y_ref[0, :, cols] = (jnp.concatenate(yd, axis=1) + y_off + other).astype(bf16)
        xdd16 = (xg * _expand_heads(dt_dec_end, g, hoff)).astype(bf16)
        st_ref[g] = st * eac_x[end:end + 1, :] + jnp.dot(bg.astype(f32).T.astype(bf16), xdd16, preferred_element_type=f32)

    @pl.when(ci == pl.num_programs(1) - 1)
    def _():
        fin_ref[0] = st_ref[...]


def _ssd_scan_p(xbc, dt_raw, dt_raw_t, dt_bias, a, init, reverse, direction, y_prev=None, d_skip=None):
    b, length, _ = xbc.shape
    nc = length // SSD_CHUNK
    q = SSD_CHUNK
    cidx = (lambda j: nc - 1 - j) if reverse else (lambda j: j)
    nb = SSD_D_INNER // SSD_GN
    hoff = direction * SSD_HEADS
    full2 = lambda i, j: (0, 0)
    st_shape = (SSD_GROUPS, SSD_STATE, SSD_GROUP_W)
    lanes = lambda v: jnp.pad(v, (hoff, LANES - hoff - SSD_HEADS)).reshape(1, LANES)
    add_prev = y_prev is not None
    if add_prev:
        extra = y_prev
        extra_spec = pl.BlockSpec((1, q, SSD_D_INNER), lambda i, j: (i, cidx(j), 0))
    else:
        extra = jnp.repeat(d_skip, SSD_HEAD_DIM).reshape(1, SSD_D_INNER)
        extra_spec = pl.BlockSpec((1, SSD_D_INNER), full2)
    return pl.pallas_call(
        functools.partial(_ssd_scan_kernel, reverse=reverse, hoff=hoff, add_prev=add_prev),
        grid=(b, nc),
        in_specs=[pl.BlockSpec((1, q, SSD_D_INNER), lambda i, j: (i, cidx(j), 0)),
                  pl.BlockSpec((1, q, SSD_GN), lambda i, j: (i, cidx(j), nb)),
                  pl.BlockSpec((1, q, SSD_GN), lambda i, j: (i, cidx(j), nb + 1)),
                  pl.BlockSpec((1, q, LANES), lambda i, j: (i, cidx(j), 0)),
                  pl.BlockSpec((1, 2 * SSD_HEADS, q), lambda i, j: (i, 0, cidx(j))),
                  pl.BlockSpec((1, LANES), full2), pl.BlockSpec((SSD_HEADS, 1), full2),
                  pl.BlockSpec((1, LANES), full2), pl.BlockSpec((SSD_HEADS, 1), full2),
                  pl.BlockSpec((1,) + st_shape, lambda i, j: (i, 0, 0, 0)),
                  extra_spec],
        out_specs=[pl.BlockSpec((1, q, SSD_D_INNER), lambda i, j: (i, cidx(j), 0)),
                   pl.BlockSpec((1,) + st_shape, lambda i, j: (i, 0, 0, 0))],
        out_shape=[jax.ShapeDtypeStruct((b, length, SSD_D_INNER), jnp.bfloat16),
                   jax.ShapeDtypeStruct((b,) + st_shape, jnp.float32)],
        scratch_shapes=[pltpu.VMEM(st_shape, jnp.float32)],
        compiler_params=pltpu.CompilerParams(
            dimension_semantics=("parallel", "arbitrary"), vmem_limit_bytes=VMEM_LIMIT_BYTES),
        name="ssd_scan",
    )(xbc, xbc, xbc, dt_raw, dt_raw_t, lanes(dt_bias), dt_bias.reshape(-1, 1),
      lanes(a), a.reshape(-1, 1), init, extra)


def _ssd_out_kernel(y_ref, z_ref, ng_ref, w_ref, o_ref):
    z = z_ref[0].astype(jnp.float32)
    y = y_ref[0].astype(jnp.float32) * (z * jax.nn.sigmoid(z))
    ms = jnp.mean(y * y, axis=-1, keepdims=True)
    y = y * lax.rsqrt(ms + RMS_EPS) * ng_ref[...]
    o_ref[0] = jnp.dot(y.astype(jnp.bfloat16), w_ref[...], preferred_element_type=jnp.float32)


def _ssd_out(y, z, norm_g, out_w):
    b, length, di = y.shape
    d = out_w.shape[1]
    tm = min(length, 2 * ROW_TILE)
    row = lambda i, j: (i, j, 0)
    full2 = lambda i, j: (0, 0)
    return pl.pallas_call(
        _ssd_out_kernel,
        grid=(b, length // tm),
        in_specs=[pl.BlockSpec((1, tm, di), row), pl.BlockSpec((1, tm, di), row),
                  pl.BlockSpec((1, di), full2), pl.BlockSpec((di, d), full2)],
        out_specs=pl.BlockSpec((1, tm, d), row),
        out_shape=jax.ShapeDtypeStruct((b, length, d), jnp.float32),
        compiler_params=pltpu.CompilerParams(
            dimension_semantics=("parallel", "parallel"), vmem_limit_bytes=VMEM_LIMIT_BYTES),
        name="ssd_out",
    )(y, z, norm_g.reshape(1, di), out_w.astype(jnp.bfloat16))


def _state_to_kernel(s):
    b = s.shape[0]
    s = s.reshape(b, SSD_GROUPS, SSD_HEADS_PER_GROUP, SSD_HEAD_DIM, SSD_STATE)
    return jnp.transpose(s, (0, 1, 4, 2, 3)).reshape(b, SSD_GROUPS, SSD_STATE, SSD_GROUP_W)


def _state_from_kernel(s):
    b = s.shape[0]
    s = s.reshape(b, SSD_GROUPS, SSD_STATE, SSD_HEADS_PER_GROUP, SSD_HEAD_DIM)
    return jnp.transpose(s, (0, 1, 3, 4, 2)).reshape(b, SSD_HEADS, SSD_HEAD_DIM, SSD_STATE)


def _ssd_mixer_p(x, ng, sh, sc, init_f, init_b, in_w, conv_w, conv_b, dt_bias, a_log, d_skip, norm_g, out_w):
    z, xbc, dt_raw, dt_raw_t = _ssd_in(x, ng, sh, sc, in_w, conv_w, conv_b)
    a = -jnp.exp(a_log)
    yf, s_f = _ssd_scan_p(xbc, dt_raw, dt_raw_t, dt_bias[0], a[0], _state_to_kernel(init_f), False, 0,
                          d_skip=d_skip)
    y, s_b = _ssd_scan_p(xbc, dt_raw, dt_raw_t, dt_bias[1], a[1], _state_to_kernel(init_b), True, 1, y_prev=yf)
    m = _ssd_out(y, z, norm_g, out_w)
    return m, _state_from_kernel(s_f), _state_from_kernel(s_b)


HY_MAX_BLOCK = 512
HY_CC = 128
HY_MAC_ELEMS = 2048
HY_HIDDEN = 64
HY_FEAT_ROWS = 64


def _odd_dft_tables(n):
    m = np.arange(n, dtype=np.int64)[:, None]
    f = np.arange(n // 2, dtype=np.int64)[None, :]
    ang = 2.0 * np.pi * (((2 * f + 1) * m) % (2 * n)).astype(np.float64) / (2 * n)
    return np.cos(ang), np.sin(ang)


def _hy_in_kernel(xm_ref, xp_ref, xn_ref, ng_ref, sh_ref, sc_ref, w_ref, b_ref, cw_ref, cb_ref, o_ref, pad_ref):
    tm = xm_ref.shape[1]
    h, valid = _halo_rows(xm_ref, xp_ref, xn_ref, ng_ref, sh_ref, sc_ref)
    u = jnp.dot(h, w_ref[...], preferred_element_type=jnp.float32) + b_ref[...]
    pad_ref[...] = jnp.where(valid, u, 0.0)
    _conv_rows(pad_ref, cw_ref, cb_ref, o_ref, tm, silu=False)


def _hy_in(x, ng, sh, sc, in_w, in_b, short_w, short_b):
    b, length, d = x.shape
    n = in_w.shape[1]
    tm = min(length, ROW_TILE)
    taps = short_w.shape[0]
    per_b = lambda i, j: (i, 0, 0)
    full2 = lambda i, j: (0, 0)
    return pl.pallas_call(
        _hy_in_kernel,
        grid=(b, length // tm),
        in_specs=_halo_specs(length, tm, d) + [
            pl.BlockSpec((1, d), full2), pl.BlockSpec((1, 1, d), per_b), pl.BlockSpec((1, 1, d), per_b),
            pl.BlockSpec((d, n), full2), pl.BlockSpec((1, n), full2),
            pl.BlockSpec((taps, n), full2), pl.BlockSpec((1, n), full2)],
        out_specs=pl.BlockSpec((1, tm, n), lambda i, j: (i, j, 0)),
        out_shape=jax.ShapeDtypeStruct((b, length, n), jnp.bfloat16),
        scratch_shapes=[pltpu.VMEM((tm + 2 * HALO, n), jnp.float32)],
        compiler_params=pltpu.CompilerParams(
            dimension_semantics=("parallel", "parallel"), vmem_limit_bytes=VMEM_LIMIT_BYTES),
        name="hy_in",
    )(x, x, x, ng.reshape(1, d), sh, sc, in_w.astype(jnp.bfloat16), in_b.reshape(1, n), short_w, short_b.reshape(1, n))


def _dot3(a, b):
    a_hi, a_lo = _split_bf16(a)
    b_hi, b_lo = _split_bf16(b)
    f32 = jnp.float32
    return (jnp.dot(a_hi, b_hi, preferred_element_type=f32) + jnp.dot(a_lo, b_hi, preferred_element_type=f32)
            + jnp.dot(a_hi, b_lo, preferred_element_type=f32))


def _hy_filter_kernel(w1t_ref, b1_ref, w2t_ref, b2_ref, w3t_ref, fr_ref, dl_ref, o_ref, *, length, blk):
    k = pl.program_id(0)
    q = (lax.broadcasted_iota(jnp.int32, (1, blk), 1) + k * blk)
    pos = jnp.abs(q - length).astype(jnp.float32)
    t = pos / float(length - 1)
    w = (2.0 * math.pi / length) * pos
    band = lax.broadcasted_iota(jnp.int32, (HY_BANDS, 1), 0).astype(jnp.float32)
    fb = 1e-4 + band * ((HY_BANDS - 1 - 1e-4) / (HY_BANDS - 1))
    z = jnp.concatenate([jnp.broadcast_to(t, (8, blk)), jnp.cos(fb * w), -jnp.sin(fb * w),
                         jnp.zeros((HY_FEAT_ROWS - 8 - 2 * HY_BANDS, blk), jnp.float32)], axis=0)
    h = jnp.sin(fr_ref[...] * (_dot3(w1t_ref[...], z) + b1_ref[...]))
    h = jnp.sin(fr_ref[...] * (_dot3(w2t_ref[...], h) + b2_ref[...]))
    kt = _dot3(w3t_ref[0], h)
    o_ref[0] = kt * jnp.exp(-t * dl_ref[...])


def _hy_filter(length, blk, f_w1, f_b1, f_w2, f_b2, f_w3, f_freq):
    d = f_w3.shape[1] // 2
    nk = 2 * length // blk
    w1t = jnp.concatenate([f_w1[0:1].T, jnp.zeros((HY_HIDDEN, 7), jnp.float32), f_w1[1:].T,
                           jnp.zeros((HY_HIDDEN, HY_FEAT_ROWS - 8 - 2 * HY_BANDS), jnp.float32)], axis=1)
    w3t = jnp.stack([f_w3[:, d:].T, f_w3[:, :d].T])
    deltas = jnp.abs(jnp.linspace(HY_MIN_DECAY, HY_MAX_DECAY, d, dtype=jnp.float32)).reshape(d, 1)
    col = lambda v: v.reshape(HY_HIDDEN, 1)
    full2 = lambda k: (0, 0)
    half = length // blk
    return pl.pallas_call(
        functools.partial(_hy_filter_kernel, length=length, blk=blk),
        grid=(nk,),
        in_specs=[pl.BlockSpec((HY_HIDDEN, HY_FEAT_ROWS), full2), pl.BlockSpec((HY_HIDDEN, 1), full2),
                  pl.BlockSpec((HY_HIDDEN, HY_HIDDEN), full2), pl.BlockSpec((HY_HIDDEN, 1), full2),
                  pl.BlockSpec((1, d, HY_HIDDEN), lambda k: (k // half, 0, 0)),
                  pl.BlockSpec((HY_HIDDEN, 1), full2), pl.BlockSpec((d, 1), full2)],
        out_specs=pl.BlockSpec((1, d, blk), lambda k: (k, 0, 0)),
        out_shape=jax.ShapeDtypeStruct((nk, d, blk), jnp.float32),
        compiler_params=pltpu.CompilerParams(
            dimension_semantics=("parallel",), vmem_limit_bytes=VMEM_LIMIT_BYTES),
        name="hy_filter",
    )(w1t, col(f_b1), f_w2.T, col(f_b2), w3t, col(f_freq), deltas)


def _hy_gspec_kernel(hi_ref, lo_ref, ft_ref, fb_ref, o_ref):
    o_ref[0] = _dot3(hi_ref[0], ft_ref[...]) + _dot3(lo_ref[0], fb_ref[...])


def _hy_gspec(kt):
    nk, d, blk = kt.shape
    cos, sin = _odd_dft_tables(2 * blk)
    top = np.concatenate([cos[:blk], -sin[:blk]], axis=1)
    bot = -np.concatenate([cos[blk:], -sin[blk:]], axis=1)
    bot[0] = 0.0
    tm = 512
    full2 = lambda e, i: (0, 0)
    return pl.pallas_call(
        _hy_gspec_kernel,
        grid=(nk - 1, d // tm),
        in_specs=[pl.BlockSpec((1, tm, blk), lambda e, i: (e + 1, i, 0)),
                  pl.BlockSpec((1, tm, blk), lambda e, i: (e, i, 0)),
                  pl.BlockSpec((blk, 2 * blk), full2), pl.BlockSpec((blk, 2 * blk), full2)],
        out_specs=pl.BlockSpec((1, tm, 2 * blk), lambda e, i: (e, i, 0)),
        out_shape=jax.ShapeDtypeStruct((nk - 1, d, 2 * blk), jnp.float32),
        compiler_params=pltpu.CompilerParams(
            dimension_semantics=("parallel", "parallel"), vmem_limit_bytes=VMEM_LIMIT_BYTES),
        name="hy_gspec",
    )(kt, kt, jnp.asarray(top, jnp.float32), jnp.asarray(bot, jnp.float32))


def _hy_conv_kernel(x0_ref, x1_ref, v_ref, g_ref, fb_ref, ff_ref, fi_ref, o_ref, lhs_ref, u_ref, y_ref, *, nb):
    cc, bsz = HY_CC, ff_ref.shape[0]
    mrows = BF16_TILE_ROWS
    mlanes = HY_MAC_ELEMS // mrows
    for j in range(nb):
        sl = slice(j * bsz, (j + 1) * bsz)
        wj = v_ref[0, sl, :].astype(jnp.float32) * x1_ref[0, sl, :].astype(jnp.float32)
        lhs_ref[j * cc:(j + 1) * cc, :] = wj.T.astype(jnp.bfloat16)
    u_ref[...] = jnp.dot(lhs_ref[...], ff_ref[...], preferred_element_type=jnp.float32)

    def per_out_block(i, carry):
        def per_rows(rc, carry2):
            rows = pl.ds(pl.multiple_of(rc * mrows, mrows), mrows)
            yrows = pl.ds(pl.multiple_of(i * cc + rc * mrows, mrows), mrows)
            for l0 in range(0, bsz, mlanes):
                re = slice(l0, l0 + mlanes)
                im = slice(bsz + l0, bsz + l0 + mlanes)
                acc_r = jnp.zeros((mrows, mlanes), jnp.float32)
                acc_i = jnp.zeros((mrows, mlanes), jnp.float32)
                for j in range(nb):
                    e = i - j + (nb - 1)
                    gr = g_ref[e, rows, re]
                    gi = g_ref[e, rows, im]
                    urows = pl.ds(pl.multiple_of(j * cc + rc * mrows, mrows), mrows)
                    ur = u_ref[urows, re]
                    ui = u_ref[urows, im]
                    acc_r = acc_r + gr * ur - gi * ui
                    acc_i = acc_i + gr * ui + gi * ur
                y_ref[yrows, re] = acc_r.astype(jnp.bfloat16)
                y_ref[yrows, im] = acc_i.astype(jnp.bfloat16)
            return carry2
        return lax.fori_loop(0, cc // mrows, per_rows, carry)
    lax.fori_loop(0, nb, per_out_block, 0)

    yt = jnp.dot(y_ref[...], fi_ref[...], preferred_element_type=jnp.float32)
    for i in range(nb):
        sl = slice(i * bsz, (i + 1) * bsz)
        w = v_ref[0, sl, :].astype(jnp.float32) * x1_ref[0, sl, :].astype(jnp.float32)
        gated = (yt[i * cc:(i + 1) * cc, :].T + fb_ref[...] * w) * x0_ref[0, sl, :].astype(jnp.float32)
        o_ref[0, sl, :] = gated.astype(o_ref.dtype)


def _hy_conv(u, g, f_bias, blk):
    b, length, d3 = u.shape
    d = d3 // 3
    nb = length // blk
    ncb = d // HY_CC
    cos, sin = _odd_dft_tables(2 * blk)
    fwd = np.concatenate([cos[:blk], -sin[:blk]], axis=1)
    inv = (1.0 / blk) * np.concatenate([cos[:blk].T, -sin[:blk].T], axis=0)
    col = lambda off: pl.BlockSpec((1, length, HY_CC), lambda c, i, off=off: (i, 0, off + c))
    full2 = lambda c, i: (0, 0)
    return pl.pallas_call(
        functools.partial(_hy_conv_kernel, nb=nb),
        grid=(ncb, b),
        in_specs=[col(0), col(ncb), col(2 * ncb),
                  pl.BlockSpec((2 * nb - 1, HY_CC, 2 * blk), lambda c, i: (0, c, 0)),
                  pl.BlockSpec((1, HY_CC), lambda c, i: (0, c)),
                  pl.BlockSpec((blk, 2 * blk), full2), pl.BlockSpec((2 * blk, blk), full2)],
        out_specs=pl.BlockSpec((1, length, HY_CC), lambda c, i: (i, 0, c)),
        out_shape=jax.ShapeDtypeStruct((b, length, d), jnp.bfloat16),
        scratch_shapes=[pltpu.VMEM((nb * HY_CC, blk), jnp.bfloat16),
                        pltpu.VMEM((nb * HY_CC, 2 * blk), jnp.float32),
                        pltpu.VMEM((nb * HY_CC, 2 * blk), jnp.bfloat16)],
        compiler_params=pltpu.CompilerParams(
            dimension_semantics=("parallel", "arbitrary"), vmem_limit_bytes=56 * 1024 * 1024),
        name="hy_conv",
    )(u, u, u, g, f_bias.reshape(1, d), jnp.asarray(fwd, jnp.bfloat16), jnp.asarray(inv, jnp.bfloat16))


def _hyena_mixer_p(x, ng, sh, sc, in_w, in_b, short_w, short_b, f_w1, f_b1, f_w2, f_b2, f_w3, f_freq, f_bias, out_w):
    length = x.shape[1]
    blk = min(HY_MAX_BLOCK, length)
    u = _hy_in(x, ng, sh, sc, in_w, in_b, short_w, short_b)
    g = _hy_gspec(_hy_filter(length, blk, f_w1, f_b1, f_w2, f_b2, f_w3, f_freq))
    return _mm3(_hy_conv(u, g, f_bias, blk), out_w)


TOK_TILE = 256
MOE_ROWS = 512
SEG_CHUNK = 64
BF16_TILE_ROWS = 16
LANES = 128


def _split_bf16(w):
    hi = w.astype(jnp.bfloat16)
    lo = (w - hi.astype(jnp.float32)).astype(jnp.bfloat16)
    return hi, lo


def _moe_pre_kernel(x_ref, m_ref, g1_ref, ng_ref, sh_ref, sc_ref, wrh_ref, wrl_ref,
                    xo_ref, hpk_ref, lg_ref):
    x = x_ref[0] + g1_ref[0] * m_ref[0]
    xo_ref[0] = x
    ms = jnp.mean(x * x, axis=-1, keepdims=True)
    h = x * lax.rsqrt(ms + RMS_EPS) * ng_ref[...]
    h = h * (1.0 + sc_ref[0]) + sh_ref[0]
    h_hi = h.astype(jnp.bfloat16)
    h_lo = (h - h_hi.astype(jnp.float32)).astype(jnp.bfloat16)
    dn = (((1,), (1,)), ((), ()))
    lg = lax.dot_general(wrh_ref[...], h_hi, dn, preferred_element_type=jnp.float32)
    lg += lax.dot_general(wrh_ref[...], h_lo, dn, preferred_element_type=jnp.float32)
    lg += lax.dot_general(wrl_ref[...], h_hi, dn, preferred_element_type=jnp.float32)
    lg_ref[0] = lg
    half = h.shape[1] // 2
    wa = pltpu.bitcast(h_hi[:, :half].astype(jnp.float32), jnp.uint32) >> 16
    wb = pltpu.bitcast(h_hi[:, half:].astype(jnp.float32), jnp.uint32) & jnp.uint32(0xFFFF0000)
    hpk_ref[0] = wa | wb


def _moe_pre(x, m, g1, ng, sh, sc, w_router):
    b, length, d = x.shape
    tm = min(length, 512)
    wrh, wrl = _split_bf16(w_router.T)
    row = lambda i, j: (i, j, 0)
    per_b = lambda i, j: (i, 0, 0)
    full2 = lambda i, j: (0, 0)
    return pl.pallas_call(
        _moe_pre_kernel,
        grid=(b, length // tm),
        in_specs=[pl.BlockSpec((1, tm, d), row), pl.BlockSpec((1, tm, d), row),
                  pl.BlockSpec((1, 1, d), per_b), pl.BlockSpec((1, d), full2),
                  pl.BlockSpec((1, 1, d), per_b), pl.BlockSpec((1, 1, d), per_b),
                  pl.BlockSpec((N_EXPERTS, d), full2), pl.BlockSpec((N_EXPERTS, d), full2)],
        out_specs=[pl.BlockSpec((1, tm, d), row), pl.BlockSpec((1, tm, d // 2), row),
                   pl.BlockSpec((1, N_EXPERTS, tm), lambda i, j: (i, 0, j))],
        out_shape=[jax.ShapeDtypeStruct((b, length, d), jnp.float32),
                   jax.ShapeDtypeStruct((b, length, d // 2), jnp.uint32),
                   jax.ShapeDtypeStruct((b, N_EXPERTS, length), jnp.float32)],
        compiler_params=pltpu.CompilerParams(
            dimension_semantics=("parallel", "parallel"), vmem_limit_bytes=VMEM_LIMIT_BYTES),
        name="moe_pre",
    )(x, m, g1, ng.reshape(1, d), sh, sc, wrh, wrl)


def _moe_ffn_kernel(idx_ref, nidx_ref, h_hbm, gate_ref, wg_ref, wu_ref, wd_ref, y_ref,
                    xe_ref, wgb, wub, wdb, sem):
    nblk = pl.num_programs(1)
    step = pl.program_id(0) * nblk + pl.program_id(1)
    last = pl.num_programs(0) * nblk - 1

    def issue(ids_ref, slot):
        base = slot * MOE_ROWS
        for c in range(MOE_ROWS):
            pltpu.make_async_copy(h_hbm.at[pl.ds(ids_ref[0, 0, c], 1)], xe_ref.at[pl.ds(base + c, 1)],
                                  sem.at[slot]).start()

    @pl.when(step == 0)
    def _():
        issue(idx_ref, 0)

    for parity in range(2):
        @pl.when((step < last) & (step % 2 == parity))
        def _(parity=parity):
            issue(nidx_ref, 1 - parity)

    @pl.when(pl.program_id(1) == 0)
    def _():
        wgb[...] = wg_ref[0, 0].astype(jnp.bfloat16)
        wub[...] = wu_ref[0, 0].astype(jnp.bfloat16)
        wdb[...] = wd_ref[0, 0].astype(jnp.bfloat16)

    slot = step % 2
    rows = pl.ds(pl.multiple_of(slot * MOE_ROWS, MOE_ROWS), MOE_ROWS)
    pltpu.make_async_copy(h_hbm.at[pl.ds(0, MOE_ROWS)], xe_ref.at[rows], sem.at[slot]).wait()
    half = wgb.shape[0] // 2
    w = xe_ref[rows, :]
    xa = pltpu.bitcast(w << 16, jnp.float32).astype(jnp.bfloat16)
    xb = pltpu.bitcast(w & jnp.uint32(0xFFFF0000), jnp.float32).astype(jnp.bfloat16)
    hg = jnp.dot(xa, wgb[:half], preferred_element_type=jnp.float32)
    hg += jnp.dot(xb, wgb[half:], preferred_element_type=jnp.float32)
    hu = jnp.dot(xa, wub[:half], preferred_element_type=jnp.float32)
    hu += jnp.dot(xb, wub[half:], preferred_element_type=jnp.float32)
    hid = (hg * jax.nn.sigmoid(hg) * hu).astype(jnp.bfloat16)
    y = jnp.dot(hid, wdb[...], preferred_element_type=jnp.float32)
    y_ref[0] = (y * gate_ref[0]).astype(jnp.bfloat16)


def _moe_ffn(hpk, grow, gate, w_gate, w_up, w_down, layer):
    e, r = grow.shape
    d, f = w_gate.shape[2], w_gate.shape[3]
    nblk = r // MOE_ROWS
    nsteps = e * nblk
    wspec = lambda shp: pl.BlockSpec((1, 1) + shp, lambda i, j: (layer, i, 0, 0))
    ids = grow.reshape(nsteps, 1, MOE_ROWS)
    smem_ids = lambda off: pl.BlockSpec(
        (1, 1, MOE_ROWS), lambda i, j: (jnp.minimum(i * nblk + j + off, nsteps - 1), 0, 0), memory_space=pltpu.SMEM)
    return pl.pallas_call(
        _moe_ffn_kernel,
        grid=(e, nblk),
        in_specs=[smem_ids(0), smem_ids(1),
                  pl.BlockSpec(memory_space=pltpu.HBM),
                  pl.BlockSpec((1, MOE_ROWS, 1), lambda i, j: (i, j, 0)),
                  wspec((d, f)), wspec((d, f)), wspec((f, d))],
        out_specs=pl.BlockSpec((1, MOE_ROWS, d), lambda i, j: (i, j, 0)),
        out_shape=jax.ShapeDtypeStruct((e, r, d), jnp.bfloat16),
        scratch_shapes=[pltpu.VMEM((2 * MOE_ROWS, d // 2), jnp.uint32),
                        pltpu.VMEM((d, f), jnp.bfloat16), pltpu.VMEM((d, f), jnp.bfloat16),
                        pltpu.VMEM((f, d), jnp.bfloat16),
                        pltpu.SemaphoreType.DMA((2,))],
        compiler_params=pltpu.CompilerParams(
            dimension_semantics=("arbitrary", "arbitrary"), vmem_limit_bytes=VMEM_LIMIT_BYTES),
        name="moe_ffn",
    )(ids, ids, hpk, gate, w_gate, w_up, w_down)


def _moe_comb_kernel(cs_ref, x_ref, g2_ref, y_ref, idx_ref, fg_ref, o_ref, ycat, acc, *, cap, ch, ntile, final_norm):
    b = pl.program_id(0)
    t = pl.program_id(1)
    base = t * TOK_TILE
    sub = lax.broadcasted_iota(jnp.int32, (TOK_TILE, LANES), 0) + base
    if ntile == 1:
        for e in range(N_EXPERTS):
            ycat[e * ch:(e + 1) * ch, :] = y_ref[e, 0:ch, :]
        v = idx_ref[0]
        tiles = [(v[:, p * LANES:(p + 1) * LANES] == sub).astype(jnp.bfloat16)
                 for p in range(N_EXPERTS * ch // LANES)]
        acc[...] = jnp.dot(jnp.concatenate(tiles, axis=1), ycat[...], preferred_element_type=jnp.float32)
    else:
        lane = lax.broadcasted_iota(jnp.int32, (1, LANES), 1)
        per = LANES // ch
        sts = []
        for e in range(N_EXPERTS):
            s0 = cs_ref[(b * N_EXPERTS + e) * (ntile + 1) + t]
            st = jnp.minimum((s0 // BF16_TILE_ROWS) * BF16_TILE_ROWS, cap - ch)
            st = pl.multiple_of(st, BF16_TILE_ROWS)
            sts.append(st)
            ycat[e * ch:(e + 1) * ch, :] = y_ref[e, pl.ds(st, ch), :]
        tiles = []
        for p in range(N_EXPERTS // per):
            v = None
            for q in range(per):
                e = p * per + q
                r = pltpu.roll(idx_ref[0, e:e + 1, :], (2 * cap - sts[e] + q * ch) % cap, 1)[:, :LANES]
                v = r if v is None else jnp.where(lane >= q * ch, r, v)
            tiles.append((v == sub).astype(jnp.bfloat16))
        acc[...] = jnp.dot(jnp.concatenate(tiles, axis=1), ycat[...], preferred_element_type=jnp.float32)
        sub_c = lax.broadcasted_iota(jnp.int32, (TOK_TILE, ch), 0) + base
        lane_c = lax.broadcasted_iota(jnp.int32, (1, ch), 1)
        for e in range(N_EXPERTS):
            s1 = cs_ref[(b * N_EXPERTS + e) * (ntile + 1) + t + 1]
            first_end = sts[e] + ch
            n_extra = jnp.maximum(s1 - first_end + ch - 1, 0) // ch

            def extra(q, carry, e=e, first_end=first_end):
                lo = first_end + q * ch
                stq = pl.multiple_of(jnp.minimum(lo, cap - ch), BF16_TILE_ROWS)
                r = pltpu.roll(idx_ref[0, e:e + 1, :], (2 * cap - stq) % cap, 1)[:, :ch]
                hit = (r == sub_c) & (lane_c + stq >= lo)
                acc[...] += jnp.dot(hit.astype(jnp.bfloat16), y_ref[e, pl.ds(stq, ch), :],
                                    preferred_element_type=jnp.float32)
                return carry
            lax.fori_loop(0, n_extra, extra, 0)
    out = x_ref[0] + g2_ref[0] * acc[...]
    if final_norm:
        ms = jnp.mean(out * out, axis=-1, keepdims=True)
        out = out * lax.rsqrt(ms + RMS_EPS) * fg_ref[...]
    o_ref[0] = out


def _moe_combine(x, g2, y, idx, cs, final_g=None):
    b, length, d = x.shape
    final_norm = final_g is not None
    fg = (final_g if final_norm else jnp.ones((d,), jnp.float32)).reshape(1, d)
    cap = idx.shape[2]
    ntile = length // TOK_TILE
    ch = min(SEG_CHUNK, cap)
    if ntile == 1:
        idx_in = idx.reshape(b, 1, N_EXPERTS * cap)
        idx_spec = pl.BlockSpec((1, 1, N_EXPERTS * cap), lambda i, j, c: (i, 0, 0))
    else:
        idx_in = idx
        idx_spec = pl.BlockSpec((1, N_EXPERTS, cap), lambda i, j, c: (i, 0, 0))
    grid_spec = pltpu.PrefetchScalarGridSpec(
        num_scalar_prefetch=1,
        grid=(b, ntile),
        in_specs=[pl.BlockSpec((1, TOK_TILE, d), lambda i, j, c: (i, j, 0)),
                  pl.BlockSpec((1, 1, d), lambda i, j, c: (i, 0, 0)),
                  pl.BlockSpec((N_EXPERTS, cap, d), lambda i, j, c: (0, i, 0)),
                  idx_spec,
                  pl.BlockSpec((1, d), lambda i, j, c: (0, 0))],
        out_specs=pl.BlockSpec((1, TOK_TILE, d), lambda i, j, c: (i, j, 0)),
        scratch_shapes=[pltpu.VMEM((N_EXPERTS * ch, d), jnp.bfloat16),
                        pltpu.VMEM((TOK_TILE, d), jnp.float32)])
    return pl.pallas_call(
        functools.partial(_moe_comb_kernel, cap=cap, ch=ch, ntile=ntile, final_norm=final_norm),
        grid_spec=grid_spec,
        out_shape=jax.ShapeDtypeStruct((b, length, d), jnp.float32),
        compiler_params=pltpu.CompilerParams(
            dimension_semantics=("arbitrary", "arbitrary"), vmem_limit_bytes=56 * 1024 * 1024),
        name="moe_combine",
    )(cs.reshape(-1).astype(jnp.int32), x, g2, y, idx_in, fg)


def _moe_block(x, m, g1, ng, sh, sc, g2, w_router, w_gate, w_up, w_down, layer, final_g=None):
    b, length, d = x.shape
    cap = EC_FACTOR * length // N_EXPERTS
    x1, hpk, lg = _moe_pre(x, m, g1, ng, sh, sc, w_router)
    aff = jax.nn.softmax(lg, axis=1)
    _, idx = lax.top_k(aff, cap)
    idx = jnp.sort(idx, axis=-1)
    gate = jnp.take_along_axis(aff, idx, axis=-1)
    ntile = length // TOK_TILE
    bounds = jnp.arange(ntile + 1, dtype=jnp.int32) * TOK_TILE
    cs = jnp.sum(idx[:, :, :, None] < bounds, axis=2, dtype=jnp.int32)
    grow = idx + (jnp.arange(b, dtype=jnp.int32) * length)[:, None, None]
    grow = jnp.swapaxes(grow, 0, 1).reshape(N_EXPERTS, b * cap)
    gate_e = jnp.swapaxes(gate, 0, 1).reshape(N_EXPERTS, b * cap, 1)
    y = _moe_ffn(hpk.reshape(b * length, d // 2), grow, gate_e, w_gate, w_up, w_down, layer)
    return _moe_combine(x1, g2, y, idx, cs, final_g)


def kernel(x_prompt, x_sample, state_ssd, c, c_ctx, norm_g, ada_w, ada_b, hy_in_w, hy_in_b, hy_short_w, hy_short_b, hy_f_w1, hy_f_b1, hy_f_w2, hy_f_b2, hy_f_w3, hy_f_freq, hy_f_bias, hy_out_w, ssd_in_w, ssd_conv_w, ssd_conv_b, ssd_dt_bias, ssd_A_log, ssd_D, ssd_norm_g, ssd_out_w, moe_router, moe_w_gate, moe_w_up, moe_w_down, final_norm_g):
    rows = x_sample.shape[1] // GRID_W
    xp = x_prompt
    xs = x_sample + _sincos_2d(rows, GRID_W, D_MODEL)[None]
    new_ssd = []
    for i in range(DEPTH):
        sh1p, sc1p, g1p, sh2p, sc2p, g2p = _adaln(c_ctx[None, :], ada_w[i], ada_b[i])
        sh1s, sc1s, g1s, sh2s, sc2s, g2s = _adaln(c, ada_w[i], ada_b[i])
        j = i // N_MIXERS
        bp = (xp.shape[0], 1, D_MODEL)
        if i % N_MIXERS == 0:
            hy = (hy_in_w[j], hy_in_b[j], hy_short_w[j], hy_short_b[j], hy_f_w1[j], hy_f_b1[j],
                  hy_f_w2[j], hy_f_b2[j], hy_f_w3[j], hy_f_freq[j], hy_f_bias[j], hy_out_w[j])
            mp = _hyena_mixer_p(xp, norm_g[i, 0], jnp.broadcast_to(sh1p, bp), jnp.broadcast_to(sc1p, bp), *hy)
            ms = _hyena_mixer_p(xs, norm_g[i, 0], sh1s, sc1s, *hy)
        else:
            sp = (ssd_in_w[j], ssd_conv_w[j], ssd_conv_b[j], ssd_dt_bias[j], ssd_A_log[j],
                  ssd_D[j], ssd_norm_g[j], ssd_out_w[j])
            zeros = jnp.zeros((xp.shape[0], SSD_HEADS, SSD_HEAD_DIM, SSD_STATE), jnp.float32)
            mp, s_f, s_b = _ssd_mixer_p(xp, norm_g[i, 0], jnp.broadcast_to(sh1p, bp),
                                        jnp.broadcast_to(sc1p, bp), zeros, zeros, *sp)
            new_ssd.append(jnp.stack([s_f, s_b], axis=1))
            ms, _, _ = _ssd_mixer_p(xs, norm_g[i, 0], sh1s, sc1s, state_ssd[:, j, 0], state_ssd[:, j, 1], *sp)
        moe = (moe_router[i], moe_w_gate, moe_w_up, moe_w_down, i, final_norm_g if i == DEPTH - 1 else None)
        xp = _moe_block(xp, mp, jnp.broadcast_to(g1p, bp), norm_g[i, 1], jnp.broadcast_to(sh2p, bp),
                        jnp.broadcast_to(sc2p, bp), jnp.broadcast_to(g2p, bp), *moe)
        xs = _moe_block(xs, ms, g1s, norm_g[i, 1], sh2s, sc2s, g2s, *moe)
    new_state_ssd = jnp.stack(new_ssd, axis=1)
    return (xp, xs, new_state_ssd)
```

```python
import functools
import math

import jax
import jax.numpy as jnp
import numpy as np
from jax import lax
from jax.experimental import pallas as pl
from jax.experimental.pallas import tpu as pltpu

D_MODEL = 1024
DEPTH = 2
GRID_W = 64
N_MIXERS = 2
RMS_EPS = 1e-6
HY_EMB = 33
HY_BANDS = (HY_EMB - 1) // 2
HY_SHORT_DECAY_FRAC = 0.3
HY_LONG_DECAY_FRAC = 1.5
HY_DECAY_TARGET = 1e-2
HY_MAX_DECAY = math.log(HY_DECAY_TARGET) / HY_SHORT_DECAY_FRAC
HY_MIN_DECAY = math.log(HY_DECAY_TARGET) / HY_LONG_DECAY_FRAC
SSD_D_INNER = 2 * D_MODEL
SSD_HEAD_DIM = 64
SSD_HEADS = SSD_D_INNER // SSD_HEAD_DIM
SSD_GROUPS = 4
SSD_STATE = 128
SSD_CHUNK = 128
SSD_XBC = SSD_D_INNER + 2 * SSD_GROUPS * SSD_STATE
N_EXPERTS = 16
EC_FACTOR = 2

VMEM_LIMIT_BYTES = 48 * 1024 * 1024


def _mm_kernel(a_ref, b_ref, o_ref, acc_ref):
    @pl.when(pl.program_id(2) == 0)
    def _():
        acc_ref[...] = jnp.zeros_like(acc_ref)

    acc_ref[...] += jnp.dot(a_ref[...].astype(jnp.bfloat16), b_ref[...],
                            preferred_element_type=jnp.float32)

    @pl.when(pl.program_id(2) == pl.num_programs(2) - 1)
    def _():
        o_ref[...] = acc_ref[...]


def _pick(n, pref):
    for t in pref:
        if n % t == 0:
            return t
    return n


def _mm(a, b):
    m, k = a.shape
    n = b.shape[1]
    mp = -(-m // 8) * 8
    if mp != m:
        a = jnp.pad(a, ((0, mp - m), (0, 0)))
    tm = _pick(mp, (512, 256, 128, 64, 32, 16, 8))
    tn = _pick(n, (512, 256, 128))
    tk = _pick(k, (1024, 512, 256, 128))
    out = pl.pallas_call(
        _mm_kernel,
        grid=(mp // tm, n // tn, k // tk),
        in_specs=[pl.BlockSpec((tm, tk), lambda i, j, l: (i, l)),
                  pl.BlockSpec((tk, tn), lambda i, j, l: (l, j))],
        out_specs=pl.BlockSpec((tm, tn), lambda i, j, l: (i, j)),
        out_shape=jax.ShapeDtypeStruct((mp, n), jnp.float32),
        scratch_shapes=[pltpu.VMEM((tm, tn), jnp.float32)],
        compiler_params=pltpu.CompilerParams(
            dimension_semantics=("parallel", "parallel", "arbitrary"),
            vmem_limit_bytes=VMEM_LIMIT_BYTES),
        name="mm",
    )(a, b.astype(jnp.bfloat16))
    return out[:m]


def _mm3(a, b):
    lead = a.shape[:-1]
    return _mm(a.reshape(-1, a.shape[-1]), b).reshape(*lead, b.shape[1])


def _adaln(cond, ada_w, ada_b):
    m = _mm(jax.nn.silu(cond), ada_w) + ada_b
    return jnp.split(m[:, None, :], 6, axis=-1)


def _sincos_2d(rows, cols, d):
    q = d // 4
    omega = 1.0 / (10000.0 ** (jnp.arange(q, dtype=jnp.float32) / q))
    t = jnp.arange(rows * cols)
    er = (t // cols).astype(jnp.float32)[:, None] * omega[None, :]
    ec = (t % cols).astype(jnp.float32)[:, None] * omega[None, :]
    return jnp.concatenate([jnp.sin(er), jnp.cos(er), jnp.sin(ec), jnp.cos(ec)], axis=-1)


SSD_GN = SSD_GROUPS * SSD_STATE
SSD_GROUP_W = SSD_D_INNER // SSD_GROUPS
SSD_HEADS_PER_GROUP = SSD_HEADS // SSD_GROUPS
ROW_TILE = 256


def _modnorm(x, ng, sh, sc):
    ms = jnp.mean(x * x, axis=-1, keepdims=True)
    return (x * lax.rsqrt(ms + RMS_EPS) * ng) * (1.0 + sc) + sh


HALO = 8
CONV_COLS = 512


def _halo_rows(xm_ref, xp_ref, xn_ref, ng_ref, sh_ref, sc_ref):
    j = pl.program_id(1)
    xa = jnp.concatenate([xp_ref[0], xm_ref[0], xn_ref[0]], axis=0)
    h = _modnorm(xa, ng_ref[...], sh_ref[0], sc_ref[0]).astype(jnp.bfloat16)
    tm = xm_ref.shape[1]
    r = lax.broadcasted_iota(jnp.int32, (tm + 2 * HALO, 1), 0)
    valid = ((r >= HALO) | (j > 0)) & ((r < tm + HALO) | (j < pl.num_programs(1) - 1))
    return h, valid


def _conv_rows(pad_ref, cw_ref, cb_ref, o_ref, tm, silu):
    taps = cw_ref.shape[0]
    ncol = pad_ref.shape[1]
    for c0 in range(0, ncol, CONV_COLS):
        cols = slice(c0, c0 + CONV_COLS)
        acc = cb_ref[:, cols] + jnp.zeros((tm, CONV_COLS), jnp.float32)
        for k in range(taps):
            off = HALO + k - taps // 2
            acc = acc + cw_ref[k:k + 1, cols] * pad_ref[off:off + tm, cols]
        if silu:
            acc = acc * jax.nn.sigmoid(acc)
        o_ref[0, :, cols] = acc


def _halo_specs(length, tm, d):
    nh = length // HALO
    per = tm // HALO
    main = pl.BlockSpec((1, tm, d), lambda i, j: (i, j, 0))
    prev = pl.BlockSpec((1, HALO, d), lambda i, j: (i, jnp.maximum(j * per - 1, 0), 0))
    nxt = pl.BlockSpec((1, HALO, d), lambda i, j: (i, jnp.minimum((j + 1) * per, nh - 1), 0))
    return [main, prev, nxt]


def _ssd_in_kernel(xm_ref, xp_ref, xn_ref, ng_ref, sh_ref, sc_ref, w_ref, wdt_ref, wdtt_ref, cw_ref, cb_ref,
                   z_ref, xbc_ref, dt_ref, dtt_ref, pad_ref):
    tm = xm_ref.shape[1]
    h, valid = _halo_rows(xm_ref, xp_ref, xn_ref, ng_ref, sh_ref, sc_ref)
    zx = jnp.dot(h, w_ref[...], preferred_element_type=jnp.float32)
    z_ref[0] = zx[HALO:HALO + tm, :SSD_D_INNER].astype(jnp.bfloat16)
    pad_ref[...] = jnp.where(valid, zx[:, SSD_D_INNER:], 0.0)
    _conv_rows(pad_ref, cw_ref, cb_ref, xbc_ref, tm, silu=True)
    hm = h[HALO:HALO + tm]
    dt_ref[0] = jnp.dot(hm, wdt_ref[...], preferred_element_type=jnp.float32)
    dtt_ref[0] = lax.dot_general(wdtt_ref[...], hm, (((1,), (1,)), ((), ())),
                                 preferred_element_type=jnp.float32)


def _ssd_in(x, ng, sh, sc, in_w, conv_w, conv_b):
    b, length, d = x.shape
    tm = min(length, ROW_TILE)
    nzx = SSD_D_INNER + SSD_XBC
    w = in_w[:, :nzx].astype(jnp.bfloat16)
    wdt = in_w[:, nzx:]
    wdt_p = jnp.pad(wdt, ((0, 0), (0, LANES - 2 * SSD_HEADS))).astype(jnp.bfloat16)
    wdt_t = wdt.T.astype(jnp.bfloat16)
    taps = conv_w.shape[0]
    row = lambda i, j: (i, j, 0)
    per_b = lambda i, j: (i, 0, 0)
    full2 = lambda i, j: (0, 0)
    return pl.pallas_call(
        _ssd_in_kernel,
        grid=(b, length // tm),
        in_specs=_halo_specs(length, tm, d) + [
            pl.BlockSpec((1, d), full2), pl.BlockSpec((1, 1, d), per_b), pl.BlockSpec((1, 1, d), per_b),
            pl.BlockSpec((d, nzx), full2), pl.BlockSpec((d, LANES), full2),
            pl.BlockSpec((2 * SSD_HEADS, d), full2),
            pl.BlockSpec((taps, SSD_XBC), full2), pl.BlockSpec((1, SSD_XBC), full2)],
        out_specs=[pl.BlockSpec((1, tm, SSD_D_INNER), row), pl.BlockSpec((1, tm, SSD_XBC), row),
                   pl.BlockSpec((1, tm, LANES), row),
                   pl.BlockSpec((1, 2 * SSD_HEADS, tm), lambda i, j: (i, 0, j))],
        out_shape=[jax.ShapeDtypeStruct((b, length, SSD_D_INNER), jnp.bfloat16),
                   jax.ShapeDtypeStruct((b, length, SSD_XBC), jnp.float32),
                   jax.ShapeDtypeStruct((b, length, LANES), jnp.float32),
                   jax.ShapeDtypeStruct((b, 2 * SSD_HEADS, length), jnp.float32)],
        scratch_shapes=[pltpu.VMEM((tm + 2 * HALO, SSD_XBC), jnp.float32)],
        compiler_params=pltpu.CompilerParams(
            dimension_semantics=("parallel", "parallel"), vmem_limit_bytes=56 * 1024 * 1024),
        name="ssd_in",
    )(x, x, x, ng.reshape(1, d), sh, sc, w, wdt_p, wdt_t, conv_w, conv_b.reshape(1, SSD_XBC))


def _split3_bf16(v):
    p1 = v.astype(jnp.bfloat16)
    r1 = v - p1.astype(jnp.float32)
    p2 = r1.astype(jnp.bfloat16)
    p3 = (r1 - p2.astype(jnp.float32)).astype(jnp.bfloat16)
    return p1, p2, p3


def _softplus(v):
    return jnp.maximum(v, 0.0) + jnp.log1p(jnp.exp(-jnp.abs(v)))


def _expand_heads(cols, g, hoff):
    q = cols.shape[0]
    lane = lax.broadcasted_iota(jnp.int32, (q, LANES), 1)
    tiles = []
    for k in range(SSD_HEADS_PER_GROUP // 2):
        ha = hoff + g * SSD_HEADS_PER_GROUP + 2 * k
        tiles.append(jnp.take_along_axis(cols, jnp.where(lane < SSD_HEAD_DIM, ha, ha + 1), axis=1))
    return jnp.concatenate(tiles, axis=1)


def _ssd_scan_kernel(x_ref, b_ref, c_ref, dt_ref, dtt_ref, dtb_ref, dtbt_ref, a_ref, at_ref, init_ref, extra_ref,
                     y_ref, fin_ref, st_ref, *, reverse, hoff, add_prev):
    ci = pl.program_id(1)

    @pl.when(ci == 0)
    def _():
        st_ref[...] = init_ref[0]

    q = SSD_CHUNK
    f32, bf16 = jnp.float32, jnp.bfloat16
    dt = _softplus(dt_ref[0] + dtb_ref[...])
    dtt = _softplus(dtt_ref[0][hoff:hoff + SSD_HEADS, :] + dtbt_ref[...])
    ri = lax.broadcasted_iota(jnp.int32, (q, q), 0)
    cj = lax.broadcasted_iota(jnp.int32, (q, q), 1)
    keep = (cj >= ri) if reverse else (cj <= ri)
    tri = keep.astype(bf16)
    tri_t = ((ri >= cj) if reverse else (ri <= cj)).astype(bf16)
    acum = sum(jnp.dot(tri, p, preferred_element_type=f32) for p in _split3_bf16(dt * a_ref[...]))
    acum_t = sum(jnp.dot(p, tri_t, preferred_element_type=f32) for p in _split3_bf16(dtt * at_ref[...]))
    end = 0 if reverse else q - 1
    a_end = acum[end:end + 1, :]
    eacum = jnp.exp(acum)
    dt_dec_end = dt * jnp.exp(a_end - acum)
    lane = lax.broadcasted_iota(jnp.int32, (q, LANES), 1)
    for g in range(SSD_GROUPS):
        cg = c_ref[0][:, g * SSD_STATE:(g + 1) * SSD_STATE]
        bg = b_ref[0][:, g * SSD_STATE:(g + 1) * SSD_STATE]
        cg16 = cg.astype(bf16)
        cb = lax.dot_general(cg16, bg.astype(bf16), (((1,), (1,)), ((), ())), preferred_element_type=f32)
        xg = x_ref[0][:, g * SSD_GROUP_W:(g + 1) * SSD_GROUP_W]
        xg16 = xg.astype(bf16)
        eac_x = _expand_heads(eacum, g, hoff)
        yd = []
        for k in range(SSD_HEADS_PER_GROUP // 2):
            xp = xg16[:, k * LANES:(k + 1) * LANES]
            ys = []
            for hh in range(2):
                h = g * SSD_HEADS_PER_GROUP + 2 * k + hh
                seg = acum[:, hoff + h:hoff + h + 1] - acum_t[h:h + 1, :]
                lmat = jnp.exp(jnp.where(keep, seg, -jnp.inf)) * dtt[h:h + 1, :]
                ys.append(jnp.dot((cb * lmat).astype(bf16), xp, preferred_element_type=f32))
            yd.append(jnp.where(lane < SSD_HEAD_DIM, ys[0], ys[1]))
        st = st_ref[g]
        y_off = jnp.dot(cg16, st.astype(bf16), preferred_element_type=f32) * eac_x
        cols = slice(g * SSD_GROUP_W, (g + 1) * SSD_GROUP_W)
        if add_prev:
            other = extra_ref[0, :, cols].astype(f32)
        else:
            other = extra_ref[:, cols] * xg
        y_ref[0, :, cols] = (jnp.concatenate(yd, axis=1) + y_off + other).astype(bf16)
        xdd16 = (xg * _expand_heads(dt_dec_end, g, hoff)).astype(bf16)
        st_ref[g] = st * eac_x[end:end + 1, :] + jnp.dot(bg.T.astype(bf16), xdd16, preferred_element_type=f32)

    @pl.when(ci == pl.num_programs(1) - 1)
    def _():
        fin_ref[0] = st_ref[...]


def _ssd_scan_p(xbc, dt_raw, dt_raw_t, dt_bias, a, init, reverse, direction, y_prev=None, d_skip=None):
    b, length, _ = xbc.shape
    nc = length // SSD_CHUNK
    q = SSD_CHUNK
    cidx = (lambda j: nc - 1 - j) if reverse else (lambda j: j)
    nb = SSD_D_INNER // SSD_GN
    hoff = direction * SSD_HEADS
    full2 = lambda i, j: (0, 0)
    st_shape = (SSD_GROUPS, SSD_STATE, SSD_GROUP_W)
    lanes = lambda v: jnp.pad(v, (hoff, LANES - hoff - SSD_HEADS)).reshape(1, LANES)
    add_prev = y_prev is not None
    if add_prev:
        extra = y_prev
        extra_spec = pl.BlockSpec((1, q, SSD_D_INNER), lambda i, j: (i, cidx(j), 0))
    else:
        extra = jnp.repeat(d_skip, SSD_HEAD_DIM).reshape(1, SSD_D_INNER)
        extra_spec = pl.BlockSpec((1, SSD_D_INNER), full2)
    return pl.pallas_call(
        functools.partial(_ssd_scan_kernel, reverse=reverse, hoff=hoff, add_prev=add_prev),
        grid=(b, nc),
        in_specs=[pl.BlockSpec((1, q, SSD_D_INNER), lambda i, j: (i, cidx(j), 0)),
                  pl.BlockSpec((1, q, SSD_GN), lambda i, j: (i, cidx(j), nb)),
                  pl.BlockSpec((1, q, SSD_GN), lambda i, j: (i, cidx(j), nb + 1)),
                  pl.BlockSpec((1, q, LANES), lambda i, j: (i, cidx(j), 0)),
                  pl.BlockSpec((1, 2 * SSD_HEADS, q), lambda i, j: (i, 0, cidx(j))),
                  pl.BlockSpec((1, LANES), full2), pl.BlockSpec((SSD_HEADS, 1), full2),
                  pl.BlockSpec((1, LANES), full2), pl.BlockSpec((SSD_HEADS, 1), full2),
                  pl.BlockSpec((1,) + st_shape, lambda i, j: (i, 0, 0, 0)),
                  extra_spec],
        out_specs=[pl.BlockSpec((1, q, SSD_D_INNER), lambda i, j: (i, cidx(j), 0)),
                   pl.BlockSpec((1,) + st_shape, lambda i, j: (i, 0, 0, 0))],
        out_shape=[jax.ShapeDtypeStruct((b, length, SSD_D_INNER), jnp.bfloat16),
                   jax.ShapeDtypeStruct((b,) + st_shape, jnp.float32)],
        scratch_shapes=[pltpu.VMEM(st_shape, jnp.float32)],
        compiler_params=pltpu.CompilerParams(
            dimension_semantics=("parallel", "arbitrary"), vmem_limit_bytes=VMEM_LIMIT_BYTES),
        name="ssd_scan",
    )(xbc, xbc, xbc, dt_raw, dt_raw_t, lanes(dt_bias), dt_bias.reshape(-1, 1),
      lanes(a), a.reshape(-1, 1), init, extra)


def _ssd_out_kernel(y_ref, z_ref, ng_ref, w_ref, o_ref):
    z = z_ref[0].astype(jnp.float32)
    y = y_ref[0].astype(jnp.float32) * (z * jax.nn.sigmoid(z))
    ms = jnp.mean(y * y, axis=-1, keepdims=True)
    y = y * lax.rsqrt(ms + RMS_EPS) * ng_ref[...]
    o_ref[0] = jnp.dot(y.astype(jnp.bfloat16), w_ref[...], preferred_element_type=jnp.float32)


def _ssd_out(y, z, norm_g, out_w):
    b, length, di = y.shape
    d = out_w.shape[1]
    tm = min(length, 2 * ROW_TILE)
    row = lambda i, j: (i, j, 0)
    full2 = lambda i, j: (0, 0)
    return pl.pallas_call(
        _ssd_out_kernel,
        grid=(b, length // tm),
        in_specs=[pl.BlockSpec((1, tm, di), row), pl.BlockSpec((1, tm, di), row),
                  pl.BlockSpec((1, di), full2), pl.BlockSpec((di, d), full2)],
        out_specs=pl.BlockSpec((1, tm, d), row),
        out_shape=jax.ShapeDtypeStruct((b, length, d), jnp.float32),
        compiler_params=pltpu.CompilerParams(
            dimension_semantics=("parallel", "parallel"), vmem_limit_bytes=VMEM_LIMIT_BYTES),
        name="ssd_out",
    )(y, z, norm_g.reshape(1, di), out_w.astype(jnp.bfloat16))


def _state_to_kernel(s):
    b = s.shape[0]
    s = s.reshape(b, SSD_GROUPS, SSD_HEADS_PER_GROUP, SSD_HEAD_DIM, SSD_STATE)
    return jnp.transpose(s, (0, 1, 4, 2, 3)).reshape(b, SSD_GROUPS, SSD_STATE, SSD_GROUP_W)


def _state_from_kernel(s):
    b = s.shape[0]
    s = s.reshape(b, SSD_GROUPS, SSD_STATE, SSD_HEADS_PER_GROUP, SSD_HEAD_DIM)
    return jnp.transpose(s, (0, 1, 3, 4, 2)).reshape(b, SSD_HEADS, SSD_HEAD_DIM, SSD_STATE)


def _ssd_mixer_p(x, ng, sh, sc, init_f, init_b, in_w, conv_w, conv_b, dt_bias, a_log, d_skip, norm_g, out_w):
    z, xbc, dt_raw, dt_raw_t = _ssd_in(x, ng, sh, sc, in_w, conv_w, conv_b)
    a = -jnp.exp(a_log)
    yf, s_f = _ssd_scan_p(xbc, dt_raw, dt_raw_t, dt_bias[0], a[0], _state_to_kernel(init_f), False, 0,
                          d_skip=d_skip)
    y, s_b = _ssd_scan_p(xbc, dt_raw, dt_raw_t, dt_bias[1], a[1], _state_to_kernel(init_b), True, 1, y_prev=yf)
    m = _ssd_out(y, z, norm_g, out_w)
    return m, _state_from_kernel(s_f), _state_from_kernel(s_b)


HY_MAX_BLOCK = 512
HY_CC = 128
HY_MAC_ELEMS = 2048
HY_HIDDEN = 64
HY_FEAT_ROWS = 64


def _odd_dft_tables(n):
    m = np.arange(n, dtype=np.int64)[:, None]
    f = np.arange(n // 2, dtype=np.int64)[None, :]
    ang = 2.0 * np.pi * (((2 * f + 1) * m) % (2 * n)).astype(np.float64) / (2 * n)
    return np.cos(ang), np.sin(ang)


def _hy_in_kernel(xm_ref, xp_ref, xn_ref, ng_ref, sh_ref, sc_ref, w_ref, b_ref, cw_ref, cb_ref, o_ref, pad_ref):
    tm = xm_ref.shape[1]
    h, valid = _halo_rows(xm_ref, xp_ref, xn_ref, ng_ref, sh_ref, sc_ref)
    u = jnp.dot(h, w_ref[...], preferred_element_type=jnp.float32) + b_ref[...]
    pad_ref[...] = jnp.where(valid, u, 0.0)
    _conv_rows(pad_ref, cw_ref, cb_ref, o_ref, tm, silu=False)


def _hy_in(x, ng, sh, sc, in_w, in_b, short_w, short_b):
    b, length, d = x.shape
    n = in_w.shape[1]
    tm = min(length, ROW_TILE)
    taps = short_w.shape[0]
    per_b = lambda i, j: (i, 0, 0)
    full2 = lambda i, j: (0, 0)
    return pl.pallas_call(
        _hy_in_kernel,
        grid=(b, length // tm),
        in_specs=_halo_specs(length, tm, d) + [
            pl.BlockSpec((1, d), full2), pl.BlockSpec((1, 1, d), per_b), pl.BlockSpec((1, 1, d), per_b),
            pl.BlockSpec((d, n), full2), pl.BlockSpec((1, n), full2),
            pl.BlockSpec((taps, n), full2), pl.BlockSpec((1, n), full2)],
        out_specs=pl.BlockSpec((1, tm, n), lambda i, j: (i, j, 0)),
        out_shape=jax.ShapeDtypeStruct((b, length, n), jnp.float32),
        scratch_shapes=[pltpu.VMEM((tm + 2 * HALO, n), jnp.float32)],
        compiler_params=pltpu.CompilerParams(
            dimension_semantics=("parallel", "parallel"), vmem_limit_bytes=VMEM_LIMIT_BYTES),
        name="hy_in",
    )(x, x, x, ng.reshape(1, d), sh, sc, in_w.astype(jnp.bfloat16), in_b.reshape(1, n), short_w, short_b.reshape(1, n))


def _dot3(a, b):
    a_hi, a_lo = _split_bf16(a)
    b_hi, b_lo = _split_bf16(b)
    f32 = jnp.float32
    return (jnp.dot(a_hi, b_hi, preferred_element_type=f32) + jnp.dot(a_lo, b_hi, preferred_element_type=f32)
            + jnp.dot(a_hi, b_lo, preferred_element_type=f32))


def _hy_filter_kernel(w1t_ref, b1_ref, w2t_ref, b2_ref, w3t_ref, fr_ref, dl_ref, o_ref, *, length, blk):
    k = pl.program_id(0)
    q = (lax.broadcasted_iota(jnp.int32, (1, blk), 1) + k * blk)
    pos = jnp.abs(q - length).astype(jnp.float32)
    t = pos / float(length - 1)
    w = (2.0 * math.pi / length) * pos
    band = lax.broadcasted_iota(jnp.int32, (HY_BANDS, 1), 0).astype(jnp.float32)
    fb = 1e-4 + band * ((HY_BANDS - 1 - 1e-4) / (HY_BANDS - 1))
    z = jnp.concatenate([jnp.broadcast_to(t, (8, blk)), jnp.cos(fb * w), -jnp.sin(fb * w),
                         jnp.zeros((HY_FEAT_ROWS - 8 - 2 * HY_BANDS, blk), jnp.float32)], axis=0)
    h = jnp.sin(fr_ref[...] * (_dot3(w1t_ref[...], z) + b1_ref[...]))
    h = jnp.sin(fr_ref[...] * (_dot3(w2t_ref[...], h) + b2_ref[...]))
    kt = _dot3(w3t_ref[0], h)
    o_ref[0] = kt * jnp.exp(-t * dl_ref[...])


def _hy_filter(length, blk, f_w1, f_b1, f_w2, f_b2, f_w3, f_freq):
    d = f_w3.shape[1] // 2
    nk = 2 * length // blk
    w1t = jnp.concatenate([f_w1[0:1].T, jnp.zeros((HY_HIDDEN, 7), jnp.float32), f_w1[1:].T,
                           jnp.zeros((HY_HIDDEN, HY_FEAT_ROWS - 8 - 2 * HY_BANDS), jnp.float32)], axis=1)
    w3t = jnp.stack([f_w3[:, d:].T, f_w3[:, :d].T])
    deltas = jnp.abs(jnp.linspace(HY_MIN_DECAY, HY_MAX_DECAY, d, dtype=jnp.float32)).reshape(d, 1)
    col = lambda v: v.reshape(HY_HIDDEN, 1)
    full2 = lambda k: (0, 0)
    half = length // blk
    return pl.pallas_call(
        functools.partial(_hy_filter_kernel, length=length, blk=blk),
        grid=(nk,),
        in_specs=[pl.BlockSpec((HY_HIDDEN, HY_FEAT_ROWS), full2), pl.BlockSpec((HY_HIDDEN, 1), full2),
                  pl.BlockSpec((HY_HIDDEN, HY_HIDDEN), full2), pl.BlockSpec((HY_HIDDEN, 1), full2),
                  pl.BlockSpec((1, d, HY_HIDDEN), lambda k: (k // half, 0, 0)),
                  pl.BlockSpec((HY_HIDDEN, 1), full2), pl.BlockSpec((d, 1), full2)],
        out_specs=pl.BlockSpec((1, d, blk), lambda k: (k, 0, 0)),
        out_shape=jax.ShapeDtypeStruct((nk, d, blk), jnp.float32),
        compiler_params=pltpu.CompilerParams(
            dimension_semantics=("parallel",), vmem_limit_bytes=VMEM_LIMIT_BYTES),
        name="hy_filter",
    )(w1t, col(f_b1), f_w2.T, col(f_b2), w3t, col(f_freq), deltas)


def _hy_gspec_kernel(hi_ref, lo_ref, ft_ref, fb_ref, o_ref):
    o_ref[0] = _dot3(hi_ref[0], ft_ref[...]) + _dot3(lo_ref[0], fb_ref[...])


def _hy_gspec(kt):
    nk, d, blk = kt.shape
    cos, sin = _odd_dft_tables(2 * blk)
    top = np.concatenate([cos[:blk], -sin[:blk]], axis=1)
    bot = -np.concatenate([cos[blk:], -sin[blk:]], axis=1)
    bot[0] = 0.0
    tm = 512
    full2 = lambda e, i: (0, 0)
    return pl.pallas_call(
        _hy_gspec_kernel,
        grid=(nk - 1, d // tm),
        in_specs=[pl.BlockSpec((1, tm, blk), lambda e, i: (e + 1, i, 0)),
                  pl.BlockSpec((1, tm, blk), lambda e, i: (e, i, 0)),
                  pl.BlockSpec((blk, 2 * blk), full2), pl.BlockSpec((blk, 2 * blk), full2)],
        out_specs=pl.BlockSpec((1, tm, 2 * blk), lambda e, i: (e, i, 0)),
        out_shape=jax.ShapeDtypeStruct((nk - 1, d, 2 * blk), jnp.float32),
        compiler_params=pltpu.CompilerParams(
            dimension_semantics=("parallel", "parallel"), vmem_limit_bytes=VMEM_LIMIT_BYTES),
        name="hy_gspec",
    )(kt, kt, jnp.asarray(top, jnp.float32), jnp.asarray(bot, jnp.float32))


def _hy_conv_kernel(x0_ref, x1_ref, v_ref, g_ref, fb_ref, ff_ref, fi_ref, o_ref, lhs_ref, u_ref, y_ref, *, nb):
    cc, bsz = HY_CC, ff_ref.shape[0]
    mrows = BF16_TILE_ROWS
    mlanes = HY_MAC_ELEMS // mrows
    for j in range(nb):
        sl = slice(j * bsz, (j + 1) * bsz)
        wj = v_ref[0, sl, :] * x1_ref[0, sl, :]
        lhs_ref[j * cc:(j + 1) * cc, :] = wj.T.astype(jnp.bfloat16)
    u_ref[...] = jnp.dot(lhs_ref[...], ff_ref[...], preferred_element_type=jnp.float32)

    def per_out_block(i, carry):
        def per_rows(rc, carry2):
            rows = pl.ds(pl.multiple_of(rc * mrows, mrows), mrows)
            yrows = pl.ds(pl.multiple_of(i * cc + rc * mrows, mrows), mrows)
            for l0 in range(0, bsz, mlanes):
                re = slice(l0, l0 + mlanes)
                im = slice(bsz + l0, bsz + l0 + mlanes)
                acc_r = jnp.zeros((mrows, mlanes), jnp.float32)
                acc_i = jnp.zeros((mrows, mlanes), jnp.float32)
                for j in range(nb):
                    e = i - j + (nb - 1)
                    gr = g_ref[e, rows, re]
                    gi = g_ref[e, rows, im]
                    urows = pl.ds(pl.multiple_of(j * cc + rc * mrows, mrows), mrows)
                    ur = u_ref[urows, re]
                    ui = u_ref[urows, im]
                    acc_r = acc_r + gr * ur - gi * ui
                    acc_i = acc_i + gr * ui + gi * ur
                y_ref[yrows, re] = acc_r.astype(jnp.bfloat16)
                y_ref[yrows, im] = acc_i.astype(jnp.bfloat16)
            return carry2
        return lax.fori_loop(0, cc // mrows, per_rows, carry)
    lax.fori_loop(0, nb, per_out_block, 0)

    yt = jnp.dot(y_ref[...], fi_ref[...], preferred_element_type=jnp.float32)
    for i in range(nb):
        sl = slice(i * bsz, (i + 1) * bsz)
        w = v_ref[0, sl, :] * x1_ref[0, sl, :]
        gated = (yt[i * cc:(i + 1) * cc, :].T + fb_ref[...] * w) * x0_ref[0, sl, :]
        o_ref[0, sl, :] = gated.astype(o_ref.dtype)


def _hy_conv(u, g, f_bias, blk):
    b, length, d3 = u.shape
    d = d3 // 3
    nb = length // blk
    ncb = d // HY_CC
    cos, sin = _odd_dft_tables(2 * blk)
    fwd = np.concatenate([cos[:blk], -sin[:blk]], axis=1)
    inv = (1.0 / blk) * np.concatenate([cos[:blk].T, -sin[:blk].T], axis=0)
    col = lambda off: pl.BlockSpec((1, length, HY_CC), lambda c, i, off=off: (i, 0, off + c))
    full2 = lambda c, i: (0, 0)
    return pl.pallas_call(
        functools.partial(_hy_conv_kernel, nb=nb),
        grid=(ncb, b),
        in_specs=[col(0), col(ncb), col(2 * ncb),
                  pl.BlockSpec((2 * nb - 1, HY_CC, 2 * blk), lambda c, i: (0, c, 0)),
                  pl.BlockSpec((1, HY_CC), lambda c, i: (0, c)),
                  pl.BlockSpec((blk, 2 * blk), full2), pl.BlockSpec((2 * blk, blk), full2)],
        out_specs=pl.BlockSpec((1, length, HY_CC), lambda c, i: (i, 0, c)),
        out_shape=jax.ShapeDtypeStruct((b, length, d), jnp.bfloat16),
        scratch_shapes=[pltpu.VMEM((nb * HY_CC, blk), jnp.bfloat16),
                        pltpu.VMEM((nb * HY_CC, 2 * blk), jnp.float32),
                        pltpu.VMEM((nb * HY_CC, 2 * blk), jnp.bfloat16)],
        compiler_params=pltpu.CompilerParams(
            dimension_semantics=("parallel", "arbitrary"), vmem_limit_bytes=56 * 1024 * 1024),
        name="hy_conv",
    )(u, u, u, g, f_bias.reshape(1, d), jnp.asarray(fwd, jnp.bfloat16), jnp.asarray(inv, jnp.bfloat16))


def _hyena_mixer_p(x, ng, sh, sc, in_w, in_b, short_w, short_b, f_w1, f_b1, f_w2, f_b2, f_w3, f_freq, f_bias, out_w):
    length = x.shape[1]
    blk = min(HY_MAX_BLOCK, length)
    u = _hy_in(x, ng, sh, sc, in_w, in_b, short_w, short_b)
    g = _hy_gspec(_hy_filter(length, blk, f_w1, f_b1, f_w2, f_b2, f_w3, f_freq))
    return _mm3(_hy_conv(u, g, f_bias, blk), out_w)


TOK_TILE = 256
MOE_ROWS = 512
SEG_CHUNK = 64
BF16_TILE_ROWS = 16
LANES = 128


def _split_bf16(w):
    hi = w.astype(jnp.bfloat16)
    lo = (w - hi.astype(jnp.float32)).astype(jnp.bfloat16)
    return hi, lo


def _moe_pre_kernel(x_ref, m_ref, g1_ref, ng_ref, sh_ref, sc_ref, wrh_ref, wrl_ref,
                    xo_ref, hpk_ref, lg_ref):
    x = x_ref[0] + g1_ref[0] * m_ref[0]
    xo_ref[0] = x
    ms = jnp.mean(x * x, axis=-1, keepdims=True)
    h = x * lax.rsqrt(ms + RMS_EPS) * ng_ref[...]
    h = h * (1.0 + sc_ref[0]) + sh_ref[0]
    h_hi = h.astype(jnp.bfloat16)
    h_lo = (h - h_hi.astype(jnp.float32)).astype(jnp.bfloat16)
    dn = (((1,), (1,)), ((), ()))
    lg = lax.dot_general(wrh_ref[...], h_hi, dn, preferred_element_type=jnp.float32)
    lg += lax.dot_general(wrh_ref[...], h_lo, dn, preferred_element_type=jnp.float32)
    lg += lax.dot_general(wrl_ref[...], h_hi, dn, preferred_element_type=jnp.float32)
    lg_ref[0] = lg
    half = h.shape[1] // 2
    wa = pltpu.bitcast(h_hi[:, :half].astype(jnp.float32), jnp.uint32) >> 16
    wb = pltpu.bitcast(h_hi[:, half:].astype(jnp.float32), jnp.uint32) & jnp.uint32(0xFFFF0000)
    hpk_ref[0] = wa | wb


def _moe_pre(x, m, g1, ng, sh, sc, w_router):
    b, length, d = x.shape
    tm = min(length, 512)
    wrh, wrl = _split_bf16(w_router.T)
    row = lambda i, j: (i, j, 0)
    per_b = lambda i, j: (i, 0, 0)
    full2 = lambda i, j: (0, 0)
    return pl.pallas_call(
        _moe_pre_kernel,
        grid=(b, length // tm),
        in_specs=[pl.BlockSpec((1, tm, d), row), pl.BlockSpec((1, tm, d), row),
                  pl.BlockSpec((1, 1, d), per_b), pl.BlockSpec((1, d), full2),
                  pl.BlockSpec((1, 1, d), per_b), pl.BlockSpec((1, 1, d), per_b),
                  pl.BlockSpec((N_EXPERTS, d), full2), pl.BlockSpec((N_EXPERTS, d), full2)],
        out_specs=[pl.BlockSpec((1, tm, d), row), pl.BlockSpec((1, tm, d // 2), row),
                   pl.BlockSpec((1, N_EXPERTS, tm), lambda i, j: (i, 0, j))],
        out_shape=[jax.ShapeDtypeStruct((b, length, d), jnp.float32),
                   jax.ShapeDtypeStruct((b, length, d // 2), jnp.uint32),
                   jax.ShapeDtypeStruct((b, N_EXPERTS, length), jnp.float32)],
        compiler_params=pltpu.CompilerParams(
            dimension_semantics=("parallel", "parallel"), vmem_limit_bytes=VMEM_LIMIT_BYTES),
        name="moe_pre",
    )(x, m, g1, ng.reshape(1, d), sh, sc, wrh, wrl)


def _moe_ffn_kernel(idx_ref, nidx_ref, h_hbm, gate_ref, wg_ref, wu_ref, wd_ref, y_ref,
                    xe_ref, wgb, wub, wdb, sem):
    nblk = pl.num_programs(1)
    step = pl.program_id(0) * nblk + pl.program_id(1)
    last = pl.num_programs(0) * nblk - 1

    def issue(ids_ref, slot):
        base = slot * MOE_ROWS
        for c in range(MOE_ROWS):
            pltpu.make_async_copy(h_hbm.at[pl.ds(ids_ref[0, 0, c], 1)], xe_ref.at[pl.ds(base + c, 1)],
                                  sem.at[slot]).start()

    @pl.when(step == 0)
    def _():
        issue(idx_ref, 0)

    for parity in range(2):
        @pl.when((step < last) & (step % 2 == parity))
        def _(parity=parity):
            issue(nidx_ref, 1 - parity)

    @pl.when(pl.program_id(1) == 0)
    def _():
        wgb[...] = wg_ref[0, 0].astype(jnp.bfloat16)
        wub[...] = wu_ref[0, 0].astype(jnp.bfloat16)
        wdb[...] = wd_ref[0, 0].astype(jnp.bfloat16)

    slot = step % 2
    rows = pl.ds(pl.multiple_of(slot * MOE_ROWS, MOE_ROWS), MOE_ROWS)
    pltpu.make_async_copy(h_hbm.at[pl.ds(0, MOE_ROWS)], xe_ref.at[rows], sem.at[slot]).wait()
    half = wgb.shape[0] // 2
    w = xe_ref[rows, :]
    xa = pltpu.bitcast(w << 16, jnp.float32).astype(jnp.bfloat16)
    xb = pltpu.bitcast(w & jnp.uint32(0xFFFF0000), jnp.float32).astype(jnp.bfloat16)
    hg = jnp.dot(xa, wgb[:half], preferred_element_type=jnp.float32)
    hg += jnp.dot(xb, wgb[half:], preferred_element_type=jnp.float32)
    hu = jnp.dot(xa, wub[:half], preferred_element_type=jnp.float32)
    hu += jnp.dot(xb, wub[half:], preferred_element_type=jnp.float32)
    hid = (hg * jax.nn.sigmoid(hg) * hu).astype(jnp.bfloat16)
    y = jnp.dot(hid, wdb[...], preferred_element_type=jnp.float32)
    y_ref[0] = (y * gate_ref[0]).astype(jnp.bfloat16)


def _moe_ffn(hpk, grow, gate, w_gate, w_up, w_down, layer):
    e, r = grow.shape
    d, f = w_gate.shape[2], w_gate.shape[3]
    nblk = r // MOE_ROWS
    nsteps = e * nblk
    wspec = lambda shp: pl.BlockSpec((1, 1) + shp, lambda i, j: (layer, i, 0, 0))
    ids = grow.reshape(nsteps, 1, MOE_ROWS)
    smem_ids = lambda off: pl.BlockSpec(
        (1, 1, MOE_ROWS), lambda i, j: (jnp.minimum(i * nblk + j + off, nsteps - 1), 0, 0), memory_space=pltpu.SMEM)
    return pl.pallas_call(
        _moe_ffn_kernel,
        grid=(e, nblk),
        in_specs=[smem_ids(0), smem_ids(1),
                  pl.BlockSpec(memory_space=pltpu.HBM),
                  pl.BlockSpec((1, MOE_ROWS, 1), lambda i, j: (i, j, 0)),
                  wspec((d, f)), wspec((d, f)), wspec((f, d))],
        out_specs=pl.BlockSpec((1, MOE_ROWS, d), lambda i, j: (i, j, 0)),
        out_shape=jax.ShapeDtypeStruct((e, r, d), jnp.bfloat16),
        scratch_shapes=[pltpu.VMEM((2 * MOE_ROWS, d // 2), jnp.uint32),
                        pltpu.VMEM((d, f), jnp.bfloat16), pltpu.VMEM((d, f), jnp.bfloat16),
                        pltpu.VMEM((f, d), jnp.bfloat16),
                        pltpu.SemaphoreType.DMA((2,))],
        compiler_params=pltpu.CompilerParams(
            dimension_semantics=("arbitrary", "arbitrary"), vmem_limit_bytes=VMEM_LIMIT_BYTES),
        name="moe_ffn",
    )(ids, ids, hpk, gate, w_gate, w_up, w_down)


def _moe_comb_kernel(cs_ref, x_ref, g2_ref, y_ref, idx_ref, fg_ref, o_ref, ycat, acc, *, cap, ch, ntile, final_norm):
    b = pl.program_id(0)
    t = pl.program_id(1)
    base = t * TOK_TILE
    sub = lax.broadcasted_iota(jnp.int32, (TOK_TILE, LANES), 0) + base
    if ntile == 1:
        for e in range(N_EXPERTS):
            ycat[e * ch:(e + 1) * ch, :] = y_ref[e, 0:ch, :]
        v = idx_ref[0]
        tiles = [(v[:, p * LANES:(p + 1) * LANES] == sub).astype(jnp.bfloat16)
                 for p in range(N_EXPERTS * ch // LANES)]
        acc[...] = jnp.dot(jnp.concatenate(tiles, axis=1), ycat[...], preferred_element_type=jnp.float32)
    else:
        lane = lax.broadcasted_iota(jnp.int32, (1, LANES), 1)
        per = LANES // ch
        sts = []
        for e in range(N_EXPERTS):
            s0 = cs_ref[(b * N_EXPERTS + e) * (ntile + 1) + t]
            st = jnp.minimum((s0 // BF16_TILE_ROWS) * BF16_TILE_ROWS, cap - ch)
            st = pl.multiple_of(st, BF16_TILE_ROWS)
            sts.append(st)
            ycat[e * ch:(e + 1) * ch, :] = y_ref[e, pl.ds(st, ch), :]
        tiles = []
        for p in range(N_EXPERTS // per):
            v = None
            for q in range(per):
                e = p * per + q
                r = pltpu.roll(idx_ref[0, e:e + 1, :], (2 * cap - sts[e] + q * ch) % cap, 1)[:, :LANES]
                v = r if v is None else jnp.where(lane >= q * ch, r, v)
            tiles.append((v == sub).astype(jnp.bfloat16))
        acc[...] = jnp.dot(jnp.concatenate(tiles, axis=1), ycat[...], preferred_element_type=jnp.float32)
        sub_c = lax.broadcasted_iota(jnp.int32, (TOK_TILE, ch), 0) + base
        lane_c = lax.broadcasted_iota(jnp.int32, (1, ch), 1)
        for e in range(N_EXPERTS):
            s1 = cs_ref[(b * N_EXPERTS + e) * (ntile + 1) + t + 1]
            first_end = sts[e] + ch
            n_extra = jnp.maximum(s1 - first_end + ch - 1, 0) // ch

            def extra(q, carry, e=e, first_end=first_end):
                lo = first_end + q * ch
                stq = pl.multiple_of(jnp.minimum(lo, cap - ch), BF16_TILE_ROWS)
                r = pltpu.roll(idx_ref[0, e:e + 1, :], (2 * cap - stq) % cap, 1)[:, :ch]
                hit = (r == sub_c) & (lane_c + stq >= lo)
                acc[...] += jnp.dot(hit.astype(jnp.bfloat16), y_ref[e, pl.ds(stq, ch), :],
                                    preferred_element_type=jnp.float32)
                return carry
            lax.fori_loop(0, n_extra, extra, 0)
    out = x_ref[0] + g2_ref[0] * acc[...]
    if final_norm:
        ms = jnp.mean(out * out, axis=-1, keepdims=True)
        out = out * lax.rsqrt(ms + RMS_EPS) * fg_ref[...]
    o_ref[0] = out


def _moe_combine(x, g2, y, idx, cs, final_g=None):
    b, length, d = x.shape
    final_norm = final_g is not None
    fg = (final_g if final_norm else jnp.ones((d,), jnp.float32)).reshape(1, d)
    cap = idx.shape[2]
    ntile = length // TOK_TILE
    ch = min(SEG_CHUNK, cap)
    if ntile == 1:
        idx_in = idx.reshape(b, 1, N_EXPERTS * cap)
        idx_spec = pl.BlockSpec((1, 1, N_EXPERTS * cap), lambda i, j, c: (i, 0, 0))
    else:
        idx_in = idx
        idx_spec = pl.BlockSpec((1, N_EXPERTS, cap), lambda i, j, c: (i, 0, 0))
    grid_spec = pltpu.PrefetchScalarGridSpec(
        num_scalar_prefetch=1,
        grid=(b, ntile),
        in_specs=[pl.BlockSpec((1, TOK_TILE, d), lambda i, j, c: (i, j, 0)),
                  pl.BlockSpec((1, 1, d), lambda i, j, c: (i, 0, 0)),
                  pl.BlockSpec((N_EXPERTS, cap, d), lambda i, j, c: (0, i, 0)),
                  idx_spec,
                  pl.BlockSpec((1, d), lambda i, j, c: (0, 0))],
        out_specs=pl.BlockSpec((1, TOK_TILE, d), lambda i, j, c: (i, j, 0)),
        scratch_shapes=[pltpu.VMEM((N_EXPERTS * ch, d), jnp.bfloat16),
                        pltpu.VMEM((TOK_TILE, d), jnp.float32)])
    return pl.pallas_call(
        functools.partial(_moe_comb_kernel, cap=cap, ch=ch, ntile=ntile, final_norm=final_norm),
        grid_spec=grid_spec,
        out_shape=jax.ShapeDtypeStruct((b, length, d), jnp.float32),
        compiler_params=pltpu.CompilerParams(
            dimension_semantics=("arbitrary", "arbitrary"), vmem_limit_bytes=56 * 1024 * 1024),
        name="moe_combine",
    )(cs.reshape(-1).astype(jnp.int32), x, g2, y, idx_in, fg)


def _moe_block(x, m, g1, ng, sh, sc, g2, w_router, w_gate, w_up, w_down, layer, final_g=None):
    b, length, d = x.shape
    cap = EC_FACTOR * length // N_EXPERTS
    x1, hpk, lg = _moe_pre(x, m, g1, ng, sh, sc, w_router)
    aff = jax.nn.softmax(lg, axis=1)
    _, idx = lax.top_k(aff, cap)
    idx = jnp.sort(idx, axis=-1)
    gate = jnp.take_along_axis(aff, idx, axis=-1)
    ntile = length // TOK_TILE
    bounds = jnp.arange(ntile + 1, dtype=jnp.int32) * TOK_TILE
    cs = jnp.sum(idx[:, :, :, None] < bounds, axis=2, dtype=jnp.int32)
    grow = idx + (jnp.arange(b, dtype=jnp.int32) * length)[:, None, None]
    grow = jnp.swapaxes(grow, 0, 1).reshape(N_EXPERTS, b * cap)
    gate_e = jnp.swapaxes(gate, 0, 1).reshape(N_EXPERTS, b * cap, 1)
    y = _moe_ffn(hpk.reshape(b * length, d // 2), grow, gate_e, w_gate, w_up, w_down, layer)
    return _moe_combine(x1, g2, y, idx, cs, final_g)


def kernel(x_prompt, x_sample, state_ssd, c, c_ctx, norm_g, ada_w, ada_b, hy_in_w, hy_in_b, hy_short_w, hy_short_b, hy_f_w1, hy_f_b1, hy_f_w2, hy_f_b2, hy_f_w3, hy_f_freq, hy_f_bias, hy_out_w, ssd_in_w, ssd_conv_w, ssd_conv_b, ssd_dt_bias, ssd_A_log, ssd_D, ssd_norm_g, ssd_out_w, moe_router, moe_w_gate, moe_w_up, moe_w_down, final_norm_g):
    rows = x_sample.shape[1] // GRID_W
    xp = x_prompt
    xs = x_sample + _sincos_2d(rows, GRID_W, D_MODEL)[None]
    new_ssd = []
    for i in range(DEPTH):
        nb_s = c.shape[0]
        mods = _adaln(jnp.concatenate([c, c_ctx[None, :]], axis=0), ada_w[i], ada_b[i])
        sh1s, sc1s, g1s, sh2s, sc2s, g2s = [m[:nb_s] for m in mods]
        sh1p, sc1p, g1p, sh2p, sc2p, g2p = [m[nb_s:] for m in mods]
        j = i // N_MIXERS
        bp = (xp.shape[0], 1, D_MODEL)
        if i % N_MIXERS == 0:
            hy = (hy_in_w[j], hy_in_b[j], hy_short_w[j], hy_short_b[j], hy_f_w1[j], hy_f_b1[j],
                  hy_f_w2[j], hy_f_b2[j], hy_f_w3[j], hy_f_freq[j], hy_f_bias[j], hy_out_w[j])
            mp = _hyena_mixer_p(xp, norm_g[i, 0], jnp.broadcast_to(sh1p, bp), jnp.broadcast_to(sc1p, bp), *hy)
            ms = _hyena_mixer_p(xs, norm_g[i, 0], sh1s, sc1s, *hy)
        else:
            sp = (ssd_in_w[j], ssd_conv_w[j], ssd_conv_b[j], ssd_dt_bias[j], ssd_A_log[j],
                  ssd_D[j], ssd_norm_g[j], ssd_out_w[j])
            zeros = jnp.zeros((xp.shape[0], SSD_HEADS, SSD_HEAD_DIM, SSD_STATE), jnp.float32)
            mp, s_f, s_b = _ssd_mixer_p(xp, norm_g[i, 0], jnp.broadcast_to(sh1p, bp),
                                        jnp.broadcast_to(sc1p, bp), zeros, zeros, *sp)
            new_ssd.append(jnp.stack([s_f, s_b], axis=1))
            ms, _, _ = _ssd_mixer_p(xs, norm_g[i, 0], sh1s, sc1s, state_ssd[:, j, 0], state_ssd[:, j, 1], *sp)
        moe = (moe_router[i], moe_w_gate, moe_w_up, moe_w_down, i, final_norm_g if i == DEPTH - 1 else None)
        xp = _moe_block(xp, mp, jnp.broadcast_to(g1p, bp), norm_g[i, 1], jnp.broadcast_to(sh2p, bp),
                        jnp.broadcast_to(sc2p, bp), jnp.broadcast_to(g2p, bp), *moe)
        xs = _moe_block(xs, ms, g1s, norm_g[i, 1], sh2s, sc2s, g2s, *moe)
    new_state_ssd = jnp.stack(new_ssd, axis=1)
    return (xp, xs, new_state_ssd)
```

```python
import functools
import math

import jax
import jax.numpy as jnp
import numpy as np
from jax import lax
from jax.experimental import pallas as pl
from jax.experimental.pallas import tpu as pltpu

D_MODEL = 1024
DEPTH = 2
GRID_W = 64
N_MIXERS = 2
RMS_EPS = 1e-6
HY_EMB = 33
HY_BANDS = (HY_EMB - 1) // 2
HY_SHORT_DECAY_FRAC = 0.3
HY_LONG_DECAY_FRAC = 1.5
HY_DECAY_TARGET = 1e-2
HY_MAX_DECAY = math.log(HY_DECAY_TARGET) / HY_SHORT_DECAY_FRAC
HY_MIN_DECAY = math.log(HY_DECAY_TARGET) / HY_LONG_DECAY_FRAC
SSD_D_INNER = 2 * D_MODEL
SSD_HEAD_DIM = 64
SSD_HEADS = SSD_D_INNER // SSD_HEAD_DIM
SSD_GROUPS = 4
SSD_STATE = 128
SSD_CHUNK = 128
SSD_XBC = SSD_D_INNER + 2 * SSD_GROUPS * SSD_STATE
N_EXPERTS = 16
EC_FACTOR = 2

VMEM_LIMIT_BYTES = 48 * 1024 * 1024


def _mm_kernel(a_ref, b_ref, o_ref, acc_ref):
    @pl.when(pl.program_id(2) == 0)
    def _():
        acc_ref[...] = jnp.zeros_like(acc_ref)

    acc_ref[...] += jnp.dot(a_ref[...].astype(jnp.bfloat16), b_ref[...],
                            preferred_element_type=jnp.float32)

    @pl.when(pl.program_id(2) == pl.num_programs(2) - 1)
    def _():
        o_ref[...] = acc_ref[...]


def _pick(n, pref):
    for t in pref:
        if n % t == 0:
            return t
    return n


def _mm(a, b):
    m, k = a.shape
    n = b.shape[1]
    mp = -(-m // 8) * 8
    if mp != m:
        a = jnp.pad(a, ((0, mp - m), (0, 0)))
    tm = _pick(mp, (512, 256, 128, 64, 32, 16, 8))
    tn = _pick(n, (512, 256, 128))
    tk = _pick(k, (1024, 512, 256, 128))
    out = pl.pallas_call(
        _mm_kernel,
        grid=(mp // tm, n // tn, k // tk),
        in_specs=[pl.BlockSpec((tm, tk), lambda i, j, l: (i, l)),
                  pl.BlockSpec((tk, tn), lambda i, j, l: (l, j))],
        out_specs=pl.BlockSpec((tm, tn), lambda i, j, l: (i, j)),
        out_shape=jax.ShapeDtypeStruct((mp, n), jnp.float32),
        scratch_shapes=[pltpu.VMEM((tm, tn), jnp.float32)],
        compiler_params=pltpu.CompilerParams(
            dimension_semantics=("parallel", "parallel", "arbitrary"),
            vmem_limit_bytes=VMEM_LIMIT_BYTES),
        name="mm",
    )(a, b.astype(jnp.bfloat16))
    return out[:m]


def _mm3(a, b):
    lead = a.shape[:-1]
    return _mm(a.reshape(-1, a.shape[-1]), b).reshape(*lead, b.shape[1])


def _adaln(cond, ada_w, ada_b):
    m = _mm(jax.nn.silu(cond), ada_w) + ada_b
    return jnp.split(m[:, None, :], 6, axis=-1)


def _sincos_2d(rows, cols, d):
    q = d // 4
    omega = 1.0 / (10000.0 ** (jnp.arange(q, dtype=jnp.float32) / q))
    t = jnp.arange(rows * cols)
    er = (t // cols).astype(jnp.float32)[:, None] * omega[None, :]
    ec = (t % cols).astype(jnp.float32)[:, None] * omega[None, :]
    return jnp.concatenate([jnp.sin(er), jnp.cos(er), jnp.sin(ec), jnp.cos(ec)], axis=-1)


SSD_GN = SSD_GROUPS * SSD_STATE
SSD_GROUP_W = SSD_D_INNER // SSD_GROUPS
SSD_HEADS_PER_GROUP = SSD_HEADS // SSD_GROUPS
ROW_TILE = 256


def _modnorm(x, ng, sh, sc):
    ms = jnp.mean(x * x, axis=-1, keepdims=True)
    return (x * lax.rsqrt(ms + RMS_EPS) * ng) * (1.0 + sc) + sh


HALO = 8
CONV_COLS = 512


def _halo_rows(xm_ref, xp_ref, xn_ref, ng_ref, sh_ref, sc_ref):
    j = pl.program_id(1)
    xa = jnp.concatenate([xp_ref[0], xm_ref[0], xn_ref[0]], axis=0)
    h = _modnorm(xa, ng_ref[...], sh_ref[0], sc_ref[0]).astype(jnp.bfloat16)
    tm = xm_ref.shape[1]
    r = lax.broadcasted_iota(jnp.int32, (tm + 2 * HALO, 1), 0)
    valid = ((r >= HALO) | (j > 0)) & ((r < tm + HALO) | (j < pl.num_programs(1) - 1))
    return h, valid


def _conv_rows(pad_ref, cw_ref, cb_ref, o_ref, tm, silu):
    taps = cw_ref.shape[0]
    ncol = pad_ref.shape[1]
    for c0 in range(0, ncol, CONV_COLS):
        cols = slice(c0, c0 + CONV_COLS)
        acc = cb_ref[:, cols] + jnp.zeros((tm, CONV_COLS), jnp.float32)
        for k in range(taps):
            off = HALO + k - taps // 2
            acc = acc + cw_ref[k:k + 1, cols] * pad_ref[off:off + tm, cols]
        if silu:
            acc = acc * jax.nn.sigmoid(acc)
        o_ref[0, :, cols] = acc


def _halo_specs(length, tm, d):
    nh = length // HALO
    per = tm // HALO
    main = pl.BlockSpec((1, tm, d), lambda i, j: (i, j, 0))
    prev = pl.BlockSpec((1, HALO, d), lambda i, j: (i, jnp.maximum(j * per - 1, 0), 0))
    nxt = pl.BlockSpec((1, HALO, d), lambda i, j: (i, jnp.minimum((j + 1) * per, nh - 1), 0))
    return [main, prev, nxt]


def _ssd_in_kernel(xm_ref, xp_ref, xn_ref, ng_ref, sh_ref, sc_ref, w_ref, wdt_ref, wdtt_ref, cw_ref, cb_ref,
                   z_ref, xbc_ref, dt_ref, dtt_ref, pad_ref):
    tm = xm_ref.shape[1]
    h, valid = _halo_rows(xm_ref, xp_ref, xn_ref, ng_ref, sh_ref, sc_ref)
    zx = jnp.dot(h, w_ref[...], preferred_element_type=jnp.float32)
    z_ref[0] = zx[HALO:HALO + tm, :SSD_D_INNER].astype(jnp.bfloat16)
    pad_ref[...] = jnp.where(valid, zx[:, SSD_D_INNER:], 0.0)
    _conv_rows(pad_ref, cw_ref, cb_ref, xbc_ref, tm, silu=True)
    hm = h[HALO:HALO + tm]
    dt_ref[0] = jnp.dot(hm, wdt_ref[...], preferred_element_type=jnp.float32)
    dtt_ref[0] = lax.dot_general(wdtt_ref[...], hm, (((1,), (1,)), ((), ())),
                                 preferred_element_type=jnp.float32)


def _ssd_in(x, ng, sh, sc, in_w, conv_w, conv_b):
    b, length, d = x.shape
    tm = min(length, ROW_TILE)
    nzx = SSD_D_INNER + SSD_XBC
    w = in_w[:, :nzx].astype(jnp.bfloat16)
    wdt = in_w[:, nzx:]
    wdt_p = jnp.pad(wdt, ((0, 0), (0, LANES - 2 * SSD_HEADS))).astype(jnp.bfloat16)
    wdt_t = wdt.T.astype(jnp.bfloat16)
    taps = conv_w.shape[0]
    row = lambda i, j: (i, j, 0)
    per_b = lambda i, j: (i, 0, 0)
    full2 = lambda i, j: (0, 0)
    return pl.pallas_call(
        _ssd_in_kernel,
        grid=(b, length // tm),
        in_specs=_halo_specs(length, tm, d) + [
            pl.BlockSpec((1, d), full2), pl.BlockSpec((1, 1, d), per_b), pl.BlockSpec((1, 1, d), per_b),
            pl.BlockSpec((d, nzx), full2), pl.BlockSpec((d, LANES), full2),
            pl.BlockSpec((2 * SSD_HEADS, d), full2),
            pl.BlockSpec((taps, SSD_XBC), full2), pl.BlockSpec((1, SSD_XBC), full2)],
        out_specs=[pl.BlockSpec((1, tm, SSD_D_INNER), row), pl.BlockSpec((1, tm, SSD_XBC), row),
                   pl.BlockSpec((1, tm, LANES), row),
                   pl.BlockSpec((1, 2 * SSD_HEADS, tm), lambda i, j: (i, 0, j))],
        out_shape=[jax.ShapeDtypeStruct((b, length, SSD_D_INNER), jnp.bfloat16),
                   jax.ShapeDtypeStruct((b, length, SSD_XBC), jnp.float32),
                   jax.ShapeDtypeStruct((b, length, LANES), jnp.float32),
                   jax.ShapeDtypeStruct((b, 2 * SSD_HEADS, length), jnp.float32)],
        scratch_shapes=[pltpu.VMEM((tm + 2 * HALO, SSD_XBC), jnp.float32)],
        compiler_params=pltpu.CompilerParams(
            dimension_semantics=("parallel", "parallel"), vmem_limit_bytes=56 * 1024 * 1024),
        name="ssd_in",
    )(x, x, x, ng.reshape(1, d), sh, sc, w, wdt_p, wdt_t, conv_w, conv_b.reshape(1, SSD_XBC))


def _split3_bf16(v):
    p1 = v.astype(jnp.bfloat16)
    r1 = v - p1.astype(jnp.float32)
    p2 = r1.astype(jnp.bfloat16)
    p3 = (r1 - p2.astype(jnp.float32)).astype(jnp.bfloat16)
    return p1, p2, p3


def _softplus(v):
    return jnp.maximum(v, 0.0) + jnp.log1p(jnp.exp(-jnp.abs(v)))


def _expand_heads(cols, g, hoff):
    q = cols.shape[0]
    lane = lax.broadcasted_iota(jnp.int32, (q, LANES), 1)
    tiles = []
    for k in range(SSD_HEADS_PER_GROUP // 2):
        ha = hoff + g * SSD_HEADS_PER_GROUP + 2 * k
        tiles.append(jnp.take_along_axis(cols, jnp.where(lane < SSD_HEAD_DIM, ha, ha + 1), axis=1))
    return jnp.concatenate(tiles, axis=1)


def _ssd_scan_kernel(x_ref, b_ref, c_ref, dt_ref, dtt_ref, dtb_ref, dtbt_ref, a_ref, at_ref, init_ref, extra_ref,
                     y_ref, fin_ref, st_ref, *, reverse, hoff, add_prev):
    ci = pl.program_id(1)

    @pl.when(ci == 0)
    def _():
        st_ref[...] = init_ref[0]

    q = SSD_CHUNK
    f32, bf16 = jnp.float32, jnp.bfloat16
    dt = _softplus(dt_ref[0] + dtb_ref[...])
    dtt = _softplus(dtt_ref[0][hoff:hoff + SSD_HEADS, :] + dtbt_ref[...])
    ri = lax.broadcasted_iota(jnp.int32, (q, q), 0)
    cj = lax.broadcasted_iota(jnp.int32, (q, q), 1)
    keep = (cj >= ri) if reverse else (cj <= ri)
    tri = keep.astype(bf16)
    tri_t = ((ri >= cj) if reverse else (ri <= cj)).astype(bf16)
    acum = sum(jnp.dot(tri, p, preferred_element_type=f32) for p in _split3_bf16(dt * a_ref[...]))
    acum_t = sum(jnp.dot(p, tri_t, preferred_element_type=f32) for p in _split3_bf16(dtt * at_ref[...]))
    end = 0 if reverse else q - 1
    a_end = acum[end:end + 1, :]
    eacum = jnp.exp(acum)
    dt_dec_end = dt * jnp.exp(a_end - acum)
    lane = lax.broadcasted_iota(jnp.int32, (q, LANES), 1)
    for g in range(SSD_GROUPS):
        cg = c_ref[0][:, g * SSD_STATE:(g + 1) * SSD_STATE]
        bg = b_ref[0][:, g * SSD_STATE:(g + 1) * SSD_STATE]
        cg16 = cg.astype(bf16)
        cb = lax.dot_general(cg16, bg.astype(bf16), (((1,), (1,)), ((), ())), preferred_element_type=f32)
        xg = x_ref[0][:, g * SSD_GROUP_W:(g + 1) * SSD_GROUP_W]
        xg16 = xg.astype(bf16)
        eac_x = _expand_heads(eacum, g, hoff)
        yd = []
        for k in range(SSD_HEADS_PER_GROUP // 2):
            xp = xg16[:, k * LANES:(k + 1) * LANES]
            ys = []
            for hh in range(2):
                h = g * SSD_HEADS_PER_GROUP + 2 * k + hh
                seg = acum[:, hoff + h:hoff + h + 1] - acum_t[h:h + 1, :]
                lmat = jnp.exp(jnp.where(keep, seg, -jnp.inf)) * dtt[h:h + 1, :]
                ys.append(jnp.dot((cb * lmat).astype(bf16), xp, preferred_element_type=f32))
            yd.append(jnp.where(lane < SSD_HEAD_DIM, ys[0], ys[1]))
        st = st_ref[g]
        y_off = jnp.dot(cg16, st.astype(bf16), preferred_element_type=f32) * eac_x
        cols = slice(g * SSD_GROUP_W, (g + 1) * SSD_GROUP_W)
        if add_prev:
            other = extra_ref[0, :, cols].astype(f32)
        else:
            other = extra_ref[:, cols] * xg
        y_ref[0, :, cols] = (jnp.concatenate(yd, axis=1) + y_off + other).astype(bf16)
        xdd16 = (xg * _expand_heads(dt_dec_end, g, hoff)).astype(bf16)
        st_ref[g] = st * eac_x[end:end + 1, :] + jnp.dot(bg.T.astype(bf16), xdd16, preferred_element_type=f32)

    @pl.when(ci == pl.num_programs(1) - 1)
    def _():
        fin_ref[0] = st_ref[...]


def _ssd_scan_p(xbc, dt_raw, dt_raw_t, dt_bias, a, init, reverse, direction, y_prev=None, d_skip=None):
    b, length, _ = xbc.shape
    nc = length // SSD_CHUNK
    q = SSD_CHUNK
    cidx = (lambda j: nc - 1 - j) if reverse else (lambda j: j)
    nb = SSD_D_INNER // SSD_GN
    hoff = direction * SSD_HEADS
    full2 = lambda i, j: (0, 0)
    st_shape = (SSD_GROUPS, SSD_STATE, SSD_GROUP_W)
    lanes = lambda v: jnp.pad(v, (hoff, LANES - hoff - SSD_HEADS)).reshape(1, LANES)
    add_prev = y_prev is not None
    if add_prev:
        extra = y_prev
        extra_spec = pl.BlockSpec((1, q, SSD_D_INNER), lambda i, j: (i, cidx(j), 0))
    else:
        extra = jnp.repeat(d_skip, SSD_HEAD_DIM).reshape(1, SSD_D_INNER)
        extra_spec = pl.BlockSpec((1, SSD_D_INNER), full2)
    return pl.pallas_call(
        functools.partial(_ssd_scan_kernel, reverse=reverse, hoff=hoff, add_prev=add_prev),
        grid=(b, nc),
        in_specs=[pl.BlockSpec((1, q, SSD_D_INNER), lambda i, j: (i, cidx(j), 0)),
                  pl.BlockSpec((1, q, SSD_GN), lambda i, j: (i, cidx(j), nb)),
                  pl.BlockSpec((1, q, SSD_GN), lambda i, j: (i, cidx(j), nb + 1)),
                  pl.BlockSpec((1, q, LANES), lambda i, j: (i, cidx(j), 0)),
                  pl.BlockSpec((1, 2 * SSD_HEADS, q), lambda i, j: (i, 0, cidx(j))),
                  pl.BlockSpec((1, LANES), full2), pl.BlockSpec((SSD_HEADS, 1), full2),
                  pl.BlockSpec((1, LANES), full2), pl.BlockSpec((SSD_HEADS, 1), full2),
                  pl.BlockSpec((1,) + st_shape, lambda i, j: (i, 0, 0, 0)),
                  extra_spec],
        out_specs=[pl.BlockSpec((1, q, SSD_D_INNER), lambda i, j: (i, cidx(j), 0)),
                   pl.BlockSpec((1,) + st_shape, lambda i, j: (i, 0, 0, 0))],
        out_shape=[jax.ShapeDtypeStruct((b, length, SSD_D_INNER), jnp.bfloat16),
                   jax.ShapeDtypeStruct((b,) + st_shape, jnp.float32)],
        scratch_shapes=[pltpu.VMEM(st_shape, jnp.float32)],
        compiler_params=pltpu.CompilerParams(
            dimension_semantics=("parallel", "arbitrary"), vmem_limit_bytes=VMEM_LIMIT_BYTES),
        name="ssd_scan",
    )(xbc, xbc, xbc, dt_raw, dt_raw_t, lanes(dt_bias), dt_bias.reshape(-1, 1),
      lanes(a), a.reshape(-1, 1), init, extra)


def _ssd_out_kernel(y_ref, z_ref, ng_ref, w_ref, o_ref):
    z = z_ref[0].astype(jnp.float32)
    y = y_ref[0].astype(jnp.float32) * (z * jax.nn.sigmoid(z))
    ms = jnp.mean(y * y, axis=-1, keepdims=True)
    y = y * lax.rsqrt(ms + RMS_EPS) * ng_ref[...]
    o_ref[0] = jnp.dot(y.astype(jnp.bfloat16), w_ref[...], preferred_element_type=jnp.float32)


def _ssd_out(y, z, norm_g, out_w):
    b, length, di = y.shape
    d = out_w.shape[1]
    tm = min(length, 2 * ROW_TILE)
    row = lambda i, j: (i, j, 0)
    full2 = lambda i, j: (0, 0)
    return pl.pallas_call(
        _ssd_out_kernel,
        grid=(b, length // tm),
        in_specs=[pl.BlockSpec((1, tm, di), row), pl.BlockSpec((1, tm, di), row),
                  pl.BlockSpec((1, di), full2), pl.BlockSpec((di, d), full2)],
        out_specs=pl.BlockSpec((1, tm, d), row),
        out_shape=jax.ShapeDtypeStruct((b, length, d), jnp.float32),
        compiler_params=pltpu.CompilerParams(
            dimension_semantics=("parallel", "parallel"), vmem_limit_bytes=VMEM_LIMIT_BYTES),
        name="ssd_out",
    )(y, z, norm_g.reshape(1, di), out_w.astype(jnp.bfloat16))


def _state_to_kernel(s):
    b = s.shape[0]
    s = s.reshape(b, SSD_GROUPS, SSD_HEADS_PER_GROUP, SSD_HEAD_DIM, SSD_STATE)
    return jnp.transpose(s, (0, 1, 4, 2, 3)).reshape(b, SSD_GROUPS, SSD_STATE, SSD_GROUP_W)


def _state_from_kernel(s):
    b = s.shape[0]
    s = s.reshape(b, SSD_GROUPS, SSD_STATE, SSD_HEADS_PER_GROUP, SSD_HEAD_DIM)
    return jnp.transpose(s, (0, 1, 3, 4, 2)).reshape(b, SSD_HEADS, SSD_HEAD_DIM, SSD_STATE)


def _ssd_mixer_p(x, ng, sh, sc, init_f, init_b, in_w, conv_w, conv_b, dt_bias, a_log, d_skip, norm_g, out_w):
    z, xbc, dt_raw, dt_raw_t = _ssd_in(x, ng, sh, sc, in_w, conv_w, conv_b)
    a = -jnp.exp(a_log)
    yf, s_f = _ssd_scan_p(xbc, dt_raw, dt_raw_t, dt_bias[0], a[0], _state_to_kernel(init_f), False, 0,
                          d_skip=d_skip)
    y, s_b = _ssd_scan_p(xbc, dt_raw, dt_raw_t, dt_bias[1], a[1], _state_to_kernel(init_b), True, 1, y_prev=yf)
    m = _ssd_out(y, z, norm_g, out_w)
    return m, _state_from_kernel(s_f), _state_from_kernel(s_b)


HY_MAX_BLOCK = 512
HY_CC = 128
HY_MAC_ELEMS = 8192
HY_HIDDEN = 64
HY_FEAT_ROWS = 64


def _odd_dft_tables(n):
    m = np.arange(n, dtype=np.int64)[:, None]
    f = np.arange(n // 2, dtype=np.int64)[None, :]
    ang = 2.0 * np.pi * (((2 * f + 1) * m) % (2 * n)).astype(np.float64) / (2 * n)
    return np.cos(ang), np.sin(ang)


def _hy_in_kernel(xm_ref, xp_ref, xn_ref, ng_ref, sh_ref, sc_ref, w_ref, b_ref, cw_ref, cb_ref, o_ref, pad_ref):
    tm = xm_ref.shape[1]
    h, valid = _halo_rows(xm_ref, xp_ref, xn_ref, ng_ref, sh_ref, sc_ref)
    u = jnp.dot(h, w_ref[...], preferred_element_type=jnp.float32) + b_ref[...]
    pad_ref[...] = jnp.where(valid, u, 0.0)
    _conv_rows(pad_ref, cw_ref, cb_ref, o_ref, tm, silu=False)


def _hy_in(x, ng, sh, sc, in_w, in_b, short_w, short_b):
    b, length, d = x.shape
    n = in_w.shape[1]
    tm = min(length, ROW_TILE)
    taps = short_w.shape[0]
    per_b = lambda i, j: (i, 0, 0)
    full2 = lambda i, j: (0, 0)
    return pl.pallas_call(
        _hy_in_kernel,
        grid=(b, length // tm),
        in_specs=_halo_specs(length, tm, d) + [
            pl.BlockSpec((1, d), full2), pl.BlockSpec((1, 1, d), per_b), pl.BlockSpec((1, 1, d), per_b),
            pl.BlockSpec((d, n), full2), pl.BlockSpec((1, n), full2),
            pl.BlockSpec((taps, n), full2), pl.BlockSpec((1, n), full2)],
        out_specs=pl.BlockSpec((1, tm, n), lambda i, j: (i, j, 0)),
        out_shape=jax.ShapeDtypeStruct((b, length, n), jnp.float32),
        scratch_shapes=[pltpu.VMEM((tm + 2 * HALO, n), jnp.float32)],
        compiler_params=pltpu.CompilerParams(
            dimension_semantics=("parallel", "parallel"), vmem_limit_bytes=VMEM_LIMIT_BYTES),
        name="hy_in",
    )(x, x, x, ng.reshape(1, d), sh, sc, in_w.astype(jnp.bfloat16), in_b.reshape(1, n), short_w, short_b.reshape(1, n))


def _dot3(a, b):
    a_hi, a_lo = _split_bf16(a)
    b_hi, b_lo = _split_bf16(b)
    f32 = jnp.float32
    return (jnp.dot(a_hi, b_hi, preferred_element_type=f32) + jnp.dot(a_lo, b_hi, preferred_element_type=f32)
            + jnp.dot(a_hi, b_lo, preferred_element_type=f32))


def _hy_filter_kernel(w1t_ref, b1_ref, w2t_ref, b2_ref, w3t_ref, fr_ref, dl_ref, o_ref, *, length, blk):
    k = pl.program_id(0)
    q = (lax.broadcasted_iota(jnp.int32, (1, blk), 1) + k * blk)
    pos = jnp.abs(q - length).astype(jnp.float32)
    t = pos / float(length - 1)
    w = (2.0 * math.pi / length) * pos
    band = lax.broadcasted_iota(jnp.int32, (HY_BANDS, 1), 0).astype(jnp.float32)
    fb = 1e-4 + band * ((HY_BANDS - 1 - 1e-4) / (HY_BANDS - 1))
    z = jnp.concatenate([jnp.broadcast_to(t, (8, blk)), jnp.cos(fb * w), -jnp.sin(fb * w),
                         jnp.zeros((HY_FEAT_ROWS - 8 - 2 * HY_BANDS, blk), jnp.float32)], axis=0)
    h = jnp.sin(fr_ref[...] * (_dot3(w1t_ref[...], z) + b1_ref[...]))
    h = jnp.sin(fr_ref[...] * (_dot3(w2t_ref[...], h) + b2_ref[...]))
    kt = _dot3(w3t_ref[0], h)
    o_ref[0] = kt * jnp.exp(-t * dl_ref[...])


def _hy_filter(length, blk, f_w1, f_b1, f_w2, f_b2, f_w3, f_freq):
    d = f_w3.shape[1] // 2
    nk = 2 * length // blk
    w1t = jnp.concatenate([f_w1[0:1].T, jnp.zeros((HY_HIDDEN, 7), jnp.float32), f_w1[1:].T,
                           jnp.zeros((HY_HIDDEN, HY_FEAT_ROWS - 8 - 2 * HY_BANDS), jnp.float32)], axis=1)
    w3t = jnp.stack([f_w3[:, d:].T, f_w3[:, :d].T])
    deltas = jnp.abs(jnp.linspace(HY_MIN_DECAY, HY_MAX_DECAY, d, dtype=jnp.float32)).reshape(d, 1)
    col = lambda v: v.reshape(HY_HIDDEN, 1)
    full2 = lambda k: (0, 0)
    half = length // blk
    return pl.pallas_call(
        functools.partial(_hy_filter_kernel, length=length, blk=blk),
        grid=(nk,),
        in_specs=[pl.BlockSpec((HY_HIDDEN, HY_FEAT_ROWS), full2), pl.BlockSpec((HY_HIDDEN, 1), full2),
                  pl.BlockSpec((HY_HIDDEN, HY_HIDDEN), full2), pl.BlockSpec((HY_HIDDEN, 1), full2),
                  pl.BlockSpec((1, d, HY_HIDDEN), lambda k: (k // half, 0, 0)),
                  pl.BlockSpec((HY_HIDDEN, 1), full2), pl.BlockSpec((d, 1), full2)],
        out_specs=pl.BlockSpec((1, d, blk), lambda k: (k, 0, 0)),
        out_shape=jax.ShapeDtypeStruct((nk, d, blk), jnp.float32),
        compiler_params=pltpu.CompilerParams(
            dimension_semantics=("parallel",), vmem_limit_bytes=VMEM_LIMIT_BYTES),
        name="hy_filter",
    )(w1t, col(f_b1), f_w2.T, col(f_b2), w3t, col(f_freq), deltas)


def _hy_gspec_kernel(hi_ref, lo_ref, ft_ref, fb_ref, o_ref):
    o_ref[0] = _dot3(hi_ref[0], ft_ref[...]) + _dot3(lo_ref[0], fb_ref[...])


def _hy_gspec(kt):
    nk, d, blk = kt.shape
    cos, sin = _odd_dft_tables(2 * blk)
    top = np.concatenate([cos[:blk], -sin[:blk]], axis=1)
    bot = -np.concatenate([cos[blk:], -sin[blk:]], axis=1)
    bot[0] = 0.0
    tm = 512
    full2 = lambda e, i: (0, 0)
    return pl.pallas_call(
        _hy_gspec_kernel,
        grid=(nk - 1, d // tm),
        in_specs=[pl.BlockSpec((1, tm, blk), lambda e, i: (e + 1, i, 0)),
                  pl.BlockSpec((1, tm, blk), lambda e, i: (e, i, 0)),
                  pl.BlockSpec((blk, 2 * blk), full2), pl.BlockSpec((blk, 2 * blk), full2)],
        out_specs=pl.BlockSpec((1, tm, 2 * blk), lambda e, i: (e, i, 0)),
        out_shape=jax.ShapeDtypeStruct((nk - 1, d, 2 * blk), jnp.float32),
        compiler_params=pltpu.CompilerParams(
            dimension_semantics=("parallel", "parallel"), vmem_limit_bytes=VMEM_LIMIT_BYTES),
        name="hy_gspec",
    )(kt, kt, jnp.asarray(top, jnp.float32), jnp.asarray(bot, jnp.float32))


def _hy_conv_kernel(x0_ref, x1_ref, v_ref, g_ref, fb_ref, ff_ref, fi_ref, o_ref, lhs_ref, u_ref, y_ref, *, nb):
    cc, bsz = HY_CC, ff_ref.shape[0]
    mrows = HY_MAC_ELEMS // bsz
    for j in range(nb):
        sl = slice(j * bsz, (j + 1) * bsz)
        wj = v_ref[0, sl, :] * x1_ref[0, sl, :]
        lhs_ref[j * cc:(j + 1) * cc, :] = wj.T.astype(jnp.bfloat16)
    u_ref[...] = jnp.dot(lhs_ref[...], ff_ref[...], preferred_element_type=jnp.float32)

    def per_out_block(i, carry):
        def per_rows(rc, carry2):
            rows = pl.ds(pl.multiple_of(rc * mrows, mrows), mrows)
            acc_r = jnp.zeros((mrows, bsz), jnp.float32)
            acc_i = jnp.zeros((mrows, bsz), jnp.float32)
            for j in range(nb):
                e = i - j + (nb - 1)
                gr = g_ref[e, rows, 0:bsz]
                gi = g_ref[e, rows, bsz:2 * bsz]
                urows = pl.ds(pl.multiple_of(j * cc + rc * mrows, mrows), mrows)
                ur = u_ref[urows, 0:bsz]
                ui = u_ref[urows, bsz:2 * bsz]
                acc_r = acc_r + gr * ur - gi * ui
                acc_i = acc_i + gr * ui + gi * ur
            yrows = pl.ds(pl.multiple_of(i * cc + rc * mrows, mrows), mrows)
            y_ref[yrows, 0:bsz] = acc_r.astype(jnp.bfloat16)
            y_ref[yrows, bsz:2 * bsz] = acc_i.astype(jnp.bfloat16)
            return carry2
        return lax.fori_loop(0, cc // mrows, per_rows, carry)
    lax.fori_loop(0, nb, per_out_block, 0)

    yt = jnp.dot(y_ref[...], fi_ref[...], preferred_element_type=jnp.float32)
    for i in range(nb):
        sl = slice(i * bsz, (i + 1) * bsz)
        w = v_ref[0, sl, :] * x1_ref[0, sl, :]
        gated = (yt[i * cc:(i + 1) * cc, :].T + fb_ref[...] * w) * x0_ref[0, sl, :]
        o_ref[0, sl, :] = gated.astype(o_ref.dtype)


def _hy_conv(u, g, f_bias, blk):
    b, length, d3 = u.shape
    d = d3 // 3
    nb = length // blk
    ncb = d // HY_CC
    cos, sin = _odd_dft_tables(2 * blk)
    fwd = np.concatenate([cos[:blk], -sin[:blk]], axis=1)
    inv = (1.0 / blk) * np.concatenate([cos[:blk].T, -sin[:blk].T], axis=0)
    col = lambda off: pl.BlockSpec((1, length, HY_CC), lambda c, i, off=off: (i, 0, off + c))
    full2 = lambda c, i: (0, 0)
    return pl.pallas_call(
        functools.partial(_hy_conv_kernel, nb=nb),
        grid=(ncb, b),
        in_specs=[col(0), col(ncb), col(2 * ncb),
                  pl.BlockSpec((2 * nb - 1, HY_CC, 2 * blk), lambda c, i: (0, c, 0)),
                  pl.BlockSpec((1, HY_CC), lambda c, i: (0, c)),
                  pl.BlockSpec((blk, 2 * blk), full2), pl.BlockSpec((2 * blk, blk), full2)],
        out_specs=pl.BlockSpec((1, length, HY_CC), lambda c, i: (i, 0, c)),
        out_shape=jax.ShapeDtypeStruct((b, length, d), jnp.bfloat16),
        scratch_shapes=[pltpu.VMEM((nb * HY_CC, blk), jnp.bfloat16),
                        pltpu.VMEM((nb * HY_CC, 2 * blk), jnp.float32),
                        pltpu.VMEM((nb * HY_CC, 2 * blk), jnp.bfloat16)],
        compiler_params=pltpu.CompilerParams(
            dimension_semantics=("parallel", "arbitrary"), vmem_limit_bytes=56 * 1024 * 1024),
        name="hy_conv",
    )(u, u, u, g, f_bias.reshape(1, d), jnp.asarray(fwd, jnp.bfloat16), jnp.asarray(inv, jnp.bfloat16))


def _hyena_mixer_p(x, ng, sh, sc, in_w, in_b, short_w, short_b, f_w1, f_b1, f_w2, f_b2, f_w3, f_freq, f_bias, out_w):
    length = x.shape[1]
    blk = min(HY_MAX_BLOCK, length)
    u = _hy_in(x, ng, sh, sc, in_w, in_b, short_w, short_b)
    g = _hy_gspec(_hy_filter(length, blk, f_w1, f_b1, f_w2, f_b2, f_w3, f_freq))
    return _mm3(_hy_conv(u, g, f_bias, blk), out_w)


TOK_TILE = 256
MOE_ROWS = 512
SEG_CHUNK = 64
BF16_TILE_ROWS = 16
LANES = 128


def _split_bf16(w):
    hi = w.astype(jnp.bfloat16)
    lo = (w - hi.astype(jnp.float32)).astype(jnp.bfloat16)
    return hi, lo


def _moe_pre_kernel(x_ref, m_ref, g1_ref, ng_ref, sh_ref, sc_ref, wrh_ref, wrl_ref,
                    xo_ref, hpk_ref, lg_ref):
    x = x_ref[0] + g1_ref[0] * m_ref[0]
    xo_ref[0] = x
    ms = jnp.mean(x * x, axis=-1, keepdims=True)
    h = x * lax.rsqrt(ms + RMS_EPS) * ng_ref[...]
    h = h * (1.0 + sc_ref[0]) + sh_ref[0]
    h_hi = h.astype(jnp.bfloat16)
    h_lo = (h - h_hi.astype(jnp.float32)).astype(jnp.bfloat16)
    dn = (((1,), (1,)), ((), ()))
    lg = lax.dot_general(wrh_ref[...], h_hi, dn, preferred_element_type=jnp.float32)
    lg += lax.dot_general(wrh_ref[...], h_lo, dn, preferred_element_type=jnp.float32)
    lg += lax.dot_general(wrl_ref[...], h_hi, dn, preferred_element_type=jnp.float32)
    lg_ref[0] = lg
    half = h.shape[1] // 2
    wa = pltpu.bitcast(h_hi[:, :half].astype(jnp.float32), jnp.uint32) >> 16
    wb = pltpu.bitcast(h_hi[:, half:].astype(jnp.float32), jnp.uint32) & jnp.uint32(0xFFFF0000)
    hpk_ref[0] = wa | wb


def _moe_pre(x, m, g1, ng, sh, sc, w_router):
    b, length, d = x.shape
    tm = min(length, 512)
    wrh, wrl = _split_bf16(w_router.T)
    row = lambda i, j: (i, j, 0)
    per_b = lambda i, j: (i, 0, 0)
    full2 = lambda i, j: (0, 0)
    return pl.pallas_call(
        _moe_pre_kernel,
        grid=(b, length // tm),
        in_specs=[pl.BlockSpec((1, tm, d), row), pl.BlockSpec((1, tm, d), row),
                  pl.BlockSpec((1, 1, d), per_b), pl.BlockSpec((1, d), full2),
                  pl.BlockSpec((1, 1, d), per_b), pl.BlockSpec((1, 1, d), per_b),
                  pl.BlockSpec((N_EXPERTS, d), full2), pl.BlockSpec((N_EXPERTS, d), full2)],
        out_specs=[pl.BlockSpec((1, tm, d), row), pl.BlockSpec((1, tm, d // 2), row),
                   pl.BlockSpec((1, N_EXPERTS, tm), lambda i, j: (i, 0, j))],
        out_shape=[jax.ShapeDtypeStruct((b, length, d), jnp.float32),
                   jax.ShapeDtypeStruct((b, length, d // 2), jnp.uint32),
                   jax.ShapeDtypeStruct((b, N_EXPERTS, length), jnp.float32)],
        compiler_params=pltpu.CompilerParams(
            dimension_semantics=("parallel", "parallel"), vmem_limit_bytes=VMEM_LIMIT_BYTES),
        name="moe_pre",
    )(x, m, g1, ng.reshape(1, d), sh, sc, wrh, wrl)


SEL_BLOCK = 256
SEL_ROWS = 64
F32_INF_BITS = 0x7F800000


def _prefix_counts(flags, inclusive):
    e, length = flags.shape
    r = lax.broadcasted_iota(jnp.int32, (SEL_BLOCK, SEL_BLOCK), 0)
    c = lax.broadcasted_iota(jnp.int32, (SEL_BLOCK, SEL_BLOCK), 1)
    tri = ((r <= c) if inclusive else (r < c)).astype(jnp.bfloat16)
    off = jnp.zeros((e, 1), jnp.float32)
    blocks = []
    for k in range(length // SEL_BLOCK):
        blk = flags[:, k * SEL_BLOCK:(k + 1) * SEL_BLOCK]
        blocks.append(jnp.dot(blk.astype(jnp.bfloat16), tri, preferred_element_type=jnp.float32) + off)
        off = off + jnp.sum(blk, axis=1, keepdims=True)
    return jnp.concatenate(blocks, axis=1)


def _moe_select_kernel(lg_ref, aff_ref, idx_ref, rank_ref, cnt_ref, *, cap):
    lg = lg_ref[0]
    ne, length = lg.shape
    ex = jnp.exp(lg - jnp.max(lg, axis=0, keepdims=True))
    aff = ex / jnp.sum(ex, axis=0, keepdims=True)
    aff_ref[0] = aff
    bits = pltpu.bitcast(aff, jnp.int32)

    def bisect(_, carry):
        lo, hi = carry
        mid = lo + ((hi - lo + 1) >> 1)
        cnt = jnp.sum((bits >= mid).astype(jnp.float32), axis=1, keepdims=True)
        ok = cnt >= cap
        return jnp.where(ok, mid, lo), jnp.where(ok, hi, mid - 1)
    lo0 = jnp.zeros((ne, 1), jnp.int32)
    tau, _ = lax.fori_loop(0, 32, bisect, (lo0, lo0 + F32_INF_BITS))
    gt = bits > tau
    eq = (bits == tau).astype(jnp.float32)
    need = cap - jnp.sum(gt.astype(jnp.float32), axis=1, keepdims=True)
    keep = jnp.where(gt, 1.0, jnp.where(_prefix_counts(eq, False) < need, eq, 0.0))
    rank_ref[...] = _prefix_counts(keep, True)

    rows = min(SEL_ROWS, cap)
    lane = lax.broadcasted_iota(jnp.int32, (cap, LANES), 1)
    cnt_ref[...] = jnp.zeros((cap, LANES), jnp.float32)

    def per_expert(ei, carry):
        rk = rank_ref[pl.ds(ei, 1), :]
        cols = []
        for ck in range(cap // rows):
            slot = (lax.broadcasted_iota(jnp.int32, (rows, LANES), 0) + ck * rows).astype(jnp.float32)
            acc = jnp.zeros((rows, LANES), jnp.float32)
            for j in range(length // LANES):
                acc = acc + jnp.where(rk[:, j * LANES:(j + 1) * LANES] <= slot, 1.0, 0.0)
            cols.append(jnp.sum(acc, axis=1, keepdims=True))
        col = jnp.concatenate(cols, axis=0)
        cnt_ref[...] = jnp.where(lane == ei, col, cnt_ref[...])
        return carry
    lax.fori_loop(0, ne, per_expert, 0)
    idx_ref[0] = cnt_ref[:, :ne].astype(jnp.int32)


def _moe_select(lg, cap):
    b, ne, length = lg.shape
    aff, idx = pl.pallas_call(
        functools.partial(_moe_select_kernel, cap=cap),
        grid=(b,),
        in_specs=[pl.BlockSpec((1, ne, length), lambda i: (i, 0, 0))],
        out_specs=[pl.BlockSpec((1, ne, length), lambda i: (i, 0, 0)),
                   pl.BlockSpec((1, cap, ne), lambda i: (i, 0, 0))],
        out_shape=[jax.ShapeDtypeStruct((b, ne, length), jnp.float32),
                   jax.ShapeDtypeStruct((b, cap, ne), jnp.int32)],
        scratch_shapes=[pltpu.VMEM((ne, length), jnp.float32), pltpu.VMEM((cap, LANES), jnp.float32)],
        compiler_params=pltpu.CompilerParams(
            dimension_semantics=("parallel",), vmem_limit_bytes=VMEM_LIMIT_BYTES),
        name="moe_select",
    )(lg)
    return aff, jnp.swapaxes(idx, 1, 2)


def _moe_ffn_kernel(idx_ref, nidx_ref, h_hbm, gate_ref, wg_ref, wu_ref, wd_ref, y_ref,
                    xe_ref, wgb, wub, wdb, sem):
    nblk = pl.num_programs(1)
    step = pl.program_id(0) * nblk + pl.program_id(1)
    last = pl.num_programs(0) * nblk - 1

    def issue(ids_ref, slot):
        base = slot * MOE_ROWS
        for c in range(MOE_ROWS):
            pltpu.make_async_copy(h_hbm.at[pl.ds(ids_ref[0, 0, c], 1)], xe_ref.at[pl.ds(base + c, 1)],
                                  sem.at[slot]).start()

    @pl.when(step == 0)
    def _():
        issue(idx_ref, 0)

    for parity in range(2):
        @pl.when((step < last) & (step % 2 == parity))
        def _(parity=parity):
            issue(nidx_ref, 1 - parity)

    @pl.when(pl.program_id(1) == 0)
    def _():
        wgb[...] = wg_ref[0, 0].astype(jnp.bfloat16)
        wub[...] = wu_ref[0, 0].astype(jnp.bfloat16)
        wdb[...] = wd_ref[0, 0].astype(jnp.bfloat16)

    slot = step % 2
    rows = pl.ds(pl.multiple_of(slot * MOE_ROWS, MOE_ROWS), MOE_ROWS)
    pltpu.make_async_copy(h_hbm.at[pl.ds(0, MOE_ROWS)], xe_ref.at[rows], sem.at[slot]).wait()
    half = wgb.shape[0] // 2
    w = xe_ref[rows, :]
    xa = pltpu.bitcast(w << 16, jnp.float32).astype(jnp.bfloat16)
    xb = pltpu.bitcast(w & jnp.uint32(0xFFFF0000), jnp.float32).astype(jnp.bfloat16)
    hg = jnp.dot(xa, wgb[:half], preferred_element_type=jnp.float32)
    hg += jnp.dot(xb, wgb[half:], preferred_element_type=jnp.float32)
    hu = jnp.dot(xa, wub[:half], preferred_element_type=jnp.float32)
    hu += jnp.dot(xb, wub[half:], preferred_element_type=jnp.float32)
    hid = (hg * jax.nn.sigmoid(hg) * hu).astype(jnp.bfloat16)
    y = jnp.dot(hid, wdb[...], preferred_element_type=jnp.float32)
    y_ref[0] = (y * gate_ref[0]).astype(jnp.bfloat16)


def _moe_ffn(hpk, grow, gate, w_gate, w_up, w_down, layer):
    e, r = grow.shape
    d, f = w_gate.shape[2], w_gate.shape[3]
    nblk = r // MOE_ROWS
    nsteps = e * nblk
    wspec = lambda shp: pl.BlockSpec((1, 1) + shp, lambda i, j: (layer, i, 0, 0))
    ids = grow.reshape(nsteps, 1, MOE_ROWS)
    smem_ids = lambda off: pl.BlockSpec(
        (1, 1, MOE_ROWS), lambda i, j: (jnp.minimum(i * nblk + j + off, nsteps - 1), 0, 0), memory_space=pltpu.SMEM)
    return pl.pallas_call(
        _moe_ffn_kernel,
        grid=(e, nblk),
        in_specs=[smem_ids(0), smem_ids(1),
                  pl.BlockSpec(memory_space=pltpu.HBM),
                  pl.BlockSpec((1, MOE_ROWS, 1), lambda i, j: (i, j, 0)),
                  wspec((d, f)), wspec((d, f)), wspec((f, d))],
        out_specs=pl.BlockSpec((1, MOE_ROWS, d), lambda i, j: (i, j, 0)),
        out_shape=jax.ShapeDtypeStruct((e, r, d), jnp.bfloat16),
        scratch_shapes=[pltpu.VMEM((2 * MOE_ROWS, d // 2), jnp.uint32),
                        pltpu.VMEM((d, f), jnp.bfloat16), pltpu.VMEM((d, f), jnp.bfloat16),
                        pltpu.VMEM((f, d), jnp.bfloat16),
                        pltpu.SemaphoreType.DMA((2,))],
        compiler_params=pltpu.CompilerParams(
            dimension_semantics=("arbitrary", "arbitrary"), vmem_limit_bytes=VMEM_LIMIT_BYTES),
        name="moe_ffn",
    )(ids, ids, hpk, gate, w_gate, w_up, w_down)


def _moe_comb_kernel(cs_ref, x_ref, g2_ref, y_ref, idx_ref, fg_ref, o_ref, ycat, acc, *, cap, ch, ntile, final_norm):
    b = pl.program_id(0)
    t = pl.program_id(1)
    base = t * TOK_TILE
    sub = lax.broadcasted_iota(jnp.int32, (TOK_TILE, LANES), 0) + base
    if ntile == 1:
        for e in range(N_EXPERTS):
            ycat[e * ch:(e + 1) * ch, :] = y_ref[e, 0:ch, :]
        v = idx_ref[0]
        tiles = [(v[:, p * LANES:(p + 1) * LANES] == sub).astype(jnp.bfloat16)
                 for p in range(N_EXPERTS * ch // LANES)]
        acc[...] = jnp.dot(jnp.concatenate(tiles, axis=1), ycat[...], preferred_element_type=jnp.float32)
    else:
        lane = lax.broadcasted_iota(jnp.int32, (1, LANES), 1)
        per = LANES // ch
        sts = []
        for e in range(N_EXPERTS):
            s0 = cs_ref[(b * N_EXPERTS + e) * (ntile + 1) + t]
            st = jnp.minimum((s0 // BF16_TILE_ROWS) * BF16_TILE_ROWS, cap - ch)
            st = pl.multiple_of(st, BF16_TILE_ROWS)
            sts.append(st)
            ycat[e * ch:(e + 1) * ch, :] = y_ref[e, pl.ds(st, ch), :]
        tiles = []
        for p in range(N_EXPERTS // per):
            v = None
            for q in range(per):
                e = p * per + q
                r = pltpu.roll(idx_ref[0, e:e + 1, :], (2 * cap - sts[e] + q * ch) % cap, 1)[:, :LANES]
                v = r if v is None else jnp.where(lane >= q * ch, r, v)
            tiles.append((v == sub).astype(jnp.bfloat16))
        acc[...] = jnp.dot(jnp.concatenate(tiles, axis=1), ycat[...], preferred_element_type=jnp.float32)
        sub_c = lax.broadcasted_iota(jnp.int32, (TOK_TILE, ch), 0) + base
        lane_c = lax.broadcasted_iota(jnp.int32, (1, ch), 1)
        for e in range(N_EXPERTS):
            s1 = cs_ref[(b * N_EXPERTS + e) * (ntile + 1) + t + 1]
            first_end = sts[e] + ch
            n_extra = jnp.maximum(s1 - first_end + ch - 1, 0) // ch

            def extra(q, carry, e=e, first_end=first_end):
                lo = first_end + q * ch
                stq = pl.multiple_of(jnp.minimum(lo, cap - ch), BF16_TILE_ROWS)
                r = pltpu.roll(idx_ref[0, e:e + 1, :], (2 * cap - stq) % cap, 1)[:, :ch]
                hit = (r == sub_c) & (lane_c + stq >= lo)
                acc[...] += jnp.dot(hit.astype(jnp.bfloat16), y_ref[e, pl.ds(stq, ch), :],
                                    preferred_element_type=jnp.float32)
                return carry
            lax.fori_loop(0, n_extra, extra, 0)
    out = x_ref[0] + g2_ref[0] * acc[...]
    if final_norm:
        ms = jnp.mean(out * out, axis=-1, keepdims=True)
        out = out * lax.rsqrt(ms + RMS_EPS) * fg_ref[...]
    o_ref[0] = out


def _moe_combine(x, g2, y, idx, cs, final_g=None):
    b, length, d = x.shape
    final_norm = final_g is not None
    fg = (final_g if final_norm else jnp.ones((d,), jnp.float32)).reshape(1, d)
    cap = idx.shape[2]
    ntile = length // TOK_TILE
    ch = min(SEG_CHUNK, cap)
    if ntile == 1:
        idx_in = idx.reshape(b, 1, N_EXPERTS * cap)
        idx_spec = pl.BlockSpec((1, 1, N_EXPERTS * cap), lambda i, j, c: (i, 0, 0))
    else:
        idx_in = idx
        idx_spec = pl.BlockSpec((1, N_EXPERTS, cap), lambda i, j, c: (i, 0, 0))
    grid_spec = pltpu.PrefetchScalarGridSpec(
        num_scalar_prefetch=1,
        grid=(b, ntile),
        in_specs=[pl.BlockSpec((1, TOK_TILE, d), lambda i, j, c: (i, j, 0)),
                  pl.BlockSpec((1, 1, d), lambda i, j, c: (i, 0, 0)),
                  pl.BlockSpec((N_EXPERTS, cap, d), lambda i, j, c: (0, i, 0)),
                  idx_spec,
                  pl.BlockSpec((1, d), lambda i, j, c: (0, 0))],
        out_specs=pl.BlockSpec((1, TOK_TILE, d), lambda i, j, c: (i, j, 0)),
        scratch_shapes=[pltpu.VMEM((N_EXPERTS * ch, d), jnp.bfloat16),
                        pltpu.VMEM((TOK_TILE, d), jnp.float32)])
    return pl.pallas_call(
        functools.partial(_moe_comb_kernel, cap=cap, ch=ch, ntile=ntile, final_norm=final_norm),
        grid_spec=grid_spec,
        out_shape=jax.ShapeDtypeStruct((b, length, d), jnp.float32),
        compiler_params=pltpu.CompilerParams(
            dimension_semantics=("arbitrary", "arbitrary"), vmem_limit_bytes=56 * 1024 * 1024),
        name="moe_combine",
    )(cs.reshape(-1).astype(jnp.int32), x, g2, y, idx_in, fg)


def _moe_block(x, m, g1, ng, sh, sc, g2, w_router, w_gate, w_up, w_down, layer, final_g=None):
    b, length, d = x.shape
    cap = EC_FACTOR * length // N_EXPERTS
    x1, hpk, lg = _moe_pre(x, m, g1, ng, sh, sc, w_router)
    aff, idx = _moe_select(lg, cap)
    gate = jnp.take_along_axis(aff, idx, axis=-1)
    ntile = length // TOK_TILE
    bounds = jnp.arange(ntile + 1, dtype=jnp.int32) * TOK_TILE
    cs = jnp.sum(idx[:, :, :, None] < bounds, axis=2, dtype=jnp.int32)
    grow = idx + (jnp.arange(b, dtype=jnp.int32) * length)[:, None, None]
    grow = jnp.swapaxes(grow, 0, 1).reshape(N_EXPERTS, b * cap)
    gate_e = jnp.swapaxes(gate, 0, 1).reshape(N_EXPERTS, b * cap, 1)
    y = _moe_ffn(hpk.reshape(b * length, d // 2), grow, gate_e, w_gate, w_up, w_down, layer)
    return _moe_combine(x1, g2, y, idx, cs, final_g)


def kernel(x_prompt, x_sample, state_ssd, c, c_ctx, norm_g, ada_w, ada_b, hy_in_w, hy_in_b, hy_short_w, hy_short_b, hy_f_w1, hy_f_b1, hy_f_w2, hy_f_b2, hy_f_w3, hy_f_freq, hy_f_bias, hy_out_w, ssd_in_w, ssd_conv_w, ssd_conv_b, ssd_dt_bias, ssd_A_log, ssd_D, ssd_norm_g, ssd_out_w, moe_router, moe_w_gate, moe_w_up, moe_w_down, final_norm_g):
    rows = x_sample.shape[1] // GRID_W
    xp = x_prompt
    xs = x_sample + _sincos_2d(rows, GRID_W, D_MODEL)[None]
    new_ssd = []
    for i in range(DEPTH):
        nb_s = c.shape[0]
        mods = _adaln(jnp.concatenate([c, c_ctx[None, :]], axis=0), ada_w[i], ada_b[i])
        sh1s, sc1s, g1s, sh2s, sc2s, g2s = [m[:nb_s] for m in mods]
        sh1p, sc1p, g1p, sh2p, sc2p, g2p = [m[nb_s:] for m in mods]
        j = i // N_MIXERS
        bp = (xp.shape[0], 1, D_MODEL)
        if i % N_MIXERS == 0:
            hy = (hy_in_w[j], hy_in_b[j], hy_short_w[j], hy_short_b[j], hy_f_w1[j], hy_f_b1[j],
                  hy_f_w2[j], hy_f_b2[j], hy_f_w3[j], hy_f_freq[j], hy_f_bias[j], hy_out_w[j])
            mp = _hyena_mixer_p(xp, norm_g[i, 0], jnp.broadcast_to(sh1p, bp), jnp.broadcast_to(sc1p, bp), *hy)
            ms = _hyena_mixer_p(xs, norm_g[i, 0], sh1s, sc1s, *hy)
        else:
            sp = (ssd_in_w[j], ssd_conv_w[j], ssd_conv_b[j], ssd_dt_bias[j], ssd_A_log[j],
                  ssd_D[j], ssd_norm_g[j], ssd_out_w[j])
            zeros = jnp.zeros((xp.shape[0], SSD_HEADS, SSD_HEAD_DIM, SSD_STATE), jnp.float32)
            mp, s_f, s_b = _ssd_mixer_p(xp, norm_g[i, 0], jnp.broadcast_to(sh1p, bp),
                                        jnp.broadcast_to(sc1p, bp), zeros, zeros, *sp)
            new_ssd.append(jnp.stack([s_f, s_b], axis=1))
            ms, _, _ = _ssd_mixer_p(xs, norm_g[i, 0], sh1s, sc1s, state_ssd[:, j, 0], state_ssd[:, j, 1], *sp)
        moe = (moe_router[i], moe_w_gate, moe_w_up, moe_w_down, i, final_norm_g if i == DEPTH - 1 else None)
        xp = _moe_block(xp, mp, jnp.broadcast_to(g1p, bp), norm_g[i, 1], jnp.broadcast_to(sh2p, bp),
                        jnp.broadcast_to(sc2p, bp), jnp.broadcast_to(g2p, bp), *moe)
        xs = _moe_block(xs, ms, g1s, norm_g[i, 1], sh2s, sc2s, g2s, *moe)
    new_state_ssd = jnp.stack(new_ssd, axis=1)
    return (xp, xs, new_state_ssd)
```

```python
import functools
import math

import jax
import jax.numpy as jnp
import numpy as np
from jax import lax
from jax.experimental import pallas as pl
from jax.experimental.pallas import tpu as pltpu

D_MODEL = 1024
DEPTH = 2
GRID_W = 64
N_MIXERS = 2
RMS_EPS = 1e-6
HY_EMB = 33
HY_BANDS = (HY_EMB - 1) // 2
HY_SHORT_DECAY_FRAC = 0.3
HY_LONG_DECAY_FRAC = 1.5
HY_DECAY_TARGET = 1e-2
HY_MAX_DECAY = math.log(HY_DECAY_TARGET) / HY_SHORT_DECAY_FRAC
HY_MIN_DECAY = math.log(HY_DECAY_TARGET) / HY_LONG_DECAY_FRAC
SSD_D_INNER = 2 * D_MODEL
SSD_HEAD_DIM = 64
SSD_HEADS = SSD_D_INNER // SSD_HEAD_DIM
SSD_GROUPS = 4
SSD_STATE = 128
SSD_CHUNK = 128
SSD_XBC = SSD_D_INNER + 2 * SSD_GROUPS * SSD_STATE
N_EXPERTS = 16
EC_FACTOR = 2

VMEM_LIMIT_BYTES = 48 * 1024 * 1024


def _mm_kernel(a_ref, b_ref, o_ref, acc_ref):
    @pl.when(pl.program_id(2) == 0)
    def _():
        acc_ref[...] = jnp.zeros_like(acc_ref)

    acc_ref[...] += jnp.dot(a_ref[...].astype(jnp.bfloat16), b_ref[...],
                            preferred_element_type=jnp.float32)

    @pl.when(pl.program_id(2) == pl.num_programs(2) - 1)
    def _():
        o_ref[...] = acc_ref[...]


def _pick(n, pref):
    for t in pref:
        if n % t == 0:
            return t
    return n


def _mm(a, b):
    m, k = a.shape
    n = b.shape[1]
    mp = -(-m // 8) * 8
    if mp != m:
        a = jnp.pad(a, ((0, mp - m), (0, 0)))
    tm = _pick(mp, (512, 256, 128, 64, 32, 16, 8))
    tn = _pick(n, (512, 256, 128))
    tk = _pick(k, (1024, 512, 256, 128))
    out = pl.pallas_call(
        _mm_kernel,
        grid=(mp // tm, n // tn, k // tk),
        in_specs=[pl.BlockSpec((tm, tk), lambda i, j, l: (i, l)),
                  pl.BlockSpec((tk, tn), lambda i, j, l: (l, j))],
        out_specs=pl.BlockSpec((tm, tn), lambda i, j, l: (i, j)),
        out_shape=jax.ShapeDtypeStruct((mp, n), jnp.float32),
        scratch_shapes=[pltpu.VMEM((tm, tn), jnp.float32)],
        compiler_params=pltpu.CompilerParams(
            dimension_semantics=("parallel", "parallel", "arbitrary"),
            vmem_limit_bytes=VMEM_LIMIT_BYTES),
        name="mm",
    )(a, b.astype(jnp.bfloat16))
    return out[:m]


def _mm3(a, b):
    lead = a.shape[:-1]
    return _mm(a.reshape(-1, a.shape[-1]), b).reshape(*lead, b.shape[1])


def _adaln(cond, ada_w, ada_b):
    m = _mm(jax.nn.silu(cond), ada_w) + ada_b
    return jnp.split(m[:, None, :], 6, axis=-1)


def _sincos_2d(rows, cols, d):
    q = d // 4
    omega = 1.0 / (10000.0 ** (jnp.arange(q, dtype=jnp.float32) / q))
    t = jnp.arange(rows * cols)
    er = (t // cols).astype(jnp.float32)[:, None] * omega[None, :]
    ec = (t % cols).astype(jnp.float32)[:, None] * omega[None, :]
    return jnp.concatenate([jnp.sin(er), jnp.cos(er), jnp.sin(ec), jnp.cos(ec)], axis=-1)


SSD_GN = SSD_GROUPS * SSD_STATE
SSD_GROUP_W = SSD_D_INNER // SSD_GROUPS
SSD_HEADS_PER_GROUP = SSD_HEADS // SSD_GROUPS
ROW_TILE = 256


def _modnorm(x, ng, sh, sc):
    ms = jnp.mean(x * x, axis=-1, keepdims=True)
    return (x * lax.rsqrt(ms + RMS_EPS) * ng) * (1.0 + sc) + sh


HALO = 8
CONV_COLS = 512


def _halo_rows(xm_ref, xp_ref, xn_ref, ng_ref, sh_ref, sc_ref):
    j = pl.program_id(1)
    xa = jnp.concatenate([xp_ref[0], xm_ref[0], xn_ref[0]], axis=0)
    h = _modnorm(xa, ng_ref[...], sh_ref[0], sc_ref[0]).astype(jnp.bfloat16)
    tm = xm_ref.shape[1]
    r = lax.broadcasted_iota(jnp.int32, (tm + 2 * HALO, 1), 0)
    valid = ((r >= HALO) | (j > 0)) & ((r < tm + HALO) | (j < pl.num_programs(1) - 1))
    return h, valid


def _proj_conv(h, valid, w_ref, w_col0, pb_ref, cw_ref, cb_ref, o_ref, tm, silu):
    taps = cw_ref.shape[0]
    ncol = o_ref.shape[2]
    for c0 in range(0, ncol, CONV_COLS):
        cols = slice(c0, c0 + CONV_COLS)
        u = jnp.dot(h, w_ref[:, w_col0 + c0:w_col0 + c0 + CONV_COLS], preferred_element_type=jnp.float32)
        if pb_ref is not None:
            u = u + pb_ref[:, cols]
        u = jnp.where(valid, u, 0.0)
        nrow = u.shape[0]
        acc = cb_ref[:, cols] + jnp.zeros((tm, CONV_COLS), jnp.float32)
        for k in range(taps):
            shifted = u if k == taps // 2 else pltpu.roll(u, (taps // 2 - k) % nrow, 0)
            acc = acc + cw_ref[k:k + 1, cols] * shifted[HALO:HALO + tm]
        if silu:
            acc = acc * jax.nn.sigmoid(acc)
        o_ref[0, :, cols] = acc


def _halo_specs(length, tm, d):
    nh = length // HALO
    per = tm // HALO
    main = pl.BlockSpec((1, tm, d), lambda i, j: (i, j, 0))
    prev = pl.BlockSpec((1, HALO, d), lambda i, j: (i, jnp.maximum(j * per - 1, 0), 0))
    nxt = pl.BlockSpec((1, HALO, d), lambda i, j: (i, jnp.minimum((j + 1) * per, nh - 1), 0))
    return [main, prev, nxt]


def _ssd_in_kernel(xm_ref, xp_ref, xn_ref, ng_ref, sh_ref, sc_ref, w_ref, wdt_ref, wdtt_ref, cw_ref, cb_ref,
                   z_ref, xbc_ref, dt_ref, dtt_ref):
    tm = xm_ref.shape[1]
    h, valid = _halo_rows(xm_ref, xp_ref, xn_ref, ng_ref, sh_ref, sc_ref)
    hm = h[HALO:HALO + tm]
    z_ref[0] = jnp.dot(hm, w_ref[:, :SSD_D_INNER], preferred_element_type=jnp.float32).astype(jnp.bfloat16)
    _proj_conv(h, valid, w_ref, SSD_D_INNER, None, cw_ref, cb_ref, xbc_ref, tm, silu=True)
    dt_ref[0] = jnp.dot(hm, wdt_ref[...], preferred_element_type=jnp.float32)
    dtt_ref[0] = lax.dot_general(wdtt_ref[...], hm, (((1,), (1,)), ((), ())),
                                 preferred_element_type=jnp.float32)


def _ssd_in(x, ng, sh, sc, in_w, conv_w, conv_b):
    b, length, d = x.shape
    tm = min(length, ROW_TILE)
    nzx = SSD_D_INNER + SSD_XBC
    w = in_w[:, :nzx].astype(jnp.bfloat16)
    wdt = in_w[:, nzx:]
    wdt_p = jnp.pad(wdt, ((0, 0), (0, LANES - 2 * SSD_HEADS))).astype(jnp.bfloat16)
    wdt_t = wdt.T.astype(jnp.bfloat16)
    taps = conv_w.shape[0]
    row = lambda i, j: (i, j, 0)
    per_b = lambda i, j: (i, 0, 0)
    full2 = lambda i, j: (0, 0)
    return pl.pallas_call(
        _ssd_in_kernel,
        grid=(b, length // tm),
        in_specs=_halo_specs(length, tm, d) + [
            pl.BlockSpec((1, d), full2), pl.BlockSpec((1, 1, d), per_b), pl.BlockSpec((1, 1, d), per_b),
            pl.BlockSpec((d, nzx), full2), pl.BlockSpec((d, LANES), full2),
            pl.BlockSpec((2 * SSD_HEADS, d), full2),
            pl.BlockSpec((taps, SSD_XBC), full2), pl.BlockSpec((1, SSD_XBC), full2)],
        out_specs=[pl.BlockSpec((1, tm, SSD_D_INNER), row), pl.BlockSpec((1, tm, SSD_XBC), row),
                   pl.BlockSpec((1, tm, LANES), row),
                   pl.BlockSpec((1, 2 * SSD_HEADS, tm), lambda i, j: (i, 0, j))],
        out_shape=[jax.ShapeDtypeStruct((b, length, SSD_D_INNER), jnp.bfloat16),
                   jax.ShapeDtypeStruct((b, length, SSD_XBC), jnp.float32),
                   jax.ShapeDtypeStruct((b, length, LANES), jnp.float32),
                   jax.ShapeDtypeStruct((b, 2 * SSD_HEADS, length), jnp.float32)],
        compiler_params=pltpu.CompilerParams(
            dimension_semantics=("parallel", "parallel"), vmem_limit_bytes=56 * 1024 * 1024),
        name="ssd_in",
    )(x, x, x, ng.reshape(1, d), sh, sc, w, wdt_p, wdt_t, conv_w, conv_b.reshape(1, SSD_XBC))


def _split3_bf16(v):
    p1 = v.astype(jnp.bfloat16)
    r1 = v - p1.astype(jnp.float32)
    p2 = r1.astype(jnp.bfloat16)
    p3 = (r1 - p2.astype(jnp.float32)).astype(jnp.bfloat16)
    return p1, p2, p3


def _softplus(v):
    return jnp.maximum(v, 0.0) + jnp.log1p(jnp.exp(-jnp.abs(v)))


def _expand_heads(cols, g, hoff):
    q = cols.shape[0]
    lane = lax.broadcasted_iota(jnp.int32, (q, LANES), 1)
    tiles = []
    for k in range(SSD_HEADS_PER_GROUP // 2):
        ha = hoff + g * SSD_HEADS_PER_GROUP + 2 * k
        tiles.append(jnp.take_along_axis(cols, jnp.where(lane < SSD_HEAD_DIM, ha, ha + 1), axis=1))
    return jnp.concatenate(tiles, axis=1)


def _ssd_scan_kernel(x_ref, b_ref, c_ref, dt_ref, dtt_ref, dtb_ref, dtbt_ref, a_ref, at_ref, init_ref, extra_ref,
                     y_ref, fin_ref, st_ref, *, reverse, hoff, add_prev):
    ci = pl.program_id(1)

    @pl.when(ci == 0)
    def _():
        st_ref[...] = init_ref[0]

    q = SSD_CHUNK
    f32, bf16 = jnp.float32, jnp.bfloat16
    dt = _softplus(dt_ref[0] + dtb_ref[...])
    dtt = _softplus(dtt_ref[0][hoff:hoff + SSD_HEADS, :] + dtbt_ref[...])
    ri = lax.broadcasted_iota(jnp.int32, (q, q), 0)
    cj = lax.broadcasted_iota(jnp.int32, (q, q), 1)
    keep = (cj >= ri) if reverse else (cj <= ri)
    tri = keep.astype(bf16)
    tri_t = ((ri >= cj) if reverse else (ri <= cj)).astype(bf16)
    acum = sum(jnp.dot(tri, p, preferred_element_type=f32) for p in _split3_bf16(dt * a_ref[...]))
    acum_t = sum(jnp.dot(p, tri_t, preferred_element_type=f32) for p in _split3_bf16(dtt * at_ref[...]))
    end = 0 if reverse else q - 1
    a_end = acum[end:end + 1, :]
    eacum = jnp.exp(acum)
    dt_dec_end = dt * jnp.exp(a_end - acum)
    lane = lax.broadcasted_iota(jnp.int32, (q, LANES), 1)
    for g in range(SSD_GROUPS):
        cg = c_ref[0][:, g * SSD_STATE:(g + 1) * SSD_STATE]
        bg = b_ref[0][:, g * SSD_STATE:(g + 1) * SSD_STATE]
        cg16 = cg.astype(bf16)
        cb = lax.dot_general(cg16, bg.astype(bf16), (((1,), (1,)), ((), ())), preferred_element_type=f32)
        xg = x_ref[0][:, g * SSD_GROUP_W:(g + 1) * SSD_GROUP_W]
        xg16 = xg.astype(bf16)
        eac_x = _expand_heads(eacum, g, hoff)
        yd = []
        for k in range(SSD_HEADS_PER_GROUP // 2):
            xp = xg16[:, k * LANES:(k + 1) * LANES]
            ys = []
            for hh in range(2):
                h = g * SSD_HEADS_PER_GROUP + 2 * k + hh
                seg = acum[:, hoff + h:hoff + h + 1] - acum_t[h:h + 1, :]
                lmat = jnp.exp(jnp.where(keep, seg, -jnp.inf)) * dtt[h:h + 1, :]
                ys.append(jnp.dot((cb * lmat).astype(bf16), xp, preferred_element_type=f32))
            yd.append(jnp.where(lane < SSD_HEAD_DIM, ys[0], ys[1]))
        st = st_ref[g]
        y_off = jnp.dot(cg16, st.astype(bf16), preferred_element_type=f32) * eac_x
        cols = slice(g * SSD_GROUP_W, (g + 1) * SSD_GROUP_W)
        if add_prev:
            other = extra_ref[0, :, cols].astype(f32)
        else:
            other = extra_ref[:, cols] * xg
        y_ref[0, :, cols] = (jnp.concatenate(yd, axis=1) + y_off + other).astype(bf16)
        xdd16 = (xg * _expand_heads(dt_dec_end, g, hoff)).astype(bf16)
        st_ref[g] = st * eac_x[end:end + 1, :] + jnp.dot(bg.T.astype(bf16), xdd16, preferred_element_type=f32)

    @pl.when(ci == pl.num_programs(1) - 1)
    def _():
        fin_ref[0] = st_ref[...]


def _ssd_scan_p(xbc, dt_raw, dt_raw_t, dt_bias, a, init, reverse, direction, y_prev=None, d_skip=None):
    b, length, _ = xbc.shape
    nc = length // SSD_CHUNK
    q = SSD_CHUNK
    cidx = (lambda j: nc - 1 - j) if reverse else (lambda j: j)
    nb = SSD_D_INNER // SSD_GN
    hoff = direction * SSD_HEADS
    full2 = lambda i, j: (0, 0)
    st_shape = (SSD_GROUPS, SSD_STATE, SSD_GROUP_W)
    lanes = lambda v: jnp.pad(v, (hoff, LANES - hoff - SSD_HEADS)).reshape(1, LANES)
    add_prev = y_prev is not None
    if add_prev:
        extra = y_prev
        extra_spec = pl.BlockSpec((1, q, SSD_D_INNER), lambda i, j: (i, cidx(j), 0))
    else:
        extra = jnp.repeat(d_skip, SSD_HEAD_DIM).reshape(1, SSD_D_INNER)
        extra_spec = pl.BlockSpec((1, SSD_D_INNER), full2)
    return pl.pallas_call(
        functools.partial(_ssd_scan_kernel, reverse=reverse, hoff=hoff, add_prev=add_prev),
        grid=(b, nc),
        in_specs=[pl.BlockSpec((1, q, SSD_D_INNER), lambda i, j: (i, cidx(j), 0)),
                  pl.BlockSpec((1, q, SSD_GN), lambda i, j: (i, cidx(j), nb)),
                  pl.BlockSpec((1, q, SSD_GN), lambda i, j: (i, cidx(j), nb + 1)),
                  pl.BlockSpec((1, q, LANES), lambda i, j: (i, cidx(j), 0)),
                  pl.BlockSpec((1, 2 * SSD_HEADS, q), lambda i, j: (i, 0, cidx(j))),
                  pl.BlockSpec((1, LANES), full2), pl.BlockSpec((SSD_HEADS, 1), full2),
                  pl.BlockSpec((1, LANES), full2), pl.BlockSpec((SSD_HEADS, 1), full2),
                  pl.BlockSpec((1,) + st_shape, lambda i, j: (i, 0, 0, 0)),
                  extra_spec],
        out_specs=[pl.BlockSpec((1, q, SSD_D_INNER), lambda i, j: (i, cidx(j), 0)),
                   pl.BlockSpec((1,) + st_shape, lambda i, j: (i, 0, 0, 0))],
        out_shape=[jax.ShapeDtypeStruct((b, length, SSD_D_INNER), jnp.bfloat16),
                   jax.ShapeDtypeStruct((b,) + st_shape, jnp.float32)],
        scratch_shapes=[pltpu.VMEM(st_shape, jnp.float32)],
        compiler_params=pltpu.CompilerParams(
            dimension_semantics=("parallel", "arbitrary"), vmem_limit_bytes=VMEM_LIMIT_BYTES),
        name="ssd_scan",
    )(xbc, xbc, xbc, dt_raw, dt_raw_t, lanes(dt_bias), dt_bias.reshape(-1, 1),
      lanes(a), a.reshape(-1, 1), init, extra)


def _ssd_out_kernel(y_ref, z_ref, ng_ref, w_ref, o_ref):
    z = z_ref[0].astype(jnp.float32)
    y = y_ref[0].astype(jnp.float32) * (z * jax.nn.sigmoid(z))
    ms = jnp.mean(y * y, axis=-1, keepdims=True)
    y = y * lax.rsqrt(ms + RMS_EPS) * ng_ref[...]
    o_ref[0] = jnp.dot(y.astype(jnp.bfloat16), w_ref[...], preferred_element_type=jnp.float32)


def _ssd_out(y, z, norm_g, out_w):
    b, length, di = y.shape
    d = out_w.shape[1]
    tm = min(length, 2 * ROW_TILE)
    row = lambda i, j: (i, j, 0)
    full2 = lambda i, j: (0, 0)
    return pl.pallas_call(
        _ssd_out_kernel,
        grid=(b, length // tm),
        in_specs=[pl.BlockSpec((1, tm, di), row), pl.BlockSpec((1, tm, di), row),
                  pl.BlockSpec((1, di), full2), pl.BlockSpec((di, d), full2)],
        out_specs=pl.BlockSpec((1, tm, d), row),
        out_shape=jax.ShapeDtypeStruct((b, length, d), jnp.float32),
        compiler_params=pltpu.CompilerParams(
            dimension_semantics=("parallel", "parallel"), vmem_limit_bytes=VMEM_LIMIT_BYTES),
        name="ssd_out",
    )(y, z, norm_g.reshape(1, di), out_w.astype(jnp.bfloat16))


def _state_to_kernel(s):
    b = s.shape[0]
    s = s.reshape(b, SSD_GROUPS, SSD_HEADS_PER_GROUP, SSD_HEAD_DIM, SSD_STATE)
    return jnp.transpose(s, (0, 1, 4, 2, 3)).reshape(b, SSD_GROUPS, SSD_STATE, SSD_GROUP_W)


def _state_from_kernel(s):
    b = s.shape[0]
    s = s.reshape(b, SSD_GROUPS, SSD_STATE, SSD_HEADS_PER_GROUP, SSD_HEAD_DIM)
    return jnp.transpose(s, (0, 1, 3, 4, 2)).reshape(b, SSD_HEADS, SSD_HEAD_DIM, SSD_STATE)


def _ssd_mixer_p(x, ng, sh, sc, init_f, init_b, in_w, conv_w, conv_b, dt_bias, a_log, d_skip, norm_g, out_w):
    z, xbc, dt_raw, dt_raw_t = _ssd_in(x, ng, sh, sc, in_w, conv_w, conv_b)
    a = -jnp.exp(a_log)
    yf, s_f = _ssd_scan_p(xbc, dt_raw, dt_raw_t, dt_bias[0], a[0], _state_to_kernel(init_f), False, 0,
                          d_skip=d_skip)
    y, s_b = _ssd_scan_p(xbc, dt_raw, dt_raw_t, dt_bias[1], a[1], _state_to_kernel(init_b), True, 1, y_prev=yf)
    m = _ssd_out(y, z, norm_g, out_w)
    return m, _state_from_kernel(s_f), _state_from_kernel(s_b)


HY_MAX_BLOCK = 512
HY_CC = 128
HY_MAC_ELEMS = 8192
HY_HIDDEN = 64
HY_FEAT_ROWS = 64


def _odd_dft_tables(n):
    m = np.arange(n, dtype=np.int64)[:, None]
    f = np.arange(n // 2, dtype=np.int64)[None, :]
    ang = 2.0 * np.pi * (((2 * f + 1) * m) % (2 * n)).astype(np.float64) / (2 * n)
    return np.cos(ang), np.sin(ang)


def _hy_in_kernel(xm_ref, xp_ref, xn_ref, ng_ref, sh_ref, sc_ref, w_ref, b_ref, cw_ref, cb_ref, o_ref):
    tm = xm_ref.shape[1]
    h, valid = _halo_rows(xm_ref, xp_ref, xn_ref, ng_ref, sh_ref, sc_ref)
    _proj_conv(h, valid, w_ref, 0, b_ref, cw_ref, cb_ref, o_ref, tm, silu=False)


def _hy_in(x, ng, sh, sc, in_w, in_b, short_w, short_b):
    b, length, d = x.shape
    n = in_w.shape[1]
    tm = min(length, ROW_TILE)
    taps = short_w.shape[0]
    per_b = lambda i, j: (i, 0, 0)
    full2 = lambda i, j: (0, 0)
    return pl.pallas_call(
        _hy_in_kernel,
        grid=(b, length // tm),
        in_specs=_halo_specs(length, tm, d) + [
            pl.BlockSpec((1, d), full2), pl.BlockSpec((1, 1, d), per_b), pl.BlockSpec((1, 1, d), per_b),
            pl.BlockSpec((d, n), full2), pl.BlockSpec((1, n), full2),
            pl.BlockSpec((taps, n), full2), pl.BlockSpec((1, n), full2)],
        out_specs=pl.BlockSpec((1, tm, n), lambda i, j: (i, j, 0)),
        out_shape=jax.ShapeDtypeStruct((b, length, n), jnp.float32),
        compiler_params=pltpu.CompilerParams(
            dimension_semantics=("parallel", "parallel"), vmem_limit_bytes=VMEM_LIMIT_BYTES),
        name="hy_in",
    )(x, x, x, ng.reshape(1, d), sh, sc, in_w.astype(jnp.bfloat16), in_b.reshape(1, n), short_w, short_b.reshape(1, n))


def _dot3(a, b):
    a_hi, a_lo = _split_bf16(a)
    b_hi, b_lo = _split_bf16(b)
    f32 = jnp.float32
    return (jnp.dot(a_hi, b_hi, preferred_element_type=f32) + jnp.dot(a_lo, b_hi, preferred_element_type=f32)
            + jnp.dot(a_hi, b_lo, preferred_element_type=f32))


def _hy_filter_kernel(w1t_ref, b1_ref, w2t_ref, b2_ref, w3t_ref, fr_ref, dl_ref, o_ref, *, length, blk):
    k = pl.program_id(0)
    q = (lax.broadcasted_iota(jnp.int32, (1, blk), 1) + k * blk)
    pos = jnp.abs(q - length).astype(jnp.float32)
    t = pos / float(length - 1)
    w = (2.0 * math.pi / length) * pos
    band = lax.broadcasted_iota(jnp.int32, (HY_BANDS, 1), 0).astype(jnp.float32)
    fb = 1e-4 + band * ((HY_BANDS - 1 - 1e-4) / (HY_BANDS - 1))
    z = jnp.concatenate([jnp.broadcast_to(t, (8, blk)), jnp.cos(fb * w), -jnp.sin(fb * w),
                         jnp.zeros((HY_FEAT_ROWS - 8 - 2 * HY_BANDS, blk), jnp.float32)], axis=0)
    h = jnp.sin(fr_ref[...] * (_dot3(w1t_ref[...], z) + b1_ref[...]))
    h = jnp.sin(fr_ref[...] * (_dot3(w2t_ref[...], h) + b2_ref[...]))
    kt = _dot3(w3t_ref[0], h)
    o_ref[0] = kt * jnp.exp(-t * dl_ref[...])


def _hy_filter(length, blk, f_w1, f_b1, f_w2, f_b2, f_w3, f_freq):
    d = f_w3.shape[1] // 2
    nk = 2 * length // blk
    w1t = jnp.concatenate([f_w1[0:1].T, jnp.zeros((HY_HIDDEN, 7), jnp.float32), f_w1[1:].T,
                           jnp.zeros((HY_HIDDEN, HY_FEAT_ROWS - 8 - 2 * HY_BANDS), jnp.float32)], axis=1)
    w3t = jnp.stack([f_w3[:, d:].T, f_w3[:, :d].T])
    deltas = jnp.abs(jnp.linspace(HY_MIN_DECAY, HY_MAX_DECAY, d, dtype=jnp.float32)).reshape(d, 1)
    col = lambda v: v.reshape(HY_HIDDEN, 1)
    full2 = lambda k: (0, 0)
    half = length // blk
    return pl.pallas_call(
        functools.partial(_hy_filter_kernel, length=length, blk=blk),
        grid=(nk,),
        in_specs=[pl.BlockSpec((HY_HIDDEN, HY_FEAT_ROWS), full2), pl.BlockSpec((HY_HIDDEN, 1), full2),
                  pl.BlockSpec((HY_HIDDEN, HY_HIDDEN), full2), pl.BlockSpec((HY_HIDDEN, 1), full2),
                  pl.BlockSpec((1, d, HY_HIDDEN), lambda k: (k // half, 0, 0)),
                  pl.BlockSpec((HY_HIDDEN, 1), full2), pl.BlockSpec((d, 1), full2)],
        out_specs=pl.BlockSpec((1, d, blk), lambda k: (k, 0, 0)),
        out_shape=jax.ShapeDtypeStruct((nk, d, blk), jnp.float32),
        compiler_params=pltpu.CompilerParams(
            dimension_semantics=("parallel",), vmem_limit_bytes=VMEM_LIMIT_BYTES),
        name="hy_filter",
    )(w1t, col(f_b1), f_w2.T, col(f_b2), w3t, col(f_freq), deltas)


def _hy_gspec_kernel(hi_ref, lo_ref, ft_ref, fb_ref, o_ref):
    o_ref[0] = _dot3(hi_ref[0], ft_ref[...]) + _dot3(lo_ref[0], fb_ref[...])


def _hy_gspec(kt):
    nk, d, blk = kt.shape
    cos, sin = _odd_dft_tables(2 * blk)
    top = np.concatenate([cos[:blk], -sin[:blk]], axis=1)
    bot = -np.concatenate([cos[blk:], -sin[blk:]], axis=1)
    bot[0] = 0.0
    tm = 512
    full2 = lambda e, i: (0, 0)
    return pl.pallas_call(
        _hy_gspec_kernel,
        grid=(nk - 1, d // tm),
        in_specs=[pl.BlockSpec((1, tm, blk), lambda e, i: (e + 1, i, 0)),
                  pl.BlockSpec((1, tm, blk), lambda e, i: (e, i, 0)),
                  pl.BlockSpec((blk, 2 * blk), full2), pl.BlockSpec((blk, 2 * blk), full2)],
        out_specs=pl.BlockSpec((1, tm, 2 * blk), lambda e, i: (e, i, 0)),
        out_shape=jax.ShapeDtypeStruct((nk - 1, d, 2 * blk), jnp.float32),
        compiler_params=pltpu.CompilerParams(
            dimension_semantics=("parallel", "parallel"), vmem_limit_bytes=VMEM_LIMIT_BYTES),
        name="hy_gspec",
    )(kt, kt, jnp.asarray(top, jnp.float32), jnp.asarray(bot, jnp.float32))


def _hy_conv_kernel(x0_ref, x1_ref, v_ref, g_ref, fb_ref, ff_ref, fi_ref, o_ref, lhs_ref, u_ref, y_ref, *, nb):
    cc, bsz = HY_CC, ff_ref.shape[0]
    mrows = HY_MAC_ELEMS // bsz
    for j in range(nb):
        sl = slice(j * bsz, (j + 1) * bsz)
        wj = v_ref[0, sl, :] * x1_ref[0, sl, :]
        lhs_ref[j * cc:(j + 1) * cc, :] = wj.T.astype(jnp.bfloat16)
    u_ref[...] = jnp.dot(lhs_ref[...], ff_ref[...], preferred_element_type=jnp.float32)

    def per_out_block(i, carry):
        def per_rows(rc, carry2):
            rows = pl.ds(pl.multiple_of(rc * mrows, mrows), mrows)
            acc_r = jnp.zeros((mrows, bsz), jnp.float32)
            acc_i = jnp.zeros((mrows, bsz), jnp.float32)
            for j in range(nb):
                e = i - j + (nb - 1)
                gr = g_ref[e, rows, 0:bsz]
                gi = g_ref[e, rows, bsz:2 * bsz]
                urows = pl.ds(pl.multiple_of(j * cc + rc * mrows, mrows), mrows)
                ur = u_ref[urows, 0:bsz]
                ui = u_ref[urows, bsz:2 * bsz]
                acc_r = acc_r + gr * ur - gi * ui
                acc_i = acc_i + gr * ui + gi * ur
            yrows = pl.ds(pl.multiple_of(i * cc + rc * mrows, mrows), mrows)
            y_ref[yrows, 0:bsz] = acc_r.astype(jnp.bfloat16)
            y_ref[yrows, bsz:2 * bsz] = acc_i.astype(jnp.bfloat16)
            return carry2
        return lax.fori_loop(0, cc // mrows, per_rows, carry)
    lax.fori_loop(0, nb, per_out_block, 0)

    yt = jnp.dot(y_ref[...], fi_ref[...], preferred_element_type=jnp.float32)
    for i in range(nb):
        sl = slice(i * bsz, (i + 1) * bsz)
        w = v_ref[0, sl, :] * x1_ref[0, sl, :]
        gated = (yt[i * cc:(i + 1) * cc, :].T + fb_ref[...] * w) * x0_ref[0, sl, :]
        o_ref[0, sl, :] = gated.astype(o_ref.dtype)


def _hy_conv(u, g, f_bias, blk):
    b, length, d3 = u.shape
    d = d3 // 3
    nb = length // blk
    ncb = d // HY_CC
    cos, sin = _odd_dft_tables(2 * blk)
    fwd = np.concatenate([cos[:blk], -sin[:blk]], axis=1)
    inv = (1.0 / blk) * np.concatenate([cos[:blk].T, -sin[:blk].T], axis=0)
    col = lambda off: pl.BlockSpec((1, length, HY_CC), lambda c, i, off=off: (i, 0, off + c))
    full2 = lambda c, i: (0, 0)
    return pl.pallas_call(
        functools.partial(_hy_conv_kernel, nb=nb),
        grid=(ncb, b),
        in_specs=[col(0), col(ncb), col(2 * ncb),
                  pl.BlockSpec((2 * nb - 1, HY_CC, 2 * blk), lambda c, i: (0, c, 0)),
                  pl.BlockSpec((1, HY_CC), lambda c, i: (0, c)),
                  pl.BlockSpec((blk, 2 * blk), full2), pl.BlockSpec((2 * blk, blk), full2)],
        out_specs=pl.BlockSpec((1, length, HY_CC), lambda c, i: (i, 0, c)),
        out_shape=jax.ShapeDtypeStruct((b, length, d), jnp.bfloat16),
        scratch_shapes=[pltpu.VMEM((nb * HY_CC, blk), jnp.bfloat16),
                        pltpu.VMEM((nb * HY_CC, 2 * blk), jnp.float32),
                        pltpu.VMEM((nb * HY_CC, 2 * blk), jnp.bfloat16)],
        compiler_params=pltpu.CompilerParams(
            dimension_semantics=("parallel", "arbitrary"), vmem_limit_bytes=56 * 1024 * 1024),
        name="hy_conv",
    )(u, u, u, g, f_bias.reshape(1, d), jnp.asarray(fwd, jnp.bfloat16), jnp.asarray(inv, jnp.bfloat16))


def _hyena_mixer_p(x, ng, sh, sc, in_w, in_b, short_w, short_b, f_w1, f_b1, f_w2, f_b2, f_w3, f_freq, f_bias, out_w):
    length = x.shape[1]
    blk = min(HY_MAX_BLOCK, length)
    u = _hy_in(x, ng, sh, sc, in_w, in_b, short_w, short_b)
    g = _hy_gspec(_hy_filter(length, blk, f_w1, f_b1, f_w2, f_b2, f_w3, f_freq))
    return _mm3(_hy_conv(u, g, f_bias, blk), out_w)


TOK_TILE = 256
MOE_ROWS = 512
SEG_CHUNK = 64
BF16_TILE_ROWS = 16
LANES = 128


def _split_bf16(w):
    hi = w.astype(jnp.bfloat16)
    lo = (w - hi.astype(jnp.float32)).astype(jnp.bfloat16)
    return hi, lo


def _moe_pre_kernel(x_ref, m_ref, g1_ref, ng_ref, sh_ref, sc_ref, wrh_ref, wrl_ref,
                    xo_ref, hpk_ref, lg_ref):
    x = x_ref[0] + g1_ref[0] * m_ref[0]
    xo_ref[0] = x
    ms = jnp.mean(x * x, axis=-1, keepdims=True)
    h = x * lax.rsqrt(ms + RMS_EPS) * ng_ref[...]
    h = h * (1.0 + sc_ref[0]) + sh_ref[0]
    h_hi = h.astype(jnp.bfloat16)
    h_lo = (h - h_hi.astype(jnp.float32)).astype(jnp.bfloat16)
    dn = (((1,), (1,)), ((), ()))
    lg = lax.dot_general(wrh_ref[...], h_hi, dn, preferred_element_type=jnp.float32)
    lg += lax.dot_general(wrh_ref[...], h_lo, dn, preferred_element_type=jnp.float32)
    lg += lax.dot_general(wrl_ref[...], h_hi, dn, preferred_element_type=jnp.float32)
    lg_ref[0] = lg
    half = h.shape[1] // 2
    wa = pltpu.bitcast(h_hi[:, :half].astype(jnp.float32), jnp.uint32) >> 16
    wb = pltpu.bitcast(h_hi[:, half:].astype(jnp.float32), jnp.uint32) & jnp.uint32(0xFFFF0000)
    hpk_ref[0] = wa | wb


def _moe_pre(x, m, g1, ng, sh, sc, w_router):
    b, length, d = x.shape
    tm = min(length, 512)
    wrh, wrl = _split_bf16(w_router.T)
    row = lambda i, j: (i, j, 0)
    per_b = lambda i, j: (i, 0, 0)
    full2 = lambda i, j: (0, 0)
    return pl.pallas_call(
        _moe_pre_kernel,
        grid=(b, length // tm),
        in_specs=[pl.BlockSpec((1, tm, d), row), pl.BlockSpec((1, tm, d), row),
                  pl.BlockSpec((1, 1, d), per_b), pl.BlockSpec((1, d), full2),
                  pl.BlockSpec((1, 1, d), per_b), pl.BlockSpec((1, 1, d), per_b),
                  pl.BlockSpec((N_EXPERTS, d), full2), pl.BlockSpec((N_EXPERTS, d), full2)],
        out_specs=[pl.BlockSpec((1, tm, d), row), pl.BlockSpec((1, tm, d // 2), row),
                   pl.BlockSpec((1, N_EXPERTS, tm), lambda i, j: (i, 0, j))],
        out_shape=[jax.ShapeDtypeStruct((b, length, d), jnp.float32),
                   jax.ShapeDtypeStruct((b, length, d // 2), jnp.uint32),
                   jax.ShapeDtypeStruct((b, N_EXPERTS, length), jnp.float32)],
        compiler_params=pltpu.CompilerParams(
            dimension_semantics=("parallel", "parallel"), vmem_limit_bytes=VMEM_LIMIT_BYTES),
        name="moe_pre",
    )(x, m, g1, ng.reshape(1, d), sh, sc, wrh, wrl)


SEL_BLOCK = 256
SEL_ROWS = 64
F32_INF_BITS = 0x7F800000


def _prefix_counts(flags, inclusive):
    e, length = flags.shape
    r = lax.broadcasted_iota(jnp.int32, (SEL_BLOCK, SEL_BLOCK), 0)
    c = lax.broadcasted_iota(jnp.int32, (SEL_BLOCK, SEL_BLOCK), 1)
    tri = ((r <= c) if inclusive else (r < c)).astype(jnp.bfloat16)
    off = jnp.zeros((e, 1), jnp.float32)
    blocks = []
    for k in range(length // SEL_BLOCK):
        blk = flags[:, k * SEL_BLOCK:(k + 1) * SEL_BLOCK]
        blocks.append(jnp.dot(blk.astype(jnp.bfloat16), tri, preferred_element_type=jnp.float32) + off)
        off = off + jnp.sum(blk, axis=1, keepdims=True)
    return jnp.concatenate(blocks, axis=1)


def _moe_select_kernel(lg_ref, aff_ref, idx_ref, rank_ref, cnt_ref, *, cap):
    lg = lg_ref[0]
    ne, length = lg.shape
    ex = jnp.exp(lg - jnp.max(lg, axis=0, keepdims=True))
    aff = ex / jnp.sum(ex, axis=0, keepdims=True)
    aff_ref[0] = aff
    bits = pltpu.bitcast(aff, jnp.int32)

    def bisect(_, carry):
        lo, hi = carry
        mid = lo + ((hi - lo + 1) >> 1)
        cnt = jnp.sum((bits >= mid).astype(jnp.float32), axis=1, keepdims=True)
        ok = cnt >= cap
        return jnp.where(ok, mid, lo), jnp.where(ok, hi, mid - 1)
    lo0 = jnp.zeros((ne, 1), jnp.int32)
    tau, _ = lax.fori_loop(0, 32, bisect, (lo0, lo0 + F32_INF_BITS))
    gt = bits > tau
    eq = (bits == tau).astype(jnp.float32)
    need = cap - jnp.sum(gt.astype(jnp.float32), axis=1, keepdims=True)
    keep = jnp.where(gt, 1.0, jnp.where(_prefix_counts(eq, False) < need, eq, 0.0))
    rank_ref[...] = _prefix_counts(keep, True)

    rows = min(SEL_ROWS, cap)
    lane = lax.broadcasted_iota(jnp.int32, (cap, LANES), 1)
    cnt_ref[...] = jnp.zeros((cap, LANES), jnp.float32)

    def per_expert(ei, carry):
        rk = rank_ref[pl.ds(ei, 1), :]
        cols = []
        for ck in range(cap // rows):
            slot = (lax.broadcasted_iota(jnp.int32, (rows, LANES), 0) + ck * rows).astype(jnp.float32)
            acc = jnp.zeros((rows, LANES), jnp.float32)
            for j in range(length // LANES):
                acc = acc + jnp.where(rk[:, j * LANES:(j + 1) * LANES] <= slot, 1.0, 0.0)
            cols.append(jnp.sum(acc, axis=1, keepdims=True))
        col = jnp.concatenate(cols, axis=0)
        cnt_ref[...] = jnp.where(lane == ei, col, cnt_ref[...])
        return carry
    lax.fori_loop(0, ne, per_expert, 0)
    idx_ref[0] = cnt_ref[:, :ne].astype(jnp.int32)


def _moe_select(lg, cap):
    b, ne, length = lg.shape
    aff, idx = pl.pallas_call(
        functools.partial(_moe_select_kernel, cap=cap),
        grid=(b,),
        in_specs=[pl.BlockSpec((1, ne, length), lambda i: (i, 0, 0))],
        out_specs=[pl.BlockSpec((1, ne, length), lambda i: (i, 0, 0)),
                   pl.BlockSpec((1, cap, ne), lambda i: (i, 0, 0))],
        out_shape=[jax.ShapeDtypeStruct((b, ne, length), jnp.float32),
                   jax.ShapeDtypeStruct((b, cap, ne), jnp.int32)],
        scratch_shapes=[pltpu.VMEM((ne, length), jnp.float32), pltpu.VMEM((cap, LANES), jnp.float32)],
        compiler_params=pltpu.CompilerParams(
            dimension_semantics=("parallel",), vmem_limit_bytes=VMEM_LIMIT_BYTES),
        name="moe_select",
    )(lg)
    return aff, jnp.swapaxes(idx, 1, 2)


def _moe_ffn_kernel(idx_ref, nidx_ref, h_hbm, gate_ref, wg_ref, wu_ref, wd_ref, y_ref,
                    xe_ref, wgb, wub, wdb, sem):
    nblk = pl.num_programs(1)
    step = pl.program_id(0) * nblk + pl.program_id(1)
    last = pl.num_programs(0) * nblk - 1

    def issue(ids_ref, slot):
        base = slot * MOE_ROWS
        for c in range(MOE_ROWS):
            pltpu.make_async_copy(h_hbm.at[pl.ds(ids_ref[0, 0, c], 1)], xe_ref.at[pl.ds(base + c, 1)],
                                  sem.at[slot]).start()

    @pl.when(step == 0)
    def _():
        issue(idx_ref, 0)

    for parity in range(2):
        @pl.when((step < last) & (step % 2 == parity))
        def _(parity=parity):
            issue(nidx_ref, 1 - parity)

    @pl.when(pl.program_id(1) == 0)
    def _():
        wgb[...] = wg_ref[0, 0].astype(jnp.bfloat16)
        wub[...] = wu_ref[0, 0].astype(jnp.bfloat16)
        wdb[...] = wd_ref[0, 0].astype(jnp.bfloat16)

    slot = step % 2
    rows = pl.ds(pl.multiple_of(slot * MOE_ROWS, MOE_ROWS), MOE_ROWS)
    pltpu.make_async_copy(h_hbm.at[pl.ds(0, MOE_ROWS)], xe_ref.at[rows], sem.at[slot]).wait()
    half = wgb.shape[0] // 2
    w = xe_ref[rows, :]
    xa = pltpu.bitcast(w << 16, jnp.float32).astype(jnp.bfloat16)
    xb = pltpu.bitcast(w & jnp.uint32(0xFFFF0000), jnp.float32).astype(jnp.bfloat16)
    hg = jnp.dot(xa, wgb[:half], preferred_element_type=jnp.float32)
    hg += jnp.dot(xb, wgb[half:], preferred_element_type=jnp.float32)
    hu = jnp.dot(xa, wub[:half], preferred_element_type=jnp.float32)
    hu += jnp.dot(xb, wub[half:], preferred_element_type=jnp.float32)
    hid = (hg * jax.nn.sigmoid(hg) * hu).astype(jnp.bfloat16)
    y = jnp.dot(hid, wdb[...], preferred_element_type=jnp.float32)
    y_ref[0] = (y * gate_ref[0]).astype(jnp.bfloat16)


def _moe_ffn(hpk, grow, gate, w_gate, w_up, w_down, layer):
    e, r = grow.shape
    d, f = w_gate.shape[2], w_gate.shape[3]
    nblk = r // MOE_ROWS
    nsteps = e * nblk
    wspec = lambda shp: pl.BlockSpec((1, 1) + shp, lambda i, j: (layer, i, 0, 0))
    ids = grow.reshape(nsteps, 1, MOE_ROWS)
    smem_ids = lambda off: pl.BlockSpec(
        (1, 1, MOE_ROWS), lambda i, j: (jnp.minimum(i * nblk + j + off, nsteps - 1), 0, 0), memory_space=pltpu.SMEM)
    return pl.pallas_call(
        _moe_ffn_kernel,
        grid=(e, nblk),
        in_specs=[smem_ids(0), smem_ids(1),
                  pl.BlockSpec(memory_space=pltpu.HBM),
                  pl.BlockSpec((1, MOE_ROWS, 1), lambda i, j: (i, j, 0)),
                  wspec((d, f)), wspec((d, f)), wspec((f, d))],
        out_specs=pl.BlockSpec((1, MOE_ROWS, d), lambda i, j: (i, j, 0)),
        out_shape=jax.ShapeDtypeStruct((e, r, d), jnp.bfloat16),
        scratch_shapes=[pltpu.VMEM((2 * MOE_ROWS, d // 2), jnp.uint32),
                        pltpu.VMEM((d, f), jnp.bfloat16), pltpu.VMEM((d, f), jnp.bfloat16),
                        pltpu.VMEM((f, d), jnp.bfloat16),
                        pltpu.SemaphoreType.DMA((2,))],
        compiler_params=pltpu.CompilerParams(
            dimension_semantics=("arbitrary", "arbitrary"), vmem_limit_bytes=VMEM_LIMIT_BYTES),
        name="moe_ffn",
    )(ids, ids, hpk, gate, w_gate, w_up, w_down)


def _moe_comb_kernel(cs_ref, x_ref, g2_ref, y_ref, idx_ref, fg_ref, o_ref, ycat, acc, *, cap, ch, ntile, final_norm):
    b = pl.program_id(0)
    t = pl.program_id(1)
    base = t * TOK_TILE
    sub = lax.broadcasted_iota(jnp.int32, (TOK_TILE, LANES), 0) + base
    if ntile == 1:
        for e in range(N_EXPERTS):
            ycat[e * ch:(e + 1) * ch, :] = y_ref[e, 0:ch, :]
        v = idx_ref[0]
        tiles = [(v[:, p * LANES:(p + 1) * LANES] == sub).astype(jnp.bfloat16)
                 for p in range(N_EXPERTS * ch // LANES)]
        acc[...] = jnp.dot(jnp.concatenate(tiles, axis=1), ycat[...], preferred_element_type=jnp.float32)
    else:
        lane = lax.broadcasted_iota(jnp.int32, (1, LANES), 1)
        per = LANES // ch
        sts = []
        for e in range(N_EXPERTS):
            s0 = cs_ref[(b * N_EXPERTS + e) * (ntile + 1) + t]
            st = jnp.minimum((s0 // BF16_TILE_ROWS) * BF16_TILE_ROWS, cap - ch)
            st = pl.multiple_of(st, BF16_TILE_ROWS)
            sts.append(st)
            ycat[e * ch:(e + 1) * ch, :] = y_ref[e, pl.ds(st, ch), :]
        tiles = []
        for p in range(N_EXPERTS // per):
            v = None
            for q in range(per):
                e = p * per + q
                r = pltpu.roll(idx_ref[0, e:e + 1, :], (2 * cap - sts[e] + q * ch) % cap, 1)[:, :LANES]
                v = r if v is None else jnp.where(lane >= q * ch, r, v)
            tiles.append((v == sub).astype(jnp.bfloat16))
        acc[...] = jnp.dot(jnp.concatenate(tiles, axis=1), ycat[...], preferred_element_type=jnp.float32)
        sub_c = lax.broadcasted_iota(jnp.int32, (TOK_TILE, ch), 0) + base
        lane_c = lax.broadcasted_iota(jnp.int32, (1, ch), 1)
        for e in range(N_EXPERTS):
            s1 = cs_ref[(b * N_EXPERTS + e) * (ntile + 1) + t + 1]
            first_end = sts[e] + ch
            n_extra = jnp.maximum(s1 - first_end + ch - 1, 0) // ch

            def extra(q, carry, e=e, first_end=first_end):
                lo = first_end + q * ch
                stq = pl.multiple_of(jnp.minimum(lo, cap - ch), BF16_TILE_ROWS)
                r = pltpu.roll(idx_ref[0, e:e + 1, :], (2 * cap - stq) % cap, 1)[:, :ch]
                hit = (r == sub_c) & (lane_c + stq >= lo)
                acc[...] += jnp.dot(hit.astype(jnp.bfloat16), y_ref[e, pl.ds(stq, ch), :],
                                    preferred_element_type=jnp.float32)
                return carry
            lax.fori_loop(0, n_extra, extra, 0)
    out = x_ref[0] + g2_ref[0] * acc[...]
    if final_norm:
        ms = jnp.mean(out * out, axis=-1, keepdims=True)
        out = out * lax.rsqrt(ms + RMS_EPS) * fg_ref[...]
    o_ref[0] = out


def _moe_combine(x, g2, y, idx, cs, final_g=None):
    b, length, d = x.shape
    final_norm = final_g is not None
    fg = (final_g if final_norm else jnp.ones((d,), jnp.float32)).reshape(1, d)
    cap = idx.shape[2]
    ntile = length // TOK_TILE
    ch = min(SEG_CHUNK, cap)
    if ntile == 1:
        idx_in = idx.reshape(b, 1, N_EXPERTS * cap)
        idx_spec = pl.BlockSpec((1, 1, N_EXPERTS * cap), lambda i, j, c: (i, 0, 0))
    else:
        idx_in = idx
        idx_spec = pl.BlockSpec((1, N_EXPERTS, cap), lambda i, j, c: (i, 0, 0))
    grid_spec = pltpu.PrefetchScalarGridSpec(
        num_scalar_prefetch=1,
        grid=(b, ntile),
        in_specs=[pl.BlockSpec((1, TOK_TILE, d), lambda i, j, c: (i, j, 0)),
                  pl.BlockSpec((1, 1, d), lambda i, j, c: (i, 0, 0)),
                  pl.BlockSpec((N_EXPERTS, cap, d), lambda i, j, c: (0, i, 0)),
                  idx_spec,
                  pl.BlockSpec((1, d), lambda i, j, c: (0, 0))],
        out_specs=pl.BlockSpec((1, TOK_TILE, d), lambda i, j, c: (i, j, 0)),
        scratch_shapes=[pltpu.VMEM((N_EXPERTS * ch, d), jnp.bfloat16),
                        pltpu.VMEM((TOK_TILE, d), jnp.float32)])
    return pl.pallas_call(
        functools.partial(_moe_comb_kernel, cap=cap, ch=ch, ntile=ntile, final_norm=final_norm),
        grid_spec=grid_spec,
        out_shape=jax.ShapeDtypeStruct((b, length, d), jnp.float32),
        compiler_params=pltpu.CompilerParams(
            dimension_semantics=("arbitrary", "arbitrary"), vmem_limit_bytes=56 * 1024 * 1024),
        name="moe_combine",
    )(cs.reshape(-1).astype(jnp.int32), x, g2, y, idx_in, fg)


def _moe_block(x, m, g1, ng, sh, sc, g2, w_router, w_gate, w_up, w_down, layer, final_g=None):
    b, length, d = x.shape
    cap = EC_FACTOR * length // N_EXPERTS
    x1, hpk, lg = _moe_pre(x, m, g1, ng, sh, sc, w_router)
    aff, idx = _moe_select(lg, cap)
    gate = jnp.take_along_axis(aff, idx, axis=-1)
    ntile = length // TOK_TILE
    bounds = jnp.arange(ntile + 1, dtype=jnp.int32) * TOK_TILE
    cs = jnp.sum(idx[:, :, :, None] < bounds, axis=2, dtype=jnp.int32)
    grow = idx + (jnp.arange(b, dtype=jnp.int32) * length)[:, None, None]
    grow = jnp.swapaxes(grow, 0, 1).reshape(N_EXPERTS, b * cap)
    gate_e = jnp.swapaxes(gate, 0, 1).reshape(N_EXPERTS, b * cap, 1)
    y = _moe_ffn(hpk.reshape(b * length, d // 2), grow, gate_e, w_gate, w_up, w_down, layer)
    return _moe_combine(x1, g2, y, idx, cs, final_g)


def kernel(x_prompt, x_sample, state_ssd, c, c_ctx, norm_g, ada_w, ada_b, hy_in_w, hy_in_b, hy_short_w, hy_short_b, hy_f_w1, hy_f_b1, hy_f_w2, hy_f_b2, hy_f_w3, hy_f_freq, hy_f_bias, hy_out_w, ssd_in_w, ssd_conv_w, ssd_conv_b, ssd_dt_bias, ssd_A_log, ssd_D, ssd_norm_g, ssd_out_w, moe_router, moe_w_gate, moe_w_up, moe_w_down, final_norm_g):
    rows = x_sample.shape[1] // GRID_W
    xp = x_prompt
    xs = x_sample + _sincos_2d(rows, GRID_W, D_MODEL)[None]
    new_ssd = []
    for i in range(DEPTH):
        nb_s = c.shape[0]
        mods = _adaln(jnp.concatenate([c, c_ctx[None, :]], axis=0), ada_w[i], ada_b[i])
        sh1s, sc1s, g1s, sh2s, sc2s, g2s = [m[:nb_s] for m in mods]
        sh1p, sc1p, g1p, sh2p, sc2p, g2p = [m[nb_s:] for m in mods]
        j = i // N_MIXERS
        bp = (xp.shape[0], 1, D_MODEL)
        if i % N_MIXERS == 0:
            hy = (hy_in_w[j], hy_in_b[j], hy_short_w[j], hy_short_b[j], hy_f_w1[j], hy_f_b1[j],
                  hy_f_w2[j], hy_f_b2[j], hy_f_w3[j], hy_f_freq[j], hy_f_bias[j], hy_out_w[j])
            mp = _hyena_mixer_p(xp, norm_g[i, 0], jnp.broadcast_to(sh1p, bp), jnp.broadcast_to(sc1p, bp), *hy)
            ms = _hyena_mixer_p(xs, norm_g[i, 0], sh1s, sc1s, *hy)
        else:
            sp = (ssd_in_w[j], ssd_conv_w[j], ssd_conv_b[j], ssd_dt_bias[j], ssd_A_log[j],
                  ssd_D[j], ssd_norm_g[j], ssd_out_w[j])
            zeros = jnp.zeros((xp.shape[0], SSD_HEADS, SSD_HEAD_DIM, SSD_STATE), jnp.float32)
            mp, s_f, s_b = _ssd_mixer_p(xp, norm_g[i, 0], jnp.broadcast_to(sh1p, bp),
                                        jnp.broadcast_to(sc1p, bp), zeros, zeros, *sp)
            new_ssd.append(jnp.stack([s_f, s_b], axis=1))
            ms, _, _ = _ssd_mixer_p(xs, norm_g[i, 0], sh1s, sc1s, state_ssd[:, j, 0], state_ssd[:, j, 1], *sp)
        moe = (moe_router[i], moe_w_gate, moe_w_up, moe_w_down, i, final_norm_g if i == DEPTH - 1 else None)
        xp = _moe_block(xp, mp, jnp.broadcast_to(g1p, bp), norm_g[i, 1], jnp.broadcast_to(sh2p, bp),
                        jnp.broadcast_to(sc2p, bp), jnp.broadcast_to(g2p, bp), *moe)
        xs = _moe_block(xs, ms, g1s, norm_g[i, 1], sh2s, sc2s, g2s, *moe)
    new_state_ssd = jnp.stack(new_ssd, axis=1)
    return (xp, xs, new_state_ssd)
```

```python
import functools
import math

import jax
import jax.numpy as jnp
import numpy as np
from jax import lax
from jax.experimental import pallas as pl
from jax.experimental.pallas import tpu as pltpu

D_MODEL = 1024
DEPTH = 2
GRID_W = 64
N_MIXERS = 2
RMS_EPS = 1e-6
HY_EMB = 33
HY_BANDS = (HY_EMB - 1) // 2
HY_SHORT_DECAY_FRAC = 0.3
HY_LONG_DECAY_FRAC = 1.5
HY_DECAY_TARGET = 1e-2
HY_MAX_DECAY = math.log(HY_DECAY_TARGET) / HY_SHORT_DECAY_FRAC
HY_MIN_DECAY = math.log(HY_DECAY_TARGET) / HY_LONG_DECAY_FRAC
SSD_D_INNER = 2 * D_MODEL
SSD_HEAD_DIM = 64
SSD_HEADS = SSD_D_INNER // SSD_HEAD_DIM
SSD_GROUPS = 4
SSD_STATE = 128
SSD_CHUNK = 128
SSD_XBC = SSD_D_INNER + 2 * SSD_GROUPS * SSD_STATE
N_EXPERTS = 16
EC_FACTOR = 2

VMEM_LIMIT_BYTES = 48 * 1024 * 1024


def _mm_kernel(a_ref, b_ref, o_ref, acc_ref):
    @pl.when(pl.program_id(2) == 0)
    def _():
        acc_ref[...] = jnp.zeros_like(acc_ref)

    acc_ref[...] += jnp.dot(a_ref[...].astype(jnp.bfloat16), b_ref[...],
                            preferred_element_type=jnp.float32)

    @pl.when(pl.program_id(2) == pl.num_programs(2) - 1)
    def _():
        o_ref[...] = acc_ref[...]


def _pick(n, pref):
    for t in pref:
        if n % t == 0:
            return t
    return n


def _mm(a, b):
    m, k = a.shape
    n = b.shape[1]
    mp = -(-m // 8) * 8
    if mp != m:
        a = jnp.pad(a, ((0, mp - m), (0, 0)))
    tm = _pick(mp, (512, 256, 128, 64, 32, 16, 8))
    tn = _pick(n, (512, 256, 128))
    tk = _pick(k, (1024, 512, 256, 128))
    out = pl.pallas_call(
        _mm_kernel,
        grid=(mp // tm, n // tn, k // tk),
        in_specs=[pl.BlockSpec((tm, tk), lambda i, j, l: (i, l)),
                  pl.BlockSpec((tk, tn), lambda i, j, l: (l, j))],
        out_specs=pl.BlockSpec((tm, tn), lambda i, j, l: (i, j)),
        out_shape=jax.ShapeDtypeStruct((mp, n), jnp.float32),
        scratch_shapes=[pltpu.VMEM((tm, tn), jnp.float32)],
        compiler_params=pltpu.CompilerParams(
            dimension_semantics=("parallel", "parallel", "arbitrary"),
            vmem_limit_bytes=VMEM_LIMIT_BYTES),
        name="mm",
    )(a, b.astype(jnp.bfloat16))
    return out[:m]


def _mm3(a, b):
    lead = a.shape[:-1]
    return _mm(a.reshape(-1, a.shape[-1]), b).reshape(*lead, b.shape[1])


def _adaln(cond, ada_w, ada_b):
    m = _mm(jax.nn.silu(cond), ada_w) + ada_b
    return jnp.split(m[:, None, :], 6, axis=-1)


def _sincos_2d(rows, cols, d):
    q = d // 4
    omega = 1.0 / (10000.0 ** (jnp.arange(q, dtype=jnp.float32) / q))
    t = jnp.arange(rows * cols)
    er = (t // cols).astype(jnp.float32)[:, None] * omega[None, :]
    ec = (t % cols).astype(jnp.float32)[:, None] * omega[None, :]
    return jnp.concatenate([jnp.sin(er), jnp.cos(er), jnp.sin(ec), jnp.cos(ec)], axis=-1)


SSD_GN = SSD_GROUPS * SSD_STATE
SSD_GROUP_W = SSD_D_INNER // SSD_GROUPS
SSD_HEADS_PER_GROUP = SSD_HEADS // SSD_GROUPS
ROW_TILE = 256


def _modnorm(x, ng, sh, sc):
    ms = jnp.mean(x * x, axis=-1, keepdims=True)
    return (x * lax.rsqrt(ms + RMS_EPS) * ng) * (1.0 + sc) + sh


HALO = 8
CONV_COLS = 512


def _halo_rows(xm_ref, xp_ref, xn_ref, ng_ref, sh_ref, sc_ref):
    j = pl.program_id(1)
    xa = jnp.concatenate([xp_ref[0], xm_ref[0], xn_ref[0]], axis=0)
    h = _modnorm(xa, ng_ref[...], sh_ref[0], sc_ref[0]).astype(jnp.bfloat16)
    tm = xm_ref.shape[1]
    r = lax.broadcasted_iota(jnp.int32, (tm + 2 * HALO, 1), 0)
    valid = ((r >= HALO) | (j > 0)) & ((r < tm + HALO) | (j < pl.num_programs(1) - 1))
    return h, valid


def _proj_conv(h, valid, w_ref, w_col0, pb_ref, cw_ref, cb_ref, o_ref, tm, silu):
    taps = cw_ref.shape[0]
    ncol = o_ref.shape[2]
    for c0 in range(0, ncol, CONV_COLS):
        cols = slice(c0, c0 + CONV_COLS)
        u = jnp.dot(h, w_ref[:, w_col0 + c0:w_col0 + c0 + CONV_COLS], preferred_element_type=jnp.float32)
        if pb_ref is not None:
            u = u + pb_ref[:, cols]
        u = jnp.where(valid, u, 0.0)
        nrow = u.shape[0]
        acc = cb_ref[:, cols] + jnp.zeros((tm, CONV_COLS), jnp.float32)
        for k in range(taps):
            shifted = u if k == taps // 2 else pltpu.roll(u, (taps // 2 - k) % nrow, 0)
            acc = acc + cw_ref[k:k + 1, cols] * shifted[HALO:HALO + tm]
        if silu:
            acc = acc * jax.nn.sigmoid(acc)
        o_ref[0, :, cols] = acc


def _halo_specs(length, tm, d):
    nh = length // HALO
    per = tm // HALO
    main = pl.BlockSpec((1, tm, d), lambda i, j: (i, j, 0))
    prev = pl.BlockSpec((1, HALO, d), lambda i, j: (i, jnp.maximum(j * per - 1, 0), 0))
    nxt = pl.BlockSpec((1, HALO, d), lambda i, j: (i, jnp.minimum((j + 1) * per, nh - 1), 0))
    return [main, prev, nxt]


def _ssd_in_kernel(xm_ref, xp_ref, xn_ref, ng_ref, sh_ref, sc_ref, w_ref, wdt_ref, wdtt_ref, cw_ref, cb_ref,
                   z_ref, xbc_ref, dt_ref, dtt_ref):
    tm = xm_ref.shape[1]
    h, valid = _halo_rows(xm_ref, xp_ref, xn_ref, ng_ref, sh_ref, sc_ref)
    hm = h[HALO:HALO + tm]
    z_ref[0] = jnp.dot(hm, w_ref[:, :SSD_D_INNER], preferred_element_type=jnp.float32).astype(jnp.bfloat16)
    _proj_conv(h, valid, w_ref, SSD_D_INNER, None, cw_ref, cb_ref, xbc_ref, tm, silu=True)
    dt_ref[0] = jnp.dot(hm, wdt_ref[...], preferred_element_type=jnp.float32)
    dtt_ref[0] = lax.dot_general(wdtt_ref[...], hm, (((1,), (1,)), ((), ())),
                                 preferred_element_type=jnp.float32)


def _ssd_in(x, ng, sh, sc, in_w, conv_w, conv_b):
    b, length, d = x.shape
    tm = min(length, ROW_TILE)
    nzx = SSD_D_INNER + SSD_XBC
    w = in_w[:, :nzx].astype(jnp.bfloat16)
    wdt = in_w[:, nzx:]
    wdt_p = jnp.pad(wdt, ((0, 0), (0, LANES - 2 * SSD_HEADS))).astype(jnp.bfloat16)
    wdt_t = wdt.T.astype(jnp.bfloat16)
    taps = conv_w.shape[0]
    row = lambda i, j: (i, j, 0)
    per_b = lambda i, j: (i, 0, 0)
    full2 = lambda i, j: (0, 0)
    return pl.pallas_call(
        _ssd_in_kernel,
        grid=(b, length // tm),
        in_specs=_halo_specs(length, tm, d) + [
            pl.BlockSpec((1, d), full2), pl.BlockSpec((1, 1, d), per_b), pl.BlockSpec((1, 1, d), per_b),
            pl.BlockSpec((d, nzx), full2), pl.BlockSpec((d, LANES), full2),
            pl.BlockSpec((2 * SSD_HEADS, d), full2),
            pl.BlockSpec((taps, SSD_XBC), full2), pl.BlockSpec((1, SSD_XBC), full2)],
        out_specs=[pl.BlockSpec((1, tm, SSD_D_INNER), row), pl.BlockSpec((1, tm, SSD_XBC), row),
                   pl.BlockSpec((1, tm, LANES), row),
                   pl.BlockSpec((1, 2 * SSD_HEADS, tm), lambda i, j: (i, 0, j))],
        out_shape=[jax.ShapeDtypeStruct((b, length, SSD_D_INNER), jnp.bfloat16),
                   jax.ShapeDtypeStruct((b, length, SSD_XBC), jnp.float32),
                   jax.ShapeDtypeStruct((b, length, LANES), jnp.float32),
                   jax.ShapeDtypeStruct((b, 2 * SSD_HEADS, length), jnp.float32)],
        compiler_params=pltpu.CompilerParams(
            dimension_semantics=("parallel", "parallel"), vmem_limit_bytes=56 * 1024 * 1024),
        name="ssd_in",
    )(x, x, x, ng.reshape(1, d), sh, sc, w, wdt_p, wdt_t, conv_w, conv_b.reshape(1, SSD_XBC))


def _split3_bf16(v):
    p1 = v.astype(jnp.bfloat16)
    r1 = v - p1.astype(jnp.float32)
    p2 = r1.astype(jnp.bfloat16)
    p3 = (r1 - p2.astype(jnp.float32)).astype(jnp.bfloat16)
    return p1, p2, p3


def _softplus(v):
    return jnp.maximum(v, 0.0) + jnp.log1p(jnp.exp(-jnp.abs(v)))


def _expand_heads(cols, g, hoff):
    q = cols.shape[0]
    lane = lax.broadcasted_iota(jnp.int32, (q, LANES), 1)
    tiles = []
    for k in range(SSD_HEADS_PER_GROUP // 2):
        ha = hoff + g * SSD_HEADS_PER_GROUP + 2 * k
        tiles.append(jnp.take_along_axis(cols, jnp.where(lane < SSD_HEAD_DIM, ha, ha + 1), axis=1))
    return jnp.concatenate(tiles, axis=1)


def _ssd_scan_kernel(x_ref, b_ref, c_ref, dt_ref, dtt_ref, dtb_ref, dtbt_ref, a_ref, at_ref, init_ref, extra_ref,
                     y_ref, fin_ref, st_ref, *, reverse, hoff, add_prev):
    ci = pl.program_id(1)

    @pl.when(ci == 0)
    def _():
        st_ref[...] = init_ref[0]

    q = SSD_CHUNK
    f32, bf16 = jnp.float32, jnp.bfloat16
    dt = _softplus(dt_ref[0] + dtb_ref[...])
    dtt = _softplus(dtt_ref[0][hoff:hoff + SSD_HEADS, :] + dtbt_ref[...])
    ri = lax.broadcasted_iota(jnp.int32, (q, q), 0)
    cj = lax.broadcasted_iota(jnp.int32, (q, q), 1)
    keep = (cj >= ri) if reverse else (cj <= ri)
    tri = keep.astype(bf16)
    tri_t = ((ri >= cj) if reverse else (ri <= cj)).astype(bf16)
    acum = sum(jnp.dot(tri, p, preferred_element_type=f32) for p in _split3_bf16(dt * a_ref[...]))
    acum_t = sum(jnp.dot(p, tri_t, preferred_element_type=f32) for p in _split3_bf16(dtt * at_ref[...]))
    end = 0 if reverse else q - 1
    a_end = acum[end:end + 1, :]
    eacum = jnp.exp(acum)
    dt_dec_end = dt * jnp.exp(a_end - acum)
    lane = lax.broadcasted_iota(jnp.int32, (q, LANES), 1)
    for g in range(SSD_GROUPS):
        cg = c_ref[0][:, g * SSD_STATE:(g + 1) * SSD_STATE]
        bg = b_ref[0][:, g * SSD_STATE:(g + 1) * SSD_STATE]
        cg16 = cg.astype(bf16)
        cb = lax.dot_general(cg16, bg.astype(bf16), (((1,), (1,)), ((), ())), preferred_element_type=f32)
        xg = x_ref[0][:, g * SSD_GROUP_W:(g + 1) * SSD_GROUP_W]
        xg16 = xg.astype(bf16)
        eac_x = _expand_heads(eacum, g, hoff)
        yd = []
        for k in range(SSD_HEADS_PER_GROUP // 2):
            xp = xg16[:, k * LANES:(k + 1) * LANES]
            ys = []
            for hh in range(2):
                h = g * SSD_HEADS_PER_GROUP + 2 * k + hh
                seg = acum[:, hoff + h:hoff + h + 1] - acum_t[h:h + 1, :]
                lmat = jnp.exp(jnp.where(keep, seg, -jnp.inf)) * dtt[h:h + 1, :]
                ys.append(jnp.dot((cb * lmat).astype(bf16), xp, preferred_element_type=f32))
            yd.append(jnp.where(lane < SSD_HEAD_DIM, ys[0], ys[1]))
        st = st_ref[g]
        y_off = jnp.dot(cg16, st.astype(bf16), preferred_element_type=f32) * eac_x
        cols = slice(g * SSD_GROUP_W, (g + 1) * SSD_GROUP_W)
        if add_prev:
            other = extra_ref[0, :, cols].astype(f32)
        else:
            other = extra_ref[:, cols] * xg
        y_ref[0, :, cols] = (jnp.concatenate(yd, axis=1) + y_off + other).astype(bf16)
        xdd16 = (xg * _expand_heads(dt_dec_end, g, hoff)).astype(bf16)
        st_ref[g] = st * eac_x[end:end + 1, :] + jnp.dot(bg.T.astype(bf16), xdd16, preferred_element_type=f32)

    @pl.when(ci == pl.num_programs(1) - 1)
    def _():
        fin_ref[0] = st_ref[...]


def _ssd_scan_p(xbc, dt_raw, dt_raw_t, dt_bias, a, init, reverse, direction, y_prev=None, d_skip=None):
    b, length, _ = xbc.shape
    nc = length // SSD_CHUNK
    q = SSD_CHUNK
    cidx = (lambda j: nc - 1 - j) if reverse else (lambda j: j)
    nb = SSD_D_INNER // SSD_GN
    hoff = direction * SSD_HEADS
    full2 = lambda i, j: (0, 0)
    st_shape = (SSD_GROUPS, SSD_STATE, SSD_GROUP_W)
    lanes = lambda v: jnp.pad(v, (hoff, LANES - hoff - SSD_HEADS)).reshape(1, LANES)
    add_prev = y_prev is not None
    if add_prev:
        extra = y_prev
        extra_spec = pl.BlockSpec((1, q, SSD_D_INNER), lambda i, j: (i, cidx(j), 0))
    else:
        extra = jnp.repeat(d_skip, SSD_HEAD_DIM).reshape(1, SSD_D_INNER)
        extra_spec = pl.BlockSpec((1, SSD_D_INNER), full2)
    return pl.pallas_call(
        functools.partial(_ssd_scan_kernel, reverse=reverse, hoff=hoff, add_prev=add_prev),
        grid=(b, nc),
        in_specs=[pl.BlockSpec((1, q, SSD_D_INNER), lambda i, j: (i, cidx(j), 0)),
                  pl.BlockSpec((1, q, SSD_GN), lambda i, j: (i, cidx(j), nb)),
                  pl.BlockSpec((1, q, SSD_GN), lambda i, j: (i, cidx(j), nb + 1)),
                  pl.BlockSpec((1, q, LANES), lambda i, j: (i, cidx(j), 0)),
                  pl.BlockSpec((1, 2 * SSD_HEADS, q), lambda i, j: (i, 0, cidx(j))),
                  pl.BlockSpec((1, LANES), full2), pl.BlockSpec((SSD_HEADS, 1), full2),
                  pl.BlockSpec((1, LANES), full2), pl.BlockSpec((SSD_HEADS, 1), full2),
                  pl.BlockSpec((1,) + st_shape, lambda i, j: (i, 0, 0, 0)),
                  extra_spec],
        out_specs=[pl.BlockSpec((1, q, SSD_D_INNER), lambda i, j: (i, cidx(j), 0)),
                   pl.BlockSpec((1,) + st_shape, lambda i, j: (i, 0, 0, 0))],
        out_shape=[jax.ShapeDtypeStruct((b, length, SSD_D_INNER), jnp.bfloat16),
                   jax.ShapeDtypeStruct((b,) + st_shape, jnp.float32)],
        scratch_shapes=[pltpu.VMEM(st_shape, jnp.float32)],
        compiler_params=pltpu.CompilerParams(
            dimension_semantics=("parallel", "arbitrary"), vmem_limit_bytes=VMEM_LIMIT_BYTES),
        name="ssd_scan",
    )(xbc, xbc, xbc, dt_raw, dt_raw_t, lanes(dt_bias), dt_bias.reshape(-1, 1),
      lanes(a), a.reshape(-1, 1), init, extra)


def _ssd_out_kernel(y_ref, z_ref, ng_ref, w_ref, o_ref):
    z = z_ref[0].astype(jnp.float32)
    y = y_ref[0].astype(jnp.float32) * (z * jax.nn.sigmoid(z))
    ms = jnp.mean(y * y, axis=-1, keepdims=True)
    y = y * lax.rsqrt(ms + RMS_EPS) * ng_ref[...]
    o_ref[0] = jnp.dot(y.astype(jnp.bfloat16), w_ref[...], preferred_element_type=jnp.float32)


def _ssd_out(y, z, norm_g, out_w):
    b, length, di = y.shape
    d = out_w.shape[1]
    tm = min(length, 2 * ROW_TILE)
    row = lambda i, j: (i, j, 0)
    full2 = lambda i, j: (0, 0)
    return pl.pallas_call(
        _ssd_out_kernel,
        grid=(b, length // tm),
        in_specs=[pl.BlockSpec((1, tm, di), row), pl.BlockSpec((1, tm, di), row),
                  pl.BlockSpec((1, di), full2), pl.BlockSpec((di, d), full2)],
        out_specs=pl.BlockSpec((1, tm, d), row),
        out_shape=jax.ShapeDtypeStruct((b, length, d), jnp.float32),
        compiler_params=pltpu.CompilerParams(
            dimension_semantics=("parallel", "parallel"), vmem_limit_bytes=VMEM_LIMIT_BYTES),
        name="ssd_out",
    )(y, z, norm_g.reshape(1, di), out_w.astype(jnp.bfloat16))


def _state_to_kernel(s):
    b = s.shape[0]
    s = s.reshape(b, SSD_GROUPS, SSD_HEADS_PER_GROUP, SSD_HEAD_DIM, SSD_STATE)
    return jnp.transpose(s, (0, 1, 4, 2, 3)).reshape(b, SSD_GROUPS, SSD_STATE, SSD_GROUP_W)


def _state_from_kernel(s):
    b = s.shape[0]
    s = s.reshape(b, SSD_GROUPS, SSD_STATE, SSD_HEADS_PER_GROUP, SSD_HEAD_DIM)
    return jnp.transpose(s, (0, 1, 3, 4, 2)).reshape(b, SSD_HEADS, SSD_HEAD_DIM, SSD_STATE)


def _ssd_mixer_p(x, ng, sh, sc, init_f, init_b, in_w, conv_w, conv_b, dt_bias, a_log, d_skip, norm_g, out_w):
    z, xbc, dt_raw, dt_raw_t = _ssd_in(x, ng, sh, sc, in_w, conv_w, conv_b)
    a = -jnp.exp(a_log)
    yf, s_f = _ssd_scan_p(xbc, dt_raw, dt_raw_t, dt_bias[0], a[0], _state_to_kernel(init_f), False, 0,
                          d_skip=d_skip)
    y, s_b = _ssd_scan_p(xbc, dt_raw, dt_raw_t, dt_bias[1], a[1], _state_to_kernel(init_b), True, 1, y_prev=yf)
    m = _ssd_out(y, z, norm_g, out_w)
    return m, _state_from_kernel(s_f), _state_from_kernel(s_b)


HY_MAX_BLOCK = 512
HY_CC = 128
HY_MAC_ELEMS = 8192
HY_HIDDEN = 64
HY_FEAT_ROWS = 64


def _odd_dft_tables(n):
    m = np.arange(n, dtype=np.int64)[:, None]
    f = np.arange(n // 2, dtype=np.int64)[None, :]
    ang = 2.0 * np.pi * (((2 * f + 1) * m) % (2 * n)).astype(np.float64) / (2 * n)
    return np.cos(ang), np.sin(ang)


def _hy_in_kernel(xm_ref, xp_ref, xn_ref, ng_ref, sh_ref, sc_ref, w_ref, b_ref, cw_ref, cb_ref, o_ref):
    tm = xm_ref.shape[1]
    h, valid = _halo_rows(xm_ref, xp_ref, xn_ref, ng_ref, sh_ref, sc_ref)
    _proj_conv(h, valid, w_ref, 0, b_ref, cw_ref, cb_ref, o_ref, tm, silu=False)


def _hy_in(x, ng, sh, sc, in_w, in_b, short_w, short_b):
    b, length, d = x.shape
    n = in_w.shape[1]
    tm = min(length, ROW_TILE)
    taps = short_w.shape[0]
    per_b = lambda i, j: (i, 0, 0)
    full2 = lambda i, j: (0, 0)
    return pl.pallas_call(
        _hy_in_kernel,
        grid=(b, length // tm),
        in_specs=_halo_specs(length, tm, d) + [
            pl.BlockSpec((1, d), full2), pl.BlockSpec((1, 1, d), per_b), pl.BlockSpec((1, 1, d), per_b),
            pl.BlockSpec((d, n), full2), pl.BlockSpec((1, n), full2),
            pl.BlockSpec((taps, n), full2), pl.BlockSpec((1, n), full2)],
        out_specs=pl.BlockSpec((1, tm, n), lambda i, j: (i, j, 0)),
        out_shape=jax.ShapeDtypeStruct((b, length, n), jnp.float32),
        compiler_params=pltpu.CompilerParams(
            dimension_semantics=("parallel", "parallel"), vmem_limit_bytes=VMEM_LIMIT_BYTES),
        name="hy_in",
    )(x, x, x, ng.reshape(1, d), sh, sc, in_w.astype(jnp.bfloat16), in_b.reshape(1, n), short_w, short_b.reshape(1, n))


def _dot3(a, b):
    a_hi, a_lo = _split_bf16(a)
    b_hi, b_lo = _split_bf16(b)
    f32 = jnp.float32
    return (jnp.dot(a_hi, b_hi, preferred_element_type=f32) + jnp.dot(a_lo, b_hi, preferred_element_type=f32)
            + jnp.dot(a_hi, b_lo, preferred_element_type=f32))


def _hy_filter_kernel(w1t_ref, b1_ref, w2t_ref, b2_ref, w3t_ref, fr_ref, dl_ref, o_ref, *, length, blk):
    k = pl.program_id(0)
    q = (lax.broadcasted_iota(jnp.int32, (1, blk), 1) + k * blk)
    pos = jnp.abs(q - length).astype(jnp.float32)
    t = pos / float(length - 1)
    w = (2.0 * math.pi / length) * pos
    band = lax.broadcasted_iota(jnp.int32, (HY_BANDS, 1), 0).astype(jnp.float32)
    fb = 1e-4 + band * ((HY_BANDS - 1 - 1e-4) / (HY_BANDS - 1))
    z = jnp.concatenate([jnp.broadcast_to(t, (8, blk)), jnp.cos(fb * w), -jnp.sin(fb * w),
                         jnp.zeros((HY_FEAT_ROWS - 8 - 2 * HY_BANDS, blk), jnp.float32)], axis=0)
    h = jnp.sin(fr_ref[...] * (_dot3(w1t_ref[...], z) + b1_ref[...]))
    h = jnp.sin(fr_ref[...] * (_dot3(w2t_ref[...], h) + b2_ref[...]))
    kt = _dot3(w3t_ref[0], h)
    o_ref[0] = kt * jnp.exp(-t * dl_ref[...])


def _hy_filter(length, blk, f_w1, f_b1, f_w2, f_b2, f_w3, f_freq):
    d = f_w3.shape[1] // 2
    nk = 2 * length // blk
    w1t = jnp.concatenate([f_w1[0:1].T, jnp.zeros((HY_HIDDEN, 7), jnp.float32), f_w1[1:].T,
                           jnp.zeros((HY_HIDDEN, HY_FEAT_ROWS - 8 - 2 * HY_BANDS), jnp.float32)], axis=1)
    w3t = jnp.stack([f_w3[:, d:].T, f_w3[:, :d].T])
    deltas = jnp.abs(jnp.linspace(HY_MIN_DECAY, HY_MAX_DECAY, d, dtype=jnp.float32)).reshape(d, 1)
    col = lambda v: v.reshape(HY_HIDDEN, 1)
    full2 = lambda k: (0, 0)
    half = length // blk
    return pl.pallas_call(
        functools.partial(_hy_filter_kernel, length=length, blk=blk),
        grid=(nk,),
        in_specs=[pl.BlockSpec((HY_HIDDEN, HY_FEAT_ROWS), full2), pl.BlockSpec((HY_HIDDEN, 1), full2),
                  pl.BlockSpec((HY_HIDDEN, HY_HIDDEN), full2), pl.BlockSpec((HY_HIDDEN, 1), full2),
                  pl.BlockSpec((1, d, HY_HIDDEN), lambda k: (k // half, 0, 0)),
                  pl.BlockSpec((HY_HIDDEN, 1), full2), pl.BlockSpec((d, 1), full2)],
        out_specs=pl.BlockSpec((1, d, blk), lambda k: (k, 0, 0)),
        out_shape=jax.ShapeDtypeStruct((nk, d, blk), jnp.float32),
        compiler_params=pltpu.CompilerParams(
            dimension_semantics=("parallel",), vmem_limit_bytes=VMEM_LIMIT_BYTES),
        name="hy_filter",
    )(w1t, col(f_b1), f_w2.T, col(f_b2), w3t, col(f_freq), deltas)


def _hy_gspec_kernel(hi_ref, lo_ref, ft_ref, fb_ref, o_ref):
    o_ref[0] = _dot3(hi_ref[0], ft_ref[...]) + _dot3(lo_ref[0], fb_ref[...])


def _hy_gspec(kt):
    nk, d, blk = kt.shape
    cos, sin = _odd_dft_tables(2 * blk)
    top = np.concatenate([cos[:blk], -sin[:blk]], axis=1)
    bot = -np.concatenate([cos[blk:], -sin[blk:]], axis=1)
    bot[0] = 0.0
    tm = 512
    full2 = lambda e, i: (0, 0)
    return pl.pallas_call(
        _hy_gspec_kernel,
        grid=(nk - 1, d // tm),
        in_specs=[pl.BlockSpec((1, tm, blk), lambda e, i: (e + 1, i, 0)),
                  pl.BlockSpec((1, tm, blk), lambda e, i: (e, i, 0)),
                  pl.BlockSpec((blk, 2 * blk), full2), pl.BlockSpec((blk, 2 * blk), full2)],
        out_specs=pl.BlockSpec((1, tm, 2 * blk), lambda e, i: (e, i, 0)),
        out_shape=jax.ShapeDtypeStruct((nk - 1, d, 2 * blk), jnp.float32),
        compiler_params=pltpu.CompilerParams(
            dimension_semantics=("parallel", "parallel"), vmem_limit_bytes=VMEM_LIMIT_BYTES),
        name="hy_gspec",
    )(kt, kt, jnp.asarray(top, jnp.float32), jnp.asarray(bot, jnp.float32))


def _hy_conv_kernel(x0_ref, x1_ref, v_ref, g_ref, fb_ref, ff_ref, fi_ref, o_ref, lhs_ref, u_ref, y_ref, *, nb):
    cc, bsz = HY_CC, ff_ref.shape[0]
    mrows = HY_MAC_ELEMS // bsz
    for j in range(nb):
        sl = slice(j * bsz, (j + 1) * bsz)
        wj = v_ref[0, sl, :] * x1_ref[0, sl, :]
        lhs_ref[j * cc:(j + 1) * cc, :] = wj.T.astype(jnp.bfloat16)
    u_ref[...] = jnp.dot(lhs_ref[...], ff_ref[...], preferred_element_type=jnp.float32)

    def per_out_block(i, carry):
        def per_rows(rc, carry2):
            rows = pl.ds(pl.multiple_of(rc * mrows, mrows), mrows)
            acc_r = jnp.zeros((mrows, bsz), jnp.float32)
            acc_i = jnp.zeros((mrows, bsz), jnp.float32)
            for j in range(nb):
                e = i - j + (nb - 1)
                gr = g_ref[e, rows, 0:bsz]
                gi = g_ref[e, rows, bsz:2 * bsz]
                urows = pl.ds(pl.multiple_of(j * cc + rc * mrows, mrows), mrows)
                ur = u_ref[urows, 0:bsz]
                ui = u_ref[urows, bsz:2 * bsz]
                acc_r = acc_r + gr * ur - gi * ui
                acc_i = acc_i + gr * ui + gi * ur
            yrows = pl.ds(pl.multiple_of(i * cc + rc * mrows, mrows), mrows)
            y_ref[yrows, 0:bsz] = acc_r.astype(jnp.bfloat16)
            y_ref[yrows, bsz:2 * bsz] = acc_i.astype(jnp.bfloat16)
            return carry2
        return lax.fori_loop(0, cc // mrows, per_rows, carry)
    lax.fori_loop(0, nb, per_out_block, 0)

    yt = jnp.dot(y_ref[...], fi_ref[...], preferred_element_type=jnp.float32)
    for i in range(nb):
        sl = slice(i * bsz, (i + 1) * bsz)
        w = v_ref[0, sl, :] * x1_ref[0, sl, :]
        gated = (yt[i * cc:(i + 1) * cc, :].T + fb_ref[...] * w) * x0_ref[0, sl, :]
        o_ref[0, sl, :] = gated.astype(o_ref.dtype)


def _hy_conv(u, g, f_bias, blk):
    b, length, d3 = u.shape
    d = d3 // 3
    nb = length // blk
    ncb = d // HY_CC
    cos, sin = _odd_dft_tables(2 * blk)
    fwd = np.concatenate([cos[:blk], -sin[:blk]], axis=1)
    inv = (1.0 / blk) * np.concatenate([cos[:blk].T, -sin[:blk].T], axis=0)
    col = lambda off: pl.BlockSpec((1, length, HY_CC), lambda c, i, off=off: (i, 0, off + c))
    full2 = lambda c, i: (0, 0)
    return pl.pallas_call(
        functools.partial(_hy_conv_kernel, nb=nb),
        grid=(ncb, b),
        in_specs=[col(0), col(ncb), col(2 * ncb),
                  pl.BlockSpec((2 * nb - 1, HY_CC, 2 * blk), lambda c, i: (0, c, 0)),
                  pl.BlockSpec((1, HY_CC), lambda c, i: (0, c)),
                  pl.BlockSpec((blk, 2 * blk), full2), pl.BlockSpec((2 * blk, blk), full2)],
        out_specs=pl.BlockSpec((1, length, HY_CC), lambda c, i: (i, 0, c)),
        out_shape=jax.ShapeDtypeStruct((b, length, d), jnp.bfloat16),
        scratch_shapes=[pltpu.VMEM((nb * HY_CC, blk), jnp.bfloat16),
                        pltpu.VMEM((nb * HY_CC, 2 * blk), jnp.float32),
                        pltpu.VMEM((nb * HY_CC, 2 * blk), jnp.bfloat16)],
        compiler_params=pltpu.CompilerParams(
            dimension_semantics=("parallel", "arbitrary"), vmem_limit_bytes=56 * 1024 * 1024),
        name="hy_conv",
    )(u, u, u, g, f_bias.reshape(1, d), jnp.asarray(fwd, jnp.bfloat16), jnp.asarray(inv, jnp.bfloat16))


def _hyena_mixer_p(x, ng, sh, sc, in_w, in_b, short_w, short_b, f_w1, f_b1, f_w2, f_b2, f_w3, f_freq, f_bias, out_w):
    length = x.shape[1]
    blk = min(HY_MAX_BLOCK, length)
    u = _hy_in(x, ng, sh, sc, in_w, in_b, short_w, short_b)
    g = _hy_gspec(_hy_filter(length, blk, f_w1, f_b1, f_w2, f_b2, f_w3, f_freq))
    return _mm3(_hy_conv(u, g, f_bias, blk), out_w)


TOK_TILE = 256
MOE_ROWS = 512
SEG_CHUNK = 64
BF16_TILE_ROWS = 16
LANES = 128


def _split_bf16(w):
    hi = w.astype(jnp.bfloat16)
    lo = (w - hi.astype(jnp.float32)).astype(jnp.bfloat16)
    return hi, lo


def _moe_pre_kernel(x_ref, m_ref, g1_ref, ng_ref, sh_ref, sc_ref, wrh_ref, wrl_ref,
                    xo_ref, hpk_ref, lg_ref):
    x = x_ref[0] + g1_ref[0] * m_ref[0]
    xo_ref[0] = x
    ms = jnp.mean(x * x, axis=-1, keepdims=True)
    h = x * lax.rsqrt(ms + RMS_EPS) * ng_ref[...]
    h = h * (1.0 + sc_ref[0]) + sh_ref[0]
    h_hi = h.astype(jnp.bfloat16)
    h_lo = (h - h_hi.astype(jnp.float32)).astype(jnp.bfloat16)
    dn = (((1,), (1,)), ((), ()))
    lg = lax.dot_general(wrh_ref[...], h_hi, dn, preferred_element_type=jnp.float32)
    lg += lax.dot_general(wrh_ref[...], h_lo, dn, preferred_element_type=jnp.float32)
    lg += lax.dot_general(wrl_ref[...], h_hi, dn, preferred_element_type=jnp.float32)
    lg_ref[0] = lg
    half = h.shape[1] // 2
    wa = pltpu.bitcast(h_hi[:, :half].astype(jnp.float32), jnp.uint32) >> 16
    wb = pltpu.bitcast(h_hi[:, half:].astype(jnp.float32), jnp.uint32) & jnp.uint32(0xFFFF0000)
    hpk_ref[0] = wa | wb


def _moe_pre(x, m, g1, ng, sh, sc, w_router):
    b, length, d = x.shape
    tm = min(length, 512)
    wrh, wrl = _split_bf16(w_router.T)
    row = lambda i, j: (i, j, 0)
    per_b = lambda i, j: (i, 0, 0)
    full2 = lambda i, j: (0, 0)
    return pl.pallas_call(
        _moe_pre_kernel,
        grid=(b, length // tm),
        in_specs=[pl.BlockSpec((1, tm, d), row), pl.BlockSpec((1, tm, d), row),
                  pl.BlockSpec((1, 1, d), per_b), pl.BlockSpec((1, d), full2),
                  pl.BlockSpec((1, 1, d), per_b), pl.BlockSpec((1, 1, d), per_b),
                  pl.BlockSpec((N_EXPERTS, d), full2), pl.BlockSpec((N_EXPERTS, d), full2)],
        out_specs=[pl.BlockSpec((1, tm, d), row), pl.BlockSpec((1, tm, d // 2), row),
                   pl.BlockSpec((1, N_EXPERTS, tm), lambda i, j: (i, 0, j))],
        out_shape=[jax.ShapeDtypeStruct((b, length, d), jnp.float32),
                   jax.ShapeDtypeStruct((b, length, d // 2), jnp.uint32),
                   jax.ShapeDtypeStruct((b, N_EXPERTS, length), jnp.float32)],
        compiler_params=pltpu.CompilerParams(
            dimension_semantics=("parallel", "parallel"), vmem_limit_bytes=VMEM_LIMIT_BYTES),
        name="moe_pre",
    )(x, m, g1, ng.reshape(1, d), sh, sc, wrh, wrl)


SEL_BLOCK = 256
SEL_ROWS = 64
F32_INF_BITS = 0x7F800000


def _prefix_counts(flags, inclusive):
    e, length = flags.shape
    r = lax.broadcasted_iota(jnp.int32, (SEL_BLOCK, SEL_BLOCK), 0)
    c = lax.broadcasted_iota(jnp.int32, (SEL_BLOCK, SEL_BLOCK), 1)
    tri = ((r <= c) if inclusive else (r < c)).astype(jnp.bfloat16)
    off = jnp.zeros((e, 1), jnp.float32)
    blocks = []
    for k in range(length // SEL_BLOCK):
        blk = flags[:, k * SEL_BLOCK:(k + 1) * SEL_BLOCK]
        blocks.append(jnp.dot(blk.astype(jnp.bfloat16), tri, preferred_element_type=jnp.float32) + off)
        off = off + jnp.sum(blk, axis=1, keepdims=True)
    return jnp.concatenate(blocks, axis=1)


def _moe_select_kernel(lg_ref, aff_ref, idx_ref, rank_ref, cnt_ref, *, cap):
    lg = lg_ref[0]
    ne, length = lg.shape
    ex = jnp.exp(lg - jnp.max(lg, axis=0, keepdims=True))
    aff = ex / jnp.sum(ex, axis=0, keepdims=True)
    aff_ref[0] = aff
    bits = pltpu.bitcast(aff, jnp.int32)

    def bisect(_, carry):
        lo, hi = carry
        mid = lo + ((hi - lo + 1) >> 1)
        cnt = jnp.sum((bits >= mid).astype(jnp.float32), axis=1, keepdims=True)
        ok = cnt >= cap
        return jnp.where(ok, mid, lo), jnp.where(ok, hi, mid - 1)
    lo0 = jnp.zeros((ne, 1), jnp.int32)
    tau, _ = lax.fori_loop(0, 32, bisect, (lo0, lo0 + F32_INF_BITS))
    gt = bits > tau
    eq = (bits == tau).astype(jnp.float32)
    need = cap - jnp.sum(gt.astype(jnp.float32), axis=1, keepdims=True)
    keep = jnp.where(gt, 1.0, jnp.where(_prefix_counts(eq, False) < need, eq, 0.0))
    rank_ref[...] = _prefix_counts(keep, True)

    rows = min(SEL_ROWS, cap)
    lane = lax.broadcasted_iota(jnp.int32, (cap, LANES), 1)
    cnt_ref[...] = jnp.zeros((cap, LANES), jnp.float32)

    def per_expert(ei, carry):
        rk = rank_ref[pl.ds(ei, 1), :]
        cols = []
        for ck in range(cap // rows):
            slot = (lax.broadcasted_iota(jnp.int32, (rows, LANES), 0) + ck * rows).astype(jnp.float32)
            acc = jnp.zeros((rows, LANES), jnp.float32)
            for j in range(length // LANES):
                acc = acc + jnp.where(rk[:, j * LANES:(j + 1) * LANES] <= slot, 1.0, 0.0)
            cols.append(jnp.sum(acc, axis=1, keepdims=True))
        col = jnp.concatenate(cols, axis=0)
        cnt_ref[...] = jnp.where(lane == ei, col, cnt_ref[...])
        return carry
    lax.fori_loop(0, ne, per_expert, 0)
    idx_ref[0] = cnt_ref[:, :ne].astype(jnp.int32)


def _moe_select(lg, cap):
    b, ne, length = lg.shape
    aff, idx = pl.pallas_call(
        functools.partial(_moe_select_kernel, cap=cap),
        grid=(b,),
        in_specs=[pl.BlockSpec((1, ne, length), lambda i: (i, 0, 0))],
        out_specs=[pl.BlockSpec((1, ne, length), lambda i: (i, 0, 0)),
                   pl.BlockSpec((1, cap, ne), lambda i: (i, 0, 0))],
        out_shape=[jax.ShapeDtypeStruct((b, ne, length), jnp.float32),
                   jax.ShapeDtypeStruct((b, cap, ne), jnp.int32)],
        scratch_shapes=[pltpu.VMEM((ne, length), jnp.float32), pltpu.VMEM((cap, LANES), jnp.float32)],
        compiler_params=pltpu.CompilerParams(
            dimension_semantics=("parallel",), vmem_limit_bytes=VMEM_LIMIT_BYTES),
        name="moe_select",
    )(lg)
    return aff, jnp.swapaxes(idx, 1, 2)


def _moe_ffn_kernel(idx_ref, nidx_ref, h_hbm, gate_ref, wg_ref, wu_ref, wd_ref, y_ref,
                    xe_ref, wgb, wub, wdb, sem):
    nblk = pl.num_programs(1)
    step = pl.program_id(0) * nblk + pl.program_id(1)
    last = pl.num_programs(0) * nblk - 1

    def issue(ids_ref, slot):
        base = slot * MOE_ROWS
        for c in range(MOE_ROWS):
            pltpu.make_async_copy(h_hbm.at[pl.ds(ids_ref[0, 0, c], 1)], xe_ref.at[pl.ds(base + c, 1)],
                                  sem.at[slot]).start(priority=c % 2)

    @pl.when(step == 0)
    def _():
        issue(idx_ref, 0)

    for parity in range(2):
        @pl.when((step < last) & (step % 2 == parity))
        def _(parity=parity):
            issue(nidx_ref, 1 - parity)

    @pl.when(pl.program_id(1) == 0)
    def _():
        wgb[...] = wg_ref[0, 0].astype(jnp.bfloat16)
        wub[...] = wu_ref[0, 0].astype(jnp.bfloat16)
        wdb[...] = wd_ref[0, 0].astype(jnp.bfloat16)

    slot = step % 2
    rows = pl.ds(pl.multiple_of(slot * MOE_ROWS, MOE_ROWS), MOE_ROWS)
    pltpu.make_async_copy(h_hbm.at[pl.ds(0, MOE_ROWS)], xe_ref.at[rows], sem.at[slot]).wait()
    half = wgb.shape[0] // 2
    w = xe_ref[rows, :]
    xa = pltpu.bitcast(w << 16, jnp.float32).astype(jnp.bfloat16)
    xb = pltpu.bitcast(w & jnp.uint32(0xFFFF0000), jnp.float32).astype(jnp.bfloat16)
    hg = jnp.dot(xa, wgb[:half], preferred_element_type=jnp.float32)
    hg += jnp.dot(xb, wgb[half:], preferred_element_type=jnp.float32)
    hu = jnp.dot(xa, wub[:half], preferred_element_type=jnp.float32)
    hu += jnp.dot(xb, wub[half:], preferred_element_type=jnp.float32)
    hid = (hg * jax.nn.sigmoid(hg) * hu).astype(jnp.bfloat16)
    y = jnp.dot(hid, wdb[...], preferred_element_type=jnp.float32)
    y_ref[0] = (y * gate_ref[0]).astype(jnp.bfloat16)


def _moe_ffn(hpk, grow, gate, w_gate, w_up, w_down, layer):
    e, r = grow.shape
    d, f = w_gate.shape[2], w_gate.shape[3]
    nblk = r // MOE_ROWS
    nsteps = e * nblk
    wspec = lambda shp: pl.BlockSpec((1, 1) + shp, lambda i, j: (layer, i, 0, 0))
    ids = grow.reshape(nsteps, 1, MOE_ROWS)
    smem_ids = lambda off: pl.BlockSpec(
        (1, 1, MOE_ROWS), lambda i, j: (jnp.minimum(i * nblk + j + off, nsteps - 1), 0, 0), memory_space=pltpu.SMEM)
    return pl.pallas_call(
        _moe_ffn_kernel,
        grid=(e, nblk),
        in_specs=[smem_ids(0), smem_ids(1),
                  pl.BlockSpec(memory_space=pltpu.HBM),
                  pl.BlockSpec((1, MOE_ROWS, 1), lambda i, j: (i, j, 0)),
                  wspec((d, f)), wspec((d, f)), wspec((f, d))],
        out_specs=pl.BlockSpec((1, MOE_ROWS, d), lambda i, j: (i, j, 0)),
        out_shape=jax.ShapeDtypeStruct((e, r, d), jnp.bfloat16),
        scratch_shapes=[pltpu.VMEM((2 * MOE_ROWS, d // 2), jnp.uint32),
                        pltpu.VMEM((d, f), jnp.bfloat16), pltpu.VMEM((d, f), jnp.bfloat16),
                        pltpu.VMEM((f, d), jnp.bfloat16),
                        pltpu.SemaphoreType.DMA((2,))],
        compiler_params=pltpu.CompilerParams(
            dimension_semantics=("arbitrary", "arbitrary"), vmem_limit_bytes=VMEM_LIMIT_BYTES),
        name="moe_ffn",
    )(ids, ids, hpk, gate, w_gate, w_up, w_down)


def _moe_comb_kernel(cs_ref, x_ref, g2_ref, y_ref, idx_ref, fg_ref, o_ref, ycat, acc, *, cap, ch, ntile, final_norm):
    b = pl.program_id(0)
    t = pl.program_id(1)
    base = t * TOK_TILE
    sub = lax.broadcasted_iota(jnp.int32, (TOK_TILE, LANES), 0) + base
    if ntile == 1:
        for e in range(N_EXPERTS):
            ycat[e * ch:(e + 1) * ch, :] = y_ref[e, 0:ch, :]
        v = idx_ref[0]
        tiles = [(v[:, p * LANES:(p + 1) * LANES] == sub).astype(jnp.bfloat16)
                 for p in range(N_EXPERTS * ch // LANES)]
        acc[...] = jnp.dot(jnp.concatenate(tiles, axis=1), ycat[...], preferred_element_type=jnp.float32)
    else:
        lane = lax.broadcasted_iota(jnp.int32, (1, LANES), 1)
        per = LANES // ch
        sts = []
        for e in range(N_EXPERTS):
            s0 = cs_ref[(b * N_EXPERTS + e) * (ntile + 1) + t]
            st = jnp.minimum((s0 // BF16_TILE_ROWS) * BF16_TILE_ROWS, cap - ch)
            st = pl.multiple_of(st, BF16_TILE_ROWS)
            sts.append(st)
            ycat[e * ch:(e + 1) * ch, :] = y_ref[e, pl.ds(st, ch), :]
        tiles = []
        for p in range(N_EXPERTS // per):
            v = None
            for q in range(per):
                e = p * per + q
                r = pltpu.roll(idx_ref[0, e:e + 1, :], (2 * cap - sts[e] + q * ch) % cap, 1)[:, :LANES]
                v = r if v is None else jnp.where(lane >= q * ch, r, v)
            tiles.append((v == sub).astype(jnp.bfloat16))
        acc[...] = jnp.dot(jnp.concatenate(tiles, axis=1), ycat[...], preferred_element_type=jnp.float32)
        sub_c = lax.broadcasted_iota(jnp.int32, (TOK_TILE, ch), 0) + base
        lane_c = lax.broadcasted_iota(jnp.int32, (1, ch), 1)
        for e in range(N_EXPERTS):
            s1 = cs_ref[(b * N_EXPERTS + e) * (ntile + 1) + t + 1]
            first_end = sts[e] + ch
            n_extra = jnp.maximum(s1 - first_end + ch - 1, 0) // ch

            def extra(q, carry, e=e, first_end=first_end):
                lo = first_end + q * ch
                stq = pl.multiple_of(jnp.minimum(lo, cap - ch), BF16_TILE_ROWS)
                r = pltpu.roll(idx_ref[0, e:e + 1, :], (2 * cap - stq) % cap, 1)[:, :ch]
                hit = (r == sub_c) & (lane_c + stq >= lo)
                acc[...] += jnp.dot(hit.astype(jnp.bfloat16), y_ref[e, pl.ds(stq, ch), :],
                                    preferred_element_type=jnp.float32)
                return carry
            lax.fori_loop(0, n_extra, extra, 0)
    out = x_ref[0] + g2_ref[0] * acc[...]
    if final_norm:
        ms = jnp.mean(out * out, axis=-1, keepdims=True)
        out = out * lax.rsqrt(ms + RMS_EPS) * fg_ref[...]
    o_ref[0] = out


def _moe_combine(x, g2, y, idx, cs, final_g=None):
    b, length, d = x.shape
    final_norm = final_g is not None
    fg = (final_g if final_norm else jnp.ones((d,), jnp.float32)).reshape(1, d)
    cap = idx.shape[2]
    ntile = length // TOK_TILE
    ch = min(SEG_CHUNK, cap)
    if ntile == 1:
        idx_in = idx.reshape(b, 1, N_EXPERTS * cap)
        idx_spec = pl.BlockSpec((1, 1, N_EXPERTS * cap), lambda i, j, c: (i, 0, 0))
    else:
        idx_in = idx
        idx_spec = pl.BlockSpec((1, N_EXPERTS, cap), lambda i, j, c: (i, 0, 0))
    grid_spec = pltpu.PrefetchScalarGridSpec(
        num_scalar_prefetch=1,
        grid=(b, ntile),
        in_specs=[pl.BlockSpec((1, TOK_TILE, d), lambda i, j, c: (i, j, 0)),
                  pl.BlockSpec((1, 1, d), lambda i, j, c: (i, 0, 0)),
                  pl.BlockSpec((N_EXPERTS, cap, d), lambda i, j, c: (0, i, 0)),
                  idx_spec,
                  pl.BlockSpec((1, d), lambda i, j, c: (0, 0))],
        out_specs=pl.BlockSpec((1, TOK_TILE, d), lambda i, j, c: (i, j, 0)),
        scratch_shapes=[pltpu.VMEM((N_EXPERTS * ch, d), jnp.bfloat16),
                        pltpu.VMEM((TOK_TILE, d), jnp.float32)])
    return pl.pallas_call(
        functools.partial(_moe_comb_kernel, cap=cap, ch=ch, ntile=ntile, final_norm=final_norm),
        grid_spec=grid_spec,
        out_shape=jax.ShapeDtypeStruct((b, length, d), jnp.float32),
        compiler_params=pltpu.CompilerParams(
            dimension_semantics=("arbitrary", "arbitrary"), vmem_limit_bytes=56 * 1024 * 1024),
        name="moe_combine",
    )(cs.reshape(-1).astype(jnp.int32), x, g2, y, idx_in, fg)


def _moe_block(x, m, g1, ng, sh, sc, g2, w_router, w_gate, w_up, w_down, layer, final_g=None):
    b, length, d = x.shape
    cap = EC_FACTOR * length // N_EXPERTS
    x1, hpk, lg = _moe_pre(x, m, g1, ng, sh, sc, w_router)
    aff, idx = _moe_select(lg, cap)
    gate = jnp.take_along_axis(aff, idx, axis=-1)
    ntile = length // TOK_TILE
    bounds = jnp.arange(ntile + 1, dtype=jnp.int32) * TOK_TILE
    cs = jnp.sum(idx[:, :, :, None] < bounds, axis=2, dtype=jnp.int32)
    grow = idx + (jnp.arange(b, dtype=jnp.int32) * length)[:, None, None]
    grow = jnp.swapaxes(grow, 0, 1).reshape(N_EXPERTS, b * cap)
    gate_e = jnp.swapaxes(gate, 0, 1).reshape(N_EXPERTS, b * cap, 1)
    y = _moe_ffn(hpk.reshape(b * length, d // 2), grow, gate_e, w_gate, w_up, w_down, layer)
    return _moe_combine(x1, g2, y, idx, cs, final_g)


def kernel(x_prompt, x_sample, state_ssd, c, c_ctx, norm_g, ada_w, ada_b, hy_in_w, hy_in_b, hy_short_w, hy_short_b, hy_f_w1, hy_f_b1, hy_f_w2, hy_f_b2, hy_f_w3, hy_f_freq, hy_f_bias, hy_out_w, ssd_in_w, ssd_conv_w, ssd_conv_b, ssd_dt_bias, ssd_A_log, ssd_D, ssd_norm_g, ssd_out_w, moe_router, moe_w_gate, moe_w_up, moe_w_down, final_norm_g):
    rows = x_sample.shape[1] // GRID_W
    xp = x_prompt
    xs = x_sample + _sincos_2d(rows, GRID_W, D_MODEL)[None]
    new_ssd = []
    for i in range(DEPTH):
        nb_s = c.shape[0]
        mods = _adaln(jnp.concatenate([c, c_ctx[None, :]], axis=0), ada_w[i], ada_b[i])
        sh1s, sc1s, g1s, sh2s, sc2s, g2s = [m[:nb_s] for m in mods]
        sh1p, sc1p, g1p, sh2p, sc2p, g2p = [m[nb_s:] for m in mods]
        j = i // N_MIXERS
        bp = (xp.shape[0], 1, D_MODEL)
        if i % N_MIXERS == 0:
            hy = (hy_in_w[j], hy_in_b[j], hy_short_w[j], hy_short_b[j], hy_f_w1[j], hy_f_b1[j],
                  hy_f_w2[j], hy_f_b2[j], hy_f_w3[j], hy_f_freq[j], hy_f_bias[j], hy_out_w[j])
            mp = _hyena_mixer_p(xp, norm_g[i, 0], jnp.broadcast_to(sh1p, bp), jnp.broadcast_to(sc1p, bp), *hy)
            ms = _hyena_mixer_p(xs, norm_g[i, 0], sh1s, sc1s, *hy)
        else:
            sp = (ssd_in_w[j], ssd_conv_w[j], ssd_conv_b[j], ssd_dt_bias[j], ssd_A_log[j],
                  ssd_D[j], ssd_norm_g[j], ssd_out_w[j])
            zeros = jnp.zeros((xp.shape[0], SSD_HEADS, SSD_HEAD_DIM, SSD_STATE), jnp.float32)
            mp, s_f, s_b = _ssd_mixer_p(xp, norm_g[i, 0], jnp.broadcast_to(sh1p, bp),
                                        jnp.broadcast_to(sc1p, bp), zeros, zeros, *sp)
            new_ssd.append(jnp.stack([s_f, s_b], axis=1))
            ms, _, _ = _ssd_mixer_p(xs, norm_g[i, 0], sh1s, sc1s, state_ssd[:, j, 0], state_ssd[:, j, 1], *sp)
        moe = (moe_router[i], moe_w_gate, moe_w_up, moe_w_down, i, final_norm_g if i == DEPTH - 1 else None)
        xp = _moe_block(xp, mp, jnp.broadcast_to(g1p, bp), norm_g[i, 1], jnp.broadcast_to(sh2p, bp),
                        jnp.broadcast_to(sc2p, bp), jnp.broadcast_to(g2p, bp), *moe)
        xs = _moe_block(xs, ms, g1s, norm_g[i, 1], sh2s, sc2s, g2s, *moe)
    new_state_ssd = jnp.stack(new_ssd, axis=1)
    return (xp, xs, new_state_ssd)
```

```python
import functools
import math

import jax
import jax.numpy as jnp
import numpy as np
from jax import lax
from jax.experimental import pallas as pl
from jax.experimental.pallas import tpu as pltpu

D_MODEL = 1024
DEPTH = 2
GRID_W = 64
N_MIXERS = 2
RMS_EPS = 1e-6
HY_EMB = 33
HY_BANDS = (HY_EMB - 1) // 2
HY_SHORT_DECAY_FRAC = 0.3
HY_LONG_DECAY_FRAC = 1.5
HY_DECAY_TARGET = 1e-2
HY_MAX_DECAY = math.log(HY_DECAY_TARGET) / HY_SHORT_DECAY_FRAC
HY_MIN_DECAY = math.log(HY_DECAY_TARGET) / HY_LONG_DECAY_FRAC
SSD_D_INNER = 2 * D_MODEL
SSD_HEAD_DIM = 64
SSD_HEADS = SSD_D_INNER // SSD_HEAD_DIM
SSD_GROUPS = 4
SSD_STATE = 128
SSD_CHUNK = 128
SSD_XBC = SSD_D_INNER + 2 * SSD_GROUPS * SSD_STATE
N_EXPERTS = 16
EC_FACTOR = 2

VMEM_LIMIT_BYTES = 48 * 1024 * 1024


def _mm_kernel(a_ref, b_ref, o_ref, acc_ref):
    @pl.when(pl.program_id(2) == 0)
    def _():
        acc_ref[...] = jnp.zeros_like(acc_ref)

    acc_ref[...] += jnp.dot(a_ref[...].astype(jnp.bfloat16), b_ref[...],
                            preferred_element_type=jnp.float32)

    @pl.when(pl.program_id(2) == pl.num_programs(2) - 1)
    def _():
        o_ref[...] = acc_ref[...]


def _pick(n, pref):
    for t in pref:
        if n % t == 0:
            return t
    return n


def _mm(a, b):
    m, k = a.shape
    n = b.shape[1]
    mp = -(-m // 8) * 8
    if mp != m:
        a = jnp.pad(a, ((0, mp - m), (0, 0)))
    tm = _pick(mp, (512, 256, 128, 64, 32, 16, 8))
    tn = _pick(n, (512, 256, 128))
    tk = _pick(k, (1024, 512, 256, 128))
    out = pl.pallas_call(
        _mm_kernel,
        grid=(mp // tm, n // tn, k // tk),
        in_specs=[pl.BlockSpec((tm, tk), lambda i, j, l: (i, l)),
                  pl.BlockSpec((tk, tn), lambda i, j, l: (l, j))],
        out_specs=pl.BlockSpec((tm, tn), lambda i, j, l: (i, j)),
        out_shape=jax.ShapeDtypeStruct((mp, n), jnp.float32),
        scratch_shapes=[pltpu.VMEM((tm, tn), jnp.float32)],
        compiler_params=pltpu.CompilerParams(
            dimension_semantics=("parallel", "parallel", "arbitrary"),
            vmem_limit_bytes=VMEM_LIMIT_BYTES),
        name="mm",
    )(a, b.astype(jnp.bfloat16))
    return out[:m]


def _mm3(a, b):
    lead = a.shape[:-1]
    return _mm(a.reshape(-1, a.shape[-1]), b).reshape(*lead, b.shape[1])


def _adaln(cond, ada_w, ada_b):
    m = _mm(jax.nn.silu(cond), ada_w) + ada_b
    return jnp.split(m[:, None, :], 6, axis=-1)


def _sincos_2d(rows, cols, d):
    q = d // 4
    omega = 1.0 / (10000.0 ** (jnp.arange(q, dtype=jnp.float32) / q))
    t = jnp.arange(rows * cols)
    er = (t // cols).astype(jnp.float32)[:, None] * omega[None, :]
    ec = (t % cols).astype(jnp.float32)[:, None] * omega[None, :]
    return jnp.concatenate([jnp.sin(er), jnp.cos(er), jnp.sin(ec), jnp.cos(ec)], axis=-1)


SSD_GN = SSD_GROUPS * SSD_STATE
SSD_GROUP_W = SSD_D_INNER // SSD_GROUPS
SSD_HEADS_PER_GROUP = SSD_HEADS // SSD_GROUPS
ROW_TILE = 256


def _modnorm(x, ng, sh, sc):
    ms = jnp.mean(x * x, axis=-1, keepdims=True)
    return (x * lax.rsqrt(ms + RMS_EPS) * ng) * (1.0 + sc) + sh


HALO = 8
CONV_COLS = 512


def _halo_rows(xm_ref, xp_ref, xn_ref, ng_ref, sh_ref, sc_ref):
    j = pl.program_id(1)
    xa = jnp.concatenate([xp_ref[0], xm_ref[0], xn_ref[0]], axis=0)
    h = _modnorm(xa, ng_ref[...], sh_ref[0], sc_ref[0]).astype(jnp.bfloat16)
    tm = xm_ref.shape[1]
    r = lax.broadcasted_iota(jnp.int32, (tm + 2 * HALO, 1), 0)
    valid = ((r >= HALO) | (j > 0)) & ((r < tm + HALO) | (j < pl.num_programs(1) - 1))
    return h, valid


def _proj_conv(h, valid, w_ref, w_col0, pb_ref, cw_ref, cb_ref, o_ref, tm, silu):
    taps = cw_ref.shape[0]
    ncol = o_ref.shape[2]
    for c0 in range(0, ncol, CONV_COLS):
        cols = slice(c0, c0 + CONV_COLS)
        u = jnp.dot(h, w_ref[:, w_col0 + c0:w_col0 + c0 + CONV_COLS], preferred_element_type=jnp.float32)
        if pb_ref is not None:
            u = u + pb_ref[:, cols]
        u = jnp.where(valid, u, 0.0)
        nrow = u.shape[0]
        acc = cb_ref[:, cols] + jnp.zeros((tm, CONV_COLS), jnp.float32)
        for k in range(taps):
            shifted = u if k == taps // 2 else pltpu.roll(u, (taps // 2 - k) % nrow, 0)
            acc = acc + cw_ref[k:k + 1, cols] * shifted[HALO:HALO + tm]
        if silu:
            acc = acc * jax.nn.sigmoid(acc)
        o_ref[0, :, cols] = acc


def _halo_specs(length, tm, d):
    nh = length // HALO
    per = tm // HALO
    main = pl.BlockSpec((1, tm, d), lambda i, j: (i, j, 0))
    prev = pl.BlockSpec((1, HALO, d), lambda i, j: (i, jnp.maximum(j * per - 1, 0), 0))
    nxt = pl.BlockSpec((1, HALO, d), lambda i, j: (i, jnp.minimum((j + 1) * per, nh - 1), 0))
    return [main, prev, nxt]


def _ssd_in_kernel(xm_ref, xp_ref, xn_ref, ng_ref, sh_ref, sc_ref, w_ref, wdt_ref, wdtt_ref, cw_ref, cb_ref,
                   z_ref, xbc_ref, dt_ref, dtt_ref):
    tm = xm_ref.shape[1]
    h, valid = _halo_rows(xm_ref, xp_ref, xn_ref, ng_ref, sh_ref, sc_ref)
    hm = h[HALO:HALO + tm]
    z_ref[0] = jnp.dot(hm, w_ref[:, :SSD_D_INNER], preferred_element_type=jnp.float32).astype(jnp.bfloat16)
    _proj_conv(h, valid, w_ref, SSD_D_INNER, None, cw_ref, cb_ref, xbc_ref, tm, silu=True)
    dt_ref[0] = jnp.dot(hm, wdt_ref[...], preferred_element_type=jnp.float32)
    dtt_ref[0] = lax.dot_general(wdtt_ref[...], hm, (((1,), (1,)), ((), ())),
                                 preferred_element_type=jnp.float32)


def _ssd_in(x, ng, sh, sc, in_w, conv_w, conv_b):
    b, length, d = x.shape
    tm = min(length, ROW_TILE)
    nzx = SSD_D_INNER + SSD_XBC
    w = in_w[:, :nzx].astype(jnp.bfloat16)
    wdt = in_w[:, nzx:]
    wdt_p = jnp.pad(wdt, ((0, 0), (0, LANES - 2 * SSD_HEADS))).astype(jnp.bfloat16)
    wdt_t = wdt.T.astype(jnp.bfloat16)
    taps = conv_w.shape[0]
    row = lambda i, j: (i, j, 0)
    per_b = lambda i, j: (i, 0, 0)
    full2 = lambda i, j: (0, 0)
    return pl.pallas_call(
        _ssd_in_kernel,
        grid=(b, length // tm),
        in_specs=_halo_specs(length, tm, d) + [
            pl.BlockSpec((1, d), full2), pl.BlockSpec((1, 1, d), per_b), pl.BlockSpec((1, 1, d), per_b),
            pl.BlockSpec((d, nzx), full2), pl.BlockSpec((d, LANES), full2),
            pl.BlockSpec((2 * SSD_HEADS, d), full2),
            pl.BlockSpec((taps, SSD_XBC), full2), pl.BlockSpec((1, SSD_XBC), full2)],
        out_specs=[pl.BlockSpec((1, tm, SSD_D_INNER), row), pl.BlockSpec((1, tm, SSD_XBC), row),
                   pl.BlockSpec((1, tm, LANES), row),
                   pl.BlockSpec((1, 2 * SSD_HEADS, tm), lambda i, j: (i, 0, j))],
        out_shape=[jax.ShapeDtypeStruct((b, length, SSD_D_INNER), jnp.bfloat16),
                   jax.ShapeDtypeStruct((b, length, SSD_XBC), jnp.float32),
                   jax.ShapeDtypeStruct((b, length, LANES), jnp.float32),
                   jax.ShapeDtypeStruct((b, 2 * SSD_HEADS, length), jnp.float32)],
        compiler_params=pltpu.CompilerParams(
            dimension_semantics=("parallel", "parallel"), vmem_limit_bytes=56 * 1024 * 1024),
        name="ssd_in",
    )(x, x, x, ng.reshape(1, d), sh, sc, w, wdt_p, wdt_t, conv_w, conv_b.reshape(1, SSD_XBC))


def _split3_bf16(v):
    p1 = v.astype(jnp.bfloat16)
    r1 = v - p1.astype(jnp.float32)
    p2 = r1.astype(jnp.bfloat16)
    p3 = (r1 - p2.astype(jnp.float32)).astype(jnp.bfloat16)
    return p1, p2, p3


def _softplus(v):
    return jnp.maximum(v, 0.0) + jnp.log1p(jnp.exp(-jnp.abs(v)))


def _expand_heads(cols, g, hoff):
    q = cols.shape[0]
    lane = lax.broadcasted_iota(jnp.int32, (q, LANES), 1)
    tiles = []
    for k in range(SSD_HEADS_PER_GROUP // 2):
        ha = hoff + g * SSD_HEADS_PER_GROUP + 2 * k
        tiles.append(jnp.take_along_axis(cols, jnp.where(lane < SSD_HEAD_DIM, ha, ha + 1), axis=1))
    return jnp.concatenate(tiles, axis=1)


def _ssd_scan_kernel(x_ref, b_ref, c_ref, dt_ref, dtt_ref, dtb_ref, dtbt_ref, a_ref, at_ref, init_ref, extra_ref,
                     y_ref, fin_ref, st_ref, *, reverse, hoff, add_prev):
    ci = pl.program_id(1)

    @pl.when(ci == 0)
    def _():
        st_ref[...] = init_ref[0]

    q = SSD_CHUNK
    f32, bf16 = jnp.float32, jnp.bfloat16
    dt = _softplus(dt_ref[0] + dtb_ref[...])
    dtt = _softplus(dtt_ref[0][hoff:hoff + SSD_HEADS, :] + dtbt_ref[...])
    ri = lax.broadcasted_iota(jnp.int32, (q, q), 0)
    cj = lax.broadcasted_iota(jnp.int32, (q, q), 1)
    keep = (cj >= ri) if reverse else (cj <= ri)
    tri = keep.astype(bf16)
    tri_t = ((ri >= cj) if reverse else (ri <= cj)).astype(bf16)
    acum = sum(jnp.dot(tri, p, preferred_element_type=f32) for p in _split3_bf16(dt * a_ref[...]))
    acum_t = sum(jnp.dot(p, tri_t, preferred_element_type=f32) for p in _split3_bf16(dtt * at_ref[...]))
    end = 0 if reverse else q - 1
    a_end = acum[end:end + 1, :]
    eacum = jnp.exp(acum)
    dt_dec_end = dt * jnp.exp(a_end - acum)
    lane = lax.broadcasted_iota(jnp.int32, (q, LANES), 1)
    for g in range(SSD_GROUPS):
        cg = c_ref[0][:, g * SSD_STATE:(g + 1) * SSD_STATE]
        bg = b_ref[0][:, g * SSD_STATE:(g + 1) * SSD_STATE]
        cg16 = cg.astype(bf16)
        cb = lax.dot_general(cg16, bg.astype(bf16), (((1,), (1,)), ((), ())), preferred_element_type=f32)
        xg = x_ref[0][:, g * SSD_GROUP_W:(g + 1) * SSD_GROUP_W]
        xg16 = xg.astype(bf16)
        eac_x = _expand_heads(eacum, g, hoff)
        yd = []
        for k in range(SSD_HEADS_PER_GROUP // 2):
            xp = xg16[:, k * LANES:(k + 1) * LANES]
            ys = []
            for hh in range(2):
                h = g * SSD_HEADS_PER_GROUP + 2 * k + hh
                seg = acum[:, hoff + h:hoff + h + 1] - acum_t[h:h + 1, :]
                lmat = jnp.exp(jnp.where(keep, seg, -jnp.inf)) * dtt[h:h + 1, :]
                ys.append(jnp.dot((cb * lmat).astype(bf16), xp, preferred_element_type=f32))
            yd.append(jnp.where(lane < SSD_HEAD_DIM, ys[0], ys[1]))
        st = st_ref[g]
        y_off = jnp.dot(cg16, st.astype(bf16), preferred_element_type=f32) * eac_x
        cols = slice(g * SSD_GROUP_W, (g + 1) * SSD_GROUP_W)
        if add_prev:
            other = extra_ref[0, :, cols].astype(f32)
        else:
            other = extra_ref[:, cols] * xg
        y_ref[0, :, cols] = (jnp.concatenate(yd, axis=1) + y_off + other).astype(bf16)
        xdd16 = (xg * _expand_heads(dt_dec_end, g, hoff)).astype(bf16)
        st_ref[g] = st * eac_x[end:end + 1, :] + jnp.dot(bg.T.astype(bf16), xdd16, preferred_element_type=f32)

    @pl.when(ci == pl.num_programs(1) - 1)
    def _():
        fin_ref[0] = st_ref[...]


def _ssd_scan_p(xbc, dt_raw, dt_raw_t, dt_bias, a, init, reverse, direction, y_prev=None, d_skip=None):
    b, length, _ = xbc.shape
    nc = length // SSD_CHUNK
    q = SSD_CHUNK
    cidx = (lambda j: nc - 1 - j) if reverse else (lambda j: j)
    nb = SSD_D_INNER // SSD_GN
    hoff = direction * SSD_HEADS
    full2 = lambda i, j: (0, 0)
    st_shape = (SSD_GROUPS, SSD_STATE, SSD_GROUP_W)
    lanes = lambda v: jnp.pad(v, (hoff, LANES - hoff - SSD_HEADS)).reshape(1, LANES)
    add_prev = y_prev is not None
    if add_prev:
        extra = y_prev
        extra_spec = pl.BlockSpec((1, q, SSD_D_INNER), lambda i, j: (i, cidx(j), 0))
    else:
        extra = jnp.repeat(d_skip, SSD_HEAD_DIM).reshape(1, SSD_D_INNER)
        extra_spec = pl.BlockSpec((1, SSD_D_INNER), full2)
    return pl.pallas_call(
        functools.partial(_ssd_scan_kernel, reverse=reverse, hoff=hoff, add_prev=add_prev),
        grid=(b, nc),
        in_specs=[pl.BlockSpec((1, q, SSD_D_INNER), lambda i, j: (i, cidx(j), 0)),
                  pl.BlockSpec((1, q, SSD_GN), lambda i, j: (i, cidx(j), nb)),
                  pl.BlockSpec((1, q, SSD_GN), lambda i, j: (i, cidx(j), nb + 1)),
                  pl.BlockSpec((1, q, LANES), lambda i, j: (i, cidx(j), 0)),
                  pl.BlockSpec((1, 2 * SSD_HEADS, q), lambda i, j: (i, 0, cidx(j))),
                  pl.BlockSpec((1, LANES), full2), pl.BlockSpec((SSD_HEADS, 1), full2),
                  pl.BlockSpec((1, LANES), full2), pl.BlockSpec((SSD_HEADS, 1), full2),
                  pl.BlockSpec((1,) + st_shape, lambda i, j: (i, 0, 0, 0)),
                  extra_spec],
        out_specs=[pl.BlockSpec((1, q, SSD_D_INNER), lambda i, j: (i, cidx(j), 0)),
                   pl.BlockSpec((1,) + st_shape, lambda i, j: (i, 0, 0, 0))],
        out_shape=[jax.ShapeDtypeStruct((b, length, SSD_D_INNER), jnp.bfloat16),
                   jax.ShapeDtypeStruct((b,) + st_shape, jnp.float32)],
        scratch_shapes=[pltpu.VMEM(st_shape, jnp.float32)],
        compiler_params=pltpu.CompilerParams(
            dimension_semantics=("parallel", "arbitrary"), vmem_limit_bytes=VMEM_LIMIT_BYTES),
        name="ssd_scan",
    )(xbc, xbc, xbc, dt_raw, dt_raw_t, lanes(dt_bias), dt_bias.reshape(-1, 1),
      lanes(a), a.reshape(-1, 1), init, extra)


def _ssd_out_kernel(y_ref, z_ref, ng_ref, w_ref, o_ref):
    z = z_ref[0].astype(jnp.float32)
    y = y_ref[0].astype(jnp.float32) * (z * jax.nn.sigmoid(z))
    ms = jnp.mean(y * y, axis=-1, keepdims=True)
    y = y * lax.rsqrt(ms + RMS_EPS) * ng_ref[...]
    o_ref[0] = jnp.dot(y.astype(jnp.bfloat16), w_ref[...], preferred_element_type=jnp.float32)


def _ssd_out(y, z, norm_g, out_w):
    b, length, di = y.shape
    d = out_w.shape[1]
    tm = min(length, 2 * ROW_TILE)
    row = lambda i, j: (i, j, 0)
    full2 = lambda i, j: (0, 0)
    return pl.pallas_call(
        _ssd_out_kernel,
        grid=(b, length // tm),
        in_specs=[pl.BlockSpec((1, tm, di), row), pl.BlockSpec((1, tm, di), row),
                  pl.BlockSpec((1, di), full2), pl.BlockSpec((di, d), full2)],
        out_specs=pl.BlockSpec((1, tm, d), row),
        out_shape=jax.ShapeDtypeStruct((b, length, d), jnp.float32),
        compiler_params=pltpu.CompilerParams(
            dimension_semantics=("parallel", "parallel"), vmem_limit_bytes=VMEM_LIMIT_BYTES),
        name="ssd_out",
    )(y, z, norm_g.reshape(1, di), out_w.astype(jnp.bfloat16))


def _state_to_kernel(s):
    b = s.shape[0]
    s = s.reshape(b, SSD_GROUPS, SSD_HEADS_PER_GROUP, SSD_HEAD_DIM, SSD_STATE)
    return jnp.transpose(s, (0, 1, 4, 2, 3)).reshape(b, SSD_GROUPS, SSD_STATE, SSD_GROUP_W)


def _state_from_kernel(s):
    b = s.shape[0]
    s = s.reshape(b, SSD_GROUPS, SSD_STATE, SSD_HEADS_PER_GROUP, SSD_HEAD_DIM)
    return jnp.transpose(s, (0, 1, 3, 4, 2)).reshape(b, SSD_HEADS, SSD_HEAD_DIM, SSD_STATE)


def _ssd_mixer_p(x, ng, sh, sc, init_f, init_b, in_w, conv_w, conv_b, dt_bias, a_log, d_skip, norm_g, out_w):
    z, xbc, dt_raw, dt_raw_t = _ssd_in(x, ng, sh, sc, in_w, conv_w, conv_b)
    a = -jnp.exp(a_log)
    yf, s_f = _ssd_scan_p(xbc, dt_raw, dt_raw_t, dt_bias[0], a[0], _state_to_kernel(init_f), False, 0,
                          d_skip=d_skip)
    y, s_b = _ssd_scan_p(xbc, dt_raw, dt_raw_t, dt_bias[1], a[1], _state_to_kernel(init_b), True, 1, y_prev=yf)
    m = _ssd_out(y, z, norm_g, out_w)
    return m, _state_from_kernel(s_f), _state_from_kernel(s_b)


HY_MAX_BLOCK = 512
HY_CC = 128
HY_MAC_ELEMS = 8192
HY_HIDDEN = 64
HY_FEAT_ROWS = 64


def _odd_dft_tables(n):
    m = np.arange(n, dtype=np.int64)[:, None]
    f = np.arange(n // 2, dtype=np.int64)[None, :]
    ang = 2.0 * np.pi * (((2 * f + 1) * m) % (2 * n)).astype(np.float64) / (2 * n)
    return np.cos(ang), np.sin(ang)


def _hy_in_kernel(xm_ref, xp_ref, xn_ref, ng_ref, sh_ref, sc_ref, w_ref, b_ref, cw_ref, cb_ref, o_ref):
    tm = xm_ref.shape[1]
    h, valid = _halo_rows(xm_ref, xp_ref, xn_ref, ng_ref, sh_ref, sc_ref)
    _proj_conv(h, valid, w_ref, 0, b_ref, cw_ref, cb_ref, o_ref, tm, silu=False)


def _hy_in(x, ng, sh, sc, in_w, in_b, short_w, short_b):
    b, length, d = x.shape
    n = in_w.shape[1]
    tm = min(length, ROW_TILE)
    taps = short_w.shape[0]
    per_b = lambda i, j: (i, 0, 0)
    full2 = lambda i, j: (0, 0)
    return pl.pallas_call(
        _hy_in_kernel,
        grid=(b, length // tm),
        in_specs=_halo_specs(length, tm, d) + [
            pl.BlockSpec((1, d), full2), pl.BlockSpec((1, 1, d), per_b), pl.BlockSpec((1, 1, d), per_b),
            pl.BlockSpec((d, n), full2), pl.BlockSpec((1, n), full2),
            pl.BlockSpec((taps, n), full2), pl.BlockSpec((1, n), full2)],
        out_specs=pl.BlockSpec((1, tm, n), lambda i, j: (i, j, 0)),
        out_shape=jax.ShapeDtypeStruct((b, length, n), jnp.float32),
        compiler_params=pltpu.CompilerParams(
            dimension_semantics=("parallel", "parallel"), vmem_limit_bytes=VMEM_LIMIT_BYTES),
        name="hy_in",
    )(x, x, x, ng.reshape(1, d), sh, sc, in_w.astype(jnp.bfloat16), in_b.reshape(1, n), short_w, short_b.reshape(1, n))


def _dot3(a, b):
    a_hi, a_lo = _split_bf16(a)
    b_hi, b_lo = _split_bf16(b)
    f32 = jnp.float32
    return (jnp.dot(a_hi, b_hi, preferred_element_type=f32) + jnp.dot(a_lo, b_hi, preferred_element_type=f32)
            + jnp.dot(a_hi, b_lo, preferred_element_type=f32))


def _hy_filter_kernel(w1t_ref, b1_ref, w2t_ref, b2_ref, w3t_ref, fr_ref, dl_ref, o_ref, *, length, blk):
    k = pl.program_id(0)
    q = (lax.broadcasted_iota(jnp.int32, (1, blk), 1) + k * blk)
    pos = jnp.abs(q - length).astype(jnp.float32)
    t = pos / float(length - 1)
    w = (2.0 * math.pi / length) * pos
    band = lax.broadcasted_iota(jnp.int32, (HY_BANDS, 1), 0).astype(jnp.float32)
    fb = 1e-4 + band * ((HY_BANDS - 1 - 1e-4) / (HY_BANDS - 1))
    z = jnp.concatenate([jnp.broadcast_to(t, (8, blk)), jnp.cos(fb * w), -jnp.sin(fb * w),
                         jnp.zeros((HY_FEAT_ROWS - 8 - 2 * HY_BANDS, blk), jnp.float32)], axis=0)
    h = jnp.sin(fr_ref[...] * (_dot3(w1t_ref[...], z) + b1_ref[...]))
    h = jnp.sin(fr_ref[...] * (_dot3(w2t_ref[...], h) + b2_ref[...]))
    kt = _dot3(w3t_ref[0], h)
    o_ref[0] = kt * jnp.exp(-t * dl_ref[...])


def _hy_filter(length, blk, f_w1, f_b1, f_w2, f_b2, f_w3, f_freq):
    d = f_w3.shape[1] // 2
    nk = 2 * length // blk
    w1t = jnp.concatenate([f_w1[0:1].T, jnp.zeros((HY_HIDDEN, 7), jnp.float32), f_w1[1:].T,
                           jnp.zeros((HY_HIDDEN, HY_FEAT_ROWS - 8 - 2 * HY_BANDS), jnp.float32)], axis=1)
    w3t = jnp.stack([f_w3[:, d:].T, f_w3[:, :d].T])
    deltas = jnp.abs(jnp.linspace(HY_MIN_DECAY, HY_MAX_DECAY, d, dtype=jnp.float32)).reshape(d, 1)
    col = lambda v: v.reshape(HY_HIDDEN, 1)
    full2 = lambda k: (0, 0)
    half = length // blk
    return pl.pallas_call(
        functools.partial(_hy_filter_kernel, length=length, blk=blk),
        grid=(nk,),
        in_specs=[pl.BlockSpec((HY_HIDDEN, HY_FEAT_ROWS), full2), pl.BlockSpec((HY_HIDDEN, 1), full2),
                  pl.BlockSpec((HY_HIDDEN, HY_HIDDEN), full2), pl.BlockSpec((HY_HIDDEN, 1), full2),
                  pl.BlockSpec((1, d, HY_HIDDEN), lambda k: (k // half, 0, 0)),
                  pl.BlockSpec((HY_HIDDEN, 1), full2), pl.BlockSpec((d, 1), full2)],
        out_specs=pl.BlockSpec((1, d, blk), lambda k: (k, 0, 0)),
        out_shape=jax.ShapeDtypeStruct((nk, d, blk), jnp.float32),
        compiler_params=pltpu.CompilerParams(
            dimension_semantics=("parallel",), vmem_limit_bytes=VMEM_LIMIT_BYTES),
        name="hy_filter",
    )(w1t, col(f_b1), f_w2.T, col(f_b2), w3t, col(f_freq), deltas)


def _hy_gspec_kernel(hi_ref, lo_ref, ft_ref, fb_ref, o_ref):
    o_ref[0] = _dot3(hi_ref[0], ft_ref[...]) + _dot3(lo_ref[0], fb_ref[...])


def _hy_gspec(kt):
    nk, d, blk = kt.shape
    cos, sin = _odd_dft_tables(2 * blk)
    top = np.concatenate([cos[:blk], -sin[:blk]], axis=1)
    bot = -np.concatenate([cos[blk:], -sin[blk:]], axis=1)
    bot[0] = 0.0
    tm = 512
    full2 = lambda e, i: (0, 0)
    return pl.pallas_call(
        _hy_gspec_kernel,
        grid=(nk - 1, d // tm),
        in_specs=[pl.BlockSpec((1, tm, blk), lambda e, i: (e + 1, i, 0)),
                  pl.BlockSpec((1, tm, blk), lambda e, i: (e, i, 0)),
                  pl.BlockSpec((blk, 2 * blk), full2), pl.BlockSpec((blk, 2 * blk), full2)],
        out_specs=pl.BlockSpec((1, tm, 2 * blk), lambda e, i: (e, i, 0)),
        out_shape=jax.ShapeDtypeStruct((nk - 1, d, 2 * blk), jnp.float32),
        compiler_params=pltpu.CompilerParams(
            dimension_semantics=("parallel", "parallel"), vmem_limit_bytes=VMEM_LIMIT_BYTES),
        name="hy_gspec",
    )(kt, kt, jnp.asarray(top, jnp.float32), jnp.asarray(bot, jnp.float32))


def _hy_conv_kernel(x0_ref, x1_ref, v_ref, g_ref, fb_ref, ff_ref, fi_ref, o_ref, lhs_ref, u_ref, y_ref, *, nb):
    cc, bsz = HY_CC, ff_ref.shape[0]
    mrows = HY_MAC_ELEMS // bsz
    for j in range(nb):
        sl = slice(j * bsz, (j + 1) * bsz)
        wj = v_ref[0, sl, :] * x1_ref[0, sl, :]
        lhs_ref[j * cc:(j + 1) * cc, :] = wj.T.astype(jnp.bfloat16)
    u_ref[...] = jnp.dot(lhs_ref[...], ff_ref[...], preferred_element_type=jnp.float32)

    def per_out_block(i, carry):
        def per_rows(rc, carry2):
            rows = pl.ds(pl.multiple_of(rc * mrows, mrows), mrows)
            acc_r = jnp.zeros((mrows, bsz), jnp.float32)
            acc_i = jnp.zeros((mrows, bsz), jnp.float32)
            for j in range(nb):
                e = i - j + (nb - 1)
                gr = g_ref[e, rows, 0:bsz]
                gi = g_ref[e, rows, bsz:2 * bsz]
                urows = pl.ds(pl.multiple_of(j * cc + rc * mrows, mrows), mrows)
                ur = u_ref[urows, 0:bsz]
                ui = u_ref[urows, bsz:2 * bsz]
                acc_r = acc_r + gr * ur - gi * ui
                acc_i = acc_i + gr * ui + gi * ur
            yrows = pl.ds(pl.multiple_of(i * cc + rc * mrows, mrows), mrows)
            y_ref[yrows, 0:bsz] = acc_r.astype(jnp.bfloat16)
            y_ref[yrows, bsz:2 * bsz] = acc_i.astype(jnp.bfloat16)
            return carry2
        return lax.fori_loop(0, cc // mrows, per_rows, carry)
    lax.fori_loop(0, nb, per_out_block, 0)

    yt = jnp.dot(y_ref[...], fi_ref[...], preferred_element_type=jnp.float32)
    for i in range(nb):
        sl = slice(i * bsz, (i + 1) * bsz)
        w = v_ref[0, sl, :] * x1_ref[0, sl, :]
        gated = (yt[i * cc:(i + 1) * cc, :].T + fb_ref[...] * w) * x0_ref[0, sl, :]
        o_ref[0, sl, :] = gated.astype(o_ref.dtype)


def _hy_conv(u, g, f_bias, blk):
    b, length, d3 = u.shape
    d = d3 // 3
    nb = length // blk
    ncb = d // HY_CC
    cos, sin = _odd_dft_tables(2 * blk)
    fwd = np.concatenate([cos[:blk], -sin[:blk]], axis=1)
    inv = (1.0 / blk) * np.concatenate([cos[:blk].T, -sin[:blk].T], axis=0)
    col = lambda off: pl.BlockSpec((1, length, HY_CC), lambda c, i, off=off: (i, 0, off + c))
    full2 = lambda c, i: (0, 0)
    return pl.pallas_call(
        functools.partial(_hy_conv_kernel, nb=nb),
        grid=(ncb, b),
        in_specs=[col(0), col(ncb), col(2 * ncb),
                  pl.BlockSpec((2 * nb - 1, HY_CC, 2 * blk), lambda c, i: (0, c, 0)),
                  pl.BlockSpec((1, HY_CC), lambda c, i: (0, c)),
                  pl.BlockSpec((blk, 2 * blk), full2), pl.BlockSpec((2 * blk, blk), full2)],
        out_specs=pl.BlockSpec((1, length, HY_CC), lambda c, i: (i, 0, c)),
        out_shape=jax.ShapeDtypeStruct((b, length, d), jnp.bfloat16),
        scratch_shapes=[pltpu.VMEM((nb * HY_CC, blk), jnp.bfloat16),
                        pltpu.VMEM((nb * HY_CC, 2 * blk), jnp.float32),
                        pltpu.VMEM((nb * HY_CC, 2 * blk), jnp.bfloat16)],
        compiler_params=pltpu.CompilerParams(
            dimension_semantics=("parallel", "arbitrary"), vmem_limit_bytes=56 * 1024 * 1024),
        name="hy_conv",
    )(u, u, u, g, f_bias.reshape(1, d), jnp.asarray(fwd, jnp.bfloat16), jnp.asarray(inv, jnp.bfloat16))


def _hyena_mixer_p(x, ng, sh, sc, in_w, in_b, short_w, short_b, f_w1, f_b1, f_w2, f_b2, f_w3, f_freq, f_bias, out_w):
    length = x.shape[1]
    blk = min(HY_MAX_BLOCK, length)
    u = _hy_in(x, ng, sh, sc, in_w, in_b, short_w, short_b)
    g = _hy_gspec(_hy_filter(length, blk, f_w1, f_b1, f_w2, f_b2, f_w3, f_freq))
    return _mm3(_hy_conv(u, g, f_bias, blk), out_w)


TOK_TILE = 256
MOE_ROWS = 512
SEG_CHUNK = 64
BF16_TILE_ROWS = 16
LANES = 128


def _split_bf16(w):
    hi = w.astype(jnp.bfloat16)
    lo = (w - hi.astype(jnp.float32)).astype(jnp.bfloat16)
    return hi, lo


def _moe_pre_kernel(x_ref, m_ref, g1_ref, ng_ref, sh_ref, sc_ref, wrh_ref, wrl_ref,
                    xo_ref, hpk_ref, lg_ref):
    x = x_ref[0] + g1_ref[0] * m_ref[0]
    xo_ref[0] = x
    ms = jnp.mean(x * x, axis=-1, keepdims=True)
    h = x * lax.rsqrt(ms + RMS_EPS) * ng_ref[...]
    h = h * (1.0 + sc_ref[0]) + sh_ref[0]
    h_hi = h.astype(jnp.bfloat16)
    h_lo = (h - h_hi.astype(jnp.float32)).astype(jnp.bfloat16)
    dn = (((1,), (1,)), ((), ()))
    lg = lax.dot_general(wrh_ref[...], h_hi, dn, preferred_element_type=jnp.float32)
    lg += lax.dot_general(wrh_ref[...], h_lo, dn, preferred_element_type=jnp.float32)
    lg += lax.dot_general(wrl_ref[...], h_hi, dn, preferred_element_type=jnp.float32)
    lg_ref[0] = lg
    half = h.shape[1] // 2
    wa = pltpu.bitcast(h_hi[:, :half].astype(jnp.float32), jnp.uint32) >> 16
    wb = pltpu.bitcast(h_hi[:, half:].astype(jnp.float32), jnp.uint32) & jnp.uint32(0xFFFF0000)
    hpk_ref[0] = wa | wb


def _moe_pre(x, m, g1, ng, sh, sc, w_router):
    b, length, d = x.shape
    tm = min(length, 512)
    wrh, wrl = _split_bf16(w_router.T)
    row = lambda i, j: (i, j, 0)
    per_b = lambda i, j: (i, 0, 0)
    full2 = lambda i, j: (0, 0)
    return pl.pallas_call(
        _moe_pre_kernel,
        grid=(b, length // tm),
        in_specs=[pl.BlockSpec((1, tm, d), row), pl.BlockSpec((1, tm, d), row),
                  pl.BlockSpec((1, 1, d), per_b), pl.BlockSpec((1, d), full2),
                  pl.BlockSpec((1, 1, d), per_b), pl.BlockSpec((1, 1, d), per_b),
                  pl.BlockSpec((N_EXPERTS, d), full2), pl.BlockSpec((N_EXPERTS, d), full2)],
        out_specs=[pl.BlockSpec((1, tm, d), row), pl.BlockSpec((1, tm, d // 2), row),
                   pl.BlockSpec((1, N_EXPERTS, tm), lambda i, j: (i, 0, j))],
        out_shape=[jax.ShapeDtypeStruct((b, length, d), jnp.float32),
                   jax.ShapeDtypeStruct((b, length, d // 2), jnp.uint32),
                   jax.ShapeDtypeStruct((b, N_EXPERTS, length), jnp.float32)],
        compiler_params=pltpu.CompilerParams(
            dimension_semantics=("parallel", "parallel"), vmem_limit_bytes=VMEM_LIMIT_BYTES),
        name="moe_pre",
    )(x, m, g1, ng.reshape(1, d), sh, sc, wrh, wrl)


SEL_BLOCK = 256
SEL_ROWS = 64
SEL_LONG = 1024
F32_INF_BITS = 0x7F800000


def _prefix_counts(flags, inclusive):
    e, length = flags.shape
    r = lax.broadcasted_iota(jnp.int32, (SEL_BLOCK, SEL_BLOCK), 0)
    c = lax.broadcasted_iota(jnp.int32, (SEL_BLOCK, SEL_BLOCK), 1)
    tri = ((r <= c) if inclusive else (r < c)).astype(jnp.bfloat16)
    off = jnp.zeros((e, 1), jnp.float32)
    blocks = []
    for k in range(length // SEL_BLOCK):
        blk = flags[:, k * SEL_BLOCK:(k + 1) * SEL_BLOCK]
        blocks.append(jnp.dot(blk.astype(jnp.bfloat16), tri, preferred_element_type=jnp.float32) + off)
        off = off + jnp.sum(blk, axis=1, keepdims=True)
    return jnp.concatenate(blocks, axis=1)


def _moe_select_kernel(lg_ref, aff_ref, idx_ref, rank_ref, cnt_ref, *, cap):
    lg = lg_ref[...]
    ng, ne, length = lg.shape
    nr = ng * ne
    ex = jnp.exp(lg - jnp.max(lg, axis=1, keepdims=True))
    aff3 = ex / jnp.sum(ex, axis=1, keepdims=True)
    aff_ref[...] = aff3
    aff = aff3.reshape(nr, length)
    bits = pltpu.bitcast(aff, jnp.int32)

    def bisect(_, carry):
        lo, hi = carry
        mid = lo + ((hi - lo + 1) >> 1)
        cnt = jnp.sum((bits >= mid).astype(jnp.float32), axis=1, keepdims=True)
        ok = cnt >= cap
        return jnp.where(ok, mid, lo), jnp.where(ok, hi, mid - 1)
    lo0 = jnp.zeros((nr, 1), jnp.int32)
    tau, _ = lax.fori_loop(0, 32, bisect, (lo0, lo0 + F32_INF_BITS))
    gt = bits > tau
    eq = (bits == tau).astype(jnp.float32)
    need = cap - jnp.sum(gt.astype(jnp.float32), axis=1, keepdims=True)
    keep = jnp.where(gt, 1.0, jnp.where(_prefix_counts(eq, False) < need, eq, 0.0))
    rank_ref[...] = _prefix_counts(keep, True)

    rows = min(SEL_ROWS, cap)
    lane = lax.broadcasted_iota(jnp.int32, (cap, LANES), 1)
    cnt_ref[...] = jnp.zeros((cap, LANES), jnp.float32)

    def per_row(ri, carry):
        rk = rank_ref[pl.ds(ri, 1), :]
        cols = []
        for ck in range(cap // rows):
            slot = (lax.broadcasted_iota(jnp.int32, (rows, LANES), 0) + ck * rows).astype(jnp.float32)
            acc = jnp.zeros((rows, LANES), jnp.float32)
            for j in range(length // LANES):
                acc = acc + jnp.where(rk[:, j * LANES:(j + 1) * LANES] <= slot, 1.0, 0.0)
            cols.append(jnp.sum(acc, axis=1, keepdims=True))
        col = jnp.concatenate(cols, axis=0)
        cnt_ref[...] = jnp.where(lane == ri, col, cnt_ref[...])
        return carry
    lax.fori_loop(0, nr, per_row, 0)
    for g in range(ng):
        idx_ref[g] = cnt_ref[:, g * ne:(g + 1) * ne].astype(jnp.int32)


def _moe_select(lg, cap):
    b, ne, length = lg.shape
    grp = 1 if length >= SEL_LONG else min(b, LANES // ne)
    aff, idx = pl.pallas_call(
        functools.partial(_moe_select_kernel, cap=cap),
        grid=(b // grp,),
        in_specs=[pl.BlockSpec((grp, ne, length), lambda i: (i, 0, 0))],
        out_specs=[pl.BlockSpec((grp, ne, length), lambda i: (i, 0, 0)),
                   pl.BlockSpec((grp, cap, ne), lambda i: (i, 0, 0))],
        out_shape=[jax.ShapeDtypeStruct((b, ne, length), jnp.float32),
                   jax.ShapeDtypeStruct((b, cap, ne), jnp.int32)],
        scratch_shapes=[pltpu.VMEM((grp * ne, length), jnp.float32), pltpu.VMEM((cap, LANES), jnp.float32)],
        compiler_params=pltpu.CompilerParams(
            dimension_semantics=("parallel",), vmem_limit_bytes=VMEM_LIMIT_BYTES),
        name="moe_select",
    )(lg)
    return aff, jnp.swapaxes(idx, 1, 2)


def _moe_ffn_kernel(idx_ref, nidx_ref, h_hbm, gate_ref, wg_ref, wu_ref, wd_ref, y_ref,
                    xe_ref, wgb, wub, wdb, sem):
    nblk = pl.num_programs(1)
    step = pl.program_id(0) * nblk + pl.program_id(1)
    last = pl.num_programs(0) * nblk - 1

    def issue(ids_ref, slot):
        base = slot * MOE_ROWS
        for c in range(MOE_ROWS):
            pltpu.make_async_copy(h_hbm.at[pl.ds(ids_ref[0, 0, c], 1)], xe_ref.at[pl.ds(base + c, 1)],
                                  sem.at[slot]).start(priority=c % 2)

    @pl.when(step == 0)
    def _():
        issue(idx_ref, 0)

    for parity in range(2):
        @pl.when((step < last) & (step % 2 == parity))
        def _(parity=parity):
            issue(nidx_ref, 1 - parity)

    @pl.when(pl.program_id(1) == 0)
    def _():
        wgb[...] = wg_ref[0, 0].astype(jnp.bfloat16)
        wub[...] = wu_ref[0, 0].astype(jnp.bfloat16)
        wdb[...] = wd_ref[0, 0].astype(jnp.bfloat16)

    slot = step % 2
    rows = pl.ds(pl.multiple_of(slot * MOE_ROWS, MOE_ROWS), MOE_ROWS)
    pltpu.make_async_copy(h_hbm.at[pl.ds(0, MOE_ROWS)], xe_ref.at[rows], sem.at[slot]).wait()
    half = wgb.shape[0] // 2
    w = xe_ref[rows, :]
    xa = pltpu.bitcast(w << 16, jnp.float32).astype(jnp.bfloat16)
    xb = pltpu.bitcast(w & jnp.uint32(0xFFFF0000), jnp.float32).astype(jnp.bfloat16)
    hg = jnp.dot(xa, wgb[:half], preferred_element_type=jnp.float32)
    hg += jnp.dot(xb, wgb[half:], preferred_element_type=jnp.float32)
    hu = jnp.dot(xa, wub[:half], preferred_element_type=jnp.float32)
    hu += jnp.dot(xb, wub[half:], preferred_element_type=jnp.float32)
    hid = (hg * jax.nn.sigmoid(hg) * hu).astype(jnp.bfloat16)
    y = jnp.dot(hid, wdb[...], preferred_element_type=jnp.float32)
    y_ref[0] = (y * gate_ref[0]).astype(jnp.bfloat16)


def _moe_ffn(hpk, grow, gate, w_gate, w_up, w_down, layer):
    e, r = grow.shape
    d, f = w_gate.shape[2], w_gate.shape[3]
    nblk = r // MOE_ROWS
    nsteps = e * nblk
    wspec = lambda shp: pl.BlockSpec((1, 1) + shp, lambda i, j: (layer, i, 0, 0))
    ids = grow.reshape(nsteps, 1, MOE_ROWS)
    smem_ids = lambda off: pl.BlockSpec(
        (1, 1, MOE_ROWS), lambda i, j: (jnp.minimum(i * nblk + j + off, nsteps - 1), 0, 0), memory_space=pltpu.SMEM)
    return pl.pallas_call(
        _moe_ffn_kernel,
        grid=(e, nblk),
        in_specs=[smem_ids(0), smem_ids(1),
                  pl.BlockSpec(memory_space=pltpu.HBM),
                  pl.BlockSpec((1, MOE_ROWS, 1), lambda i, j: (i, j, 0)),
                  wspec((d, f)), wspec((d, f)), wspec((f, d))],
        out_specs=pl.BlockSpec((1, MOE_ROWS, d), lambda i, j: (i, j, 0)),
        out_shape=jax.ShapeDtypeStruct((e, r, d), jnp.bfloat16),
        scratch_shapes=[pltpu.VMEM((2 * MOE_ROWS, d // 2), jnp.uint32),
                        pltpu.VMEM((d, f), jnp.bfloat16), pltpu.VMEM((d, f), jnp.bfloat16),
                        pltpu.VMEM((f, d), jnp.bfloat16),
                        pltpu.SemaphoreType.DMA((2,))],
        compiler_params=pltpu.CompilerParams(
            dimension_semantics=("arbitrary", "arbitrary"), vmem_limit_bytes=VMEM_LIMIT_BYTES),
        name="moe_ffn",
    )(ids, ids, hpk, gate, w_gate, w_up, w_down)


def _moe_comb_kernel(cs_ref, x_ref, g2_ref, y_ref, idx_ref, fg_ref, o_ref, ycat, acc, *, cap, ch, ntile, final_norm):
    b = pl.program_id(0)
    t = pl.program_id(1)
    base = t * TOK_TILE
    sub = lax.broadcasted_iota(jnp.int32, (TOK_TILE, LANES), 0) + base
    if ntile == 1:
        for e in range(N_EXPERTS):
            ycat[e * ch:(e + 1) * ch, :] = y_ref[e, 0:ch, :]
        v = idx_ref[0]
        tiles = [(v[:, p * LANES:(p + 1) * LANES] == sub).astype(jnp.bfloat16)
                 for p in range(N_EXPERTS * ch // LANES)]
        acc[...] = jnp.dot(jnp.concatenate(tiles, axis=1), ycat[...], preferred_element_type=jnp.float32)
    else:
        lane = lax.broadcasted_iota(jnp.int32, (1, LANES), 1)
        per = LANES // ch
        sts = []
        for e in range(N_EXPERTS):
            s0 = cs_ref[(b * N_EXPERTS + e) * (ntile + 1) + t]
            st = jnp.minimum((s0 // BF16_TILE_ROWS) * BF16_TILE_ROWS, cap - ch)
            st = pl.multiple_of(st, BF16_TILE_ROWS)
            sts.append(st)
            ycat[e * ch:(e + 1) * ch, :] = y_ref[e, pl.ds(st, ch), :]
        tiles = []
        for p in range(N_EXPERTS // per):
            v = None
            for q in range(per):
                e = p * per + q
                r = pltpu.roll(idx_ref[0, e:e + 1, :], (2 * cap - sts[e] + q * ch) % cap, 1)[:, :LANES]
                v = r if v is None else jnp.where(lane >= q * ch, r, v)
            tiles.append((v == sub).astype(jnp.bfloat16))
        acc[...] = jnp.dot(jnp.concatenate(tiles, axis=1), ycat[...], preferred_element_type=jnp.float32)
        sub_c = lax.broadcasted_iota(jnp.int32, (TOK_TILE, ch), 0) + base
        lane_c = lax.broadcasted_iota(jnp.int32, (1, ch), 1)
        for e in range(N_EXPERTS):
            s1 = cs_ref[(b * N_EXPERTS + e) * (ntile + 1) + t + 1]
            first_end = sts[e] + ch
            n_extra = jnp.maximum(s1 - first_end + ch - 1, 0) // ch

            def extra(q, carry, e=e, first_end=first_end):
                lo = first_end + q * ch
                stq = pl.multiple_of(jnp.minimum(lo, cap - ch), BF16_TILE_ROWS)
                r = pltpu.roll(idx_ref[0, e:e + 1, :], (2 * cap - stq) % cap, 1)[:, :ch]
                hit = (r == sub_c) & (lane_c + stq >= lo)
                acc[...] += jnp.dot(hit.astype(jnp.bfloat16), y_ref[e, pl.ds(stq, ch), :],
                                    preferred_element_type=jnp.float32)
                return carry
            lax.fori_loop(0, n_extra, extra, 0)
    out = x_ref[0] + g2_ref[0] * acc[...]
    if final_norm:
        ms = jnp.mean(out * out, axis=-1, keepdims=True)
        out = out * lax.rsqrt(ms + RMS_EPS) * fg_ref[...]
    o_ref[0] = out


def _moe_combine(x, g2, y, idx, cs, final_g=None):
    b, length, d = x.shape
    final_norm = final_g is not None
    fg = (final_g if final_norm else jnp.ones((d,), jnp.float32)).reshape(1, d)
    cap = idx.shape[2]
    ntile = length // TOK_TILE
    ch = min(SEG_CHUNK, cap)
    if ntile == 1:
        idx_in = idx.reshape(b, 1, N_EXPERTS * cap)
        idx_spec = pl.BlockSpec((1, 1, N_EXPERTS * cap), lambda i, j, c: (i, 0, 0))
    else:
        idx_in = idx
        idx_spec = pl.BlockSpec((1, N_EXPERTS, cap), lambda i, j, c: (i, 0, 0))
    grid_spec = pltpu.PrefetchScalarGridSpec(
        num_scalar_prefetch=1,
        grid=(b, ntile),
        in_specs=[pl.BlockSpec((1, TOK_TILE, d), lambda i, j, c: (i, j, 0)),
                  pl.BlockSpec((1, 1, d), lambda i, j, c: (i, 0, 0)),
                  pl.BlockSpec((N_EXPERTS, cap, d), lambda i, j, c: (0, i, 0)),
                  idx_spec,
                  pl.BlockSpec((1, d), lambda i, j, c: (0, 0))],
        out_specs=pl.BlockSpec((1, TOK_TILE, d), lambda i, j, c: (i, j, 0)),
        scratch_shapes=[pltpu.VMEM((N_EXPERTS * ch, d), jnp.bfloat16),
                        pltpu.VMEM((TOK_TILE, d), jnp.float32)])
    return pl.pallas_call(
        functools.partial(_moe_comb_kernel, cap=cap, ch=ch, ntile=ntile, final_norm=final_norm),
        grid_spec=grid_spec,
        out_shape=jax.ShapeDtypeStruct((b, length, d), jnp.float32),
        compiler_params=pltpu.CompilerParams(
            dimension_semantics=("arbitrary", "arbitrary"), vmem_limit_bytes=56 * 1024 * 1024),
        name="moe_combine",
    )(cs.reshape(-1).astype(jnp.int32), x, g2, y, idx_in, fg)


def _moe_block(x, m, g1, ng, sh, sc, g2, w_router, w_gate, w_up, w_down, layer, final_g=None):
    b, length, d = x.shape
    cap = EC_FACTOR * length // N_EXPERTS
    x1, hpk, lg = _moe_pre(x, m, g1, ng, sh, sc, w_router)
    aff, idx = _moe_select(lg, cap)
    gate = jnp.take_along_axis(aff, idx, axis=-1)
    ntile = length // TOK_TILE
    bounds = jnp.arange(ntile + 1, dtype=jnp.int32) * TOK_TILE
    cs = jnp.sum(idx[:, :, :, None] < bounds, axis=2, dtype=jnp.int32)
    grow = idx + (jnp.arange(b, dtype=jnp.int32) * length)[:, None, None]
    grow = jnp.swapaxes(grow, 0, 1).reshape(N_EXPERTS, b * cap)
    gate_e = jnp.swapaxes(gate, 0, 1).reshape(N_EXPERTS, b * cap, 1)
    y = _moe_ffn(hpk.reshape(b * length, d // 2), grow, gate_e, w_gate, w_up, w_down, layer)
    return _moe_combine(x1, g2, y, idx, cs, final_g)


def kernel(x_prompt, x_sample, state_ssd, c, c_ctx, norm_g, ada_w, ada_b, hy_in_w, hy_in_b, hy_short_w, hy_short_b, hy_f_w1, hy_f_b1, hy_f_w2, hy_f_b2, hy_f_w3, hy_f_freq, hy_f_bias, hy_out_w, ssd_in_w, ssd_conv_w, ssd_conv_b, ssd_dt_bias, ssd_A_log, ssd_D, ssd_norm_g, ssd_out_w, moe_router, moe_w_gate, moe_w_up, moe_w_down, final_norm_g):
    rows = x_sample.shape[1] // GRID_W
    xp = x_prompt
    xs = x_sample + _sincos_2d(rows, GRID_W, D_MODEL)[None]
    new_ssd = []
    for i in range(DEPTH):
        nb_s = c.shape[0]
        mods = _adaln(jnp.concatenate([c, c_ctx[None, :]], axis=0), ada_w[i], ada_b[i])
        sh1s, sc1s, g1s, sh2s, sc2s, g2s = [m[:nb_s] for m in mods]
        sh1p, sc1p, g1p, sh2p, sc2p, g2p = [m[nb_s:] for m in mods]
        j = i // N_MIXERS
        bp = (xp.shape[0], 1, D_MODEL)
        if i % N_MIXERS == 0:
            hy = (hy_in_w[j], hy_in_b[j], hy_short_w[j], hy_short_b[j], hy_f_w1[j], hy_f_b1[j],
                  hy_f_w2[j], hy_f_b2[j], hy_f_w3[j], hy_f_freq[j], hy_f_bias[j], hy_out_w[j])
            mp = _hyena_mixer_p(xp, norm_g[i, 0], jnp.broadcast_to(sh1p, bp), jnp.broadcast_to(sc1p, bp), *hy)
            ms = _hyena_mixer_p(xs, norm_g[i, 0], sh1s, sc1s, *hy)
        else:
            sp = (ssd_in_w[j], ssd_conv_w[j], ssd_conv_b[j], ssd_dt_bias[j], ssd_A_log[j],
                  ssd_D[j], ssd_norm_g[j], ssd_out_w[j])
            zeros = jnp.zeros((xp.shape[0], SSD_HEADS, SSD_HEAD_DIM, SSD_STATE), jnp.float32)
            mp, s_f, s_b = _ssd_mixer_p(xp, norm_g[i, 0], jnp.broadcast_to(sh1p, bp),
                                        jnp.broadcast_to(sc1p, bp), zeros, zeros, *sp)
            new_ssd.append(jnp.stack([s_f, s_b], axis=1))
            ms, _, _ = _ssd_mixer_p(xs, norm_g[i, 0], sh1s, sc1s, state_ssd[:, j, 0], state_ssd[:, j, 1], *sp)
        moe = (moe_router[i], moe_w_gate, moe_w_up, moe_w_down, i, final_norm_g if i == DEPTH - 1 else None)
        xp = _moe_block(xp, mp, jnp.broadcast_to(g1p, bp), norm_g[i, 1], jnp.broadcast_to(sh2p, bp),
                        jnp.broadcast_to(sc2p, bp), jnp.broadcast_to(g2p, bp), *moe)
        xs = _moe_block(xs, ms, g1s, norm_g[i, 1], sh2s, sc2s, g2s, *moe)
    new_state_ssd = jnp.stack(new_ssd, axis=1)
    return (xp, xs, new_state_ssd)
```

```python
import functools
import math

import jax
import jax.numpy as jnp
import numpy as np
from jax import lax
from jax.experimental import pallas as pl
from jax.experimental.pallas import tpu as pltpu

D_MODEL = 1024
DEPTH = 2
GRID_W = 64
N_MIXERS = 2
RMS_EPS = 1e-6
HY_EMB = 33
HY_BANDS = (HY_EMB - 1) // 2
HY_SHORT_DECAY_FRAC = 0.3
HY_LONG_DECAY_FRAC = 1.5
HY_DECAY_TARGET = 1e-2
HY_MAX_DECAY = math.log(HY_DECAY_TARGET) / HY_SHORT_DECAY_FRAC
HY_MIN_DECAY = math.log(HY_DECAY_TARGET) / HY_LONG_DECAY_FRAC
SSD_D_INNER = 2 * D_MODEL
SSD_HEAD_DIM = 64
SSD_HEADS = SSD_D_INNER // SSD_HEAD_DIM
SSD_GROUPS = 4
SSD_STATE = 128
SSD_CHUNK = 128
SSD_XBC = SSD_D_INNER + 2 * SSD_GROUPS * SSD_STATE
N_EXPERTS = 16
EC_FACTOR = 2

VMEM_LIMIT_BYTES = 48 * 1024 * 1024


def _mm_kernel(a_ref, b_ref, o_ref, acc_ref):
    @pl.when(pl.program_id(2) == 0)
    def _():
        acc_ref[...] = jnp.zeros_like(acc_ref)

    acc_ref[...] += jnp.dot(a_ref[...].astype(jnp.bfloat16), b_ref[...],
                            preferred_element_type=jnp.float32)

    @pl.when(pl.program_id(2) == pl.num_programs(2) - 1)
    def _():
        o_ref[...] = acc_ref[...]


def _pick(n, pref):
    for t in pref:
        if n % t == 0:
            return t
    return n


def _mm(a, b):
    m, k = a.shape
    n = b.shape[1]
    mp = -(-m // 8) * 8
    if mp != m:
        a = jnp.pad(a, ((0, mp - m), (0, 0)))
    tm = _pick(mp, (512, 256, 128, 64, 32, 16, 8))
    tn = _pick(n, (512, 256, 128))
    tk = _pick(k, (1024, 512, 256, 128))
    out = pl.pallas_call(
        _mm_kernel,
        grid=(mp // tm, n // tn, k // tk),
        in_specs=[pl.BlockSpec((tm, tk), lambda i, j, l: (i, l)),
                  pl.BlockSpec((tk, tn), lambda i, j, l: (l, j))],
        out_specs=pl.BlockSpec((tm, tn), lambda i, j, l: (i, j)),
        out_shape=jax.ShapeDtypeStruct((mp, n), jnp.float32),
        scratch_shapes=[pltpu.VMEM((tm, tn), jnp.float32)],
        compiler_params=pltpu.CompilerParams(
            dimension_semantics=("parallel", "parallel", "arbitrary"),
            vmem_limit_bytes=VMEM_LIMIT_BYTES),
        name="mm",
    )(a, b.astype(jnp.bfloat16))
    return out[:m]


def _mm3(a, b):
    lead = a.shape[:-1]
    return _mm(a.reshape(-1, a.shape[-1]), b).reshape(*lead, b.shape[1])


def _adaln(cond, ada_w, ada_b):
    m = _mm(jax.nn.silu(cond), ada_w) + ada_b
    return jnp.split(m[:, None, :], 6, axis=-1)


def _sincos_2d(rows, cols, d):
    q = d // 4
    omega = 1.0 / (10000.0 ** (jnp.arange(q, dtype=jnp.float32) / q))
    t = jnp.arange(rows * cols)
    er = (t // cols).astype(jnp.float32)[:, None] * omega[None, :]
    ec = (t % cols).astype(jnp.float32)[:, None] * omega[None, :]
    return jnp.concatenate([jnp.sin(er), jnp.cos(er), jnp.sin(ec), jnp.cos(ec)], axis=-1)


SSD_GN = SSD_GROUPS * SSD_STATE
SSD_GROUP_W = SSD_D_INNER // SSD_GROUPS
SSD_HEADS_PER_GROUP = SSD_HEADS // SSD_GROUPS
ROW_TILE = 256


def _modnorm(x, ng, sh, sc):
    ms = jnp.mean(x * x, axis=-1, keepdims=True)
    return (x * lax.rsqrt(ms + RMS_EPS) * ng) * (1.0 + sc) + sh


HALO = 8
CONV_COLS = 512


def _halo_rows(xm_ref, xp_ref, xn_ref, ng_ref, sh_ref, sc_ref):
    j = pl.program_id(1)
    xa = jnp.concatenate([xp_ref[0], xm_ref[0], xn_ref[0]], axis=0)
    h = _modnorm(xa, ng_ref[...], sh_ref[0], sc_ref[0]).astype(jnp.bfloat16)
    tm = xm_ref.shape[1]
    r = lax.broadcasted_iota(jnp.int32, (tm + 2 * HALO, 1), 0)
    valid = ((r >= HALO) | (j > 0)) & ((r < tm + HALO) | (j < pl.num_programs(1) - 1))
    return h, valid


def _proj_conv(h, valid, w_ref, w_col0, pb_ref, cw_ref, cb_ref, o_ref, tm, silu):
    taps = cw_ref.shape[0]
    ncol = o_ref.shape[2]
    for c0 in range(0, ncol, CONV_COLS):
        cols = slice(c0, c0 + CONV_COLS)
        u = jnp.dot(h, w_ref[:, w_col0 + c0:w_col0 + c0 + CONV_COLS], preferred_element_type=jnp.float32)
        if pb_ref is not None:
            u = u + pb_ref[:, cols]
        u = jnp.where(valid, u, 0.0)
        nrow = u.shape[0]
        acc = cb_ref[:, cols] + jnp.zeros((tm, CONV_COLS), jnp.float32)
        for k in range(taps):
            shifted = u if k == taps // 2 else pltpu.roll(u, (taps // 2 - k) % nrow, 0)
            acc = acc + cw_ref[k:k + 1, cols] * shifted[HALO:HALO + tm]
        if silu:
            acc = acc * jax.nn.sigmoid(acc)
        o_ref[0, :, cols] = acc


def _halo_specs(length, tm, d):
    nh = length // HALO
    per = tm // HALO
    main = pl.BlockSpec((1, tm, d), lambda i, j: (i, j, 0))
    prev = pl.BlockSpec((1, HALO, d), lambda i, j: (i, jnp.maximum(j * per - 1, 0), 0))
    nxt = pl.BlockSpec((1, HALO, d), lambda i, j: (i, jnp.minimum((j + 1) * per, nh - 1), 0))
    return [main, prev, nxt]


def _ssd_in_kernel(xm_ref, xp_ref, xn_ref, ng_ref, sh_ref, sc_ref, w_ref, wdt_ref, wdtt_ref, cw_ref, cb_ref,
                   z_ref, xbc_ref, dt_ref, dtt_ref):
    tm = xm_ref.shape[1]
    h, valid = _halo_rows(xm_ref, xp_ref, xn_ref, ng_ref, sh_ref, sc_ref)
    hm = h[HALO:HALO + tm]
    z_ref[0] = jnp.dot(hm, w_ref[:, :SSD_D_INNER], preferred_element_type=jnp.float32).astype(jnp.bfloat16)
    _proj_conv(h, valid, w_ref, SSD_D_INNER, None, cw_ref, cb_ref, xbc_ref, tm, silu=True)
    dt_ref[0] = jnp.dot(hm, wdt_ref[...], preferred_element_type=jnp.float32)
    dtt_ref[0] = lax.dot_general(wdtt_ref[...], hm, (((1,), (1,)), ((), ())),
                                 preferred_element_type=jnp.float32)


def _ssd_in(x, ng, sh, sc, in_w, conv_w, conv_b):
    b, length, d = x.shape
    tm = min(length, ROW_TILE)
    nzx = SSD_D_INNER + SSD_XBC
    w = in_w[:, :nzx].astype(jnp.bfloat16)
    wdt = in_w[:, nzx:]
    wdt_p = jnp.pad(wdt, ((0, 0), (0, LANES - 2 * SSD_HEADS))).astype(jnp.bfloat16)
    wdt_t = wdt.T.astype(jnp.bfloat16)
    taps = conv_w.shape[0]
    row = lambda i, j: (i, j, 0)
    per_b = lambda i, j: (i, 0, 0)
    full2 = lambda i, j: (0, 0)
    return pl.pallas_call(
        _ssd_in_kernel,
        grid=(b, length // tm),
        in_specs=_halo_specs(length, tm, d) + [
            pl.BlockSpec((1, d), full2), pl.BlockSpec((1, 1, d), per_b), pl.BlockSpec((1, 1, d), per_b),
            pl.BlockSpec((d, nzx), full2), pl.BlockSpec((d, LANES), full2),
            pl.BlockSpec((2 * SSD_HEADS, d), full2),
            pl.BlockSpec((taps, SSD_XBC), full2), pl.BlockSpec((1, SSD_XBC), full2)],
        out_specs=[pl.BlockSpec((1, tm, SSD_D_INNER), row), pl.BlockSpec((1, tm, SSD_XBC), row),
                   pl.BlockSpec((1, tm, LANES), row),
                   pl.BlockSpec((1, 2 * SSD_HEADS, tm), lambda i, j: (i, 0, j))],
        out_shape=[jax.ShapeDtypeStruct((b, length, SSD_D_INNER), jnp.bfloat16),
                   jax.ShapeDtypeStruct((b, length, SSD_XBC), jnp.float32),
                   jax.ShapeDtypeStruct((b, length, LANES), jnp.float32),
                   jax.ShapeDtypeStruct((b, 2 * SSD_HEADS, length), jnp.float32)],
        compiler_params=pltpu.CompilerParams(
            dimension_semantics=("parallel", "parallel"), vmem_limit_bytes=56 * 1024 * 1024),
        name="ssd_in",
    )(x, x, x, ng.reshape(1, d), sh, sc, w, wdt_p, wdt_t, conv_w, conv_b.reshape(1, SSD_XBC))


def _split3_bf16(v):
    p1 = v.astype(jnp.bfloat16)
    r1 = v - p1.astype(jnp.float32)
    p2 = r1.astype(jnp.bfloat16)
    p3 = (r1 - p2.astype(jnp.float32)).astype(jnp.bfloat16)
    return p1, p2, p3


def _softplus(v):
    return jnp.maximum(v, 0.0) + jnp.log1p(jnp.exp(-jnp.abs(v)))


def _expand_heads(cols, g, hoff):
    q = cols.shape[0]
    lane = lax.broadcasted_iota(jnp.int32, (q, LANES), 1)
    tiles = []
    for k in range(SSD_HEADS_PER_GROUP // 2):
        ha = hoff + g * SSD_HEADS_PER_GROUP + 2 * k
        tiles.append(jnp.take_along_axis(cols, jnp.where(lane < SSD_HEAD_DIM, ha, ha + 1), axis=1))
    return jnp.concatenate(tiles, axis=1)


def _ssd_scan_kernel(x_ref, b_ref, c_ref, dt_ref, dtt_ref, dtb_ref, dtbt_ref, a_ref, at_ref, init_ref, extra_ref,
                     y_ref, fin_ref, st_ref, *, reverse, hoff, add_prev):
    ci = pl.program_id(1)

    @pl.when(ci == 0)
    def _():
        st_ref[...] = init_ref[0]

    q = SSD_CHUNK
    f32, bf16 = jnp.float32, jnp.bfloat16
    dt = _softplus(dt_ref[0] + dtb_ref[...])
    dtt = _softplus(dtt_ref[0][hoff:hoff + SSD_HEADS, :] + dtbt_ref[...])
    ri = lax.broadcasted_iota(jnp.int32, (q, q), 0)
    cj = lax.broadcasted_iota(jnp.int32, (q, q), 1)
    keep = (cj >= ri) if reverse else (cj <= ri)
    tri = keep.astype(bf16)
    tri_t = ((ri >= cj) if reverse else (ri <= cj)).astype(bf16)
    acum = sum(jnp.dot(tri, p, preferred_element_type=f32) for p in _split3_bf16(dt * a_ref[...]))
    acum_t = sum(jnp.dot(p, tri_t, preferred_element_type=f32) for p in _split3_bf16(dtt * at_ref[...]))
    end = 0 if reverse else q - 1
    a_end = acum[end:end + 1, :]
    eacum = jnp.exp(acum)
    dt_dec_end = dt * jnp.exp(a_end - acum)
    lane = lax.broadcasted_iota(jnp.int32, (q, LANES), 1)
    for g in range(SSD_GROUPS):
        cg = c_ref[0][:, g * SSD_STATE:(g + 1) * SSD_STATE]
        bg = b_ref[0][:, g * SSD_STATE:(g + 1) * SSD_STATE]
        cg16 = cg.astype(bf16)
        cb = lax.dot_general(cg16, bg.astype(bf16), (((1,), (1,)), ((), ())), preferred_element_type=f32)
        xg = x_ref[0][:, g * SSD_GROUP_W:(g + 1) * SSD_GROUP_W]
        xg16 = xg.astype(bf16)
        eac_x = _expand_heads(eacum, g, hoff)
        yd = []
        for k in range(SSD_HEADS_PER_GROUP // 2):
            xp = xg16[:, k * LANES:(k + 1) * LANES]
            ys = []
            for hh in range(2):
                h = g * SSD_HEADS_PER_GROUP + 2 * k + hh
                seg = acum[:, hoff + h:hoff + h + 1] - acum_t[h:h + 1, :]
                lmat = jnp.exp(jnp.where(keep, seg, -jnp.inf)) * dtt[h:h + 1, :]
                ys.append(jnp.dot((cb * lmat).astype(bf16), xp, preferred_element_type=f32))
            yd.append(jnp.where(lane < SSD_HEAD_DIM, ys[0], ys[1]))
        st = st_ref[g]
        y_off = jnp.dot(cg16, st.astype(bf16), preferred_element_type=f32) * eac_x
        cols = slice(g * SSD_GROUP_W, (g + 1) * SSD_GROUP_W)
        if add_prev:
            other = extra_ref[0, :, cols].astype(f32)
        else:
            other = extra_ref[:, cols] * xg
        y_ref[0, :, cols] = (jnp.concatenate(yd, axis=1) + y_off + other).astype(bf16)
        xdd16 = (xg * _expand_heads(dt_dec_end, g, hoff)).astype(bf16)
        st_ref[g] = st * eac_x[end:end + 1, :] + jnp.dot(bg.T.astype(bf16), xdd16, preferred_element_type=f32)

    @pl.when(ci == pl.num_programs(1) - 1)
    def _():
        fin_ref[0] = st_ref[...]


def _ssd_scan_p(xbc, dt_raw, dt_raw_t, dt_bias, a, init, reverse, direction, y_prev=None, d_skip=None):
    b, length, _ = xbc.shape
    nc = length // SSD_CHUNK
    q = SSD_CHUNK
    cidx = (lambda j: nc - 1 - j) if reverse else (lambda j: j)
    nb = SSD_D_INNER // SSD_GN
    hoff = direction * SSD_HEADS
    full2 = lambda i, j: (0, 0)
    st_shape = (SSD_GROUPS, SSD_STATE, SSD_GROUP_W)
    lanes = lambda v: jnp.pad(v, (hoff, LANES - hoff - SSD_HEADS)).reshape(1, LANES)
    add_prev = y_prev is not None
    if add_prev:
        extra = y_prev
        extra_spec = pl.BlockSpec((1, q, SSD_D_INNER), lambda i, j: (i, cidx(j), 0))
    else:
        extra = jnp.repeat(d_skip, SSD_HEAD_DIM).reshape(1, SSD_D_INNER)
        extra_spec = pl.BlockSpec((1, SSD_D_INNER), full2)
    return pl.pallas_call(
        functools.partial(_ssd_scan_kernel, reverse=reverse, hoff=hoff, add_prev=add_prev),
        grid=(b, nc),
        in_specs=[pl.BlockSpec((1, q, SSD_D_INNER), lambda i, j: (i, cidx(j), 0)),
                  pl.BlockSpec((1, q, SSD_GN), lambda i, j: (i, cidx(j), nb)),
                  pl.BlockSpec((1, q, SSD_GN), lambda i, j: (i, cidx(j), nb + 1)),
                  pl.BlockSpec((1, q, LANES), lambda i, j: (i, cidx(j), 0)),
                  pl.BlockSpec((1, 2 * SSD_HEADS, q), lambda i, j: (i, 0, cidx(j))),
                  pl.BlockSpec((1, LANES), full2), pl.BlockSpec((SSD_HEADS, 1), full2),
                  pl.BlockSpec((1, LANES), full2), pl.BlockSpec((SSD_HEADS, 1), full2),
                  pl.BlockSpec((1,) + st_shape, lambda i, j: (i, 0, 0, 0)),
                  extra_spec],
        out_specs=[pl.BlockSpec((1, q, SSD_D_INNER), lambda i, j: (i, cidx(j), 0)),
                   pl.BlockSpec((1,) + st_shape, lambda i, j: (i, 0, 0, 0))],
        out_shape=[jax.ShapeDtypeStruct((b, length, SSD_D_INNER), jnp.bfloat16),
                   jax.ShapeDtypeStruct((b,) + st_shape, jnp.float32)],
        scratch_shapes=[pltpu.VMEM(st_shape, jnp.float32)],
        compiler_params=pltpu.CompilerParams(
            dimension_semantics=("parallel", "arbitrary"), vmem_limit_bytes=VMEM_LIMIT_BYTES),
        name="ssd_scan",
    )(xbc, xbc, xbc, dt_raw, dt_raw_t, lanes(dt_bias), dt_bias.reshape(-1, 1),
      lanes(a), a.reshape(-1, 1), init, extra)


def _ssd_out_kernel(y_ref, z_ref, ng_ref, w_ref, o_ref):
    z = z_ref[0].astype(jnp.float32)
    y = y_ref[0].astype(jnp.float32) * (z * jax.nn.sigmoid(z))
    ms = jnp.mean(y * y, axis=-1, keepdims=True)
    y = y * lax.rsqrt(ms + RMS_EPS) * ng_ref[...]
    o_ref[0] = jnp.dot(y.astype(jnp.bfloat16), w_ref[...], preferred_element_type=jnp.float32)


def _ssd_out(y, z, norm_g, out_w):
    b, length, di = y.shape
    d = out_w.shape[1]
    tm = min(length, 2 * ROW_TILE)
    row = lambda i, j: (i, j, 0)
    full2 = lambda i, j: (0, 0)
    return pl.pallas_call(
        _ssd_out_kernel,
        grid=(b, length // tm),
        in_specs=[pl.BlockSpec((1, tm, di), row), pl.BlockSpec((1, tm, di), row),
                  pl.BlockSpec((1, di), full2), pl.BlockSpec((di, d), full2)],
        out_specs=pl.BlockSpec((1, tm, d), row),
        out_shape=jax.ShapeDtypeStruct((b, length, d), jnp.float32),
        compiler_params=pltpu.CompilerParams(
            dimension_semantics=("parallel", "parallel"), vmem_limit_bytes=VMEM_LIMIT_BYTES),
        name="ssd_out",
    )(y, z, norm_g.reshape(1, di), out_w.astype(jnp.bfloat16))


def _state_to_kernel(s):
    b = s.shape[0]
    s = s.reshape(b, SSD_GROUPS, SSD_HEADS_PER_GROUP, SSD_HEAD_DIM, SSD_STATE)
    return jnp.transpose(s, (0, 1, 4, 2, 3)).reshape(b, SSD_GROUPS, SSD_STATE, SSD_GROUP_W)


def _state_from_kernel(s):
    b = s.shape[0]
    s = s.reshape(b, SSD_GROUPS, SSD_STATE, SSD_HEADS_PER_GROUP, SSD_HEAD_DIM)
    return jnp.transpose(s, (0, 1, 3, 4, 2)).reshape(b, SSD_HEADS, SSD_HEAD_DIM, SSD_STATE)


def _ssd_mixer_p(x, ng, sh, sc, init_f, init_b, in_w, conv_w, conv_b, dt_bias, a_log, d_skip, norm_g, out_w):
    z, xbc, dt_raw, dt_raw_t = _ssd_in(x, ng, sh, sc, in_w, conv_w, conv_b)
    a = -jnp.exp(a_log)
    yf, s_f = _ssd_scan_p(xbc, dt_raw, dt_raw_t, dt_bias[0], a[0], _state_to_kernel(init_f), False, 0,
                          d_skip=d_skip)
    y, s_b = _ssd_scan_p(xbc, dt_raw, dt_raw_t, dt_bias[1], a[1], _state_to_kernel(init_b), True, 1, y_prev=yf)
    m = _ssd_out(y, z, norm_g, out_w)
    return m, _state_from_kernel(s_f), _state_from_kernel(s_b)


HY_MAX_BLOCK = 512
HY_CC = 128
HY_MAC_ELEMS = 8192
HY_HIDDEN = 64
HY_FEAT_ROWS = 64


def _odd_dft_tables(n):
    m = np.arange(n, dtype=np.int64)[:, None]
    f = np.arange(n // 2, dtype=np.int64)[None, :]
    ang = 2.0 * np.pi * (((2 * f + 1) * m) % (2 * n)).astype(np.float64) / (2 * n)
    return np.cos(ang), np.sin(ang)


def _hy_in_kernel(xm_ref, xp_ref, xn_ref, ng_ref, sh_ref, sc_ref, w_ref, b_ref, cw_ref, cb_ref, o_ref):
    tm = xm_ref.shape[1]
    h, valid = _halo_rows(xm_ref, xp_ref, xn_ref, ng_ref, sh_ref, sc_ref)
    _proj_conv(h, valid, w_ref, 0, b_ref, cw_ref, cb_ref, o_ref, tm, silu=False)


def _hy_in(x, ng, sh, sc, in_w, in_b, short_w, short_b):
    b, length, d = x.shape
    n = in_w.shape[1]
    tm = min(length, ROW_TILE)
    taps = short_w.shape[0]
    per_b = lambda i, j: (i, 0, 0)
    full2 = lambda i, j: (0, 0)
    return pl.pallas_call(
        _hy_in_kernel,
        grid=(b, length // tm),
        in_specs=_halo_specs(length, tm, d) + [
            pl.BlockSpec((1, d), full2), pl.BlockSpec((1, 1, d), per_b), pl.BlockSpec((1, 1, d), per_b),
            pl.BlockSpec((d, n), full2), pl.BlockSpec((1, n), full2),
            pl.BlockSpec((taps, n), full2), pl.BlockSpec((1, n), full2)],
        out_specs=pl.BlockSpec((1, tm, n), lambda i, j: (i, j, 0)),
        out_shape=jax.ShapeDtypeStruct((b, length, n), jnp.float32),
        compiler_params=pltpu.CompilerParams(
            dimension_semantics=("parallel", "parallel"), vmem_limit_bytes=VMEM_LIMIT_BYTES),
        name="hy_in",
    )(x, x, x, ng.reshape(1, d), sh, sc, in_w.astype(jnp.bfloat16), in_b.reshape(1, n), short_w, short_b.reshape(1, n))


def _dot3(a, b):
    a_hi, a_lo = _split_bf16(a)
    b_hi, b_lo = _split_bf16(b)
    f32 = jnp.float32
    return (jnp.dot(a_hi, b_hi, preferred_element_type=f32) + jnp.dot(a_lo, b_hi, preferred_element_type=f32)
            + jnp.dot(a_hi, b_lo, preferred_element_type=f32))


def _hy_filter_kernel(w1t_ref, b1_ref, w2t_ref, b2_ref, w3t_ref, fr_ref, dl_ref, o_ref, *, length, blk):
    k = pl.program_id(0)
    q = (lax.broadcasted_iota(jnp.int32, (1, blk), 1) + k * blk)
    pos = jnp.abs(q - length).astype(jnp.float32)
    t = pos / float(length - 1)
    w = (2.0 * math.pi / length) * pos
    band = lax.broadcasted_iota(jnp.int32, (HY_BANDS, 1), 0).astype(jnp.float32)
    fb = 1e-4 + band * ((HY_BANDS - 1 - 1e-4) / (HY_BANDS - 1))
    z = jnp.concatenate([jnp.broadcast_to(t, (8, blk)), jnp.cos(fb * w), -jnp.sin(fb * w),
                         jnp.zeros((HY_FEAT_ROWS - 8 - 2 * HY_BANDS, blk), jnp.float32)], axis=0)
    h = jnp.sin(fr_ref[...] * (_dot3(w1t_ref[...], z) + b1_ref[...]))
    h = jnp.sin(fr_ref[...] * (_dot3(w2t_ref[...], h) + b2_ref[...]))
    kt = _dot3(w3t_ref[0], h)
    o_ref[0] = kt * jnp.exp(-t * dl_ref[...])


def _hy_filter(length, blk, f_w1, f_b1, f_w2, f_b2, f_w3, f_freq):
    d = f_w3.shape[1] // 2
    nk = 2 * length // blk
    w1t = jnp.concatenate([f_w1[0:1].T, jnp.zeros((HY_HIDDEN, 7), jnp.float32), f_w1[1:].T,
                           jnp.zeros((HY_HIDDEN, HY_FEAT_ROWS - 8 - 2 * HY_BANDS), jnp.float32)], axis=1)
    w3t = jnp.stack([f_w3[:, d:].T, f_w3[:, :d].T])
    deltas = jnp.abs(jnp.linspace(HY_MIN_DECAY, HY_MAX_DECAY, d, dtype=jnp.float32)).reshape(d, 1)
    col = lambda v: v.reshape(HY_HIDDEN, 1)
    full2 = lambda k: (0, 0)
    half = length // blk
    return pl.pallas_call(
        functools.partial(_hy_filter_kernel, length=length, blk=blk),
        grid=(nk,),
        in_specs=[pl.BlockSpec((HY_HIDDEN, HY_FEAT_ROWS), full2), pl.BlockSpec((HY_HIDDEN, 1), full2),
                  pl.BlockSpec((HY_HIDDEN, HY_HIDDEN), full2), pl.BlockSpec((HY_HIDDEN, 1), full2),
                  pl.BlockSpec((1, d, HY_HIDDEN), lambda k: (k // half, 0, 0)),
                  pl.BlockSpec((HY_HIDDEN, 1), full2), pl.BlockSpec((d, 1), full2)],
        out_specs=pl.BlockSpec((1, d, blk), lambda k: (k, 0, 0)),
        out_shape=jax.ShapeDtypeStruct((nk, d, blk), jnp.float32),
        compiler_params=pltpu.CompilerParams(
            dimension_semantics=("parallel",), vmem_limit_bytes=VMEM_LIMIT_BYTES),
        name="hy_filter",
    )(w1t, col(f_b1), f_w2.T, col(f_b2), w3t, col(f_freq), deltas)


def _hy_gspec_kernel(hi_ref, lo_ref, ft_ref, fb_ref, o_ref):
    o_ref[0] = _dot3(hi_ref[0], ft_ref[...]) + _dot3(lo_ref[0], fb_ref[...])


def _hy_gspec(kt):
    nk, d, blk = kt.shape
    cos, sin = _odd_dft_tables(2 * blk)
    top = np.concatenate([cos[:blk], -sin[:blk]], axis=1)
    bot = -np.concatenate([cos[blk:], -sin[blk:]], axis=1)
    bot[0] = 0.0
    tm = 512
    full2 = lambda e, i: (0, 0)
    return pl.pallas_call(
        _hy_gspec_kernel,
        grid=(nk - 1, d // tm),
        in_specs=[pl.BlockSpec((1, tm, blk), lambda e, i: (e + 1, i, 0)),
                  pl.BlockSpec((1, tm, blk), lambda e, i: (e, i, 0)),
                  pl.BlockSpec((blk, 2 * blk), full2), pl.BlockSpec((blk, 2 * blk), full2)],
        out_specs=pl.BlockSpec((1, tm, 2 * blk), lambda e, i: (e, i, 0)),
        out_shape=jax.ShapeDtypeStruct((nk - 1, d, 2 * blk), jnp.float32),
        compiler_params=pltpu.CompilerParams(
            dimension_semantics=("parallel", "parallel"), vmem_limit_bytes=VMEM_LIMIT_BYTES),
        name="hy_gspec",
    )(kt, kt, jnp.asarray(top, jnp.float32), jnp.asarray(bot, jnp.float32))


def _hy_conv_kernel(x0_ref, x1_ref, v_ref, g_ref, fb_ref, ff_ref, fi_ref, o_ref, lhs_ref, u_ref, y_ref, *, nb):
    cc, bsz = HY_CC, ff_ref.shape[0]
    mrows = HY_MAC_ELEMS // bsz
    for j in range(nb):
        sl = slice(j * bsz, (j + 1) * bsz)
        wj = v_ref[0, sl, :] * x1_ref[0, sl, :]
        lhs_ref[j * cc:(j + 1) * cc, :] = wj.T.astype(jnp.bfloat16)
    u_ref[...] = jnp.dot(lhs_ref[...], ff_ref[...], preferred_element_type=jnp.float32)

    def per_out_block(i, carry):
        def per_rows(rc, carry2):
            rows = pl.ds(pl.multiple_of(rc * mrows, mrows), mrows)
            acc_r = jnp.zeros((mrows, bsz), jnp.float32)
            acc_i = jnp.zeros((mrows, bsz), jnp.float32)
            for j in range(nb):
                e = i - j + (nb - 1)
                gr = g_ref[e, rows, 0:bsz]
                gi = g_ref[e, rows, bsz:2 * bsz]
                urows = pl.ds(pl.multiple_of(j * cc + rc * mrows, mrows), mrows)
                ur = u_ref[urows, 0:bsz]
                ui = u_ref[urows, bsz:2 * bsz]
                acc_r = acc_r + gr * ur - gi * ui
                acc_i = acc_i + gr * ui + gi * ur
            yrows = pl.ds(pl.multiple_of(i * cc + rc * mrows, mrows), mrows)
            y_ref[yrows, 0:bsz] = acc_r.astype(jnp.bfloat16)
            y_ref[yrows, bsz:2 * bsz] = acc_i.astype(jnp.bfloat16)
            return carry2
        return lax.fori_loop(0, cc // mrows, per_rows, carry)
    lax.fori_loop(0, nb, per_out_block, 0)

    yt = jnp.dot(y_ref[...], fi_ref[...], preferred_element_type=jnp.float32)
    for i in range(nb):
        sl = slice(i * bsz, (i + 1) * bsz)
        w = v_ref[0, sl, :] * x1_ref[0, sl, :]
        gated = (yt[i * cc:(i + 1) * cc, :].T + fb_ref[...] * w) * x0_ref[0, sl, :]
        o_ref[0, sl, :] = gated.astype(o_ref.dtype)


def _hy_conv(u, g, f_bias, blk):
    b, length, d3 = u.shape
    d = d3 // 3
    nb = length // blk
    ncb = d // HY_CC
    cos, sin = _odd_dft_tables(2 * blk)
    fwd = np.concatenate([cos[:blk], -sin[:blk]], axis=1)
    inv = (1.0 / blk) * np.concatenate([cos[:blk].T, -sin[:blk].T], axis=0)
    col = lambda off: pl.BlockSpec((1, length, HY_CC), lambda c, i, off=off: (i, 0, off + c))
    full2 = lambda c, i: (0, 0)
    return pl.pallas_call(
        functools.partial(_hy_conv_kernel, nb=nb),
        grid=(ncb, b),
        in_specs=[col(0), col(ncb), col(2 * ncb),
                  pl.BlockSpec((2 * nb - 1, HY_CC, 2 * blk), lambda c, i: (0, c, 0)),
                  pl.BlockSpec((1, HY_CC), lambda c, i: (0, c)),
                  pl.BlockSpec((blk, 2 * blk), full2), pl.BlockSpec((2 * blk, blk), full2)],
        out_specs=pl.BlockSpec((1, length, HY_CC), lambda c, i: (i, 0, c)),
        out_shape=jax.ShapeDtypeStruct((b, length, d), jnp.bfloat16),
        scratch_shapes=[pltpu.VMEM((nb * HY_CC, blk), jnp.bfloat16),
                        pltpu.VMEM((nb * HY_CC, 2 * blk), jnp.float32),
                        pltpu.VMEM((nb * HY_CC, 2 * blk), jnp.bfloat16)],
        compiler_params=pltpu.CompilerParams(
            dimension_semantics=("parallel", "arbitrary"), vmem_limit_bytes=56 * 1024 * 1024),
        name="hy_conv",
    )(u, u, u, g, f_bias.reshape(1, d), jnp.asarray(fwd, jnp.bfloat16), jnp.asarray(inv, jnp.bfloat16))


def _hyena_mixer_p(x, ng, sh, sc, in_w, in_b, short_w, short_b, f_w1, f_b1, f_w2, f_b2, f_w3, f_freq, f_bias, out_w):
    length = x.shape[1]
    blk = min(HY_MAX_BLOCK, length)
    u = _hy_in(x, ng, sh, sc, in_w, in_b, short_w, short_b)
    g = _hy_gspec(_hy_filter(length, blk, f_w1, f_b1, f_w2, f_b2, f_w3, f_freq))
    return _mm3(_hy_conv(u, g, f_bias, blk), out_w)


TOK_TILE = 256
MOE_ROWS = 512
SEG_CHUNK = 64
BF16_TILE_ROWS = 16
LANES = 128


def _split_bf16(w):
    hi = w.astype(jnp.bfloat16)
    lo = (w - hi.astype(jnp.float32)).astype(jnp.bfloat16)
    return hi, lo


def _moe_pre_kernel(x_ref, m_ref, g1_ref, ng_ref, sh_ref, sc_ref, wrh_ref, wrl_ref,
                    xo_ref, hpk_ref, lg_ref):
    x = x_ref[0] + g1_ref[0] * m_ref[0]
    xo_ref[0] = x
    ms = jnp.mean(x * x, axis=-1, keepdims=True)
    h = x * lax.rsqrt(ms + RMS_EPS) * ng_ref[...]
    h = h * (1.0 + sc_ref[0]) + sh_ref[0]
    h_hi = h.astype(jnp.bfloat16)
    h_lo = (h - h_hi.astype(jnp.float32)).astype(jnp.bfloat16)
    dn = (((1,), (1,)), ((), ()))
    lg = lax.dot_general(wrh_ref[...], h_hi, dn, preferred_element_type=jnp.float32)
    lg += lax.dot_general(wrh_ref[...], h_lo, dn, preferred_element_type=jnp.float32)
    lg += lax.dot_general(wrl_ref[...], h_hi, dn, preferred_element_type=jnp.float32)
    lg_ref[0] = lg
    half = h.shape[1] // 2
    wa = pltpu.bitcast(h_hi[:, :half].astype(jnp.float32), jnp.uint32) >> 16
    wb = pltpu.bitcast(h_hi[:, half:].astype(jnp.float32), jnp.uint32) & jnp.uint32(0xFFFF0000)
    hpk_ref[0] = wa | wb


def _moe_pre(x, m, g1, ng, sh, sc, w_router):
    b, length, d = x.shape
    tm = min(length, 512)
    wrh, wrl = _split_bf16(w_router.T)
    row = lambda i, j: (i, j, 0)
    per_b = lambda i, j: (i, 0, 0)
    full2 = lambda i, j: (0, 0)
    return pl.pallas_call(
        _moe_pre_kernel,
        grid=(b, length // tm),
        in_specs=[pl.BlockSpec((1, tm, d), row), pl.BlockSpec((1, tm, d), row),
                  pl.BlockSpec((1, 1, d), per_b), pl.BlockSpec((1, d), full2),
                  pl.BlockSpec((1, 1, d), per_b), pl.BlockSpec((1, 1, d), per_b),
                  pl.BlockSpec((N_EXPERTS, d), full2), pl.BlockSpec((N_EXPERTS, d), full2)],
        out_specs=[pl.BlockSpec((1, tm, d), row), pl.BlockSpec((1, tm, d // 2), row),
                   pl.BlockSpec((1, N_EXPERTS, tm), lambda i, j: (i, 0, j))],
        out_shape=[jax.ShapeDtypeStruct((b, length, d), jnp.float32),
                   jax.ShapeDtypeStruct((b, length, d // 2), jnp.uint32),
                   jax.ShapeDtypeStruct((b, N_EXPERTS, length), jnp.float32)],
        compiler_params=pltpu.CompilerParams(
            dimension_semantics=("parallel", "parallel"), vmem_limit_bytes=VMEM_LIMIT_BYTES),
        name="moe_pre",
    )(x, m, g1, ng.reshape(1, d), sh, sc, wrh, wrl)


SEL_BLOCK = 256
SEL_ROWS = 64
F32_INF_BITS = 0x7F800000


def _prefix_counts(flags, inclusive):
    e, length = flags.shape
    r = lax.broadcasted_iota(jnp.int32, (SEL_BLOCK, SEL_BLOCK), 0)
    c = lax.broadcasted_iota(jnp.int32, (SEL_BLOCK, SEL_BLOCK), 1)
    tri = ((r <= c) if inclusive else (r < c)).astype(jnp.bfloat16)
    off = jnp.zeros((e, 1), jnp.float32)
    blocks = []
    for k in range(length // SEL_BLOCK):
        blk = flags[:, k * SEL_BLOCK:(k + 1) * SEL_BLOCK]
        blocks.append(jnp.dot(blk.astype(jnp.bfloat16), tri, preferred_element_type=jnp.float32) + off)
        off = off + jnp.sum(blk, axis=1, keepdims=True)
    return jnp.concatenate(blocks, axis=1)


def _moe_select_kernel(lg_ref, aff_ref, idx_ref, rank_ref, cnt_ref, *, cap):
    lg = lg_ref[...]
    ng, ne, length = lg.shape
    nr = ng * ne
    ex = jnp.exp(lg - jnp.max(lg, axis=1, keepdims=True))
    aff3 = ex / jnp.sum(ex, axis=1, keepdims=True)
    aff_ref[...] = aff3
    aff = aff3.reshape(nr, length)
    bits = pltpu.bitcast(aff, jnp.int32)

    def bisect(_, carry):
        lo, hi = carry
        mid = lo + ((hi - lo + 1) >> 1)
        cnt = jnp.sum((bits >= mid).astype(jnp.float32), axis=1, keepdims=True)
        ok = cnt >= cap
        return jnp.where(ok, mid, lo), jnp.where(ok, hi, mid - 1)
    lo0 = jnp.zeros((nr, 1), jnp.int32)
    tau, _ = lax.fori_loop(0, 32, bisect, (lo0, lo0 + F32_INF_BITS))
    gt = bits > tau
    eq = (bits == tau).astype(jnp.float32)
    need = cap - jnp.sum(gt.astype(jnp.float32), axis=1, keepdims=True)
    keep = jnp.where(gt, 1.0, jnp.where(_prefix_counts(eq, False) < need, eq, 0.0))
    rank_ref[...] = _prefix_counts(keep, True)

    rows = min(SEL_ROWS, cap)
    lane = lax.broadcasted_iota(jnp.int32, (cap, LANES), 1)
    cnt_ref[...] = jnp.zeros((cap, LANES), jnp.float32)

    def per_row(ri, carry):
        rk = rank_ref[pl.ds(ri, 1), :]
        cols = []
        for ck in range(cap // rows):
            slot = (lax.broadcasted_iota(jnp.int32, (rows, LANES), 0) + ck * rows).astype(jnp.float32)
            acc = jnp.zeros((rows, LANES), jnp.float32)
            for j in range(length // LANES):
                acc = acc + jnp.where(rk[:, j * LANES:(j + 1) * LANES] <= slot, 1.0, 0.0)
            cols.append(jnp.sum(acc, axis=1, keepdims=True))
        col = jnp.concatenate(cols, axis=0)
        cnt_ref[...] = jnp.where(lane == ri, col, cnt_ref[...])
        return carry
    lax.fori_loop(0, nr, per_row, 0)
    for g in range(ng):
        idx_ref[g] = cnt_ref[:, g * ne:(g + 1) * ne].astype(jnp.int32)


def _moe_select(lg, cap):
    b, ne, length = lg.shape
    grp = min(b, LANES // ne)
    aff, idx = pl.pallas_call(
        functools.partial(_moe_select_kernel, cap=cap),
        grid=(b // grp,),
        in_specs=[pl.BlockSpec((grp, ne, length), lambda i: (i, 0, 0))],
        out_specs=[pl.BlockSpec((grp, ne, length), lambda i: (i, 0, 0)),
                   pl.BlockSpec((grp, cap, ne), lambda i: (i, 0, 0))],
        out_shape=[jax.ShapeDtypeStruct((b, ne, length), jnp.float32),
                   jax.ShapeDtypeStruct((b, cap, ne), jnp.int32)],
        scratch_shapes=[pltpu.VMEM((grp * ne, length), jnp.float32), pltpu.VMEM((cap, LANES), jnp.float32)],
        compiler_params=pltpu.CompilerParams(
            dimension_semantics=("parallel",), vmem_limit_bytes=VMEM_LIMIT_BYTES),
        name="moe_select",
    )(lg)
    return aff, jnp.swapaxes(idx, 1, 2)


def _moe_ffn_kernel(idx_ref, nidx_ref, h_hbm, gate_ref, wg_ref, wu_ref, wd_ref, y_ref,
                    xe_ref, wgb, wub, wdb, sem):
    nblk = pl.num_programs(1)
    step = pl.program_id(0) * nblk + pl.program_id(1)
    last = pl.num_programs(0) * nblk - 1

    def issue(ids_ref, slot):
        base = slot * MOE_ROWS
        for c in range(MOE_ROWS):
            pltpu.make_async_copy(h_hbm.at[pl.ds(ids_ref[0, 0, c], 1)], xe_ref.at[pl.ds(base + c, 1)],
                                  sem.at[slot]).start(priority=c % 2)

    @pl.when(step == 0)
    def _():
        issue(idx_ref, 0)

    for parity in range(2):
        @pl.when((step < last) & (step % 2 == parity))
        def _(parity=parity):
            issue(nidx_ref, 1 - parity)

    @pl.when(pl.program_id(1) == 0)
    def _():
        wgb[...] = wg_ref[0, 0].astype(jnp.bfloat16)
        wub[...] = wu_ref[0, 0].astype(jnp.bfloat16)
        wdb[...] = wd_ref[0, 0].astype(jnp.bfloat16)

    slot = step % 2
    rows = pl.ds(pl.multiple_of(slot * MOE_ROWS, MOE_ROWS), MOE_ROWS)
    pltpu.make_async_copy(h_hbm.at[pl.ds(0, MOE_ROWS)], xe_ref.at[rows], sem.at[slot]).wait()
    half = wgb.shape[0] // 2
    w = xe_ref[rows, :]
    xa = pltpu.bitcast(w << 16, jnp.float32).astype(jnp.bfloat16)
    xb = pltpu.bitcast(w & jnp.uint32(0xFFFF0000), jnp.float32).astype(jnp.bfloat16)
    hg = jnp.dot(xa, wgb[:half], preferred_element_type=jnp.float32)
    hg += jnp.dot(xb, wgb[half:], preferred_element_type=jnp.float32)
    hu = jnp.dot(xa, wub[:half], preferred_element_type=jnp.float32)
    hu += jnp.dot(xb, wub[half:], preferred_element_type=jnp.float32)
    hid = (hg * jax.nn.sigmoid(hg) * hu).astype(jnp.bfloat16)
    y = jnp.dot(hid, wdb[...], preferred_element_type=jnp.float32)
    y_ref[0] = (y * gate_ref[0]).astype(jnp.bfloat16)


def _moe_ffn(hpk, grow, gate, w_gate, w_up, w_down, layer):
    e, r = grow.shape
    d, f = w_gate.shape[2], w_gate.shape[3]
    nblk = r // MOE_ROWS
    nsteps = e * nblk
    wspec = lambda shp: pl.BlockSpec((1, 1) + shp, lambda i, j: (layer, i, 0, 0))
    ids = grow.reshape(nsteps, 1, MOE_ROWS)
    smem_ids = lambda off: pl.BlockSpec(
        (1, 1, MOE_ROWS), lambda i, j: (jnp.minimum(i * nblk + j + off, nsteps - 1), 0, 0), memory_space=pltpu.SMEM)
    return pl.pallas_call(
        _moe_ffn_kernel,
        grid=(e, nblk),
        in_specs=[smem_ids(0), smem_ids(1),
                  pl.BlockSpec(memory_space=pltpu.HBM),
                  pl.BlockSpec((1, MOE_ROWS, 1), lambda i, j: (i, j, 0)),
                  wspec((d, f)), wspec((d, f)), wspec((f, d))],
        out_specs=pl.BlockSpec((1, MOE_ROWS, d), lambda i, j: (i, j, 0)),
        out_shape=jax.ShapeDtypeStruct((e, r, d), jnp.bfloat16),
        scratch_shapes=[pltpu.VMEM((2 * MOE_ROWS, d // 2), jnp.uint32),
                        pltpu.VMEM((d, f), jnp.bfloat16), pltpu.VMEM((d, f), jnp.bfloat16),
                        pltpu.VMEM((f, d), jnp.bfloat16),
                        pltpu.SemaphoreType.DMA((2,))],
        compiler_params=pltpu.CompilerParams(
            dimension_semantics=("arbitrary", "arbitrary"), vmem_limit_bytes=VMEM_LIMIT_BYTES),
        name="moe_ffn",
    )(ids, ids, hpk, gate, w_gate, w_up, w_down)


def _moe_comb_kernel(cs_ref, x_ref, g2_ref, y_ref, idx_ref, fg_ref, o_ref, ycat, acc, *, cap, ch, ntile, final_norm):
    b = pl.program_id(0)
    t = pl.program_id(1)
    base = t * TOK_TILE
    sub = lax.broadcasted_iota(jnp.int32, (TOK_TILE, LANES), 0) + base
    if ntile == 1:
        for e in range(N_EXPERTS):
            ycat[e * ch:(e + 1) * ch, :] = y_ref[e, 0:ch, :]
        v = idx_ref[0]
        tiles = [(v[:, p * LANES:(p + 1) * LANES] == sub).astype(jnp.bfloat16)
                 for p in range(N_EXPERTS * ch // LANES)]
        acc[...] = jnp.dot(jnp.concatenate(tiles, axis=1), ycat[...], preferred_element_type=jnp.float32)
    else:
        lane = lax.broadcasted_iota(jnp.int32, (1, LANES), 1)
        per = LANES // ch
        sts = []
        for e in range(N_EXPERTS):
            s0 = cs_ref[(b * N_EXPERTS + e) * (ntile + 1) + t]
            st = jnp.minimum((s0 // BF16_TILE_ROWS) * BF16_TILE_ROWS, cap - ch)
            st = pl.multiple_of(st, BF16_TILE_ROWS)
            sts.append(st)
            ycat[e * ch:(e + 1) * ch, :] = y_ref[e, pl.ds(st, ch), :]
        tiles = []
        for p in range(N_EXPERTS // per):
            v = None
            for q in range(per):
                e = p * per + q
                r = pltpu.roll(idx_ref[0, e:e + 1, :], (2 * cap - sts[e] + q * ch) % cap, 1)[:, :LANES]
                v = r if v is None else jnp.where(lane >= q * ch, r, v)
            tiles.append((v == sub).astype(jnp.bfloat16))
        acc[...] = jnp.dot(jnp.concatenate(tiles, axis=1), ycat[...], preferred_element_type=jnp.float32)
        sub_c = lax.broadcasted_iota(jnp.int32, (TOK_TILE, ch), 0) + base
        lane_c = lax.broadcasted_iota(jnp.int32, (1, ch), 1)
        for e in range(N_EXPERTS):
            s1 = cs_ref[(b * N_EXPERTS + e) * (ntile + 1) + t + 1]
            first_end = sts[e] + ch
            n_extra = jnp.maximum(s1 - first_end + ch - 1, 0) // ch

            def extra(q, carry, e=e, first_end=first_end):
                lo = first_end + q * ch
                stq = pl.multiple_of(jnp.minimum(lo, cap - ch), BF16_TILE_ROWS)
                r = pltpu.roll(idx_ref[0, e:e + 1, :], (2 * cap - stq) % cap, 1)[:, :ch]
                hit = (r == sub_c) & (lane_c + stq >= lo)
                acc[...] += jnp.dot(hit.astype(jnp.bfloat16), y_ref[e, pl.ds(stq, ch), :],
                                    preferred_element_type=jnp.float32)
                return carry
            lax.fori_loop(0, n_extra, extra, 0)
    out = x_ref[0] + g2_ref[0] * acc[...]
    if final_norm:
        ms = jnp.mean(out * out, axis=-1, keepdims=True)
        out = out * lax.rsqrt(ms + RMS_EPS) * fg_ref[...]
    o_ref[0] = out


def _moe_combine(x, g2, y, idx, cs, final_g=None):
    b, length, d = x.shape
    final_norm = final_g is not None
    fg = (final_g if final_norm else jnp.ones((d,), jnp.float32)).reshape(1, d)
    cap = idx.shape[2]
    ntile = length // TOK_TILE
    ch = min(SEG_CHUNK, cap)
    if ntile == 1:
        idx_in = idx.reshape(b, 1, N_EXPERTS * cap)
        idx_spec = pl.BlockSpec((1, 1, N_EXPERTS * cap), lambda i, j, c: (i, 0, 0))
    else:
        idx_in = idx
        idx_spec = pl.BlockSpec((1, N_EXPERTS, cap), lambda i, j, c: (i, 0, 0))
    grid_spec = pltpu.PrefetchScalarGridSpec(
        num_scalar_prefetch=1,
        grid=(b, ntile),
        in_specs=[pl.BlockSpec((1, TOK_TILE, d), lambda i, j, c: (i, j, 0)),
                  pl.BlockSpec((1, 1, d), lambda i, j, c: (i, 0, 0)),
                  pl.BlockSpec((N_EXPERTS, cap, d), lambda i, j, c: (0, i, 0)),
                  idx_spec,
                  pl.BlockSpec((1, d), lambda i, j, c: (0, 0))],
        out_specs=pl.BlockSpec((1, TOK_TILE, d), lambda i, j, c: (i, j, 0)),
        scratch_shapes=[pltpu.VMEM((N_EXPERTS * ch, d), jnp.bfloat16),
                        pltpu.VMEM((TOK_TILE, d), jnp.float32)])
    return pl.pallas_call(
        functools.partial(_moe_comb_kernel, cap=cap, ch=ch, ntile=ntile, final_norm=final_norm),
        grid_spec=grid_spec,
        out_shape=jax.ShapeDtypeStruct((b, length, d), jnp.float32),
        compiler_params=pltpu.CompilerParams(
            dimension_semantics=("arbitrary", "arbitrary"), vmem_limit_bytes=56 * 1024 * 1024),
        name="moe_combine",
    )(cs.reshape(-1).astype(jnp.int32), x, g2, y, idx_in, fg)


def _moe_block(x, m, g1, ng, sh, sc, g2, w_router, w_gate, w_up, w_down, layer, final_g=None):
    b, length, d = x.shape
    cap = EC_FACTOR * length // N_EXPERTS
    x1, hpk, lg = _moe_pre(x, m, g1, ng, sh, sc, w_router)
    aff, idx = _moe_select(lg, cap)
    gate = jnp.take_along_axis(aff, idx, axis=-1)
    ntile = length // TOK_TILE
    bounds = jnp.arange(ntile + 1, dtype=jnp.int32) * TOK_TILE
    cs = jnp.sum(idx[:, :, :, None] < bounds, axis=2, dtype=jnp.int32)
    grow = idx + (jnp.arange(b, dtype=jnp.int32) * length)[:, None, None]
    grow = jnp.swapaxes(grow, 0, 1).reshape(N_EXPERTS, b * cap)
    gate_e = jnp.swapaxes(gate, 0, 1).reshape(N_EXPERTS, b * cap, 1)
    y = _moe_ffn(hpk.reshape(b * length, d // 2), grow, gate_e, w_gate, w_up, w_down, layer)
    return _moe_combine(x1, g2, y, idx, cs, final_g)


def kernel(x_prompt, x_sample, state_ssd, c, c_ctx, norm_g, ada_w, ada_b, hy_in_w, hy_in_b, hy_short_w, hy_short_b, hy_f_w1, hy_f_b1, hy_f_w2, hy_f_b2, hy_f_w3, hy_f_freq, hy_f_bias, hy_out_w, ssd_in_w, ssd_conv_w, ssd_conv_b, ssd_dt_bias, ssd_A_log, ssd_D, ssd_norm_g, ssd_out_w, moe_router, moe_w_gate, moe_w_up, moe_w_down, final_norm_g):
    rows = x_sample.shape[1] // GRID_W
    xp = x_prompt
    xs = x_sample + _sincos_2d(rows, GRID_W, D_MODEL)[None]
    new_ssd = []
    for i in range(DEPTH):
        nb_s = c.shape[0]
        mods = _adaln(jnp.concatenate([c, c_ctx[None, :]], axis=0), ada_w[i], ada_b[i])
        sh1s, sc1s, g1s, sh2s, sc2s, g2s = [m[:nb_s] for m in mods]
        sh1p, sc1p, g1p, sh2p, sc2p, g2p = [m[nb_s:] for m in mods]
        j = i // N_MIXERS
        bp = (xp.shape[0], 1, D_MODEL)
        if i % N_MIXERS == 0:
            hy = (hy_in_w[j], hy_in_b[j], hy_short_w[j], hy_short_b[j], hy_f_w1[j], hy_f_b1[j],
                  hy_f_w2[j], hy_f_b2[j], hy_f_w3[j], hy_f_freq[j], hy_f_bias[j], hy_out_w[j])
            mp = _hyena_mixer_p(xp, norm_g[i, 0], jnp.broadcast_to(sh1p, bp), jnp.broadcast_to(sc1p, bp), *hy)
            ms = _hyena_mixer_p(xs, norm_g[i, 0], sh1s, sc1s, *hy)
        else:
            sp = (ssd_in_w[j], ssd_conv_w[j], ssd_conv_b[j], ssd_dt_bias[j], ssd_A_log[j],
                  ssd_D[j], ssd_norm_g[j], ssd_out_w[j])
            zeros = jnp.zeros((xp.shape[0], SSD_HEADS, SSD_HEAD_DIM, SSD_STATE), jnp.float32)
            mp, s_f, s_b = _ssd_mixer_p(xp, norm_g[i, 0], jnp.broadcast_to(sh1p, bp),
                                        jnp.broadcast_to(sc1p, bp), zeros, zeros, *sp)
            new_ssd.append(jnp.stack([s_f, s_b], axis=1))
            ms, _, _ = _ssd_mixer_p(xs, norm_g[i, 0], sh1s, sc1s, state_ssd[:, j, 0], state_ssd[:, j, 1], *sp)
        moe = (moe_router[i], moe_w_gate, moe_w_up, moe_w_down, i, final_norm_g if i == DEPTH - 1 else None)
        xp = _moe_block(xp, mp, jnp.broadcast_to(g1p, bp), norm_g[i, 1], jnp.broadcast_to(sh2p, bp),
                        jnp.broadcast_to(sc2p, bp), jnp.broadcast_to(g2p, bp), *moe)
        xs = _moe_block(xs, ms, g1s, norm_g[i, 1], sh2s, sc2s, g2s, *moe)
    new_state_ssd = jnp.stack(new_ssd, axis=1)
    return (xp, xs, new_state_ssd)
```
